```python
import math
import jax
import jax.numpy as jnp
from jax import lax
import numpy as np

D_MODEL = 1024
BATCH = 32
SEQ = 256
DEPTH = 2
DEC_BATCH = 2
DEC_SEQ = 4096
PAST_LEN = 256

GRID_W = 64
ROPE_BASE = 10000.0
EPS = 1e-6
NEG_INF = -1e30
Q_BLOCK = 128
N_BRANCH = 4
BRANCH_W = 256

A_HEADS = 4
A_KV_HEADS = 2
A_GROUP = A_HEADS // A_KV_HEADS
A_HD = 64
A_WINDOW = 128
A_SCALE = A_HD ** -0.5
B_HEADS = 4
B_Q_RANK = 192
B_KV_RANK = 128
B_NOPE = 64
B_ROPE = 32
B_VD = 64
B_SCALE = (B_NOPE + B_ROPE) ** -0.5
C_GROUPS = 4
C_GW = 64
C_WINDOWS = (2, 4, 8, 16)
D_HEADS = 4
D_DK = 64
D_DV = 64
D_CONV = 5
D_CHUNK = 64
FF_RAW = -(-8 * D_MODEL // 3)
D_FF = -(-FF_RAW // 256) * 256

IN_SIZES = (A_HEADS * A_HD, A_KV_HEADS * A_HD, A_KV_HEADS * A_HD,
            B_Q_RANK, B_KV_RANK, B_ROPE,
            C_GROUPS * C_GW,
            D_HEADS * (2 * D_DK + D_DV), D_HEADS * D_DV, 2 * D_HEADS, 2 * D_HEADS,
            N_BRANCH * D_MODEL)
P_IN = sum(IN_SIZES)

kernel_name = 'hybrid_prefix_diffusion_step'


def _split_points(sizes):
    pts, acc = [], 0
    for s in sizes[:-1]:
        acc += s
        pts.append(acc)
    return pts


def rmsnorm(x, g):
    xf = x.astype(jnp.float32)
    y = xf * lax.rsqrt(jnp.mean(xf * xf, axis=-1, keepdims=True) + EPS)
    return (y * g.astype(jnp.float32)).astype(x.dtype)


def l2norm(x):
    return x * lax.rsqrt(jnp.sum(x * x, axis=-1, keepdims=True) + EPS)


def axial_rope_tables(n_tokens, dim):
    n_rows = n_tokens // GRID_W
    row = jnp.repeat(jnp.arange(n_rows), GRID_W).astype(jnp.float32)
    col = jnp.tile(jnp.arange(GRID_W), n_rows).astype(jnp.float32)
    nfreq = dim // 4
    inv = ROPE_BASE ** (-jnp.arange(nfreq, dtype=jnp.float32) / nfreq)
    ang_r = row[:, None] * inv
    ang_c = col[:, None] * inv
    ang = jnp.concatenate([ang_r, ang_r, ang_c, ang_c], axis=-1)
    return jnp.cos(ang), jnp.sin(ang)


def apply_rope(x, cos, sin):
    xf = x.astype(jnp.float32)
    x1, x2, x3, x4 = jnp.split(xf, 4, axis=-1)
    rot = jnp.concatenate([-x2, x1, -x4, x3], axis=-1)
    return (xf * cos[:, None, :] + rot * sin[:, None, :]).astype(x.dtype)


def joint_softmax(scores, sink=None):
    sizes = [s.shape[-1] for s in scores]
    s = jnp.concatenate(scores, axis=-1)
    if sink is not None:
        s = jnp.concatenate([s, jnp.broadcast_to(sink.astype(jnp.float32), s.shape[:-1] + (1,))], axis=-1)
    p = jax.nn.softmax(s, axis=-1)
    return jnp.split(p[..., :sum(sizes)], _split_points(sizes), axis=-1)


def dense_attention(q, k, v, scale, sink=None, q_ctx=None, k_ctx=None, v_ctx=None):
    B, T, KVH, G, d = q.shape
    nb = T // Q_BLOCK
    with_ctx = q_ctx is not None

    def blocks(a):
        return jnp.moveaxis(a.reshape((B, nb, Q_BLOCK) + a.shape[2:]), 1, 0)

    xs = (blocks(q),) + ((blocks(q_ctx),) if with_ctx else ())

    def one_block(qs):
        scores = [jnp.einsum('bqhgd,bshd->bhgqs', qs[0], k, preferred_element_type=jnp.float32) * scale]
        vals = [v]
        if with_ctx:
            scores.append(jnp.einsum('bqhgd,bshd->bhgqs', qs[1], k_ctx, preferred_element_type=jnp.float32) * scale)
            vals.append(v_ctx)
        probs = joint_softmax(scores, sink)
        out = jnp.einsum('bhgqs,bshe->bqhge', probs[0].astype(vals[0].dtype), vals[0])
        for p, vv in zip(probs[1:], vals[1:]):
            out = out + jnp.einsum('bhgqs,bshe->bqhge', p.astype(vv.dtype), vv)
        return out

    out = lax.map(one_block, xs)
    return jnp.moveaxis(out, 0, 1).reshape(B, T, -1)


def window_attention(q, k, v, q_ctx, k_ctx, v_ctx, sink):
    B, T, KVH, G, d = q.shape
    nb = T // Q_BLOCK
    pad = [(0, 0), (Q_BLOCK, Q_BLOCK), (0, 0), (0, 0)]

    def band(a):
        ap = jnp.pad(a, pad).reshape(B, nb + 2, Q_BLOCK, KVH, a.shape[-1])
        return jnp.concatenate([ap[:, :-2], ap[:, 1:-1], ap[:, 2:]], axis=2)

    kb, vb = band(k), band(v)
    qb = q.reshape(B, nb, Q_BLOCK, KVH, G, d)
    qcb = q_ctx.reshape(B, nb, Q_BLOCK, KVH, G, d)
    q_pos = jnp.arange(T).reshape(nb, Q_BLOCK)
    k_pos = (jnp.arange(nb)[:, None] - 1) * Q_BLOCK + jnp.arange(3 * Q_BLOCK)[None, :]
    valid = ((jnp.abs(q_pos[:, :, None] - k_pos[:, None, :]) <= A_WINDOW)
             & (k_pos >= 0)[:, None, :] & (k_pos < T)[:, None, :])
    s_loc = jnp.einsum('bnqhgd,bnshd->bnhgqs', qb, kb, preferred_element_type=jnp.float32) * A_SCALE
    s_loc = jnp.where(valid[None, :, None, None], s_loc, NEG_INF)
    s_ctx = jnp.einsum('bnqhgd,bchd->bnhgqc', qcb, k_ctx, preferred_element_type=jnp.float32) * A_SCALE
    p_loc, p_ctx = joint_softmax([s_loc, s_ctx], sink)
    out = (jnp.einsum('bnhgqs,bnshe->bnqhge', p_loc.astype(vb.dtype), vb)
           + jnp.einsum('bnhgqc,bche->bnqhge', p_ctx.astype(v_ctx.dtype), v_ctx))
    return out.reshape(B, T, -1)


def mla_keys_values(ckv, k_pe, w_ukv):
    B, S, _ = ckv.shape
    kv = jnp.matmul(ckv, w_ukv).reshape(B, S, B_HEADS, B_NOPE + B_VD)
    k_pe_h = jnp.broadcast_to(k_pe[:, :, None, :], (B, S, B_HEADS, B_ROPE)).astype(kv.dtype)
    return jnp.concatenate([kv[..., :B_NOPE], k_pe_h], axis=-1), kv[..., B_NOPE:]


def pool_mixer(x, w_pool, scale):
    B, T, _ = x.shape
    xf = x.astype(jnp.float32).reshape(B, T, C_GROUPS, C_GW)
    cs = jnp.concatenate([jnp.zeros((B, 1, C_GROUPS, C_GW), jnp.float32), jnp.cumsum(xf, axis=1)], axis=1)
    t = jnp.arange(T)
    outs = []
    for gi, w in enumerate(C_WINDOWS):
        lo = jnp.clip(t - w // 2, 0, T)
        hi = jnp.clip(t + w - w // 2, 0, T)
        win_sum = jnp.take(cs[:, :, gi], hi, axis=1) - jnp.take(cs[:, :, gi], lo, axis=1)
        outs.append(win_sum / (hi - lo).astype(jnp.float32)[None, :, None] - xf[:, :, gi])
    y = jnp.stack(outs, axis=2)
    y = jnp.einsum('btgc,gcd->btgd', y, w_pool.astype(jnp.float32))
    return (y.reshape(B, T, -1) * scale.astype(jnp.float32)).astype(x.dtype)


def short_conv(x, w):
    C = x.shape[-1]
    pad = D_CONV // 2
    return lax.conv_general_dilated(x, w[:, None, :].astype(x.dtype), window_strides=(1,),
                                    padding=[(pad, pad)], dimension_numbers=('NWC', 'WIO', 'NWC'),
                                    feature_group_count=C)


def gated_delta_chunked(q, k, v, g, beta, s0):
    B, H, T, DK = q.shape
    DV = v.shape[-1]
    C = D_CHUNK
    N = T // C

    def chunks(a):
        return a.reshape((B, H, N, C) + a.shape[3:])

    q, k, v, g, beta = chunks(q), chunks(k), chunks(v), chunks(g), chunks(beta)
    g = jnp.cumsum(g, axis=-1)
    incl = jnp.tril(jnp.ones((C, C), bool))
    strict = jnp.tril(jnp.ones((C, C), bool), -1)
    decay = jnp.exp(jnp.where(incl, g[..., :, None] - g[..., None, :], -jnp.inf))
    k_beta = k * beta[..., None]
    lmat = jnp.where(strict, jnp.einsum('bhncd,bhnsd->bhncs', k_beta, k) * decay, 0.0)
    eye = jnp.eye(C, dtype=jnp.float32)
    rhs = jnp.concatenate([v * beta[..., None], k_beta * jnp.exp(g)[..., None]], axis=-1)
    sol = lax.linalg.triangular_solve(lmat + eye, rhs, left_side=True, lower=True, unit_diagonal=True)
    u, w = sol[..., :DV], sol[..., DV:]
    a_intra = jnp.einsum('bhncd,bhnsd->bhncs', q, k) * decay

    def step(S, inp):
        q_i, k_i, u_i, w_i, g_i, a_i = inp
        v_new = u_i - jnp.einsum('bhcd,bhde->bhce', w_i, S)
        o_i = (jnp.einsum('bhcd,bhde->bhce', q_i * jnp.exp(g_i)[..., None], S)
               + jnp.einsum('bhcs,bhse->bhce', a_i, v_new))
        g_last = g_i[..., -1:]
        S = (S * jnp.exp(g_last)[..., None]
             + jnp.einsum('bhcd,bhce->bhde', k_i * jnp.exp(g_last - g_i)[..., None], v_new))
        return S, o_i

    xs = tuple(jnp.moveaxis(a, 2, 0) for a in (q, k, u, w, g, a_intra))
    s_final, o = lax.scan(step, s0.astype(jnp.float32), xs)
    return jnp.moveaxis(o, 0, 2).reshape(B, H, T, DV), s_final


def gdn_mixer(qkv, z, beta_raw, alpha_raw, conv_w, a_log, dt_bias, g_norm, s0_f, s0_b):
    B, T, _ = qkv.shape
    u = jax.nn.silu(short_conv(qkv, conv_w)).astype(jnp.float32)
    q, k, v = jnp.split(u, [D_HEADS * D_DK, 2 * D_HEADS * D_DK], axis=-1)

    def heads(a):
        return a.reshape(B, T, D_HEADS, -1).transpose(0, 2, 1, 3)

    q = l2norm(heads(q)) * (D_DK ** -0.5)
    k = l2norm(heads(k))
    v = heads(v)
    beta = jax.nn.sigmoid(beta_raw.astype(jnp.float32)).reshape(B, T, 2, D_HEADS).transpose(2, 0, 3, 1)
    alpha = alpha_raw.astype(jnp.float32).reshape(B, T, 2, D_HEADS).transpose(2, 0, 3, 1)
    g = -jnp.exp(a_log.astype(jnp.float32))[:, None, :, None] * jax.nn.softplus(
        alpha + dt_bias.astype(jnp.float32)[:, None, :, None])
    o_f, s_f = gated_delta_chunked(q, k, v, g[0], beta[0], s0_f)

    def rev(a):
        return jnp.flip(a, axis=2)

    o_b, s_b = gated_delta_chunked(rev(q), rev(k), rev(v), rev(g[1]), rev(beta[1]), s0_b)
    o = (o_f + rev(o_b)).transpose(0, 2, 1, 3)
    o = rmsnorm(o, g_norm) * jax.nn.silu(z.astype(jnp.float32).reshape(B, T, D_HEADS, D_DV))
    return o.reshape(B, T, -1).astype(qkv.dtype), s_f, s_b


def trunk_layer(x, cond, lp, ctx):
    B, T, _ = x.shape
    latent = ctx is not None
    mod = jnp.matmul(jax.nn.silu(cond), lp['w_mod']) + lp['b_mod']
    sh1, sc1, ga1, sh2, sc2, ga2 = jnp.split(mod[:, None, :], 6, axis=-1)

    h = rmsnorm(x, lp['g_pre1']) * (1 + sc1) + sh1
    proj = jnp.matmul(h, lp['w_in'])
    (a_q, a_k, a_v, b_cq, b_ckv, b_kr, c_in, d_qkv, d_z, d_beta, d_alpha,
     gate_raw) = jnp.split(proj, _split_points(IN_SIZES), axis=-1)

    qa = a_q.reshape(B, T, A_HEADS, A_HD)
    ka = a_k.reshape(B, T, A_KV_HEADS, A_HD)
    va = a_v.reshape(B, T, A_KV_HEADS, A_HD)

    def grp(t):
        return t.reshape(B, T, A_KV_HEADS, A_GROUP, A_HD)

    sink = lp['a_sink'].reshape(A_KV_HEADS, A_GROUP, 1, 1)

    cq = rmsnorm(b_cq, lp['b_g_cq'])
    qb = jnp.matmul(cq, lp['b_w_uq']).reshape(B, T, B_HEADS, B_NOPE + B_ROPE)
    ckv = rmsnorm(b_ckv, lp['b_g_ckv'])

    if latent:
        ctx_a_k, ctx_a_v, ctx_ckv, ctx_kr, s0_f, s0_b = ctx
        cos_a, sin_a = axial_rope_tables(T, A_HD)
        cos_b, sin_b = axial_rope_tables(T, B_ROPE)
        out_a = window_attention(grp(apply_rope(qa, cos_a, sin_a)), apply_rope(ka, cos_a, sin_a), va,
                                 grp(qa), ctx_a_k, ctx_a_v, sink)
        q_nope, q_pe = qb[..., :B_NOPE], qb[..., B_NOPE:]
        kr_rot = apply_rope(b_kr[:, :, None, :], cos_b, sin_b)[:, :, 0]
        k_lat, v_lat = mla_keys_values(ckv, kr_rot, lp['b_w_ukv'])
        k_ctx, v_ctx = mla_keys_values(ctx_ckv, ctx_kr, lp['b_w_ukv'])
        q_lat = jnp.concatenate([q_nope, apply_rope(q_pe, cos_b, sin_b)], axis=-1)[:, :, :, None]
        out_b = dense_attention(q_lat, k_lat, v_lat, B_SCALE,
                                q_ctx=qb[:, :, :, None], k_ctx=k_ctx, v_ctx=v_ctx)
    else:
        s0_f = jnp.zeros((B, D_HEADS, D_DK, D_DV), jnp.float32)
        s0_b = s0_f
        out_a = dense_attention(grp(qa), ka, va, A_SCALE, sink=sink)
        k_c, v_c = mla_keys_values(ckv, b_kr, lp['b_w_ukv'])
        out_b = dense_attention(qb[:, :, :, None], k_c, v_c, B_SCALE)

    out_c = pool_mixer(c_in, lp['c_w_pool'], lp['c_scale'])
    out_d, s_f, s_b = gdn_mixer(d_qkv, d_z, d_beta, d_alpha, lp['d_conv'], lp['d_a_log'],
                                lp['d_dt_bias'], lp['d_g_norm'], s0_f, s0_b)

    branches = jnp.stack([out_a.astype(x.dtype), out_b.astype(x.dtype), out_c, out_d], axis=2)
    br = jnp.einsum('btmc,mcd->btmd', branches, lp['w_br'])
    gates = jax.nn.sigmoid(gate_raw.reshape(B, T, N_BRANCH, D_MODEL))
    mix = jnp.matmul(jnp.sum(gates * br, axis=2), lp['w_o'])
    x = x + ga1 * rmsnorm(mix, lp['g_post1'])

    h2 = rmsnorm(x, lp['g_pre2']) * (1 + sc2) + sh2
    gt, up = jnp.split(jnp.matmul(h2, lp['w_up']), 2, axis=-1)
    f = jnp.matmul(jax.nn.silu(gt) * up, lp['w_down'])
    x = x + ga2 * rmsnorm(f, lp['g_post2'])

    new_ctx = None if latent else (ka, va, ckv, b_kr, s_f, s_b)
    return x, new_ctx


def setup_inputs(seed: int = 0) -> dict:
    key = jax.random.key(seed)
    ks = jax.random.split(key, 32)
    D = D_MODEL

    def nrm(i, shape, scale):
        return jax.random.normal(ks[i], shape, jnp.float32) * scale

    a_log = jnp.log(jax.random.uniform(ks[25], (DEPTH, 2, D_HEADS), jnp.float32, 1.0, 16.0))
    dt = jnp.exp(jax.random.uniform(ks[26], (DEPTH, 2, D_HEADS), jnp.float32,
                                    math.log(1e-3), math.log(1e-1)))
    dt_bias = dt + jnp.log(-jnp.expm1(-dt))
    return {
        'x_prompt': nrm(0, (BATCH, SEQ, D), 1.0),
        'x_sample': nrm(1, (DEC_BATCH, DEC_SEQ, D), 1.0),
        'cache_a_k': nrm(2, (DEC_BATCH, DEPTH, PAST_LEN, A_KV_HEADS, A_HD), 1.0),
        'cache_a_v': nrm(3, (DEC_BATCH, DEPTH, PAST_LEN, A_KV_HEADS, A_HD), 1.0),
        'cache_b_ckv': nrm(4, (DEC_BATCH, DEPTH, PAST_LEN, B_KV_RANK), 1.0),
        'cache_b_krope': nrm(5, (DEC_BATCH, DEPTH, PAST_LEN, B_ROPE), 1.0),
        'state_d_fwd': nrm(6, (DEC_BATCH, DEPTH, D_HEADS, D_DK, D_DV), 0.1),
        'state_d_bwd': nrm(7, (DEC_BATCH, DEPTH, D_HEADS, D_DK, D_DV), 0.1),
        'c': nrm(8, (DEC_BATCH, D), 1.0),
        'c_ctx': nrm(9, (D,), 1.0),
        'w_mod': nrm(10, (DEPTH, D, 6 * D), 0.5 * D ** -0.5),
        'b_mod': nrm(11, (DEPTH, 6 * D), 0.01),
        'g_pre1': 1.0 + nrm(12, (DEPTH, D), 0.1),
        'g_post1': 1.0 + nrm(13, (DEPTH, D), 0.1),
        'g_pre2': 1.0 + nrm(14, (DEPTH, D), 0.1),
        'g_post2': 1.0 + nrm(15, (DEPTH, D), 0.1),
        'w_in': nrm(16, (DEPTH, D, P_IN), D ** -0.5),
        'a_sink': nrm(17, (DEPTH, A_HEADS), 0.5),
        'b_g_cq': 1.0 + nrm(18, (DEPTH, B_Q_RANK), 0.1),
        'b_g_ckv': 1.0 + nrm(19, (DEPTH, B_KV_RANK), 0.1),
        'b_w_uq': nrm(20, (DEPTH, B_Q_RANK, B_HEADS * (B_NOPE + B_ROPE)), B_Q_RANK ** -0.5),
        'b_w_ukv': nrm(21, (DEPTH, B_KV_RANK, B_HEADS * (B_NOPE + B_VD)), B_KV_RANK ** -0.5),
        'c_w_pool': nrm(22, (DEPTH, C_GROUPS, C_GW, C_GW), C_GW ** -0.5),
        'c_scale': 1.0 + nrm(23, (DEPTH, C_GROUPS * C_GW), 0.1),
        'd_conv': nrm(24, (DEPTH, D_CONV, D_HEADS * (2 * D_DK + D_DV)), D_CONV ** -0.5),
        'd_a_log': a_log,
        'd_dt_bias': dt_bias,
        'd_g_norm': 1.0 + nrm(27, (DEPTH, D_DV), 0.1),
        'w_br': nrm(28, (DEPTH, N_BRANCH, BRANCH_W, D), BRANCH_W ** -0.5),
        'w_o': nrm(29, (DEPTH, D, D), D ** -0.5),
        'w_up': nrm(30, (DEPTH, D, 2 * D_FF), D ** -0.5),
        'w_down': nrm(31, (DEPTH, D_FF, D), D_FF ** -0.5),
    }


def reference(x_prompt, x_sample, cache_a_k, cache_a_v, cache_b_ckv, cache_b_krope,
              state_d_fwd, state_d_bwd, c, c_ctx, w_mod, b_mod, g_pre1, g_post1, g_pre2, g_post2,
              w_in, a_sink, b_g_cq, b_g_ckv, b_w_uq, b_w_ukv, c_w_pool, c_scale, d_conv,
              d_a_log, d_dt_bias, d_g_norm, w_br, w_o, w_up, w_down):
    y_prompt = x_prompt
    y_sample = x_sample
    cond_ctx = c_ctx[None, :]
    ak_l, av_l, ckv_l, kr_l, sf_l, sb_l = [], [], [], [], [], []
    for l in range(DEPTH):
        lp = {
            'w_mod': w_mod[l], 'b_mod': b_mod[l], 'g_pre1': g_pre1[l], 'g_post1': g_post1[l],
            'g_pre2': g_pre2[l], 'g_post2': g_post2[l], 'w_in': w_in[l], 'a_sink': a_sink[l],
            'b_g_cq': b_g_cq[l], 'b_g_ckv': b_g_ckv[l], 'b_w_uq': b_w_uq[l], 'b_w_ukv': b_w_ukv[l],
            'c_w_pool': c_w_pool[l], 'c_scale': c_scale[l], 'd_conv': d_conv[l],
            'd_a_log': d_a_log[l], 'd_dt_bias': d_dt_bias[l], 'd_g_norm': d_g_norm[l],
            'w_br': w_br[l], 'w_o': w_o[l], 'w_up': w_up[l], 'w_down': w_down[l],
        }
        y_prompt, (ak, av, ckv, kr, sf, sb) = trunk_layer(y_prompt, cond_ctx, lp, None)
        ak_l.append(ak)
        av_l.append(av)
        ckv_l.append(ckv)
        kr_l.append(kr)
        sf_l.append(sf)
        sb_l.append(sb)
        ctx = (cache_a_k[:, l], cache_a_v[:, l], cache_b_ckv[:, l], cache_b_krope[:, l],
               state_d_fwd[:, l], state_d_bwd[:, l])
        y_sample, _ = trunk_layer(y_sample, c, lp, ctx)
    new_cache_a_k = jnp.stack(ak_l, axis=1)
    new_cache_a_v = jnp.stack(av_l, axis=1)
    new_cache_b_ckv = jnp.stack(ckv_l, axis=1)
    new_cache_b_krope = jnp.stack(kr_l, axis=1)
    new_state_d_fwd = jnp.stack(sf_l, axis=1)
    new_state_d_bwd = jnp.stack(sb_l, axis=1)
    return (y_prompt, y_sample, new_cache_a_k, new_cache_a_v, new_cache_b_ckv, new_cache_b_krope, new_state_d_fwd, new_state_d_bwd)
```

```python
import functools
import math

import jax
import jax.numpy as jnp
from jax import lax
from jax.experimental import pallas as pl
from jax.experimental.pallas import tpu as pltpu

F32 = jnp.float32
BF16 = jnp.bfloat16

D_MODEL = 1024
GRID_W = 64
ROPE_BASE = 10000.0
EPS = 1e-6
NEG_INF = -1e30
Q_BLOCK = 128
N_BRANCH = 4
BRANCH_W = 256
A_HEADS = 4
A_KV_HEADS = 2
A_HD = 64
A_WINDOW = 128
A_SCALE = A_HD ** -0.5
B_HEADS = 4
B_Q_RANK = 192
B_KV_RANK = 128
B_NOPE = 64
B_ROPE = 32
B_VD = 64
B_SCALE = (B_NOPE + B_ROPE) ** -0.5
C_GROUPS = 4
C_GW = 64
C_WINDOWS = (2, 4, 8, 16)
D_HEADS = 4
D_DK = 64
D_DV = 64
D_CONV = 5
D_CHUNK = 64
FF_RAW = -(-8 * D_MODEL // 3)
D_FF = -(-FF_RAW // 256) * 256
IN_SIZES = (A_HEADS * A_HD, A_KV_HEADS * A_HD, A_KV_HEADS * A_HD, B_Q_RANK, B_KV_RANK, B_ROPE,
            C_GROUPS * C_GW, D_HEADS * (2 * D_DK + D_DV), D_HEADS * D_DV, 2 * D_HEADS, 2 * D_HEADS,
            N_BRANCH * D_MODEL)

LANES = 128
SUBLANES = 8
VMEM_LIMIT_BYTES = 56 * 1024 * 1024
HALO = SUBLANES

C_AQ = 0
C_AK = 256
C_AV = 512
C_CQ = 768
C_CKV = 1024
C_KR = 1152
C_CIN = 1280
C_QKV = 1536
C_Z = 2304
C_AB = 2560
P1 = 2688


def _cparams(sem):
    return pltpu.CompilerParams(dimension_semantics=sem, vmem_limit_bytes=VMEM_LIMIT_BYTES)


def _dot(a, b):
    return jnp.dot(a, b, preferred_element_type=F32)


def _dot_f32(a, b):
    return jnp.dot(a, b, preferred_element_type=F32, precision=lax.Precision.HIGHEST)


def _dot_nt(a, b):
    return lax.dot_general(a, b, (((1,), (1,)), ((), ())), preferred_element_type=F32)


def _dot_tn(a, b):
    return lax.dot_general(a, b, (((0,), (0,)), ((), ())), preferred_element_type=F32)


def _split3(x):
    hi = x.astype(BF16)
    r1 = x - hi.astype(F32)
    mid = r1.astype(BF16)
    lo = (r1 - mid.astype(F32)).astype(BF16)
    return hi, mid, lo


def _dot_sel_left(sel, x):
    hi, mid, lo = _split3(x)
    return _dot(sel, hi) + _dot(sel, mid) + _dot(sel, lo)


def _dot_sel_right(x, sel):
    hi, mid, lo = _split3(x)
    return _dot(hi, sel) + _dot(mid, sel) + _dot(lo, sel)


def _silu(x):
    return x * jax.nn.sigmoid(x)


def _softplus(x):
    return jnp.maximum(x, 0.0) + jnp.log1p(jnp.exp(-jnp.abs(x)))


def _head_mask(width, n_lanes, h):
    lane = lax.broadcasted_iota(jnp.int32, (1, n_lanes), 1)
    return (lane // width) == h


def _group_ones(n, width):
    r = lax.broadcasted_iota(jnp.int32, (n, n), 0) // width
    c = lax.broadcasted_iota(jnp.int32, (n, n), 1) // width
    return jnp.where(r == c, 1.0, 0.0).astype(BF16)


def _mod_kernel(c_ref, w_ref, b_ref, o_ref):
    s = _silu(c_ref[...])
    o_ref[0] = _dot(s.astype(BF16), w_ref[0].astype(BF16)) + b_ref[0]


def _mod_call(cond8, w_mod, b_mod):
    depth, d, n = w_mod.shape
    tn = 1536
    return pl.pallas_call(
        _mod_kernel,
        grid=(depth, n // tn),
        in_specs=[pl.BlockSpec((SUBLANES, d), lambda l, j: (0, 0)),
                  pl.BlockSpec((1, d, tn), lambda l, j: (l, 0, j)),
                  pl.BlockSpec((1, 1, tn), lambda l, j: (l, 0, j))],
        out_specs=pl.BlockSpec((1, SUBLANES, tn), lambda l, j: (l, 0, j)),
        out_shape=jax.ShapeDtypeStruct((depth, SUBLANES, n), F32),
        compiler_params=_cparams(("parallel", "parallel")),
        name="mod",
    )(cond8, w_mod, b_mod.reshape(depth, 1, n))


def _rmsnorm_rows(x, g, n):
    ms = jnp.sum(x * x, axis=-1, keepdims=True) * (1.0 / n)
    return x * lax.rsqrt(ms + EPS) * g


def _rope(x, cos, sin_next, sin_prev, quarter):
    n = x.shape[-1]
    return x * cos + pltpu.roll(x, n - quarter, 1) * sin_next + pltpu.roll(x, quarter, 1) * sin_prev


def _inproj_kernel(latent, *refs):
    if latent:
        (x_ref, mod_ref, gpre_ref, w1_ref, gcq_ref, wuq_ref, gckv_ref, wukv_ref,
         ca_ref, sna_ref, spa_ref, cb_ref, snb_ref, spb_ref,
         proj_ref, qb_ref, ckv_ref, knv_ref, aqr_ref, akr_ref, qper_ref, krr_ref) = refs
    else:
        (x_ref, mod_ref, gpre_ref, w1_ref, gcq_ref, wuq_ref, gckv_ref, wukv_ref,
         proj_ref, qb_ref, ckv_ref, knv_ref) = refs
    d = D_MODEL
    x = x_ref[...]
    mod = mod_ref[0]
    sh1 = mod[:, 0:d]
    sc1 = mod[:, d:2 * d]
    h = _rmsnorm_rows(x, gpre_ref[...], d) * (1.0 + sc1) + sh1
    proj = _dot(h.astype(BF16), w1_ref[...])
    proj_ref[...] = proj
    cqn = _rmsnorm_rows(proj[:, C_CQ:C_CQ + 256], gcq_ref[...], B_Q_RANK)
    qb = _dot(cqn.astype(BF16), wuq_ref[...])
    qb_ref[...] = qb
    ckvn = _rmsnorm_rows(proj[:, C_CKV:C_CKV + B_KV_RANK], gckv_ref[...], B_KV_RANK)
    ckv_ref[...] = ckvn
    knv_ref[...] = _dot(ckvn.astype(BF16), wukv_ref[...]).astype(knv_ref.dtype)
    if latent:
        ca, sna, spa = ca_ref[...], sna_ref[...], spa_ref[...]
        cb, snb, spb = cb_ref[...], snb_ref[...], spb_ref[...]
        aqr_ref[...] = _rope(proj[:, C_AQ:C_AQ + 256], ca, sna, spa, A_HD // 4).astype(BF16)
        akr_ref[...] = _rope(proj[:, C_AK:C_AK + 256], ca, sna, spa, A_HD // 4).astype(BF16)
        qper_ref[...] = _rope(qb[:, 256:384], cb, snb, spb, B_ROPE // 4).astype(BF16)
        krr_ref[...] = _rope(proj[:, C_KR:C_KR + 128], cb, snb, spb, B_ROPE // 4).astype(BF16)


def _inproj_call(x2d, mod3, gpre, w1, gcq, wuq, gckv, wukv, tables, seq_len, tm):
    n, d = x2d.shape
    latent = tables is not None
    tiles_per_seq = seq_len // tm if latent else 1
    n_mod = mod3.shape[0]
    if n_mod == 1:
        mod_map = lambda i: (0, 0, 0)
    else:
        mod_map = lambda i: (i // tiles_per_seq, 0, 0)
    full = lambda shape: pl.BlockSpec(shape, lambda i: tuple(0 for _ in shape))
    row = lambda w, dt=F32: (pl.BlockSpec((tm, w), lambda i: (i, 0)), jax.ShapeDtypeStruct((n, w), dt))
    in_specs = [pl.BlockSpec((tm, d), lambda i: (i, 0)),
                pl.BlockSpec((1, 1, 6 * d), mod_map),
                full((1, d)), full((d, P1)), full((1, 256)), full((256, 384)),
                full((1, B_KV_RANK)), full((B_KV_RANK, 512))]
    args = [x2d, mod3, gpre, w1, gcq, wuq, gckv, wukv]
    outs = [row(P1), row(384), row(B_KV_RANK), row(512, BF16 if latent else F32)]
    if latent:
        for t in tables:
            in_specs.append(pl.BlockSpec((tm, t.shape[1]), lambda i: (i % tiles_per_seq, 0)))
            args.append(t)
        outs += [row(256, BF16), row(256, BF16), row(128, BF16), row(128, BF16)]
    return pl.pallas_call(
        functools.partial(_inproj_kernel, latent),
        grid=(n // tm,),
        in_specs=in_specs,
        out_specs=[o[0] for o in outs],
        out_shape=[o[1] for o in outs],
        compiler_params=_cparams(("parallel",)),
        name="inproj_lat" if latent else "inproj_ctx",
    )(*args)


def _softmax_pv(scores, values, sink, hm):
    m = jnp.max(scores[0], axis=-1, keepdims=True)
    for s in scores[1:]:
        m = jnp.maximum(m, jnp.max(s, axis=-1, keepdims=True))
    if sink is not None:
        m = jnp.maximum(m, sink)
    den = None
    pv = None
    for s, v in zip(scores, values):
        p = jnp.exp(s - m)
        ps = jnp.sum(p, axis=-1, keepdims=True)
        den = ps if den is None else den + ps
        t = _dot(p.astype(BF16), v)
        pv = t if pv is None else pv + t
    if sink is not None:
        den = den + jnp.exp(sink - m)
    return jnp.where(hm, pv / den, 0.0)


def _attn_ctx_kernel(sink_ref, q_ref, k_ref, v_ref, qb_ref, knv_ref, kr_ref, oa_ref, ob_ref):
    q = q_ref[...]
    k = k_ref[...].astype(BF16)
    v = v_ref[...].astype(BF16)
    acc = jnp.zeros(q.shape, F32)
    for h in range(A_HEADS):
        hm = _head_mask(A_HD, 256, h)
        s = _dot_nt(jnp.where(hm, q, 0.0).astype(BF16), k) * A_SCALE
        acc = acc + _softmax_pv([s], [v], sink_ref[h], hm)
    oa_ref[...] = acc
    qb = qb_ref[...]
    qn, qpe = qb[:, :256], qb[:, 256:]
    knv = knv_ref[...]
    kn = knv[:, :256].astype(BF16)
    vb = knv[:, 256:].astype(BF16)
    kr = kr_ref[...].astype(BF16)
    acc = jnp.zeros(qn.shape, F32)
    for h in range(B_HEADS):
        hm = _head_mask(B_NOPE, 256, h)
        hr = _head_mask(B_ROPE, 128, h)
        s = (_dot_nt(jnp.where(hm, qn, 0.0).astype(BF16), kn)
             + _dot_nt(jnp.where(hr, qpe, 0.0).astype(BF16), kr)) * B_SCALE
        acc = acc + _softmax_pv([s], [vb], None, hm)
    ob_ref[...] = acc


def _attn_ctx_call(sink, proj, qb, knv, n_seq, t):
    n = proj.shape[0]
    col = lambda w, j: pl.BlockSpec((t, w), lambda b: (b, j))
    out = (pl.BlockSpec((t, 256), lambda b: (b, 0)), jax.ShapeDtypeStruct((n, 256), F32))
    return pl.pallas_call(
        _attn_ctx_kernel,
        grid=(n_seq,),
        in_specs=[pl.BlockSpec(memory_space=pltpu.SMEM),
                  col(256, C_AQ // 256), col(256, C_AK // 256), col(256, C_AV // 256),
                  col(384, 0), col(512, 0), col(128, C_KR // 128)],
        out_specs=[out[0], out[0]],
        out_shape=[out[1], out[1]],
        compiler_params=_cparams(("parallel",)),
        name="attn_ctx",
    )(sink, proj, proj, proj, qb, knv, proj)


def _win_attn_kernel(t, sink_ref, qr_ref, q_ref, kp_ref, kc_ref, kn_ref, vp_ref, vc_ref, vn_ref,
                     kctx_ref, vctx_ref, o_ref):
    i = pl.program_id(1)
    qr = qr_ref[...]
    q = q_ref[...]
    kl = jnp.concatenate([kp_ref[...], kc_ref[...], kn_ref[...]], axis=0)
    vl = jnp.concatenate([vp_ref[...], vc_ref[...], vn_ref[...]], axis=0).astype(BF16)
    kctx = kctx_ref[0].astype(BF16)
    vctx = vctx_ref[0].astype(BF16)
    q_pos = i * Q_BLOCK + lax.broadcasted_iota(jnp.int32, (Q_BLOCK, 3 * Q_BLOCK), 0)
    k_pos = (i - 1) * Q_BLOCK + lax.broadcasted_iota(jnp.int32, (Q_BLOCK, 3 * Q_BLOCK), 1)
    valid = (jnp.abs(q_pos - k_pos) <= A_WINDOW) & (k_pos >= 0) & (k_pos < t)
    acc = jnp.zeros(q.shape, F32)
    for h in range(A_HEADS):
        hm = _head_mask(A_HD, 256, h)
        s_loc = _dot_nt(jnp.where(hm, qr, 0.0).astype(BF16), kl) * A_SCALE
        s_loc = jnp.where(valid, s_loc, NEG_INF)
        s_ctx = _dot_nt(jnp.where(hm, q, 0.0).astype(BF16), kctx) * A_SCALE
        acc = acc + _softmax_pv([s_loc, s_ctx], [vl, vctx], sink_ref[h], hm)
    o_ref[...] = acc


def _win_attn_call(sink, aqr, akr, proj, kctx, vctx, n_seq, t):
    n = proj.shape[0]
    nb = t // Q_BLOCK
    blk = lambda j, f: pl.BlockSpec((Q_BLOCK, 256), lambda b, i: (b * nb + f(i), j))
    prev = lambda i: jnp.maximum(i - 1, 0)
    cur = lambda i: i
    nxt = lambda i: jnp.minimum(i + 1, nb - 1)
    ctx = pl.BlockSpec((1,) + kctx.shape[1:], lambda b, i: (b, 0, 0))
    return pl.pallas_call(
        functools.partial(_win_attn_kernel, t),
        grid=(n_seq, nb),
        in_specs=[pl.BlockSpec(memory_space=pltpu.SMEM),
                  blk(0, cur), blk(C_AQ // 256, cur),
                  blk(0, prev), blk(0, cur), blk(0, nxt),
                  blk(C_AV // 256, prev), blk(C_AV // 256, cur), blk(C_AV // 256, nxt),
                  ctx, ctx],
        out_specs=pl.BlockSpec((Q_BLOCK, 256), lambda b, i: (b * nb + i, 0)),
        out_shape=jax.ShapeDtypeStruct((n, 256), F32),
        compiler_params=_cparams(("parallel", "parallel")),
        name="win_attn",
    )(sink, aqr, proj, akr, akr, akr, proj, proj, proj, kctx, vctx)


def _mla_lat_kernel(qb_ref, qper_ref, knv_ref, krr_ref, cckv_ref, ckr_ref, wukv_ref, o_ref):
    qb = qb_ref[...]
    qn, qpe = qb[:, :256], qb[:, 256:]
    qper = qper_ref[...]
    kn = knv_ref[:, :256]
    v = knv_ref[:, 256:]
    krr = krr_ref[...]
    kvc = _dot(cckv_ref[0].astype(BF16), wukv_ref[...])
    knc = kvc[:, :256].astype(BF16)
    vc = kvc[:, 256:].astype(BF16)
    krc = ckr_ref[0].astype(BF16)
    acc = jnp.zeros(qn.shape, F32)
    for h in range(B_HEADS):
        hm = _head_mask(B_NOPE, 256, h)
        hr = _head_mask(B_ROPE, 128, h)
        qnh = jnp.where(hm, qn, 0.0).astype(BF16)
        s_lat = (_dot_nt(qnh, kn) + _dot_nt(jnp.where(hr, qper, 0.0), krr)) * B_SCALE
        s_ctx = (_dot_nt(qnh, knc) + _dot_nt(jnp.where(hr, qpe, 0.0).astype(BF16), krc)) * B_SCALE
        acc = acc + _softmax_pv([s_lat, s_ctx], [v, vc], None, hm)
    o_ref[...] = acc


def _mla_lat_call(qb, qper, knv, krr, cckv, ckr_rep, wukv, n_seq, t, tq):
    n = qb.shape[0]
    nq = t // tq
    qblk = lambda w: pl.BlockSpec((tq, w), lambda b, i: (b * nq + i, 0))
    seq = lambda w: pl.BlockSpec((t, w), lambda b, i: (b, 0))
    ctx = lambda a: pl.BlockSpec((1,) + a.shape[1:], lambda b, i: (b, 0, 0))
    return pl.pallas_call(
        _mla_lat_kernel,
        grid=(n_seq, nq),
        in_specs=[qblk(384), qblk(128), seq(512), seq(128), ctx(cckv), ctx(ckr_rep),
                  pl.BlockSpec(wukv.shape, lambda b, i: (0, 0))],
        out_specs=pl.BlockSpec((tq, 256), lambda b, i: (b * nq + i, 0)),
        out_shape=jax.ShapeDtypeStruct((n, 256), F32),
        compiler_params=_cparams(("parallel", "parallel")),
        name="mla_lat",
    )(qb, qper, knv, krr, cckv, ckr_rep, wukv)


def _with_halo(prev_ref, cur_ref, next_ref, has_prev, has_next):
    prev = jnp.where(has_prev, prev_ref[...], 0.0)
    nxt = jnp.where(has_next, next_ref[...], 0.0)
    return jnp.concatenate([prev, cur_ref[...], nxt], axis=0)


def _shift_rows(x, k):
    n = x.shape[0]
    return pltpu.roll(x, (-k) % n, 0)


def _local_kernel(t, tm, cp_ref, cc_ref, cn_ref, qp_ref, qc_ref, qn_ref, convw_ref, wpool_ref, cscale_ref,
                  oc_ref, qkv_ref):
    i = pl.program_id(0)
    tiles_per_seq = t // tm
    j = i % tiles_per_seq
    has_prev = j > 0
    has_next = j < tiles_per_seq - 1
    x = _with_halo(cp_ref, cc_ref, cn_ref, has_prev, has_next)
    p2 = x + _shift_rows(x, -1)
    p4 = _shift_rows(p2, 1) + _shift_rows(p2, -1)
    p8 = _shift_rows(p4, 2) + _shift_rows(p4, -2)
    p16 = _shift_rows(p8, 4) + _shift_rows(p8, -4)
    grp = lax.broadcasted_iota(jnp.int32, (1, 256), 1) // C_GW
    win = jnp.where(grp == 0, p2, jnp.where(grp == 1, p4, jnp.where(grp == 2, p8, p16)))[HALO:HALO + tm]
    pos = j * tm + lax.broadcasted_iota(jnp.int32, (tm, 256), 0)
    half = jnp.where(grp == 0, 1, jnp.where(grp == 1, 2, jnp.where(grp == 2, 4, 8)))
    cnt = (jnp.minimum(pos + half, t) - jnp.maximum(pos - half, 0)).astype(F32)
    y = win / cnt - cc_ref[...]
    oc_ref[...] = _dot(y.astype(BF16), wpool_ref[...]) * cscale_ref[...]
    xq = _with_halo(qp_ref, qc_ref, qn_ref, has_prev, has_next)
    w = convw_ref[...]
    pad = D_CONV // 2
    acc = None
    for tap in range(D_CONV):
        term = _shift_rows(xq, tap - pad) * w[tap:tap + 1, :]
        acc = term if acc is None else acc + term
    u = _silu(acc[HALO:HALO + tm])
    ones = _group_ones(256, D_DK)
    q = u[:, 0:256]
    k = u[:, 256:512]
    q = q * lax.rsqrt(_dot_sel_right(q * q, ones) + EPS) * (D_DK ** -0.5)
    k = k * lax.rsqrt(_dot_sel_right(k * k, ones) + EPS)
    qkv_ref[:, 0:256] = q
    qkv_ref[:, 256:512] = k
    qkv_ref[:, 512:768] = u[:, 512:768]


def _local_call(proj, convw8, wpool_bd, cscale, t, tm):
    n = proj.shape[0]
    hb = tm // HALO
    nblk = n // HALO
    cur = lambda w, c: pl.BlockSpec((tm, w), lambda i: (i, c))
    prev = lambda w, c: pl.BlockSpec((HALO, w), lambda i: (jnp.maximum(i * hb - 1, 0), c))
    nxt = lambda w, c: pl.BlockSpec((HALO, w), lambda i: (jnp.minimum((i + 1) * hb, nblk - 1), c))
    full = lambda a: pl.BlockSpec(a.shape, lambda i: (0, 0))
    cc, qc = C_CIN // 256, C_QKV // 768
    return pl.pallas_call(
        functools.partial(_local_kernel, t, tm),
        grid=(n // tm,),
        in_specs=[prev(256, cc), cur(256, cc), nxt(256, cc), prev(768, qc), cur(768, qc), nxt(768, qc),
                  full(convw8), full(wpool_bd), full(cscale)],
        out_specs=[pl.BlockSpec((tm, 256), lambda i: (i, 0)), pl.BlockSpec((tm, 768), lambda i: (i, 0))],
        out_shape=[jax.ShapeDtypeStruct((n, 256), F32), jax.ShapeDtypeStruct((n, 768), F32)],
        compiler_params=_cparams(("parallel",)),
        name="local",
    )(proj, proj, proj, proj, proj, proj, convw8, wpool_bd, cscale)


def _gdn_kernel(nc, xf_ref, xb_ref, abf_ref, abb_ref, abrf_ref, abrb_ref, prow_ref, pcol_ref, s0_ref,
                of_ref, ob_ref, sfin_ref, s_ref):
    i = pl.program_id(1)
    c = D_CHUNK

    @pl.when(i == 0)
    def _():
        s_ref[...] = s0_ref[0]

    r = lax.broadcasted_iota(jnp.int32, (c, c), 0)
    cidx = lax.broadcasted_iota(jnp.int32, (c, c), 1)
    lower = jnp.where(cidx <= r, 1.0, 0.0).astype(BF16)
    upper = jnp.where(cidx >= r, 1.0, 0.0).astype(BF16)
    prow = prow_ref[...]
    pcol = pcol_ref[...]

    for d, (x_ref, ab_ref, abr_ref, o_ref) in enumerate(((xf_ref, abf_ref, abrf_ref, of_ref),
                                                         (xb_ref, abb_ref, abrb_ref, ob_ref))):
        x = x_ref[...]
        ab = ab_ref[...]
        abr = abr_ref[0, 0]
        beta_c = jax.nn.sigmoid(ab)
        g_c = -jnp.exp(prow[0:1, :]) * _softplus(ab + prow[1:2, :])
        g_r = -jnp.exp(pcol[:, 0:1]) * _softplus(abr + pcol[:, 1:2])
        if d == 0:
            gc_c = _dot_sel_left(lower, g_c)
            gc_r = _dot_sel_right(g_r, upper)
            incl = cidx <= r
            strict = cidx < r
            last = c - 1
        else:
            gc_c = _dot_sel_left(upper, g_c)
            gc_r = _dot_sel_right(g_r, lower)
            incl = cidx >= r
            strict = cidx > r
            last = 0
        for h in range(D_HEADS):
            hs = slice(h * D_DK, (h + 1) * D_DK)
            qh = x[:, 0:256][:, hs]
            kh = x[:, 256:512][:, hs]
            vh = x[:, 512:768][:, hs]
            lane = 4 * d + h
            bcol = beta_c[:, lane:lane + 1]
            gcol = gc_c[:, 8 + lane:9 + lane]
            grow = gc_r[8 + lane:9 + lane, :]
            glast = gcol[last:last + 1, :]
            decay = jnp.exp(jnp.where(incl, gcol - grow, NEG_INF))
            kb = kh * bcol
            kk = _dot_nt(kb.astype(BF16), kh.astype(BF16))
            p = -jnp.where(strict, kk * decay, 0.0)
            egc = jnp.exp(gcol)
            xs = jnp.concatenate([vh * bcol, kb * egc], axis=1)
            for step in range(6):
                if step < 5:
                    m = _dot_f32(p, jnp.concatenate([xs, p], axis=1))
                    xs = xs + m[:, 0:2 * D_DK]
                    p = m[:, 2 * D_DK:3 * D_DK]
                else:
                    xs = xs + _dot_f32(p, xs)
            u = xs[:, 0:D_DV]
            w = xs[:, D_DV:2 * D_DV]
            a = _dot_nt(qh.astype(BF16), kh.astype(BF16)) * decay
            s = s_ref[d, h]
            sb = s.astype(BF16)
            v_new = u - _dot(w.astype(BF16), sb)
            vb = v_new.astype(BF16)
            o = _dot((qh * egc).astype(BF16), sb) + _dot(a.astype(BF16), vb)
            kd = kh * jnp.exp(glast - gcol)
            s_new = s * jnp.exp(glast) + _dot_tn(kd.astype(BF16), vb)
            s_ref[d, h] = s_new
            o_ref[:, hs] = o

    @pl.when(i == nc - 1)
    def _():
        sfin_ref[0] = s_ref[...]


def _gdn_call(qkvn, proj, ab_row, prow, pcol, s0, n_seq, t):
    n = qkvn.shape[0]
    nc = t // D_CHUNK
    fwd = lambda b, i: b * nc + i
    bwd = lambda b, i: b * nc + nc - 1 - i
    xblk = lambda f: pl.BlockSpec((D_CHUNK, 768), lambda b, i: (f(b, i), 0))
    abblk = lambda f: pl.BlockSpec((D_CHUNK, 128), lambda b, i: (f(b, i), C_AB // 128))
    abrblk = lambda f: pl.BlockSpec((1, 1, 16, D_CHUNK), lambda b, i: (b, f(0, i), 0, 0))
    oblk = lambda f: pl.BlockSpec((D_CHUNK, 256), lambda b, i: (f(b, i), 0))
    sblk = pl.BlockSpec((1, 2, D_HEADS, D_DK, D_DV), lambda b, i: (b, 0, 0, 0, 0))
    osd = jax.ShapeDtypeStruct((n, 256), F32)
    return pl.pallas_call(
        functools.partial(_gdn_kernel, nc),
        grid=(n_seq, nc),
        in_specs=[xblk(fwd), xblk(bwd), abblk(fwd), abblk(bwd), abrblk(fwd), abrblk(bwd),
                  pl.BlockSpec(prow.shape, lambda b, i: (0, 0)), pl.BlockSpec(pcol.shape, lambda b, i: (0, 0)),
                  sblk],
        out_specs=[oblk(fwd), oblk(bwd), sblk],
        out_shape=[osd, osd, jax.ShapeDtypeStruct(s0.shape, F32)],
        scratch_shapes=[pltpu.VMEM((2, D_HEADS, D_DK, D_DV), F32)],
        compiler_params=_cparams(("parallel", "arbitrary")),
        name="gdn",
    )(qkvn, qkvn, proj, proj, ab_row, ab_row, prow, pcol, s0)


def _merge_kernel(x_ref, mod_ref, gpre_ref, oa_ref, ob_ref, oc_ref, of_ref, obw_ref, z_ref, gnorm_ref,
                  wg_ref, wbr_ref, wo_ref, gpost_ref, o_ref):
    d = D_MODEL
    x = x_ref[...]
    mod = mod_ref[0]
    sh1, sc1, ga1 = mod[:, 0:d], mod[:, d:2 * d], mod[:, 2 * d:3 * d]
    h = (_rmsnorm_rows(x, gpre_ref[...], d) * (1.0 + sc1) + sh1).astype(BF16)
    o = of_ref[...] + obw_ref[...]
    ms = _dot_sel_right(o * o, _group_ones(256, D_DV)) * (1.0 / D_DV)
    od = o * lax.rsqrt(ms + EPS) * gnorm_ref[...] * _silu(z_ref[...])
    acc = None
    for m, br in enumerate((oa_ref[...], ob_ref[...], oc_ref[...], od)):
        gate = jax.nn.sigmoid(_dot(h, wg_ref[:, m * d:(m + 1) * d]))
        term = gate * _dot(br.astype(BF16), wbr_ref[m])
        acc = term if acc is None else acc + term
    mix = _dot(acc.astype(BF16), wo_ref[...])
    o_ref[...] = x + ga1 * _rmsnorm_rows(mix, gpost_ref[...], d)


def _merge_call(x2d, mod3, gpre, oa, ob, oc, of, obw, proj, gnorm, wg, wbr, wo, gpost, seq_len, tm):
    n, d = x2d.shape
    tiles_per_seq = seq_len // tm
    mod_map = (lambda i: (0, 0, 0)) if mod3.shape[0] == 1 else (lambda i: (i // tiles_per_seq, 0, 0))
    row = lambda w, c=0: pl.BlockSpec((tm, w), lambda i: (i, c))
    full = lambda a: pl.BlockSpec(a.shape, lambda i: tuple(0 for _ in a.shape))
    return pl.pallas_call(
        _merge_kernel,
        grid=(n // tm,),
        in_specs=[row(d), pl.BlockSpec((1, 1, 6 * d), mod_map), full(gpre),
                  row(256), row(256), row(256), row(256), row(256), row(256, C_Z // 256), full(gnorm),
                  full(wg), full(wbr), full(wo), full(gpost)],
        out_specs=row(d),
        out_shape=jax.ShapeDtypeStruct((n, d), F32),
        compiler_params=_cparams(("parallel",)),
        name="merge",
    )(x2d, mod3, gpre, oa, ob, oc, of, obw, proj, gnorm, wg, wbr, wo, gpost)


def _ffn_kernel(x_ref, mod_ref, gpre_ref, wup_ref, wdown_ref, gpost_ref, o_ref):
    d = D_MODEL
    x = x_ref[...]
    mod = mod_ref[0]
    sh2, sc2, ga2 = mod[:, 3 * d:4 * d], mod[:, 4 * d:5 * d], mod[:, 5 * d:6 * d]
    h = (_rmsnorm_rows(x, gpre_ref[...], d) * (1.0 + sc2) + sh2).astype(BF16)
    up = _dot(h, wup_ref[...])
    act = _silu(up[:, 0:D_FF]) * up[:, D_FF:2 * D_FF]
    f = _dot(act.astype(BF16), wdown_ref[...])
    o_ref[...] = x + ga2 * _rmsnorm_rows(f, gpost_ref[...], d)


def _ffn_call(x2d, mod3, gpre, wup, wdown, gpost, seq_len, tm):
    n, d = x2d.shape
    tiles_per_seq = seq_len // tm
    mod_map = (lambda i: (0, 0, 0)) if mod3.shape[0] == 1 else (lambda i: (i // tiles_per_seq, 0, 0))
    row = lambda w: pl.BlockSpec((tm, w), lambda i: (i, 0))
    full = lambda a: pl.BlockSpec(a.shape, lambda i: tuple(0 for _ in a.shape))
    return pl.pallas_call(
        _ffn_kernel,
        grid=(n // tm,),
        in_specs=[row(d), pl.BlockSpec((1, 1, 6 * d), mod_map), full(gpre), full(wup), full(wdown), full(gpost)],
        out_specs=row(d),
        out_shape=jax.ShapeDtypeStruct((n, d), F32),
        compiler_params=_cparams(("parallel",)),
        name="ffn",
    )(x2d, mod3, gpre, wup, wdown, gpost)


def _rope_tables(n_tokens, dim, reps):
    n_rows = n_tokens // GRID_W
    row = jnp.repeat(jnp.arange(n_rows), GRID_W).astype(F32)
    col = jnp.tile(jnp.arange(GRID_W), n_rows).astype(F32)
    nfreq = dim // 4
    inv = ROPE_BASE ** (-jnp.arange(nfreq, dtype=F32) / nfreq)
    ang_r = row[:, None] * inv
    ang_c = col[:, None] * inv
    ang = jnp.concatenate([ang_r, ang_r, ang_c, ang_c], axis=-1)
    cos, sin = jnp.cos(ang), jnp.sin(ang)
    even = ((jnp.arange(dim) // nfreq) % 2 == 0)[None, :]
    sin_next = jnp.where(even, -sin, 0.0)
    sin_prev = jnp.where(even, 0.0, sin)
    return tuple(jnp.tile(a, (1, reps)) for a in (cos, sin_next, sin_prev))


def _split_cols(w):
    out, acc = [], 0
    for s in IN_SIZES:
        out.append(w[:, acc:acc + s])
        acc += s
    return out


def _pack_w_in(w):
    (a_q, a_k, a_v, b_cq, b_ckv, b_kr, c_in, d_qkv, d_z, d_beta, d_alpha, gate) = _split_cols(w)
    d = w.shape[0]
    z = lambda n: jnp.zeros((d, n), w.dtype)
    k0, k1 = a_k[:, :A_HD], a_k[:, A_HD:]
    v0, v1 = a_v[:, :A_HD], a_v[:, A_HD:]
    w1 = jnp.concatenate([a_q, k0, k0, k1, k1, v0, v0, v1, v1, b_cq, z(256 - B_Q_RANK), b_ckv,
                          b_kr, b_kr, b_kr, b_kr, c_in, d_qkv, d_z, d_beta, d_alpha, z(112)], axis=1)
    return w1.astype(BF16), gate.astype(BF16)


def _expand_kv_heads(a):
    return jnp.repeat(a, A_HEADS // A_KV_HEADS, axis=2).reshape(a.shape[0], a.shape[1], A_HEADS * A_HD)


def _layer_weights(l, g_pre1, g_post1, g_pre2, g_post2, w_in, a_sink, b_g_cq, b_g_ckv, b_w_uq, b_w_ukv,
                   c_w_pool, c_scale, d_conv, d_a_log, d_dt_bias, d_g_norm, w_br, w_o, w_up, w_down):
    w1, wg = _pack_w_in(w_in[l])
    wuq = b_w_uq[l].reshape(B_Q_RANK, B_HEADS, B_NOPE + B_ROPE)
    wuq = jnp.concatenate([wuq[:, :, :B_NOPE].reshape(B_Q_RANK, -1), wuq[:, :, B_NOPE:].reshape(B_Q_RANK, -1)], 1)
    wuq = jnp.pad(wuq, ((0, 256 - B_Q_RANK), (0, 0))).astype(BF16)
    wukv = b_w_ukv[l].reshape(B_KV_RANK, B_HEADS, B_NOPE + B_VD)
    wukv = jnp.concatenate([wukv[:, :, :B_NOPE].reshape(B_KV_RANK, -1),
                            wukv[:, :, B_NOPE:].reshape(B_KV_RANK, -1)], 1).astype(BF16)
    wpool = jnp.zeros((256, 256), F32)
    for g in range(C_GROUPS):
        wpool = wpool.at[g * C_GW:(g + 1) * C_GW, g * C_GW:(g + 1) * C_GW].set(c_w_pool[l, g])
    alog = d_a_log[l].reshape(8)
    dtb = d_dt_bias[l].reshape(8)
    prow = jnp.zeros((2, LANES), F32).at[0, 8:16].set(alog).at[1, 8:16].set(dtb)
    pcol = jnp.zeros((16, 2), F32).at[8:16, 0].set(alog).at[8:16, 1].set(dtb)
    return dict(
        w1=w1, wg=wg, wuq=wuq, wukv=wukv,
        gpre1=g_pre1[l][None], gpost1=g_post1[l][None], gpre2=g_pre2[l][None], gpost2=g_post2[l][None],
        gcq=jnp.pad(b_g_cq[l], (0, 256 - B_Q_RANK))[None], gckv=b_g_ckv[l][None],
        sink=a_sink[l], wpool=wpool.astype(BF16), cscale=c_scale[l][None],
        convw=jnp.pad(d_conv[l], ((0, SUBLANES - D_CONV), (0, 0))),
        prow=prow, pcol=pcol, gnorm=jnp.tile(d_g_norm[l], D_HEADS)[None],
        wbr=w_br[l].astype(BF16), wo=w_o[l].astype(BF16), wup=w_up[l].astype(BF16), wdown=w_down[l].astype(BF16))


def _trunk_layer(x2d, mod3, lw, n_seq, t, ctx):
    latent = ctx is not None
    n = x2d.shape[0]
    tm = min(512, t) if latent else 256
    if latent:
        tables = _rope_tables(t, A_HD, A_HEADS) + _rope_tables(t, B_ROPE, B_HEADS)
        proj, qb, ckvn, knv, aqr, akr, qper, krr = _inproj_call(
            x2d, mod3, lw["gpre1"], lw["w1"], lw["gcq"], lw["wuq"], lw["gckv"], lw["wukv"], tables, t, tm)
        ctx_ak, ctx_av, ctx_ckv, ctx_kr, s0f, s0b = ctx
        out_a = _win_attn_call(lw["sink"], aqr, akr, proj, _expand_kv_heads(ctx_ak), _expand_kv_heads(ctx_av),
                               n_seq, t)
        out_b = _mla_lat_call(qb, qper, knv, krr, ctx_ckv, jnp.tile(ctx_kr, (1, 1, B_HEADS)), lw["wukv"],
                              n_seq, t, min(256, t))
        s0 = jnp.stack([s0f, s0b], axis=1)
    else:
        proj, qb, ckvn, knv = _inproj_call(
            x2d, mod3, lw["gpre1"], lw["w1"], lw["gcq"], lw["wuq"], lw["gckv"], lw["wukv"], None, t, tm)
        out_a, out_b = _attn_ctx_call(lw["sink"], proj, qb, knv, n_seq, t)
        s0 = jnp.zeros((n_seq, 2, D_HEADS, D_DK, D_DV), F32)
    out_c, qkvn = _local_call(proj, lw["convw"], lw["wpool"], lw["cscale"], t, tm)
    nc = t // D_CHUNK
    ab_row = proj[:, C_AB:C_AB + 16].reshape(n_seq, nc, D_CHUNK, 16).transpose(0, 1, 3, 2)
    o_f, o_b, s_fin = _gdn_call(qkvn, proj, ab_row, lw["prow"], lw["pcol"], s0, n_seq, t)
    tmm = 256
    x1 = _merge_call(x2d, mod3, lw["gpre1"], out_a, out_b, out_c, o_f, o_b, proj, lw["gnorm"],
                     lw["wg"], lw["wbr"], lw["wo"], lw["gpost1"], t, tmm)
    x2 = _ffn_call(x1, mod3, lw["gpre2"], lw["wup"], lw["wdown"], lw["gpost2"], t, tmm)
    new_ctx = None
    if not latent:
        p3 = proj.reshape(n_seq, t, P1)
        ka = jnp.stack([p3[:, :, C_AK:C_AK + A_HD], p3[:, :, C_AK + 2 * A_HD:C_AK + 3 * A_HD]], axis=2)
        va = jnp.stack([p3[:, :, C_AV:C_AV + A_HD], p3[:, :, C_AV + 2 * A_HD:C_AV + 3 * A_HD]], axis=2)
        new_ctx = (ka, va, ckvn.reshape(n_seq, t, B_KV_RANK), p3[:, :, C_KR:C_KR + B_ROPE],
                   s_fin[:, 0], s_fin[:, 1])
    return x2, new_ctx


def kernel(x_prompt, x_sample, cache_a_k, cache_a_v, cache_b_ckv, cache_b_krope, state_d_fwd, state_d_bwd, c, c_ctx, w_mod, b_mod, g_pre1, g_post1, g_pre2, g_post2, w_in, a_sink, b_g_cq, b_g_ckv, b_w_uq, b_w_ukv, c_w_pool, c_scale, d_conv, d_a_log, d_dt_bias, d_g_norm, w_br, w_o, w_up, w_down):
    depth = w_mod.shape[0]
    bp, tp, d = x_prompt.shape
    bs, ts, _ = x_sample.shape
    assert bs + 1 <= SUBLANES
    cond8 = jnp.zeros((SUBLANES, d), F32).at[0].set(c_ctx).at[1:1 + bs].set(c)
    mod_all = _mod_call(cond8, w_mod, b_mod)
    yp = x_prompt.reshape(bp * tp, d)
    ys = x_sample.reshape(bs * ts, d)
    new = [[] for _ in range(6)]
    for l in range(depth):
        lw = _layer_weights(l, g_pre1, g_post1, g_pre2, g_post2, w_in, a_sink, b_g_cq, b_g_ckv, b_w_uq, b_w_ukv,
                            c_w_pool, c_scale, d_conv, d_a_log, d_dt_bias, d_g_norm, w_br, w_o, w_up, w_down)
        mod_ctx = mod_all[l, 0:1][:, None, :]
        mod_lat = mod_all[l, 1:1 + bs][:, None, :]
        yp, nctx = _trunk_layer(yp, mod_ctx, lw, bp, tp, None)
        for acc, val in zip(new, nctx):
            acc.append(val)
        ctx = (cache_a_k[:, l], cache_a_v[:, l], cache_b_ckv[:, l], cache_b_krope[:, l],
               state_d_fwd[:, l], state_d_bwd[:, l])
        ys, _ = _trunk_layer(ys, mod_lat, lw, bs, ts, ctx)
    return (yp.reshape(bp, tp, d), ys.reshape(bs, ts, d)) + tuple(jnp.stack(v, axis=1) for v in new)
```

```python
import functools
import math

import jax
import jax.numpy as jnp
from jax import lax
from jax.experimental import pallas as pl
from jax.experimental.pallas import tpu as pltpu

F32 = jnp.float32
BF16 = jnp.bfloat16

D_MODEL = 1024
GRID_W = 64
ROPE_BASE = 10000.0
EPS = 1e-6
NEG_INF = -1e30
Q_BLOCK = 128
N_BRANCH = 4
BRANCH_W = 256
A_HEADS = 4
A_KV_HEADS = 2
A_HD = 64
A_WINDOW = 128
A_SCALE = A_HD ** -0.5
B_HEADS = 4
B_Q_RANK = 192
B_KV_RANK = 128
B_NOPE = 64
B_ROPE = 32
B_VD = 64
B_SCALE = (B_NOPE + B_ROPE) ** -0.5
C_GROUPS = 4
C_GW = 64
C_WINDOWS = (2, 4, 8, 16)
D_HEADS = 4
D_DK = 64
D_DV = 64
D_CONV = 5
D_CHUNK = 64
FF_RAW = -(-8 * D_MODEL // 3)
D_FF = -(-FF_RAW // 256) * 256
IN_SIZES = (A_HEADS * A_HD, A_KV_HEADS * A_HD, A_KV_HEADS * A_HD, B_Q_RANK, B_KV_RANK, B_ROPE,
            C_GROUPS * C_GW, D_HEADS * (2 * D_DK + D_DV), D_HEADS * D_DV, 2 * D_HEADS, 2 * D_HEADS,
            N_BRANCH * D_MODEL)

LANES = 128
SUBLANES = 8
VMEM_LIMIT_BYTES = 56 * 1024 * 1024
HALO = SUBLANES
GDN_CHUNKS_PER_STEP = 2

C_AQ = 0
C_AK = 256
C_AV = 512
C_CQ = 768
C_CKV = 1024
C_KR = 1152
C_CIN = 1280
C_QKV = 1536
C_Z = 2304
C_AB = 2560
P1 = 2688


def _cparams(sem):
    return pltpu.CompilerParams(dimension_semantics=sem, vmem_limit_bytes=VMEM_LIMIT_BYTES)


def _dot(a, b):
    return jnp.dot(a, b, preferred_element_type=F32)


def _dot_nt(a, b):
    return lax.dot_general(a, b, (((1,), (1,)), ((), ())), preferred_element_type=F32)


def _split3(x):
    hi = x.astype(BF16)
    r1 = x - hi.astype(F32)
    mid = r1.astype(BF16)
    lo = (r1 - mid.astype(F32)).astype(BF16)
    return hi, mid, lo


def _dot_sel_right(x, sel):
    hi, mid, lo = _split3(x)
    return _dot(hi, sel) + _dot(mid, sel) + _dot(lo, sel)


def _silu(x):
    return x * jax.nn.sigmoid(x)


def _softplus(x):
    return jnp.maximum(x, 0.0) + jnp.log1p(jnp.exp(-jnp.abs(x)))


def _head_mask(width, n_lanes, h):
    lane = lax.broadcasted_iota(jnp.int32, (1, n_lanes), 1)
    return (lane // width) == h


def _group_ones(n, width):
    r = lax.broadcasted_iota(jnp.int32, (n, n), 0) // width
    c = lax.broadcasted_iota(jnp.int32, (n, n), 1) // width
    return jnp.where(r == c, 1.0, 0.0).astype(BF16)


def _mod_kernel(c_ref, w_ref, b_ref, o_ref):
    s = _silu(c_ref[...])
    o_ref[0] = _dot(s.astype(BF16), w_ref[0].astype(BF16)) + b_ref[0]


def _mod_call(cond8, w_mod, b_mod):
    depth, d, n = w_mod.shape
    tn = 1536
    return pl.pallas_call(
        _mod_kernel,
        grid=(depth, n // tn),
        in_specs=[pl.BlockSpec((SUBLANES, d), lambda l, j: (0, 0)),
                  pl.BlockSpec((1, d, tn), lambda l, j: (l, 0, j)),
                  pl.BlockSpec((1, 1, tn), lambda l, j: (l, 0, j))],
        out_specs=pl.BlockSpec((1, SUBLANES, tn), lambda l, j: (l, 0, j)),
        out_shape=jax.ShapeDtypeStruct((depth, SUBLANES, n), F32),
        compiler_params=_cparams(("parallel", "parallel")),
        name="mod",
    )(cond8, w_mod, b_mod.reshape(depth, 1, n))


def _rmsnorm_rows(x, g, n):
    ms = jnp.sum(x * x, axis=-1, keepdims=True) * (1.0 / n)
    return x * lax.rsqrt(ms + EPS) * g


def _rope(x, cos, sin_next, sin_prev, quarter):
    n = x.shape[-1]
    return x * cos + pltpu.roll(x, n - quarter, 1) * sin_next + pltpu.roll(x, quarter, 1) * sin_prev


def _inproj_kernel(latent, *refs):
    if latent:
        (x_ref, mod_ref, gpre_ref, w1_ref, gcq_ref, wuq_ref, gckv_ref, wukv_ref,
         ca_ref, sna_ref, spa_ref, cb_ref, snb_ref, spb_ref,
         proj_ref, qb_ref, ckv_ref, knv_ref, aqr_ref, akr_ref, qper_ref, krr_ref) = refs
    else:
        (x_ref, mod_ref, gpre_ref, w1_ref, gcq_ref, wuq_ref, gckv_ref, wukv_ref,
         proj_ref, qb_ref, ckv_ref, knv_ref) = refs
    d = D_MODEL
    x = x_ref[...]
    mod = mod_ref[0]
    sh1 = mod[:, 0:d]
    sc1 = mod[:, d:2 * d]
    h = _rmsnorm_rows(x, gpre_ref[...], d) * (1.0 + sc1) + sh1
    proj = _dot(h.astype(BF16), w1_ref[...])
    proj_ref[...] = proj
    cqn = _rmsnorm_rows(proj[:, C_CQ:C_CQ + 256], gcq_ref[...], B_Q_RANK)
    qb = _dot(cqn.astype(BF16), wuq_ref[...])
    qb_ref[...] = qb
    ckvn = _rmsnorm_rows(proj[:, C_CKV:C_CKV + B_KV_RANK], gckv_ref[...], B_KV_RANK)
    ckv_ref[...] = ckvn
    knv_ref[...] = _dot(ckvn.astype(BF16), wukv_ref[...]).astype(knv_ref.dtype)
    if latent:
        ca, sna, spa = ca_ref[...], sna_ref[...], spa_ref[...]
        cb, snb, spb = cb_ref[...], snb_ref[...], spb_ref[...]
        aqr_ref[...] = _rope(proj[:, C_AQ:C_AQ + 256], ca, sna, spa, A_HD // 4).astype(BF16)
        akr_ref[...] = _rope(proj[:, C_AK:C_AK + 256], ca, sna, spa, A_HD // 4).astype(BF16)
        qper_ref[...] = _rope(qb[:, 256:384], cb, snb, spb, B_ROPE // 4).astype(BF16)
        krr_ref[...] = _rope(proj[:, C_KR:C_KR + 128], cb, snb, spb, B_ROPE // 4).astype(BF16)


def _inproj_call(x2d, mod3, gpre, w1, gcq, wuq, gckv, wukv, tables, seq_len, tm):
    n, d = x2d.shape
    latent = tables is not None
    tiles_per_seq = seq_len // tm if latent else 1
    n_mod = mod3.shape[0]
    if n_mod == 1:
        mod_map = lambda i: (0, 0, 0)
    else:
        mod_map = lambda i: (i // tiles_per_seq, 0, 0)
    full = lambda shape: pl.BlockSpec(shape, lambda i: tuple(0 for _ in shape))
    row = lambda w, dt=F32: (pl.BlockSpec((tm, w), lambda i: (i, 0)), jax.ShapeDtypeStruct((n, w), dt))
    in_specs = [pl.BlockSpec((tm, d), lambda i: (i, 0)),
                pl.BlockSpec((1, 1, 6 * d), mod_map),
                full((1, d)), full((d, P1)), full((1, 256)), full((256, 384)),
                full((1, B_KV_RANK)), full((B_KV_RANK, 512))]
    args = [x2d, mod3, gpre, w1, gcq, wuq, gckv, wukv]
    outs = [row(P1), row(384), row(B_KV_RANK), row(512, BF16 if latent else F32)]
    if latent:
        for t in tables:
            in_specs.append(pl.BlockSpec((tm, t.shape[1]), lambda i: (i % tiles_per_seq, 0)))
            args.append(t)
        outs += [row(256, BF16), row(256, BF16), row(128, BF16), row(128, BF16)]
    return pl.pallas_call(
        functools.partial(_inproj_kernel, latent),
        grid=(n // tm,),
        in_specs=in_specs,
        out_specs=[o[0] for o in outs],
        out_shape=[o[1] for o in outs],
        compiler_params=_cparams(("parallel",)),
        name="inproj_lat" if latent else "inproj_ctx",
    )(*args)


def _softmax_pv(scores, values, sink, hm):
    m = jnp.max(scores[0], axis=-1, keepdims=True)
    for s in scores[1:]:
        m = jnp.maximum(m, jnp.max(s, axis=-1, keepdims=True))
    if sink is not None:
        m = jnp.maximum(m, sink)
    den = None
    pv = None
    for s, v in zip(scores, values):
        p = jnp.exp(s - m)
        ps = jnp.sum(p, axis=-1, keepdims=True)
        den = ps if den is None else den + ps
        t = _dot(p.astype(BF16), v)
        pv = t if pv is None else pv + t
    if sink is not None:
        den = den + jnp.exp(sink - m)
    return jnp.where(hm, pv / den, 0.0)


def _attn_ctx_kernel(sink_ref, q_ref, k_ref, v_ref, qb_ref, knv_ref, kr_ref, oa_ref, ob_ref):
    q = q_ref[...]
    k = k_ref[...].astype(BF16)
    v = v_ref[...].astype(BF16)
    acc = jnp.zeros(q.shape, F32)
    for h in range(A_HEADS):
        hm = _head_mask(A_HD, 256, h)
        s = _dot_nt(jnp.where(hm, q, 0.0).astype(BF16), k) * A_SCALE
        acc = acc + _softmax_pv([s], [v], sink_ref[h], hm)
    oa_ref[...] = acc
    qb = qb_ref[...]
    qn, qpe = qb[:, :256], qb[:, 256:]
    knv = knv_ref[...]
    kn = knv[:, :256].astype(BF16)
    vb = knv[:, 256:].astype(BF16)
    kr = kr_ref[...].astype(BF16)
    acc = jnp.zeros(qn.shape, F32)
    for h in range(B_HEADS):
        hm = _head_mask(B_NOPE, 256, h)
        hr = _head_mask(B_ROPE, 128, h)
        s = (_dot_nt(jnp.where(hm, qn, 0.0).astype(BF16), kn)
             + _dot_nt(jnp.where(hr, qpe, 0.0).astype(BF16), kr)) * B_SCALE
        acc = acc + _softmax_pv([s], [vb], None, hm)
    ob_ref[...] = acc


def _attn_ctx_call(sink, proj, qb, knv, n_seq, t):
    n = proj.shape[0]
    col = lambda w, j: pl.BlockSpec((t, w), lambda b: (b, j))
    out = (pl.BlockSpec((t, 256), lambda b: (b, 0)), jax.ShapeDtypeStruct((n, 256), F32))
    return pl.pallas_call(
        _attn_ctx_kernel,
        grid=(n_seq,),
        in_specs=[pl.BlockSpec(memory_space=pltpu.SMEM),
                  col(256, C_AQ // 256), col(256, C_AK // 256), col(256, C_AV // 256),
                  col(384, 0), col(512, 0), col(128, C_KR // 128)],
        out_specs=[out[0], out[0]],
        out_shape=[out[1], out[1]],
        compiler_params=_cparams(("parallel",)),
        name="attn_ctx",
    )(sink, proj, proj, proj, qb, knv, proj)


def _win_attn_kernel(t, sink_ref, qr_ref, q_ref, kp_ref, kc_ref, kn_ref, vp_ref, vc_ref, vn_ref,
                     kctx_ref, vctx_ref, o_ref):
    i = pl.program_id(1)
    qr = qr_ref[...]
    q = q_ref[...]
    kl = jnp.concatenate([kp_ref[...], kc_ref[...], kn_ref[...]], axis=0)
    vl = jnp.concatenate([vp_ref[...], vc_ref[...], vn_ref[...]], axis=0).astype(BF16)
    kctx = kctx_ref[0].astype(BF16)
    vctx = vctx_ref[0].astype(BF16)
    q_pos = i * Q_BLOCK + lax.broadcasted_iota(jnp.int32, (Q_BLOCK, 3 * Q_BLOCK), 0)
    k_pos = (i - 1) * Q_BLOCK + lax.broadcasted_iota(jnp.int32, (Q_BLOCK, 3 * Q_BLOCK), 1)
    valid = (jnp.abs(q_pos - k_pos) <= A_WINDOW) & (k_pos >= 0) & (k_pos < t)
    acc = jnp.zeros(q.shape, F32)
    for h in range(A_HEADS):
        hm = _head_mask(A_HD, 256, h)
        s_loc = _dot_nt(jnp.where(hm, qr, 0.0).astype(BF16), kl) * A_SCALE
        s_loc = jnp.where(valid, s_loc, NEG_INF)
        s_ctx = _dot_nt(jnp.where(hm, q, 0.0).astype(BF16), kctx) * A_SCALE
        acc = acc + _softmax_pv([s_loc, s_ctx], [vl, vctx], sink_ref[h], hm)
    o_ref[...] = acc


def _win_attn_call(sink, aqr, akr, proj, kctx, vctx, n_seq, t):
    n = proj.shape[0]
    nb = t // Q_BLOCK
    blk = lambda j, f: pl.BlockSpec((Q_BLOCK, 256), lambda b, i: (b * nb + f(i), j))
    prev = lambda i: jnp.maximum(i - 1, 0)
    cur = lambda i: i
    nxt = lambda i: jnp.minimum(i + 1, nb - 1)
    ctx = pl.BlockSpec((1,) + kctx.shape[1:], lambda b, i: (b, 0, 0))
    return pl.pallas_call(
        functools.partial(_win_attn_kernel, t),
        grid=(n_seq, nb),
        in_specs=[pl.BlockSpec(memory_space=pltpu.SMEM),
                  blk(0, cur), blk(C_AQ // 256, cur),
                  blk(0, prev), blk(0, cur), blk(0, nxt),
                  blk(C_AV // 256, prev), blk(C_AV // 256, cur), blk(C_AV // 256, nxt),
                  ctx, ctx],
        out_specs=pl.BlockSpec((Q_BLOCK, 256), lambda b, i: (b * nb + i, 0)),
        out_shape=jax.ShapeDtypeStruct((n, 256), F32),
        compiler_params=_cparams(("parallel", "parallel")),
        name="win_attn",
    )(sink, aqr, proj, akr, akr, akr, proj, proj, proj, kctx, vctx)


def _mla_lat_kernel(qb_ref, qper_ref, knv_ref, krr_ref, cckv_ref, ckr_ref, wukv_ref, o_ref):
    qb = qb_ref[...]
    qn, qpe = qb[:, :256], qb[:, 256:]
    qper = qper_ref[...]
    kn = knv_ref[:, :256]
    v = knv_ref[:, 256:]
    krr = krr_ref[...]
    kvc = _dot(cckv_ref[0].astype(BF16), wukv_ref[...])
    knc = kvc[:, :256].astype(BF16)
    vc = kvc[:, 256:].astype(BF16)
    krc = ckr_ref[0].astype(BF16)
    acc = jnp.zeros(qn.shape, F32)
    for h in range(B_HEADS):
        hm = _head_mask(B_NOPE, 256, h)
        hr = _head_mask(B_ROPE, 128, h)
        qnh = jnp.where(hm, qn, 0.0).astype(BF16)
        s_lat = (_dot_nt(qnh, kn) + _dot_nt(jnp.where(hr, qper, 0.0), krr)) * B_SCALE
        s_ctx = (_dot_nt(qnh, knc) + _dot_nt(jnp.where(hr, qpe, 0.0).astype(BF16), krc)) * B_SCALE
        acc = acc + _softmax_pv([s_lat, s_ctx], [v, vc], None, hm)
    o_ref[...] = acc


def _mla_lat_call(qb, qper, knv, krr, cckv, ckr_rep, wukv, n_seq, t, tq):
    n = qb.shape[0]
    nq = t // tq
    qblk = lambda w: pl.BlockSpec((tq, w), lambda b, i: (b * nq + i, 0))
    seq = lambda w: pl.BlockSpec((t, w), lambda b, i: (b, 0))
    ctx = lambda a: pl.BlockSpec((1,) + a.shape[1:], lambda b, i: (b, 0, 0))
    return pl.pallas_call(
        _mla_lat_kernel,
        grid=(n_seq, nq),
        in_specs=[qblk(384), qblk(128), seq(512), seq(128), ctx(cckv), ctx(ckr_rep),
                  pl.BlockSpec(wukv.shape, lambda b, i: (0, 0))],
        out_specs=pl.BlockSpec((tq, 256), lambda b, i: (b * nq + i, 0)),
        out_shape=jax.ShapeDtypeStruct((n, 256), F32),
        compiler_params=_cparams(("parallel", "parallel")),
        name="mla_lat",
    )(qb, qper, knv, krr, cckv, ckr_rep, wukv)


def _with_halo(prev_ref, cur_ref, next_ref, has_prev, has_next):
    prev = jnp.where(has_prev, prev_ref[...], 0.0)
    nxt = jnp.where(has_next, next_ref[...], 0.0)
    return jnp.concatenate([prev, cur_ref[...], nxt], axis=0)


def _shift_rows(x, k):
    n = x.shape[0]
    return pltpu.roll(x, (-k) % n, 0)


def _local_kernel(t, tm, cp_ref, cc_ref, cn_ref, qp_ref, qc_ref, qn_ref, convw_ref, wpool_ref, cscale_ref,
                  oc_ref, qkv_ref):
    i = pl.program_id(0)
    tiles_per_seq = t // tm
    j = i % tiles_per_seq
    has_prev = j > 0
    has_next = j < tiles_per_seq - 1
    x = _with_halo(cp_ref, cc_ref, cn_ref, has_prev, has_next)
    p2 = x + _shift_rows(x, -1)
    p4 = _shift_rows(p2, 1) + _shift_rows(p2, -1)
    p8 = _shift_rows(p4, 2) + _shift_rows(p4, -2)
    p16 = _shift_rows(p8, 4) + _shift_rows(p8, -4)
    grp = lax.broadcasted_iota(jnp.int32, (1, 256), 1) // C_GW
    win = jnp.where(grp == 0, p2, jnp.where(grp == 1, p4, jnp.where(grp == 2, p8, p16)))[HALO:HALO + tm]
    pos = j * tm + lax.broadcasted_iota(jnp.int32, (tm, 256), 0)
    half = jnp.where(grp == 0, 1, jnp.where(grp == 1, 2, jnp.where(grp == 2, 4, 8)))
    cnt = (jnp.minimum(pos + half, t) - jnp.maximum(pos - half, 0)).astype(F32)
    y = win / cnt - cc_ref[...]
    oc_ref[...] = _dot(y.astype(BF16), wpool_ref[...]) * cscale_ref[...]
    xq = _with_halo(qp_ref, qc_ref, qn_ref, has_prev, has_next)
    w = convw_ref[...]
    pad = D_CONV // 2
    acc = None
    for tap in range(D_CONV):
        term = _shift_rows(xq, tap - pad) * w[tap:tap + 1, :]
        acc = term if acc is None else acc + term
    u = _silu(acc[HALO:HALO + tm])
    ones = _group_ones(256, D_DK)
    q = u[:, 0:256]
    k = u[:, 256:512]
    q = q * lax.rsqrt(_dot_sel_right(q * q, ones) + EPS) * (D_DK ** -0.5)
    k = k * lax.rsqrt(_dot_sel_right(k * k, ones) + EPS)
    qkv_ref[:, 0:256] = q
    qkv_ref[:, 256:512] = k
    qkv_ref[:, 512:768] = u[:, 512:768]


def _local_call(proj, convw8, wpool_bd, cscale, t, tm):
    n = proj.shape[0]
    hb = tm // HALO
    nblk = n // HALO
    cur = lambda w, c: pl.BlockSpec((tm, w), lambda i: (i, c))
    prev = lambda w, c: pl.BlockSpec((HALO, w), lambda i: (jnp.maximum(i * hb - 1, 0), c))
    nxt = lambda w, c: pl.BlockSpec((HALO, w), lambda i: (jnp.minimum((i + 1) * hb, nblk - 1), c))
    full = lambda a: pl.BlockSpec(a.shape, lambda i: (0, 0))
    cc, qc = C_CIN // 256, C_QKV // 768
    return pl.pallas_call(
        functools.partial(_local_kernel, t, tm),
        grid=(n // tm,),
        in_specs=[prev(256, cc), cur(256, cc), nxt(256, cc), prev(768, qc), cur(768, qc), nxt(768, qc),
                  full(convw8), full(wpool_bd), full(cscale)],
        out_specs=[pl.BlockSpec((tm, 256), lambda i: (i, 0)), pl.BlockSpec((tm, 768), lambda i: (i, 0))],
        out_shape=[jax.ShapeDtypeStruct((n, 256), F32), jax.ShapeDtypeStruct((n, 768), F32)],
        compiler_params=_cparams(("parallel",)),
        name="local",
    )(proj, proj, proj, proj, proj, proj, convw8, wpool_bd, cscale)


def _gdn_kernel(n_steps, g_chunks, xf_ref, xb_ref, abf_ref, abb_ref, abrf_ref, abrb_ref, prow_ref, pcol_ref,
                s0_ref, of_ref, ob_ref, sfin_ref, s_ref):
    i = pl.program_id(1)
    c = D_CHUNK
    nh = D_HEADS
    nb = 2 * g_chunks * nh

    @pl.when(i == 0)
    def _():
        s_ref[...] = s0_ref[0].reshape(2 * nh, D_DK, D_DV)

    r = lax.broadcasted_iota(jnp.int32, (c, c), 0)
    cidx = lax.broadcasted_iota(jnp.int32, (c, c), 1)
    lower = jnp.where(cidx <= r, 1.0, 0.0).astype(BF16)
    upper = jnp.where(cidx >= r, 1.0, 0.0).astype(BF16)
    prow = prow_ref[...]
    pcol = pcol_ref[...]

    eye = jnp.where(cidx == r, 1.0, 0.0).astype(BF16)
    sel_lower = jnp.concatenate([lower] * 3, axis=1)
    sel_upper = jnp.concatenate([upper] * 3, axis=1)
    sel_lower_t = jnp.concatenate([lower] * 3, axis=0)
    sel_upper_t = jnp.concatenate([upper] * 3, axis=0)

    q_l, k_l, v_l, b_l, gcol_l, grow_l = [], [], [], [], [], []
    for d, (x_ref, ab_ref, abr_ref) in enumerate(((xf_ref, abf_ref, abrf_ref), (xb_ref, abb_ref, abrb_ref))):
        for gi in range(g_chunks):
            rows = slice(gi * c, (gi + 1) * c)
            x = x_ref[rows, :]
            ab = ab_ref[rows, :]
            abr = abr_ref[0, gi]
            beta_c = jax.nn.sigmoid(ab)
            g_c = -jnp.exp(prow[0:1, :]) * _softplus(ab + prow[1:2, :])
            g_r = -jnp.exp(pcol[:, 0:1]) * _softplus(abr + pcol[:, 1:2])
            if d == 0:
                gc_c = _dot(sel_lower, jnp.concatenate(_split3(g_c), axis=0))
                gc_r = _dot(jnp.concatenate(_split3(g_r), axis=1), sel_upper_t)
            else:
                gc_c = _dot(sel_upper, jnp.concatenate(_split3(g_c), axis=0))
                gc_r = _dot(jnp.concatenate(_split3(g_r), axis=1), sel_lower_t)
            for h in range(nh):
                hs = slice(h * D_DK, (h + 1) * D_DK)
                lane = nh * d + h
                q_l.append(x[:, 0:256][:, hs])
                k_l.append(x[:, 256:512][:, hs])
                v_l.append(x[:, 512:768][:, hs])
                b_l.append(beta_c[:, lane:lane + 1])
                gcol_l.append(gc_c[:, 8 + lane:9 + lane])
                grow_l.append(gc_r[8 + lane:9 + lane, :])
    qh, kh, vh = jnp.stack(q_l), jnp.stack(k_l), jnp.stack(v_l)
    bcol, gcol, grow = jnp.stack(b_l), jnp.stack(gcol_l), jnp.stack(grow_l)
    is_fwd = lax.broadcasted_iota(jnp.int32, (nb, c, c), 0) < nb // 2
    r3 = lax.broadcasted_iota(jnp.int32, (nb, c, c), 1)
    c3 = lax.broadcasted_iota(jnp.int32, (nb, c, c), 2)
    ahead = jnp.where(is_fwd, r3 - c3, c3 - r3)
    incl = ahead >= 0
    strict = ahead > 0
    glast = jnp.concatenate([gcol[:nb // 2, c - 1:c, :], gcol[nb // 2:, 0:1, :]], axis=0)
    bdot = lambda a, b: jnp.einsum('bik,bkj->bij', a, b, preferred_element_type=F32)
    bdot_nt = lambda a, b: jnp.einsum('bik,bjk->bij', a, b, preferred_element_type=F32)

    decay = jnp.exp(jnp.where(incl, gcol - grow, NEG_INF))
    kb = kh * bcol
    khb = kh.astype(BF16)
    kq = bdot_nt(jnp.concatenate([kb, qh], axis=1).astype(BF16), khb)
    p = -jnp.where(strict, kq[:, 0:c] * decay, 0.0)
    egc = jnp.exp(gcol)
    xs = jnp.concatenate([vh * bcol, kb * egc], axis=2)
    for step in range(6):
        ph = p.astype(BF16)
        pl_ = (p - ph.astype(F32)).astype(BF16)
        half = jnp.concatenate([ph, pl_], axis=2)
        lhs = jnp.concatenate([half, half], axis=2)
        y = jnp.concatenate([xs, p], axis=2) if step < 5 else xs
        yh = y.astype(BF16)
        yl = (y - yh.astype(F32)).astype(BF16)
        m = bdot(lhs, jnp.concatenate([yh, yh, yl, yl], axis=1))
        xs = xs + m[:, :, 0:2 * D_DK]
        if step < 5:
            p = m[:, :, 2 * D_DK:3 * D_DK]
    u = xs[:, :, 0:D_DV]
    w = xs[:, :, D_DV:2 * D_DV]
    a = kq[:, c:2 * c] * decay
    kd = (kh * jnp.exp(glast - gcol)).astype(BF16)
    kdt = bdot_nt(jnp.broadcast_to(eye, (nb, c, c)), kd).astype(BF16)
    lhs_s = jnp.concatenate([w, qh * egc], axis=1).astype(BF16)
    lhs_v = jnp.concatenate([a.astype(BF16), kdt], axis=1)
    eg = jnp.exp(glast)

    def chains(arr, step):
        lo_f = step * nh
        lo_b = nb // 2 + (g_chunks - 1 - step) * nh
        return jnp.concatenate([arr[lo_f:lo_f + nh], arr[lo_b:lo_b + nh]], axis=0)

    s = s_ref[...]
    outs_f, outs_b = [None] * g_chunks, [None] * g_chunks
    for step in range(g_chunks):
        ws_qs = bdot(chains(lhs_s, step), s.astype(BF16))
        vb = (chains(u, step) - ws_qs[:, 0:c]).astype(BF16)
        av = bdot(chains(lhs_v, step), vb)
        o = ws_qs[:, c:2 * c] + av[:, 0:c]
        s = s * chains(eg, step) + av[:, c:2 * c]
        outs_f[step] = jnp.concatenate([o[h] for h in range(nh)], axis=1)
        outs_b[g_chunks - 1 - step] = jnp.concatenate([o[nh + h] for h in range(nh)], axis=1)
    of_ref[...] = jnp.concatenate(outs_f, axis=0)
    ob_ref[...] = jnp.concatenate(outs_b, axis=0)
    s_ref[...] = s

    @pl.when(i == n_steps - 1)
    def _():
        sfin_ref[0] = s.reshape(2, nh, D_DK, D_DV)


def _gdn_call(qkvn, proj, ab_row, prow, pcol, s0, n_seq, t):
    n = qkvn.shape[0]
    g = GDN_CHUNKS_PER_STEP
    ns = t // (D_CHUNK * g)
    rows = D_CHUNK * g
    fwd = lambda b, i: b * ns + i
    bwd = lambda b, i: b * ns + ns - 1 - i
    xblk = lambda f: pl.BlockSpec((rows, 768), lambda b, i: (f(b, i), 0))
    abblk = lambda f: pl.BlockSpec((rows, 128), lambda b, i: (f(b, i), C_AB // 128))
    abrblk = lambda f: pl.BlockSpec((1, g, 16, D_CHUNK), lambda b, i: (b, f(0, i), 0, 0))
    oblk = lambda f: pl.BlockSpec((rows, 256), lambda b, i: (f(b, i), 0))
    sblk = pl.BlockSpec((1, 2, D_HEADS, D_DK, D_DV), lambda b, i: (b, 0, 0, 0, 0))
    osd = jax.ShapeDtypeStruct((n, 256), F32)
    return pl.pallas_call(
        functools.partial(_gdn_kernel, ns, g),
        grid=(n_seq, ns),
        in_specs=[xblk(fwd), xblk(bwd), abblk(fwd), abblk(bwd), abrblk(fwd), abrblk(bwd),
                  pl.BlockSpec(prow.shape, lambda b, i: (0, 0)), pl.BlockSpec(pcol.shape, lambda b, i: (0, 0)),
                  sblk],
        out_specs=[oblk(fwd), oblk(bwd), sblk],
        out_shape=[osd, osd, jax.ShapeDtypeStruct(s0.shape, F32)],
        scratch_shapes=[pltpu.VMEM((2 * D_HEADS, D_DK, D_DV), F32)],
        compiler_params=_cparams(("parallel", "arbitrary")),
        name="gdn",
    )(qkvn, qkvn, proj, proj, ab_row, ab_row, prow, pcol, s0)


def _merge_kernel(x_ref, mod_ref, gpre_ref, oa_ref, ob_ref, oc_ref, of_ref, obw_ref, z_ref, gnorm_ref,
                  wg_ref, wbr_ref, wo_ref, gpost_ref, o_ref):
    d = D_MODEL
    x = x_ref[...]
    mod = mod_ref[0]
    sh1, sc1, ga1 = mod[:, 0:d], mod[:, d:2 * d], mod[:, 2 * d:3 * d]
    h = (_rmsnorm_rows(x, gpre_ref[...], d) * (1.0 + sc1) + sh1).astype(BF16)
    o = of_ref[...] + obw_ref[...]
    ms = _dot_sel_right(o * o, _group_ones(256, D_DV)) * (1.0 / D_DV)
    od = o * lax.rsqrt(ms + EPS) * gnorm_ref[...] * _silu(z_ref[...])
    acc = None
    for m, br in enumerate((oa_ref[...], ob_ref[...], oc_ref[...], od)):
        gate = jax.nn.sigmoid(_dot(h, wg_ref[:, m * d:(m + 1) * d]))
        term = gate * _dot(br.astype(BF16), wbr_ref[m])
        acc = term if acc is None else acc + term
    mix = _dot(acc.astype(BF16), wo_ref[...])
    o_ref[...] = x + ga1 * _rmsnorm_rows(mix, gpost_ref[...], d)


def _merge_call(x2d, mod3, gpre, oa, ob, oc, of, obw, proj, gnorm, wg, wbr, wo, gpost, seq_len, tm):
    n, d = x2d.shape
    tiles_per_seq = seq_len // tm
    mod_map = (lambda i: (0, 0, 0)) if mod3.shape[0] == 1 else (lambda i: (i // tiles_per_seq, 0, 0))
    row = lambda w, c=0: pl.BlockSpec((tm, w), lambda i: (i, c))
    full = lambda a: pl.BlockSpec(a.shape, lambda i: tuple(0 for _ in a.shape))
    return pl.pallas_call(
        _merge_kernel,
        grid=(n // tm,),
        in_specs=[row(d), pl.BlockSpec((1, 1, 6 * d), mod_map), full(gpre),
                  row(256), row(256), row(256), row(256), row(256), row(256, C_Z // 256), full(gnorm),
                  full(wg), full(wbr), full(wo), full(gpost)],
        out_specs=row(d),
        out_shape=jax.ShapeDtypeStruct((n, d), F32),
        compiler_params=_cparams(("parallel",)),
        name="merge",
    )(x2d, mod3, gpre, oa, ob, oc, of, obw, proj, gnorm, wg, wbr, wo, gpost)


def _ffn_kernel(x_ref, mod_ref, gpre_ref, wup_ref, wdown_ref, gpost_ref, o_ref):
    d = D_MODEL
    x = x_ref[...]
    mod = mod_ref[0]
    sh2, sc2, ga2 = mod[:, 3 * d:4 * d], mod[:, 4 * d:5 * d], mod[:, 5 * d:6 * d]
    h = (_rmsnorm_rows(x, gpre_ref[...], d) * (1.0 + sc2) + sh2).astype(BF16)
    up = _dot(h, wup_ref[...])
    act = _silu(up[:, 0:D_FF]) * up[:, D_FF:2 * D_FF]
    f = _dot(act.astype(BF16), wdown_ref[...])
    o_ref[...] = x + ga2 * _rmsnorm_rows(f, gpost_ref[...], d)


def _ffn_call(x2d, mod3, gpre, wup, wdown, gpost, seq_len, tm):
    n, d = x2d.shape
    tiles_per_seq = seq_len // tm
    mod_map = (lambda i: (0, 0, 0)) if mod3.shape[0] == 1 else (lambda i: (i // tiles_per_seq, 0, 0))
    row = lambda w: pl.BlockSpec((tm, w), lambda i: (i, 0))
    full = lambda a: pl.BlockSpec(a.shape, lambda i: tuple(0 for _ in a.shape))
    return pl.pallas_call(
        _ffn_kernel,
        grid=(n // tm,),
        in_specs=[row(d), pl.BlockSpec((1, 1, 6 * d), mod_map), full(gpre), full(wup), full(wdown), full(gpost)],
        out_specs=row(d),
        out_shape=jax.ShapeDtypeStruct((n, d), F32),
        compiler_params=_cparams(("parallel",)),
        name="ffn",
    )(x2d, mod3, gpre, wup, wdown, gpost)


def _rope_tables(n_tokens, dim, reps):
    n_rows = n_tokens // GRID_W
    row = jnp.repeat(jnp.arange(n_rows), GRID_W).astype(F32)
    col = jnp.tile(jnp.arange(GRID_W), n_rows).astype(F32)
    nfreq = dim // 4
    inv = ROPE_BASE ** (-jnp.arange(nfreq, dtype=F32) / nfreq)
    ang_r = row[:, None] * inv
    ang_c = col[:, None] * inv
    ang = jnp.concatenate([ang_r, ang_r, ang_c, ang_c], axis=-1)
    cos, sin = jnp.cos(ang), jnp.sin(ang)
    even = ((jnp.arange(dim) // nfreq) % 2 == 0)[None, :]
    sin_next = jnp.where(even, -sin, 0.0)
    sin_prev = jnp.where(even, 0.0, sin)
    return tuple(jnp.tile(a, (1, reps)) for a in (cos, sin_next, sin_prev))


def _split_cols(w):
    out, acc = [], 0
    for s in IN_SIZES:
        out.append(w[:, acc:acc + s])
        acc += s
    return out


def _pack_w_in(w):
    (a_q, a_k, a_v, b_cq, b_ckv, b_kr, c_in, d_qkv, d_z, d_beta, d_alpha, gate) = _split_cols(w)
    d = w.shape[0]
    z = lambda n: jnp.zeros((d, n), w.dtype)
    k0, k1 = a_k[:, :A_HD], a_k[:, A_HD:]
    v0, v1 = a_v[:, :A_HD], a_v[:, A_HD:]
    w1 = jnp.concatenate([a_q, k0, k0, k1, k1, v0, v0, v1, v1, b_cq, z(256 - B_Q_RANK), b_ckv,
                          b_kr, b_kr, b_kr, b_kr, c_in, d_qkv, d_z, d_beta, d_alpha, z(112)], axis=1)
    return w1.astype(BF16), gate.astype(BF16)


def _expand_kv_heads(a):
    return jnp.repeat(a, A_HEADS // A_KV_HEADS, axis=2).reshape(a.shape[0], a.shape[1], A_HEADS * A_HD)


def _layer_weights(l, g_pre1, g_post1, g_pre2, g_post2, w_in, a_sink, b_g_cq, b_g_ckv, b_w_uq, b_w_ukv,
                   c_w_pool, c_scale, d_conv, d_a_log, d_dt_bias, d_g_norm, w_br, w_o, w_up, w_down):
    w1, wg = _pack_w_in(w_in[l])
    wuq = b_w_uq[l].reshape(B_Q_RANK, B_HEADS, B_NOPE + B_ROPE)
    wuq = jnp.concatenate([wuq[:, :, :B_NOPE].reshape(B_Q_RANK, -1), wuq[:, :, B_NOPE:].reshape(B_Q_RANK, -1)], 1)
    wuq = jnp.pad(wuq, ((0, 256 - B_Q_RANK), (0, 0))).astype(BF16)
    wukv = b_w_ukv[l].reshape(B_KV_RANK, B_HEADS, B_NOPE + B_VD)
    wukv = jnp.concatenate([wukv[:, :, :B_NOPE].reshape(B_KV_RANK, -1),
                            wukv[:, :, B_NOPE:].reshape(B_KV_RANK, -1)], 1).astype(BF16)
    wpool = jnp.zeros((256, 256), F32)
    for g in range(C_GROUPS):
        wpool = wpool.at[g * C_GW:(g + 1) * C_GW, g * C_GW:(g + 1) * C_GW].set(c_w_pool[l, g])
    alog = d_a_log[l].reshape(8)
    dtb = d_dt_bias[l].reshape(8)
    prow = jnp.zeros((2, LANES), F32).at[0, 8:16].set(alog).at[1, 8:16].set(dtb)
    pcol = jnp.zeros((16, 2), F32).at[8:16, 0].set(alog).at[8:16, 1].set(dtb)
    return dict(
        w1=w1, wg=wg, wuq=wuq, wukv=wukv,
        gpre1=g_pre1[l][None], gpost1=g_post1[l][None], gpre2=g_pre2[l][None], gpost2=g_post2[l][None],
        gcq=jnp.pad(b_g_cq[l], (0, 256 - B_Q_RANK))[None], gckv=b_g_ckv[l][None],
        sink=a_sink[l], wpool=wpool.astype(BF16), cscale=c_scale[l][None],
        convw=jnp.pad(d_conv[l], ((0, SUBLANES - D_CONV), (0, 0))),
        prow=prow, pcol=pcol, gnorm=jnp.tile(d_g_norm[l], D_HEADS)[None],
        wbr=w_br[l].astype(BF16), wo=w_o[l].astype(BF16), wup=w_up[l].astype(BF16), wdown=w_down[l].astype(BF16))


def _trunk_layer(x2d, mod3, lw, n_seq, t, ctx):
    latent = ctx is not None
    n = x2d.shape[0]
    tm = min(512, t) if latent else 256
    if latent:
        tables = _rope_tables(t, A_HD, A_HEADS) + _rope_tables(t, B_ROPE, B_HEADS)
        proj, qb, ckvn, knv, aqr, akr, qper, krr = _inproj_call(
            x2d, mod3, lw["gpre1"], lw["w1"], lw["gcq"], lw["wuq"], lw["gckv"], lw["wukv"], tables, t, tm)
        ctx_ak, ctx_av, ctx_ckv, ctx_kr, s0f, s0b = ctx
        out_a = _win_attn_call(lw["sink"], aqr, akr, proj, _expand_kv_heads(ctx_ak), _expand_kv_heads(ctx_av),
                               n_seq, t)
        out_b = _mla_lat_call(qb, qper, knv, krr, ctx_ckv, jnp.tile(ctx_kr, (1, 1, B_HEADS)), lw["wukv"],
                              n_seq, t, min(256, t))
        s0 = jnp.stack([s0f, s0b], axis=1)
    else:
        proj, qb, ckvn, knv = _inproj_call(
            x2d, mod3, lw["gpre1"], lw["w1"], lw["gcq"], lw["wuq"], lw["gckv"], lw["wukv"], None, t, tm)
        out_a, out_b = _attn_ctx_call(lw["sink"], proj, qb, knv, n_seq, t)
        s0 = jnp.zeros((n_seq, 2, D_HEADS, D_DK, D_DV), F32)
    out_c, qkvn = _local_call(proj, lw["convw"], lw["wpool"], lw["cscale"], t, tm)
    nc = t // D_CHUNK
    ab_row = proj[:, C_AB:C_AB + 16].reshape(n_seq, nc, D_CHUNK, 16).transpose(0, 1, 3, 2)
    o_f, o_b, s_fin = _gdn_call(qkvn, proj, ab_row, lw["prow"], lw["pcol"], s0, n_seq, t)
    tmm = 256
    x1 = _merge_call(x2d, mod3, lw["gpre1"], out_a, out_b, out_c, o_f, o_b, proj, lw["gnorm"],
                     lw["wg"], lw["wbr"], lw["wo"], lw["gpost1"], t, tmm)
    x2 = _ffn_call(x1, mod3, lw["gpre2"], lw["wup"], lw["wdown"], lw["gpost2"], t, tmm)
    new_ctx = None
    if not latent:
        p3 = proj.reshape(n_seq, t, P1)
        ka = jnp.stack([p3[:, :, C_AK:C_AK + A_HD], p3[:, :, C_AK + 2 * A_HD:C_AK + 3 * A_HD]], axis=2)
        va = jnp.stack([p3[:, :, C_AV:C_AV + A_HD], p3[:, :, C_AV + 2 * A_HD:C_AV + 3 * A_HD]], axis=2)
        new_ctx = (ka, va, ckvn.reshape(n_seq, t, B_KV_RANK), p3[:, :, C_KR:C_KR + B_ROPE],
                   s_fin[:, 0], s_fin[:, 1])
    return x2, new_ctx


def kernel(x_prompt, x_sample, cache_a_k, cache_a_v, cache_b_ckv, cache_b_krope, state_d_fwd, state_d_bwd, c, c_ctx, w_mod, b_mod, g_pre1, g_post1, g_pre2, g_post2, w_in, a_sink, b_g_cq, b_g_ckv, b_w_uq, b_w_ukv, c_w_pool, c_scale, d_conv, d_a_log, d_dt_bias, d_g_norm, w_br, w_o, w_up, w_down):
    depth = w_mod.shape[0]
    bp, tp, d = x_prompt.shape
    bs, ts, _ = x_sample.shape
    assert bs + 1 <= SUBLANES
    cond8 = jnp.zeros((SUBLANES, d), F32).at[0].set(c_ctx).at[1:1 + bs].set(c)
    mod_all = _mod_call(cond8, w_mod, b_mod)
    yp = x_prompt.reshape(bp * tp, d)
    ys = x_sample.reshape(bs * ts, d)
    new = [[] for _ in range(6)]
    for l in range(depth):
        lw = _layer_weights(l, g_pre1, g_post1, g_pre2, g_post2, w_in, a_sink, b_g_cq, b_g_ckv, b_w_uq, b_w_ukv,
                            c_w_pool, c_scale, d_conv, d_a_log, d_dt_bias, d_g_norm, w_br, w_o, w_up, w_down)
        mod_ctx = mod_all[l, 0:1][:, None, :]
        mod_lat = mod_all[l, 1:1 + bs][:, None, :]
        yp, nctx = _trunk_layer(yp, mod_ctx, lw, bp, tp, None)
        for acc, val in zip(new, nctx):
            acc.append(val)
        ctx = (cache_a_k[:, l], cache_a_v[:, l], cache_b_ckv[:, l], cache_b_krope[:, l],
               state_d_fwd[:, l], state_d_bwd[:, l])
        ys, _ = _trunk_layer(ys, mod_lat, lw, bs, ts, ctx)
    return (yp.reshape(bp, tp, d), ys.reshape(bs, ts, d)) + tuple(jnp.stack(v, axis=1) for v in new)
```

```python
import functools
import math

import jax
import jax.numpy as jnp
from jax import lax
from jax.experimental import pallas as pl
from jax.experimental.pallas import tpu as pltpu

F32 = jnp.float32
BF16 = jnp.bfloat16

D_MODEL = 1024
GRID_W = 64
ROPE_BASE = 10000.0
EPS = 1e-6
NEG_INF = -1e30
Q_BLOCK = 128
N_BRANCH = 4
BRANCH_W = 256
A_HEADS = 4
A_KV_HEADS = 2
A_HD = 64
A_WINDOW = 128
A_SCALE = A_HD ** -0.5
B_HEADS = 4
B_Q_RANK = 192
B_KV_RANK = 128
B_NOPE = 64
B_ROPE = 32
B_VD = 64
B_SCALE = (B_NOPE + B_ROPE) ** -0.5
C_GROUPS = 4
C_GW = 64
C_WINDOWS = (2, 4, 8, 16)
D_HEADS = 4
D_DK = 64
D_DV = 64
D_CONV = 5
D_CHUNK = 64
FF_RAW = -(-8 * D_MODEL // 3)
D_FF = -(-FF_RAW // 256) * 256
IN_SIZES = (A_HEADS * A_HD, A_KV_HEADS * A_HD, A_KV_HEADS * A_HD, B_Q_RANK, B_KV_RANK, B_ROPE,
            C_GROUPS * C_GW, D_HEADS * (2 * D_DK + D_DV), D_HEADS * D_DV, 2 * D_HEADS, 2 * D_HEADS,
            N_BRANCH * D_MODEL)

LANES = 128
SUBLANES = 8
VMEM_LIMIT_BYTES = 56 * 1024 * 1024
HALO = SUBLANES
GDN_CHUNKS_PER_STEP = 4
MLA_HEAD_LANES = LANES
MLA_WIDE = B_HEADS * MLA_HEAD_LANES
LOG2E = math.log2(math.e)

C_AQ = 0
C_AK = 256
C_AV = 512
C_CQ = 768
C_CKV = 1024
C_KR = 1152
C_CIN = 1280
C_QKV = 1536
C_Z = 2304
C_AB = 2560
P1 = 2688


def _cparams(sem):
    return pltpu.CompilerParams(dimension_semantics=sem, vmem_limit_bytes=VMEM_LIMIT_BYTES)


def _resident(shape):
    return pl.BlockSpec(tuple(shape), lambda *_: (0,) * len(shape), pipeline_mode=pl.Buffered(1))


def _dot(a, b):
    return jnp.dot(a, b, preferred_element_type=F32)


def _dot_nt(a, b):
    return lax.dot_general(a, b, (((1,), (1,)), ((), ())), preferred_element_type=F32)


def _split3(x):
    hi = x.astype(BF16)
    r1 = x - hi.astype(F32)
    mid = r1.astype(BF16)
    lo = (r1 - mid.astype(F32)).astype(BF16)
    return hi, mid, lo


def _dot_sel_right(x, sel):
    hi, mid, lo = _split3(x)
    return _dot(hi, sel) + _dot(mid, sel) + _dot(lo, sel)


def _silu(x):
    return x * jax.nn.sigmoid(x)


def _softplus(x):
    return jnp.maximum(x, 0.0) + jnp.log1p(jnp.exp(-jnp.abs(x)))


def _head_mask(width, n_lanes, h):
    lane = lax.broadcasted_iota(jnp.int32, (1, n_lanes), 1)
    return (lane // width) == h


def _group_ones(n, width):
    r = lax.broadcasted_iota(jnp.int32, (n, n), 0) // width
    c = lax.broadcasted_iota(jnp.int32, (n, n), 1) // width
    return jnp.where(r == c, 1.0, 0.0).astype(BF16)


def _mod_kernel(c_ref, w_ref, b_ref, o_ref):
    s = _silu(c_ref[...])
    o_ref[0] = _dot(s.astype(BF16), w_ref[0].astype(BF16)) + b_ref[0]


def _mod_call(cond8, w_mod, b_mod):
    depth, d, n = w_mod.shape
    tn = 1536
    return pl.pallas_call(
        _mod_kernel,
        grid=(depth, n // tn),
        in_specs=[pl.BlockSpec((SUBLANES, d), lambda l, j: (0, 0)),
                  pl.BlockSpec((1, d, tn), lambda l, j: (l, 0, j)),
                  pl.BlockSpec((1, 1, tn), lambda l, j: (l, 0, j))],
        out_specs=pl.BlockSpec((1, SUBLANES, tn), lambda l, j: (l, 0, j)),
        out_shape=jax.ShapeDtypeStruct((depth, SUBLANES, n), F32),
        compiler_params=_cparams(("parallel", "parallel")),
        name="mod",
    )(cond8, w_mod, b_mod.reshape(depth, 1, n))


def _rmsnorm_rows(x, g, n):
    ms = jnp.sum(x * x, axis=-1, keepdims=True) * (1.0 / n)
    return x * lax.rsqrt(ms + EPS) * g


def _rope(x, cos, sin_next, sin_prev, quarter):
    n = x.shape[-1]
    return x * cos + pltpu.roll(x, n - quarter, 1) * sin_next + pltpu.roll(x, quarter, 1) * sin_prev


def _mla_wide_kr(kr_rep):
    lane = lax.broadcasted_iota(jnp.int32, (1, MLA_HEAD_LANES), 1)
    keep = (lane >= B_NOPE) & (lane < B_NOPE + B_ROPE)
    return jnp.concatenate([jnp.where(keep, kr_rep, 0.0)] * B_HEADS, axis=1)


def _inproj_kernel(latent, *refs):
    if latent:
        (x_ref, mod_ref, gpre_ref, w1_ref, gcq_ref, wuq_ref, gckv_ref, wukv_ref,
         ca_ref, sna_ref, spa_ref, cb_ref, snb_ref, spb_ref,
         proj_ref, ckv_ref, qw_ref, kw_ref, vb_ref, aqr_ref, akr_ref, qwr_ref) = refs
    else:
        (x_ref, mod_ref, gpre_ref, w1_ref, gcq_ref, wuq_ref, gckv_ref, wukv_ref,
         proj_ref, ckv_ref, qw_ref, kw_ref, vb_ref) = refs
    d = D_MODEL
    x = x_ref[...]
    mod = mod_ref[0]
    sh1 = mod[:, 0:d]
    sc1 = mod[:, d:2 * d]
    h = _rmsnorm_rows(x, gpre_ref[...], d) * (1.0 + sc1) + sh1
    proj = _dot(h.astype(BF16), w1_ref[...])
    proj_ref[...] = proj
    cqn = _rmsnorm_rows(proj[:, C_CQ:C_CQ + 256], gcq_ref[...], B_Q_RANK)
    qw = _dot(cqn.astype(BF16), wuq_ref[...])
    qw_ref[...] = qw.astype(BF16)
    ckvn = _rmsnorm_rows(proj[:, C_CKV:C_CKV + B_KV_RANK], gckv_ref[...], B_KV_RANK)
    ckv_ref[...] = ckvn
    knv = _dot(ckvn.astype(BF16), wukv_ref[...])
    vb_ref[...] = knv[:, MLA_WIDE:].astype(BF16)
    kr_wide = _mla_wide_kr(proj[:, C_KR:C_KR + 128])
    if latent:
        ca, sna, spa = ca_ref[...], sna_ref[...], spa_ref[...]
        cb, snb, spb = cb_ref[...], snb_ref[...], spb_ref[...]
        aqr_ref[...] = _rope(proj[:, C_AQ:C_AQ + 256], ca, sna, spa, A_HD // 4).astype(BF16)
        akr_ref[...] = _rope(proj[:, C_AK:C_AK + 256], ca, sna, spa, A_HD // 4).astype(BF16)
        qwr_ref[...] = _rope(qw, cb, snb, spb, B_ROPE // 4).astype(BF16)
        kr_wide = _rope(kr_wide, cb, snb, spb, B_ROPE // 4)
    kw_ref[...] = (knv[:, :MLA_WIDE] + kr_wide).astype(BF16)


def _inproj_call(x2d, mod3, gpre, w1, gcq, wuq, gckv, wukv, tables, seq_len, tm):
    n, d = x2d.shape
    latent = tables is not None
    tiles_per_seq = seq_len // tm if latent else 1
    n_mod = mod3.shape[0]
    if n_mod == 1:
        mod_map = lambda i: (0, 0, 0)
    else:
        mod_map = lambda i: (i // tiles_per_seq, 0, 0)
    full = _resident
    row = lambda w, dt=F32: (pl.BlockSpec((tm, w), lambda i: (i, 0)), jax.ShapeDtypeStruct((n, w), dt))
    in_specs = [pl.BlockSpec((tm, d), lambda i: (i, 0)),
                pl.BlockSpec((1, 1, 6 * d), mod_map),
                full((1, d)), full((d, P1)), full((1, 256)), full(wuq.shape),
                full((1, B_KV_RANK)), full(wukv.shape)]
    args = [x2d, mod3, gpre, w1, gcq, wuq, gckv, wukv]
    outs = [row(P1), row(B_KV_RANK), row(MLA_WIDE, BF16), row(MLA_WIDE, BF16), row(256, BF16)]
    if latent:
        for t in tables:
            in_specs.append(pl.BlockSpec((tm, t.shape[1]), lambda i: (i % tiles_per_seq, 0)))
            args.append(t)
        outs += [row(256, BF16), row(256, BF16), row(MLA_WIDE, BF16)]
    return pl.pallas_call(
        functools.partial(_inproj_kernel, latent),
        grid=(n // tm,),
        in_specs=in_specs,
        out_specs=[o[0] for o in outs],
        out_shape=[o[1] for o in outs],
        compiler_params=_cparams(("parallel",)),
        name="inproj_lat" if latent else "inproj_ctx",
    )(*args)


def _softmax_pv(scores, values, scale, sink, hm):
    c1 = scale * LOG2E
    m = jnp.max(scores[0], axis=-1, keepdims=True)
    for s in scores[1:]:
        m = jnp.maximum(m, jnp.max(s, axis=-1, keepdims=True))
    m2 = m * c1
    if sink is not None:
        m2 = jnp.maximum(m2, sink * LOG2E)
    den = None
    pv = None
    for s, v in zip(scores, values):
        p = jnp.exp2(s * c1 - m2)
        ps = jnp.sum(p, axis=-1, keepdims=True)
        den = ps if den is None else den + ps
        t = _dot(p.astype(BF16), v)
        pv = t if pv is None else pv + t
    if sink is not None:
        den = den + jnp.exp2(sink * LOG2E - m2)
    return jnp.where(hm, pv / den, 0.0)


def _attn_ctx_kernel(sink_ref, q_ref, k_ref, v_ref, qw_ref, kw_ref, vb_ref, oa_ref, ob_ref):
    q = q_ref[...]
    k = k_ref[...].astype(BF16)
    v = v_ref[...].astype(BF16)
    acc = jnp.zeros(q.shape, F32)
    for h in range(A_HEADS):
        hm = _head_mask(A_HD, 256, h)
        s = _dot_nt(jnp.where(hm, q, 0.0).astype(BF16), k)
        acc = acc + _softmax_pv([s], [v], A_SCALE, sink_ref[h], hm)
    oa_ref[...] = acc
    vb = vb_ref[...]
    acc = jnp.zeros((q.shape[0], B_HEADS * B_VD), F32)
    for h in range(B_HEADS):
        hs = slice(h * MLA_HEAD_LANES, (h + 1) * MLA_HEAD_LANES)
        s = _dot_nt(qw_ref[:, hs], kw_ref[:, hs])
        acc = acc + _softmax_pv([s], [vb], B_SCALE, None, _head_mask(B_VD, 256, h))
    ob_ref[...] = acc


def _attn_ctx_call(sink, proj, qw, kw, vb, n_seq, t):
    n = proj.shape[0]
    col = lambda w, j: pl.BlockSpec((t, w), lambda b: (b, j))
    out = (pl.BlockSpec((t, 256), lambda b: (b, 0)), jax.ShapeDtypeStruct((n, 256), F32))
    return pl.pallas_call(
        _attn_ctx_kernel,
        grid=(n_seq,),
        in_specs=[pl.BlockSpec(memory_space=pltpu.SMEM),
                  col(256, C_AQ // 256), col(256, C_AK // 256), col(256, C_AV // 256),
                  col(MLA_WIDE, 0), col(MLA_WIDE, 0), col(256, 0)],
        out_specs=[out[0], out[0]],
        out_shape=[out[1], out[1]],
        compiler_params=_cparams(("parallel",)),
        name="attn_ctx",
    )(sink, proj, proj, proj, qw, kw, vb)


def _win_attn_kernel(t, sink_ref, qr_ref, q_ref, kp_ref, kc_ref, kn_ref, vp_ref, vc_ref, vn_ref,
                     kctx_ref, vctx_ref, o_ref):
    i = pl.program_id(1)
    qr = qr_ref[...]
    q = q_ref[...]
    kl = jnp.concatenate([kp_ref[...], kc_ref[...], kn_ref[...]], axis=0)
    vl = jnp.concatenate([vp_ref[...], vc_ref[...], vn_ref[...]], axis=0).astype(BF16)
    kctx = kctx_ref[0].astype(BF16)
    vctx = vctx_ref[0].astype(BF16)
    q_pos = i * Q_BLOCK + lax.broadcasted_iota(jnp.int32, (Q_BLOCK, 3 * Q_BLOCK), 0)
    k_pos = (i - 1) * Q_BLOCK + lax.broadcasted_iota(jnp.int32, (Q_BLOCK, 3 * Q_BLOCK), 1)
    valid = (jnp.abs(q_pos - k_pos) <= A_WINDOW) & (k_pos >= 0) & (k_pos < t)
    acc = jnp.zeros(q.shape, F32)
    for h in range(A_HEADS):
        hm = _head_mask(A_HD, 256, h)
        s_loc = jnp.where(valid, _dot_nt(jnp.where(hm, qr, 0.0).astype(BF16), kl), NEG_INF)
        s_ctx = _dot_nt(jnp.where(hm, q, 0.0).astype(BF16), kctx)
        acc = acc + _softmax_pv([s_loc, s_ctx], [vl, vctx], A_SCALE, sink_ref[h], hm)
    o_ref[...] = acc


def _win_attn_call(sink, aqr, akr, proj, kctx, vctx, n_seq, t):
    n = proj.shape[0]
    nb = t // Q_BLOCK
    blk = lambda j, f: pl.BlockSpec((Q_BLOCK, 256), lambda b, i: (b * nb + f(i), j))
    prev = lambda i: jnp.maximum(i - 1, 0)
    cur = lambda i: i
    nxt = lambda i: jnp.minimum(i + 1, nb - 1)
    ctx = pl.BlockSpec((1,) + kctx.shape[1:], lambda b, i: (b, 0, 0))
    return pl.pallas_call(
        functools.partial(_win_attn_kernel, t),
        grid=(n_seq, nb),
        in_specs=[pl.BlockSpec(memory_space=pltpu.SMEM),
                  blk(0, cur), blk(C_AQ // 256, cur),
                  blk(0, prev), blk(0, cur), blk(0, nxt),
                  blk(C_AV // 256, prev), blk(C_AV // 256, cur), blk(C_AV // 256, nxt),
                  ctx, ctx],
        out_specs=pl.BlockSpec((Q_BLOCK, 256), lambda b, i: (b * nb + i, 0)),
        out_shape=jax.ShapeDtypeStruct((n, 256), F32),
        compiler_params=_cparams(("parallel", "parallel")),
        name="win_attn",
    )(sink, aqr, proj, akr, akr, akr, proj, proj, proj, kctx, vctx)


def _mla_lat_kernel(qw_ref, qwr_ref, kwr_ref, vb_ref, cckv_ref, ckr_ref, wukv_ref, o_ref):
    kvc = _dot(cckv_ref[0].astype(BF16), wukv_ref[...])
    kwc = (kvc[:, :MLA_WIDE] + ckr_ref[0]).astype(BF16)
    vc = kvc[:, MLA_WIDE:].astype(BF16)
    v = vb_ref[...]
    acc = jnp.zeros((qw_ref.shape[0], B_HEADS * B_VD), F32)
    for h in range(B_HEADS):
        hs = slice(h * MLA_HEAD_LANES, (h + 1) * MLA_HEAD_LANES)
        s_lat = _dot_nt(qwr_ref[:, hs], kwr_ref[:, hs])
        s_ctx = _dot_nt(qw_ref[:, hs], kwc[:, hs])
        acc = acc + _softmax_pv([s_lat, s_ctx], [v, vc], B_SCALE, None, _head_mask(B_VD, 256, h))
    o_ref[...] = acc


def _mla_lat_call(qw, qwr, kwr, vb, cckv, ckr_wide, wukv, n_seq, t, tq):
    n = qw.shape[0]
    nq = t // tq
    qblk = lambda w: pl.BlockSpec((tq, w), lambda b, i: (b * nq + i, 0))
    seq = lambda w: pl.BlockSpec((t, w), lambda b, i: (b, 0))
    ctx = lambda a: pl.BlockSpec((1,) + a.shape[1:], lambda b, i: (b, 0, 0))
    return pl.pallas_call(
        _mla_lat_kernel,
        grid=(n_seq, nq),
        in_specs=[qblk(MLA_WIDE), qblk(MLA_WIDE), seq(MLA_WIDE), seq(256), ctx(cckv), ctx(ckr_wide),
                  pl.BlockSpec(wukv.shape, lambda b, i: (0, 0))],
        out_specs=pl.BlockSpec((tq, 256), lambda b, i: (b * nq + i, 0)),
        out_shape=jax.ShapeDtypeStruct((n, 256), F32),
        compiler_params=_cparams(("parallel", "parallel")),
        name="mla_lat",
    )(qw, qwr, kwr, vb, cckv, ckr_wide, wukv)


def _with_halo(prev_ref, cur_ref, next_ref, has_prev, has_next):
    prev = jnp.where(has_prev, prev_ref[...], 0.0)
    nxt = jnp.where(has_next, next_ref[...], 0.0)
    return jnp.concatenate([prev, cur_ref[...], nxt], axis=0)


def _shift_rows(x, k):
    n = x.shape[0]
    return pltpu.roll(x, (-k) % n, 0)


def _local_kernel(t, tm, cp_ref, cc_ref, cn_ref, qp_ref, qc_ref, qn_ref, convw_ref, wpool_ref, cscale_ref,
                  oc_ref, qkv_ref):
    i = pl.program_id(0)
    tiles_per_seq = t // tm
    j = i % tiles_per_seq
    has_prev = j > 0
    has_next = j < tiles_per_seq - 1
    x = _with_halo(cp_ref, cc_ref, cn_ref, has_prev, has_next)
    p2 = x + _shift_rows(x, -1)
    p4 = _shift_rows(p2, 1) + _shift_rows(p2, -1)
    p8 = _shift_rows(p4, 2) + _shift_rows(p4, -2)
    p16 = _shift_rows(p8, 4) + _shift_rows(p8, -4)
    grp = lax.broadcasted_iota(jnp.int32, (1, 256), 1) // C_GW
    win = jnp.where(grp == 0, p2, jnp.where(grp == 1, p4, jnp.where(grp == 2, p8, p16)))[HALO:HALO + tm]
    pos = j * tm + lax.broadcasted_iota(jnp.int32, (tm, 256), 0)
    half = jnp.where(grp == 0, 1, jnp.where(grp == 1, 2, jnp.where(grp == 2, 4, 8)))
    cnt = (jnp.minimum(pos + half, t) - jnp.maximum(pos - half, 0)).astype(F32)
    y = win / cnt - cc_ref[...]
    oc_ref[...] = _dot(y.astype(BF16), wpool_ref[...]) * cscale_ref[...]
    xq = _with_halo(qp_ref, qc_ref, qn_ref, has_prev, has_next)
    w = convw_ref[...]
    pad = D_CONV // 2
    acc = None
    for tap in range(D_CONV):
        term = _shift_rows(xq, tap - pad) * w[tap:tap + 1, :]
        acc = term if acc is None else acc + term
    u = _silu(acc[HALO:HALO + tm])
    ones = _group_ones(256, D_DK)
    q = u[:, 0:256]
    k = u[:, 256:512]
    q = q * lax.rsqrt(_dot_sel_right(q * q, ones) + EPS) * (D_DK ** -0.5)
    k = k * lax.rsqrt(_dot_sel_right(k * k, ones) + EPS)
    qkv_ref[:, 0:256] = q
    qkv_ref[:, 256:512] = k
    qkv_ref[:, 512:768] = u[:, 512:768]


def _local_call(proj, convw8, wpool_bd, cscale, t, tm):
    n = proj.shape[0]
    hb = tm // HALO
    nblk = n // HALO
    cur = lambda w, c: pl.BlockSpec((tm, w), lambda i: (i, c))
    prev = lambda w, c: pl.BlockSpec((HALO, w), lambda i: (jnp.maximum(i * hb - 1, 0), c))
    nxt = lambda w, c: pl.BlockSpec((HALO, w), lambda i: (jnp.minimum((i + 1) * hb, nblk - 1), c))
    full = lambda a: _resident(a.shape)
    cc, qc = C_CIN // 256, C_QKV // 768
    return pl.pallas_call(
        functools.partial(_local_kernel, t, tm),
        grid=(n // tm,),
        in_specs=[prev(256, cc), cur(256, cc), nxt(256, cc), prev(768, qc), cur(768, qc), nxt(768, qc),
                  full(convw8), full(wpool_bd), full(cscale)],
        out_specs=[pl.BlockSpec((tm, 256), lambda i: (i, 0)), pl.BlockSpec((tm, 768), lambda i: (i, 0))],
        out_shape=[jax.ShapeDtypeStruct((n, 256), F32), jax.ShapeDtypeStruct((n, 768), F32)],
        compiler_params=_cparams(("parallel",)),
        name="local",
    )(proj, proj, proj, proj, proj, proj, convw8, wpool_bd, cscale)


def _gdn_kernel(n_steps, g_chunks, xf_ref, xb_ref, abf_ref, abb_ref, abrf_ref, abrb_ref, prow_ref, pcol_ref,
                s0_ref, of_ref, ob_ref, sfin_ref, s_ref):
    i = pl.program_id(1)
    c = D_CHUNK
    nh = D_HEADS
    nb = 2 * g_chunks * nh

    @pl.when(i == 0)
    def _():
        s_ref[...] = s0_ref[0].reshape(2 * nh, D_DK, D_DV)

    r = lax.broadcasted_iota(jnp.int32, (c, c), 0)
    cidx = lax.broadcasted_iota(jnp.int32, (c, c), 1)
    lower = jnp.where(cidx <= r, 1.0, 0.0).astype(BF16)
    upper = jnp.where(cidx >= r, 1.0, 0.0).astype(BF16)
    prow = prow_ref[...]
    pcol = pcol_ref[...]

    eye = jnp.where(cidx == r, 1.0, 0.0).astype(BF16)
    sel_lower = jnp.concatenate([lower] * 3, axis=1)
    sel_upper = jnp.concatenate([upper] * 3, axis=1)
    sel_lower_t = jnp.concatenate([lower] * 3, axis=0)
    sel_upper_t = jnp.concatenate([upper] * 3, axis=0)

    q_l, k_l, v_l, b_l, gcol_l, grow_l = [], [], [], [], [], []
    for d, (x_ref, ab_ref, abr_ref) in enumerate(((xf_ref, abf_ref, abrf_ref), (xb_ref, abb_ref, abrb_ref))):
        for gi in range(g_chunks):
            rows = slice(gi * c, (gi + 1) * c)
            x = x_ref[rows, :]
            ab = ab_ref[rows, :]
            abr = abr_ref[0, gi]
            beta_c = jax.nn.sigmoid(ab)
            g_c = -jnp.exp(prow[0:1, :]) * _softplus(ab + prow[1:2, :])
            g_r = -jnp.exp(pcol[:, 0:1]) * _softplus(abr + pcol[:, 1:2])
            if d == 0:
                gc_c = _dot(sel_lower, jnp.concatenate(_split3(g_c), axis=0))
                gc_r = _dot(jnp.concatenate(_split3(g_r), axis=1), sel_upper_t)
            else:
                gc_c = _dot(sel_upper, jnp.concatenate(_split3(g_c), axis=0))
                gc_r = _dot(jnp.concatenate(_split3(g_r), axis=1), sel_lower_t)
            for h in range(nh):
                hs = slice(h * D_DK, (h + 1) * D_DK)
                lane = nh * d + h
                q_l.append(x[:, 0:256][:, hs])
                k_l.append(x[:, 256:512][:, hs])
                v_l.append(x[:, 512:768][:, hs])
                b_l.append(beta_c[:, lane:lane + 1])
                gcol_l.append(gc_c[:, 8 + lane:9 + lane])
                grow_l.append(gc_r[8 + lane:9 + lane, :])
    qh, kh, vh = jnp.stack(q_l), jnp.stack(k_l), jnp.stack(v_l)
    bcol, gcol, grow = jnp.stack(b_l), jnp.stack(gcol_l), jnp.stack(grow_l)
    is_fwd = lax.broadcasted_iota(jnp.int32, (nb, c, c), 0) < nb // 2
    r3 = lax.broadcasted_iota(jnp.int32, (nb, c, c), 1)
    c3 = lax.broadcasted_iota(jnp.int32, (nb, c, c), 2)
    ahead = jnp.where(is_fwd, r3 - c3, c3 - r3)
    incl = ahead >= 0
    strict = ahead > 0
    glast = jnp.concatenate([gcol[:nb // 2, c - 1:c, :], gcol[nb // 2:, 0:1, :]], axis=0)
    bdot = lambda a, b: jnp.einsum('bik,bkj->bij', a, b, preferred_element_type=F32)
    bdot_nt = lambda a, b: jnp.einsum('bik,bjk->bij', a, b, preferred_element_type=F32)

    decay = jnp.exp(jnp.where(incl, gcol - grow, NEG_INF))
    kb = kh * bcol
    khb = kh.astype(BF16)
    kq = bdot_nt(jnp.concatenate([kb, qh], axis=1).astype(BF16), khb)
    p = -jnp.where(strict, kq[:, 0:c] * decay, 0.0)
    egc = jnp.exp(gcol)
    xs = jnp.concatenate([vh * bcol, kb * egc], axis=2)
    for step in range(6):
        ph = p.astype(BF16)
        pl_ = (p - ph.astype(F32)).astype(BF16)
        half = jnp.concatenate([ph, pl_], axis=2)
        lhs = jnp.concatenate([half, half], axis=2)
        y = jnp.concatenate([xs, p], axis=2) if step < 5 else xs
        yh = y.astype(BF16)
        yl = (y - yh.astype(F32)).astype(BF16)
        m = bdot(lhs, jnp.concatenate([yh, yh, yl, yl], axis=1))
        xs = xs + m[:, :, 0:2 * D_DK]
        if step < 5:
            p = m[:, :, 2 * D_DK:3 * D_DK]
    u = xs[:, :, 0:D_DV]
    w = xs[:, :, D_DV:2 * D_DV]
    a = kq[:, c:2 * c] * decay
    kd = (kh * jnp.exp(glast - gcol)).astype(BF16)
    kdt = bdot_nt(jnp.broadcast_to(eye, (nb, c, c)), kd).astype(BF16)
    lhs_s = jnp.concatenate([w, qh * egc], axis=1).astype(BF16)
    lhs_v = jnp.concatenate([a.astype(BF16), kdt], axis=1)
    eg = jnp.exp(glast)

    def chains(arr, step):
        lo_f = step * nh
        lo_b = nb // 2 + (g_chunks - 1 - step) * nh
        return jnp.concatenate([arr[lo_f:lo_f + nh], arr[lo_b:lo_b + nh]], axis=0)

    s = s_ref[...]
    outs_f, outs_b = [None] * g_chunks, [None] * g_chunks
    for step in range(g_chunks):
        ws_qs = bdot(chains(lhs_s, step), s.astype(BF16))
        vb = (chains(u, step) - ws_qs[:, 0:c]).astype(BF16)
        av = bdot(chains(lhs_v, step), vb)
        o = ws_qs[:, c:2 * c] + av[:, 0:c]
        s = s * chains(eg, step) + av[:, c:2 * c]
        outs_f[step] = jnp.concatenate([o[h] for h in range(nh)], axis=1)
        outs_b[g_chunks - 1 - step] = jnp.concatenate([o[nh + h] for h in range(nh)], axis=1)
    of_ref[...] = jnp.concatenate(outs_f, axis=0)
    ob_ref[...] = jnp.concatenate(outs_b, axis=0)
    s_ref[...] = s

    @pl.when(i == n_steps - 1)
    def _():
        sfin_ref[0] = s.reshape(2, nh, D_DK, D_DV)


def _gdn_call(qkvn, proj, ab_row, prow, pcol, s0, n_seq, t):
    n = qkvn.shape[0]
    g = GDN_CHUNKS_PER_STEP
    ns = t // (D_CHUNK * g)
    rows = D_CHUNK * g
    fwd = lambda b, i: b * ns + i
    bwd = lambda b, i: b * ns + ns - 1 - i
    xblk = lambda f: pl.BlockSpec((rows, 768), lambda b, i: (f(b, i), 0))
    abblk = lambda f: pl.BlockSpec((rows, 128), lambda b, i: (f(b, i), C_AB // 128))
    abrblk = lambda f: pl.BlockSpec((1, g, 16, D_CHUNK), lambda b, i: (b, f(0, i), 0, 0))
    oblk = lambda f: pl.BlockSpec((rows, 256), lambda b, i: (f(b, i), 0))
    sblk = pl.BlockSpec((1, 2, D_HEADS, D_DK, D_DV), lambda b, i: (b, 0, 0, 0, 0))
    osd = jax.ShapeDtypeStruct((n, 256), F32)
    return pl.pallas_call(
        functools.partial(_gdn_kernel, ns, g),
        grid=(n_seq, ns),
        in_specs=[xblk(fwd), xblk(bwd), abblk(fwd), abblk(bwd), abrblk(fwd), abrblk(bwd),
                  pl.BlockSpec(prow.shape, lambda b, i: (0, 0)), pl.BlockSpec(pcol.shape, lambda b, i: (0, 0)),
                  sblk],
        out_specs=[oblk(fwd), oblk(bwd), sblk],
        out_shape=[osd, osd, jax.ShapeDtypeStruct(s0.shape, F32)],
        scratch_shapes=[pltpu.VMEM((2 * D_HEADS, D_DK, D_DV), F32)],
        compiler_params=_cparams(("parallel", "arbitrary")),
        name="gdn",
    )(qkvn, qkvn, proj, proj, ab_row, ab_row, prow, pcol, s0)


def _merge_kernel(x_ref, mod_ref, gpre_ref, oa_ref, ob_ref, oc_ref, of_ref, obw_ref, z_ref, gnorm_ref,
                  wg_ref, wbr_ref, wo_ref, gpost_ref, o_ref):
    d = D_MODEL
    x = x_ref[...]
    mod = mod_ref[0]
    sh1, sc1, ga1 = mod[:, 0:d], mod[:, d:2 * d], mod[:, 2 * d:3 * d]
    h = (_rmsnorm_rows(x, gpre_ref[...], d) * (1.0 + sc1) + sh1).astype(BF16)
    o = of_ref[...] + obw_ref[...]
    ms = _dot_sel_right(o * o, _group_ones(256, D_DV)) * (1.0 / D_DV)
    od = o * lax.rsqrt(ms + EPS) * gnorm_ref[...] * _silu(z_ref[...])
    acc = None
    for m, br in enumerate((oa_ref[...], ob_ref[...], oc_ref[...], od)):
        gate = jax.nn.sigmoid(_dot(h, wg_ref[:, m * d:(m + 1) * d]))
        term = gate * _dot(br.astype(BF16), wbr_ref[m])
        acc = term if acc is None else acc + term
    mix = _dot(acc.astype(BF16), wo_ref[...])
    o_ref[...] = x + ga1 * _rmsnorm_rows(mix, gpost_ref[...], d)


def _merge_call(x2d, mod3, gpre, oa, ob, oc, of, obw, proj, gnorm, wg, wbr, wo, gpost, seq_len, tm):
    n, d = x2d.shape
    tiles_per_seq = seq_len // tm
    mod_map = (lambda i: (0, 0, 0)) if mod3.shape[0] == 1 else (lambda i: (i // tiles_per_seq, 0, 0))
    row = lambda w, c=0: pl.BlockSpec((tm, w), lambda i: (i, c))
    full = lambda a: _resident(a.shape)
    return pl.pallas_call(
        _merge_kernel,
        grid=(n // tm,),
        in_specs=[row(d), pl.BlockSpec((1, 1, 6 * d), mod_map), full(gpre),
                  row(256), row(256), row(256), row(256), row(256), row(256, C_Z // 256), full(gnorm),
                  full(wg), full(wbr), full(wo), full(gpost)],
        out_specs=row(d),
        out_shape=jax.ShapeDtypeStruct((n, d), F32),
        compiler_params=_cparams(("parallel",)),
        name="merge",
    )(x2d, mod3, gpre, oa, ob, oc, of, obw, proj, gnorm, wg, wbr, wo, gpost)


def _ffn_kernel(x_ref, mod_ref, gpre_ref, wup_ref, wdown_ref, gpost_ref, o_ref):
    d = D_MODEL
    x = x_ref[...]
    mod = mod_ref[0]
    sh2, sc2, ga2 = mod[:, 3 * d:4 * d], mod[:, 4 * d:5 * d], mod[:, 5 * d:6 * d]
    h = (_rmsnorm_rows(x, gpre_ref[...], d) * (1.0 + sc2) + sh2).astype(BF16)
    up = _dot(h, wup_ref[...])
    act = _silu(up[:, 0:D_FF]) * up[:, D_FF:2 * D_FF]
    f = _dot(act.astype(BF16), wdown_ref[...])
    o_ref[...] = x + ga2 * _rmsnorm_rows(f, gpost_ref[...], d)


def _ffn_call(x2d, mod3, gpre, wup, wdown, gpost, seq_len, tm):
    n, d = x2d.shape
    tiles_per_seq = seq_len // tm
    mod_map = (lambda i: (0, 0, 0)) if mod3.shape[0] == 1 else (lambda i: (i // tiles_per_seq, 0, 0))
    row = lambda w: pl.BlockSpec((tm, w), lambda i: (i, 0))
    full = lambda a: _resident(a.shape)
    return pl.pallas_call(
        _ffn_kernel,
        grid=(n // tm,),
        in_specs=[row(d), pl.BlockSpec((1, 1, 6 * d), mod_map), full(gpre), full(wup), full(wdown), full(gpost)],
        out_specs=row(d),
        out_shape=jax.ShapeDtypeStruct((n, d), F32),
        compiler_params=_cparams(("parallel",)),
        name="ffn",
    )(x2d, mod3, gpre, wup, wdown, gpost)


def _rope_tables(n_tokens, dim, reps):
    n_rows = n_tokens // GRID_W
    row = jnp.repeat(jnp.arange(n_rows), GRID_W).astype(F32)
    col = jnp.tile(jnp.arange(GRID_W), n_rows).astype(F32)
    nfreq = dim // 4
    inv = ROPE_BASE ** (-jnp.arange(nfreq, dtype=F32) / nfreq)
    ang_r = row[:, None] * inv
    ang_c = col[:, None] * inv
    ang = jnp.concatenate([ang_r, ang_r, ang_c, ang_c], axis=-1)
    cos, sin = jnp.cos(ang), jnp.sin(ang)
    even = ((jnp.arange(dim) // nfreq) % 2 == 0)[None, :]
    sin_next = jnp.where(even, -sin, 0.0)
    sin_prev = jnp.where(even, 0.0, sin)
    return tuple(jnp.tile(a, (1, reps)) for a in (cos, sin_next, sin_prev))


def _rope_tables_mla_wide(n_tokens):
    cos, sin_next, sin_prev = _rope_tables(n_tokens, B_ROPE, 1)
    tail = MLA_HEAD_LANES - B_NOPE - B_ROPE
    wide = lambda a, fill: jnp.tile(jnp.pad(a, ((0, 0), (B_NOPE, tail)), constant_values=fill), (1, B_HEADS))
    return wide(cos, 1.0), wide(sin_next, 0.0), wide(sin_prev, 0.0)


def _split_cols(w):
    out, acc = [], 0
    for s in IN_SIZES:
        out.append(w[:, acc:acc + s])
        acc += s
    return out


def _pack_w_in(w):
    (a_q, a_k, a_v, b_cq, b_ckv, b_kr, c_in, d_qkv, d_z, d_beta, d_alpha, gate) = _split_cols(w)
    d = w.shape[0]
    z = lambda n: jnp.zeros((d, n), w.dtype)
    k0, k1 = a_k[:, :A_HD], a_k[:, A_HD:]
    v0, v1 = a_v[:, :A_HD], a_v[:, A_HD:]
    w1 = jnp.concatenate([a_q, k0, k0, k1, k1, v0, v0, v1, v1, b_cq, z(256 - B_Q_RANK), b_ckv,
                          b_kr, b_kr, b_kr, b_kr, c_in, d_qkv, d_z, d_beta, d_alpha, z(112)], axis=1)
    return w1.astype(BF16), gate.astype(BF16)


def _expand_kv_heads(a):
    return jnp.repeat(a, A_HEADS // A_KV_HEADS, axis=2).reshape(a.shape[0], a.shape[1], A_HEADS * A_HD)


def _layer_weights(l, g_pre1, g_post1, g_pre2, g_post2, w_in, a_sink, b_g_cq, b_g_ckv, b_w_uq, b_w_ukv,
                   c_w_pool, c_scale, d_conv, d_a_log, d_dt_bias, d_g_norm, w_br, w_o, w_up, w_down):
    w1, wg = _pack_w_in(w_in[l])
    wuq = b_w_uq[l].reshape(B_Q_RANK, B_HEADS, B_NOPE + B_ROPE)
    wuq = jnp.pad(wuq, ((0, 256 - B_Q_RANK), (0, 0), (0, MLA_HEAD_LANES - B_NOPE - B_ROPE)))
    wuq = wuq.reshape(256, MLA_WIDE).astype(BF16)
    wukv = b_w_ukv[l].reshape(B_KV_RANK, B_HEADS, B_NOPE + B_VD)
    wk = jnp.pad(wukv[:, :, :B_NOPE], ((0, 0), (0, 0), (0, MLA_HEAD_LANES - B_NOPE))).reshape(B_KV_RANK, MLA_WIDE)
    wukv = jnp.concatenate([wk, wukv[:, :, B_NOPE:].reshape(B_KV_RANK, -1)], 1).astype(BF16)
    wpool = jnp.zeros((256, 256), F32)
    for g in range(C_GROUPS):
        wpool = wpool.at[g * C_GW:(g + 1) * C_GW, g * C_GW:(g + 1) * C_GW].set(c_w_pool[l, g])
    alog = d_a_log[l].reshape(8)
    dtb = d_dt_bias[l].reshape(8)
    prow = jnp.zeros((2, LANES), F32).at[0, 8:16].set(alog).at[1, 8:16].set(dtb)
    pcol = jnp.zeros((16, 2), F32).at[8:16, 0].set(alog).at[8:16, 1].set(dtb)
    return dict(
        w1=w1, wg=wg, wuq=wuq, wukv=wukv,
        gpre1=g_pre1[l][None], gpost1=g_post1[l][None], gpre2=g_pre2[l][None], gpost2=g_post2[l][None],
        gcq=jnp.pad(b_g_cq[l], (0, 256 - B_Q_RANK))[None], gckv=b_g_ckv[l][None],
        sink=a_sink[l], wpool=wpool.astype(BF16), cscale=c_scale[l][None],
        convw=jnp.pad(d_conv[l], ((0, SUBLANES - D_CONV), (0, 0))),
        prow=prow, pcol=pcol, gnorm=jnp.tile(d_g_norm[l], D_HEADS)[None],
        wbr=w_br[l].astype(BF16), wo=w_o[l].astype(BF16), wup=w_up[l].astype(BF16), wdown=w_down[l].astype(BF16))


def _trunk_layer(x2d, mod3, lw, n_seq, t, ctx):
    latent = ctx is not None
    n = x2d.shape[0]
    tm = min(512, t) if latent else 256
    tm_dense = min(512, n)
    if latent:
        tables = _rope_tables(t, A_HD, A_HEADS) + _rope_tables_mla_wide(t)
        proj, ckvn, qw, kwr, vb, aqr, akr, qwr = _inproj_call(
            x2d, mod3, lw["gpre1"], lw["w1"], lw["gcq"], lw["wuq"], lw["gckv"], lw["wukv"], tables, t, tm_dense)
        ctx_ak, ctx_av, ctx_ckv, ctx_kr, s0f, s0b = ctx
        out_a = _win_attn_call(lw["sink"], aqr, akr, proj, _expand_kv_heads(ctx_ak), _expand_kv_heads(ctx_av),
                               n_seq, t)
        ckr_wide = jnp.tile(jnp.pad(ctx_kr, ((0, 0), (0, 0), (B_NOPE, MLA_HEAD_LANES - B_NOPE - B_ROPE))),
                            (1, 1, B_HEADS))
        out_b = _mla_lat_call(qw, qwr, kwr, vb, ctx_ckv, ckr_wide, lw["wukv"], n_seq, t, min(256, t))
        s0 = jnp.stack([s0f, s0b], axis=1)
    else:
        proj, ckvn, qw, kw, vb = _inproj_call(
            x2d, mod3, lw["gpre1"], lw["w1"], lw["gcq"], lw["wuq"], lw["gckv"], lw["wukv"], None, t, tm_dense)
        out_a, out_b = _attn_ctx_call(lw["sink"], proj, qw, kw, vb, n_seq, t)
        s0 = jnp.zeros((n_seq, 2, D_HEADS, D_DK, D_DV), F32)
    out_c, qkvn = _local_call(proj, lw["convw"], lw["wpool"], lw["cscale"], t, tm)
    nc = t // D_CHUNK
    ab_row = proj[:, C_AB:C_AB + 16].reshape(n_seq, nc, D_CHUNK, 16).transpose(0, 1, 3, 2)
    o_f, o_b, s_fin = _gdn_call(qkvn, proj, ab_row, lw["prow"], lw["pcol"], s0, n_seq, t)
    x1 = _merge_call(x2d, mod3, lw["gpre1"], out_a, out_b, out_c, o_f, o_b, proj, lw["gnorm"],
                     lw["wg"], lw["wbr"], lw["wo"], lw["gpost1"], t, tm_dense)
    x2 = _ffn_call(x1, mod3, lw["gpre2"], lw["wup"], lw["wdown"], lw["gpost2"], t, tm_dense)
    new_ctx = None
    if not latent:
        p3 = proj.reshape(n_seq, t, P1)
        ka = jnp.stack([p3[:, :, C_AK:C_AK + A_HD], p3[:, :, C_AK + 2 * A_HD:C_AK + 3 * A_HD]], axis=2)
        va = jnp.stack([p3[:, :, C_AV:C_AV + A_HD], p3[:, :, C_AV + 2 * A_HD:C_AV + 3 * A_HD]], axis=2)
        new_ctx = (ka, va, ckvn.reshape(n_seq, t, B_KV_RANK), p3[:, :, C_KR:C_KR + B_ROPE],
                   s_fin[:, 0], s_fin[:, 1])
    return x2, new_ctx


def kernel(x_prompt, x_sample, cache_a_k, cache_a_v, cache_b_ckv, cache_b_krope, state_d_fwd, state_d_bwd, c, c_ctx, w_mod, b_mod, g_pre1, g_post1, g_pre2, g_post2, w_in, a_sink, b_g_cq, b_g_ckv, b_w_uq, b_w_ukv, c_w_pool, c_scale, d_conv, d_a_log, d_dt_bias, d_g_norm, w_br, w_o, w_up, w_down):
    depth = w_mod.shape[0]
    bp, tp, d = x_prompt.shape
    bs, ts, _ = x_sample.shape
    assert bs + 1 <= SUBLANES
    cond8 = jnp.zeros((SUBLANES, d), F32).at[0].set(c_ctx).at[1:1 + bs].set(c)
    mod_all = _mod_call(cond8, w_mod, b_mod)
    yp = x_prompt.reshape(bp * tp, d)
    ys = x_sample.reshape(bs * ts, d)
    new = [[] for _ in range(6)]
    for l in range(depth):
        lw = _layer_weights(l, g_pre1, g_post1, g_pre2, g_post2, w_in, a_sink, b_g_cq, b_g_ckv, b_w_uq, b_w_ukv,
                            c_w_pool, c_scale, d_conv, d_a_log, d_dt_bias, d_g_norm, w_br, w_o, w_up, w_down)
        mod_ctx = mod_all[l, 0:1][:, None, :]
        mod_lat = mod_all[l, 1:1 + bs][:, None, :]
        yp, nctx = _trunk_layer(yp, mod_ctx, lw, bp, tp, None)
        for acc, val in zip(new, nctx):
            acc.append(val)
        ctx = (cache_a_k[:, l], cache_a_v[:, l], cache_b_ckv[:, l], cache_b_krope[:, l],
               state_d_fwd[:, l], state_d_bwd[:, l])
        ys, _ = _trunk_layer(ys, mod_lat, lw, bs, ts, ctx)
    return (yp.reshape(bp, tp, d), ys.reshape(bs, ts, d)) + tuple(jnp.stack(v, axis=1) for v in new)
```

```python
import functools
import math
from typing import NamedTuple

import jax
import jax.numpy as jnp
from jax import lax
from jax.experimental import pallas as pl
from jax.experimental.pallas import tpu as pltpu

F32 = jnp.float32
BF16 = jnp.bfloat16

D_MODEL = 1024
GRID_W = 64
ROPE_BASE = 10000.0
EPS = 1e-6
NEG_INF = -1e30
Q_BLOCK = 128
N_BRANCH = 4
BRANCH_W = 256
A_HEADS = 4
A_KV_HEADS = 2
A_HD = 64
A_WINDOW = 128
A_SCALE = A_HD ** -0.5
B_HEADS = 4
B_Q_RANK = 192
B_KV_RANK = 128
B_NOPE = 64
B_ROPE = 32
B_VD = 64
B_SCALE = (B_NOPE + B_ROPE) ** -0.5
C_GROUPS = 4
C_GW = 64
C_WINDOWS = (2, 4, 8, 16)
D_HEADS = 4
D_DK = 64
D_DV = 64
D_CONV = 5
D_CHUNK = 64
FF_RAW = -(-8 * D_MODEL // 3)
D_FF = -(-FF_RAW // 256) * 256
IN_SIZES = (A_HEADS * A_HD, A_KV_HEADS * A_HD, A_KV_HEADS * A_HD, B_Q_RANK, B_KV_RANK, B_ROPE,
            C_GROUPS * C_GW, D_HEADS * (2 * D_DK + D_DV), D_HEADS * D_DV, 2 * D_HEADS, 2 * D_HEADS,
            N_BRANCH * D_MODEL)

LANES = 128
SUBLANES = 8
VMEM_LIMIT_BYTES = 56 * 1024 * 1024
HALO = SUBLANES
GDN_CHUNKS_PER_STEP = 4
MLA_HEAD_LANES = LANES
MLA_WIDE = B_HEADS * MLA_HEAD_LANES
LOG2E = math.log2(math.e)
FF_CHUNK = 256

C_AQ = 0
C_AK = 256
C_AV = 512
C_CQ = 768
C_CKV = 1024
C_KR = 1152
C_CIN = 1280
C_QKV = 1536
C_Z = 2304
C_AB = 2560
P1 = 2688


def _cparams(sem):
    return pltpu.CompilerParams(dimension_semantics=sem, vmem_limit_bytes=VMEM_LIMIT_BYTES)


class _Layer(NamedTuple):
    arr: jax.Array
    l: int


def _resident(p):
    shape = p.arr.shape[1:]
    return pl.BlockSpec((None,) + shape, lambda *_: (p.l,) + (0,) * len(shape), pipeline_mode=pl.Buffered(1))


class _Mod(NamedTuple):
    arr: jax.Array
    l: int
    first: int
    count: int


def _mod_block(m, tiles_per_seq):
    width = m.arr.shape[-1]
    if m.count == 1:
        return pl.BlockSpec((None, 1, 1, width), lambda i: (m.l, m.first, 0, 0))
    return pl.BlockSpec((None, 1, 1, width), lambda i: (m.l, m.first + i // tiles_per_seq, 0, 0))


def _cache_block(p):
    return pl.BlockSpec((1, None) + p.arr.shape[2:], lambda b, i: (b, p.l, 0, 0))


def _operands(args):
    return [a.arr if isinstance(a, (_Layer, _Mod)) else a for a in args]


def _dot(a, b):
    return jnp.dot(a, b, preferred_element_type=F32)


def _dot_nt(a, b):
    return lax.dot_general(a, b, (((1,), (1,)), ((), ())), preferred_element_type=F32)


def _split3(x):
    hi = x.astype(BF16)
    r1 = x - hi.astype(F32)
    mid = r1.astype(BF16)
    lo = (r1 - mid.astype(F32)).astype(BF16)
    return hi, mid, lo


def _dot_sel_right(x, sel):
    hi, mid, lo = _split3(x)
    return _dot(hi, sel) + _dot(mid, sel) + _dot(lo, sel)


def _silu(x):
    return x * jax.nn.sigmoid(x)


def _softplus(x):
    return jnp.maximum(x, 0.0) + jnp.log1p(jnp.exp(-jnp.abs(x)))


def _head_mask(width, n_lanes, h):
    lane = lax.broadcasted_iota(jnp.int32, (1, n_lanes), 1)
    return (lane // width) == h


def _group_ones(n, width):
    r = lax.broadcasted_iota(jnp.int32, (n, n), 0) // width
    c = lax.broadcasted_iota(jnp.int32, (n, n), 1) // width
    return jnp.where(r == c, 1.0, 0.0).astype(BF16)


def _mod_kernel(c_ref, w_ref, b_ref, o_ref):
    s = _silu(c_ref[...])
    o_ref[0] = _dot(s.astype(BF16), w_ref[0].astype(BF16)) + b_ref[0]


def _mod_call(cond8, w_mod, b_mod):
    depth, d, n = w_mod.shape
    tn = 1536
    return pl.pallas_call(
        _mod_kernel,
        grid=(depth, n // tn),
        in_specs=[pl.BlockSpec((SUBLANES, d), lambda l, j: (0, 0)),
                  pl.BlockSpec((1, d, tn), lambda l, j: (l, 0, j)),
                  pl.BlockSpec((1, 1, tn), lambda l, j: (l, 0, j))],
        out_specs=pl.BlockSpec((1, SUBLANES, tn), lambda l, j: (l, 0, j)),
        out_shape=jax.ShapeDtypeStruct((depth, SUBLANES, n), F32),
        compiler_params=_cparams(("parallel", "parallel")),
        name="mod",
    )(cond8, w_mod, b_mod.reshape(depth, 1, n))


def _rmsnorm_rows(x, g, n):
    ms = jnp.sum(x * x, axis=-1, keepdims=True) * (1.0 / n)
    return x * lax.rsqrt(ms + EPS) * g


def _rope(x, cos, sin_next, sin_prev, quarter):
    n = x.shape[-1]
    return x * cos + pltpu.roll(x, n - quarter, 1) * sin_next + pltpu.roll(x, quarter, 1) * sin_prev


def _mla_wide_kr(kr_rep):
    lane = lax.broadcasted_iota(jnp.int32, (1, MLA_HEAD_LANES), 1)
    keep = (lane >= B_NOPE) & (lane < B_NOPE + B_ROPE)
    return jnp.concatenate([jnp.where(keep, kr_rep, 0.0)] * B_HEADS, axis=1)


def _inproj_kernel(latent, *refs):
    if latent:
        (x_ref, mod_ref, gpre_ref, w1_ref, gcq_ref, wuq_ref, gckv_ref, wukv_ref,
         ca_ref, sna_ref, spa_ref, cb_ref, snb_ref, spb_ref,
         proj_ref, ckv_ref, qw_ref, kw_ref, vb_ref, aqr_ref, akr_ref, qwr_ref) = refs
    else:
        (x_ref, mod_ref, gpre_ref, w1_ref, gcq_ref, wuq_ref, gckv_ref, wukv_ref,
         proj_ref, ckv_ref, qw_ref, kw_ref, vb_ref) = refs
    d = D_MODEL
    x = x_ref[...]
    mod = mod_ref[0]
    sh1 = mod[:, 0:d]
    sc1 = mod[:, d:2 * d]
    h = _rmsnorm_rows(x, gpre_ref[...], d) * (1.0 + sc1) + sh1
    proj = _dot(h.astype(BF16), w1_ref[...])
    proj_ref[...] = proj
    cqn = _rmsnorm_rows(proj[:, C_CQ:C_CQ + 256], gcq_ref[...], B_Q_RANK)
    qw = _dot(cqn.astype(BF16), wuq_ref[...])
    qw_ref[...] = qw.astype(BF16)
    ckvn = _rmsnorm_rows(proj[:, C_CKV:C_CKV + B_KV_RANK], gckv_ref[...], B_KV_RANK)
    ckv_ref[...] = ckvn
    knv = _dot(ckvn.astype(BF16), wukv_ref[...])
    vb_ref[...] = knv[:, MLA_WIDE:].astype(BF16)
    kr_wide = _mla_wide_kr(proj[:, C_KR:C_KR + 128])
    if latent:
        ca, sna, spa = ca_ref[...], sna_ref[...], spa_ref[...]
        cb, snb, spb = cb_ref[...], snb_ref[...], spb_ref[...]
        aqr_ref[...] = _rope(proj[:, C_AQ:C_AQ + 256], ca, sna, spa, A_HD // 4).astype(BF16)
        akr_ref[...] = _rope(proj[:, C_AK:C_AK + 256], ca, sna, spa, A_HD // 4).astype(BF16)
        qwr_ref[...] = _rope(qw, cb, snb, spb, B_ROPE // 4).astype(BF16)
        kr_wide = _rope(kr_wide, cb, snb, spb, B_ROPE // 4)
    kw_ref[...] = (knv[:, :MLA_WIDE] + kr_wide).astype(BF16)


def _inproj_call(x2d, mod3, gpre, w1, gcq, wuq, gckv, wukv, tables, seq_len, tm):
    n, d = x2d.shape
    latent = tables is not None
    tiles_per_seq = seq_len // tm if latent else 1
    row = lambda w, dt=F32: (pl.BlockSpec((tm, w), lambda i: (i, 0)), jax.ShapeDtypeStruct((n, w), dt))
    params = [gpre, w1, gcq, wuq, gckv, wukv]
    in_specs = [pl.BlockSpec((tm, d), lambda i: (i, 0)),
                _mod_block(mod3, seq_len // tm)] + [_resident(p) for p in params]
    args = [x2d, mod3] + params
    outs = [row(P1), row(B_KV_RANK), row(MLA_WIDE, BF16), row(MLA_WIDE, BF16), row(256, BF16)]
    if latent:
        for t in tables:
            in_specs.append(pl.BlockSpec((tm, t.shape[1]), lambda i: (i % tiles_per_seq, 0)))
            args.append(t)
        outs += [row(256, BF16), row(256, BF16), row(MLA_WIDE, BF16)]
    return pl.pallas_call(
        functools.partial(_inproj_kernel, latent),
        grid=(n // tm,),
        in_specs=in_specs,
        out_specs=[o[0] for o in outs],
        out_shape=[o[1] for o in outs],
        compiler_params=_cparams(("parallel",)),
        name="inproj_lat" if latent else "inproj_ctx",
    )(*_operands(args))


def _softmax_pv(scores, values, scale, sink, hm):
    c1 = scale * LOG2E
    m = jnp.max(scores[0], axis=-1, keepdims=True)
    for s in scores[1:]:
        m = jnp.maximum(m, jnp.max(s, axis=-1, keepdims=True))
    m2 = m * c1
    if sink is not None:
        m2 = jnp.maximum(m2, sink * LOG2E)
    den = None
    pv = None
    for s, v in zip(scores, values):
        p = jnp.exp2(s * c1 - m2)
        ps = jnp.sum(p, axis=-1, keepdims=True)
        den = ps if den is None else den + ps
        t = _dot(p.astype(BF16), v)
        pv = t if pv is None else pv + t
    if sink is not None:
        den = den + jnp.exp2(sink * LOG2E - m2)
    return jnp.where(hm, pv / den, 0.0)


def _attn_ctx_kernel(layer, sink_ref, q_ref, k_ref, v_ref, qw_ref, kw_ref, vb_ref, oa_ref, ob_ref):
    q = q_ref[...]
    k = k_ref[...].astype(BF16)
    v = v_ref[...].astype(BF16)
    acc = jnp.zeros(q.shape, F32)
    for h in range(A_HEADS):
        hm = _head_mask(A_HD, 256, h)
        s = _dot_nt(jnp.where(hm, q, 0.0).astype(BF16), k)
        acc = acc + _softmax_pv([s], [v], A_SCALE, sink_ref[layer, h], hm)
    oa_ref[...] = acc
    vb = vb_ref[...]
    acc = jnp.zeros((q.shape[0], B_HEADS * B_VD), F32)
    for h in range(B_HEADS):
        hs = slice(h * MLA_HEAD_LANES, (h + 1) * MLA_HEAD_LANES)
        s = _dot_nt(qw_ref[:, hs], kw_ref[:, hs])
        acc = acc + _softmax_pv([s], [vb], B_SCALE, None, _head_mask(B_VD, 256, h))
    ob_ref[...] = acc


def _attn_ctx_call(sink, proj, qw, kw, vb, n_seq, t):
    n = proj.shape[0]
    col = lambda w, j: pl.BlockSpec((t, w), lambda b: (b, j))
    out = (pl.BlockSpec((t, 256), lambda b: (b, 0)), jax.ShapeDtypeStruct((n, 256), F32))
    return pl.pallas_call(
        functools.partial(_attn_ctx_kernel, sink.l),
        grid=(n_seq,),
        in_specs=[pl.BlockSpec(memory_space=pltpu.SMEM),
                  col(256, C_AQ // 256), col(256, C_AK // 256), col(256, C_AV // 256),
                  col(MLA_WIDE, 0), col(MLA_WIDE, 0), col(256, 0)],
        out_specs=[out[0], out[0]],
        out_shape=[out[1], out[1]],
        compiler_params=_cparams(("parallel",)),
        name="attn_ctx",
    )(sink.arr, proj, proj, proj, qw, kw, vb)


def _win_attn_kernel(t, layer, sink_ref, qr_ref, q_ref, kp_ref, kc_ref, kn_ref, vp_ref, vc_ref, vn_ref,
                     kctx_ref, vctx_ref, o_ref):
    i = pl.program_id(1)
    qr = qr_ref[...]
    q = q_ref[...]
    kl = jnp.concatenate([kp_ref[...], kc_ref[...], kn_ref[...]], axis=0)
    vl = jnp.concatenate([vp_ref[...], vc_ref[...], vn_ref[...]], axis=0).astype(BF16)
    kctx = kctx_ref[0].astype(BF16)
    vctx = vctx_ref[0].astype(BF16)
    q_pos = i * Q_BLOCK + lax.broadcasted_iota(jnp.int32, (Q_BLOCK, 3 * Q_BLOCK), 0)
    k_pos = (i - 1) * Q_BLOCK + lax.broadcasted_iota(jnp.int32, (Q_BLOCK, 3 * Q_BLOCK), 1)
    valid = (jnp.abs(q_pos - k_pos) <= A_WINDOW) & (k_pos >= 0) & (k_pos < t)
    acc = jnp.zeros(q.shape, F32)
    for h in range(A_HEADS):
        hm = _head_mask(A_HD, 256, h)
        s_loc = jnp.where(valid, _dot_nt(jnp.where(hm, qr, 0.0).astype(BF16), kl), NEG_INF)
        s_ctx = _dot_nt(jnp.where(hm, q, 0.0).astype(BF16), kctx)
        acc = acc + _softmax_pv([s_loc, s_ctx], [vl, vctx], A_SCALE, sink_ref[layer, h], hm)
    o_ref[...] = acc


def _win_attn_call(sink, aqr, akr, proj, kctx, vctx, n_seq, t):
    n = proj.shape[0]
    nb = t // Q_BLOCK
    blk = lambda j, f: pl.BlockSpec((Q_BLOCK, 256), lambda b, i: (b * nb + f(i), j))
    prev = lambda i: jnp.maximum(i - 1, 0)
    cur = lambda i: i
    nxt = lambda i: jnp.minimum(i + 1, nb - 1)
    ctx = _cache_block(kctx)
    return pl.pallas_call(
        functools.partial(_win_attn_kernel, t, sink.l),
        grid=(n_seq, nb),
        in_specs=[pl.BlockSpec(memory_space=pltpu.SMEM),
                  blk(0, cur), blk(C_AQ // 256, cur),
                  blk(0, prev), blk(0, cur), blk(0, nxt),
                  blk(C_AV // 256, prev), blk(C_AV // 256, cur), blk(C_AV // 256, nxt),
                  ctx, ctx],
        out_specs=pl.BlockSpec((Q_BLOCK, 256), lambda b, i: (b * nb + i, 0)),
        out_shape=jax.ShapeDtypeStruct((n, 256), F32),
        compiler_params=_cparams(("parallel", "parallel")),
        name="win_attn",
    )(sink.arr, aqr, proj, akr, akr, akr, proj, proj, proj, kctx.arr, vctx.arr)


def _mla_lat_kernel(qw_ref, qwr_ref, kwr_ref, vb_ref, cckv_ref, ckr_ref, wukv_ref, o_ref):
    kvc = _dot(cckv_ref[0].astype(BF16), wukv_ref[...])
    kwc = (kvc[:, :MLA_WIDE] + ckr_ref[0]).astype(BF16)
    vc = kvc[:, MLA_WIDE:].astype(BF16)
    v = vb_ref[...]
    acc = jnp.zeros((qw_ref.shape[0], B_HEADS * B_VD), F32)
    for h in range(B_HEADS):
        hs = slice(h * MLA_HEAD_LANES, (h + 1) * MLA_HEAD_LANES)
        s_lat = _dot_nt(qwr_ref[:, hs], kwr_ref[:, hs])
        s_ctx = _dot_nt(qw_ref[:, hs], kwc[:, hs])
        acc = acc + _softmax_pv([s_lat, s_ctx], [v, vc], B_SCALE, None, _head_mask(B_VD, 256, h))
    o_ref[...] = acc


def _mla_lat_call(qw, qwr, kwr, vb, cckv, ckr_wide, wukv, n_seq, t, tq):
    n = qw.shape[0]
    nq = t // tq
    qblk = lambda w: pl.BlockSpec((tq, w), lambda b, i: (b * nq + i, 0))
    seq = lambda w: pl.BlockSpec((t, w), lambda b, i: (b, 0))
    ctx = _cache_block
    return pl.pallas_call(
        _mla_lat_kernel,
        grid=(n_seq, nq),
        in_specs=[qblk(MLA_WIDE), qblk(MLA_WIDE), seq(MLA_WIDE), seq(256), ctx(cckv), ctx(ckr_wide),
                  _resident(wukv)],
        out_specs=pl.BlockSpec((tq, 256), lambda b, i: (b * nq + i, 0)),
        out_shape=jax.ShapeDtypeStruct((n, 256), F32),
        compiler_params=_cparams(("parallel", "parallel")),
        name="mla_lat",
    )(qw, qwr, kwr, vb, cckv.arr, ckr_wide.arr, wukv.arr)


def _with_halo(prev_ref, cur_ref, next_ref, has_prev, has_next):
    prev = jnp.where(has_prev, prev_ref[...], 0.0)
    nxt = jnp.where(has_next, next_ref[...], 0.0)
    return jnp.concatenate([prev, cur_ref[...], nxt], axis=0)


def _shift_rows(x, k):
    n = x.shape[0]
    return pltpu.roll(x, (-k) % n, 0)


def _local_kernel(t, tm, cp_ref, cc_ref, cn_ref, qp_ref, qc_ref, qn_ref, convw_ref, wpool_ref, cscale_ref,
                  oc_ref, qkv_ref):
    i = pl.program_id(0)
    tiles_per_seq = t // tm
    j = i % tiles_per_seq
    has_prev = j > 0
    has_next = j < tiles_per_seq - 1
    x = _with_halo(cp_ref, cc_ref, cn_ref, has_prev, has_next)
    p2 = x + _shift_rows(x, -1)
    p4 = _shift_rows(p2, 1) + _shift_rows(p2, -1)
    p8 = _shift_rows(p4, 2) + _shift_rows(p4, -2)
    p16 = _shift_rows(p8, 4) + _shift_rows(p8, -4)
    grp = lax.broadcasted_iota(jnp.int32, (1, 256), 1) // C_GW
    win = jnp.where(grp == 0, p2, jnp.where(grp == 1, p4, jnp.where(grp == 2, p8, p16)))[HALO:HALO + tm]
    pos = j * tm + lax.broadcasted_iota(jnp.int32, (tm, 256), 0)
    half = jnp.where(grp == 0, 1, jnp.where(grp == 1, 2, jnp.where(grp == 2, 4, 8)))
    cnt = (jnp.minimum(pos + half, t) - jnp.maximum(pos - half, 0)).astype(F32)
    y = win / cnt - cc_ref[...]
    oc_ref[...] = _dot(y.astype(BF16), wpool_ref[...]) * cscale_ref[...]
    xq = _with_halo(qp_ref, qc_ref, qn_ref, has_prev, has_next)
    w = convw_ref[...]
    pad = D_CONV // 2
    acc = None
    for tap in range(D_CONV):
        term = _shift_rows(xq, tap - pad) * w[tap:tap + 1, :]
        acc = term if acc is None else acc + term
    u = _silu(acc[HALO:HALO + tm])
    ones = _group_ones(256, D_DK)
    q = u[:, 0:256]
    k = u[:, 256:512]
    q = q * lax.rsqrt(_dot_sel_right(q * q, ones) + EPS) * (D_DK ** -0.5)
    k = k * lax.rsqrt(_dot_sel_right(k * k, ones) + EPS)
    qkv_ref[:, 0:256] = q
    qkv_ref[:, 256:512] = k
    qkv_ref[:, 512:768] = u[:, 512:768]


def _local_call(proj, convw8, wpool_bd, cscale, t, tm):
    n = proj.shape[0]
    hb = tm // HALO
    nblk = n // HALO
    cur = lambda w, c: pl.BlockSpec((tm, w), lambda i: (i, c))
    prev = lambda w, c: pl.BlockSpec((HALO, w), lambda i: (jnp.maximum(i * hb - 1, 0), c))
    nxt = lambda w, c: pl.BlockSpec((HALO, w), lambda i: (jnp.minimum((i + 1) * hb, nblk - 1), c))
    full = _resident
    cc, qc = C_CIN // 256, C_QKV // 768
    return pl.pallas_call(
        functools.partial(_local_kernel, t, tm),
        grid=(n // tm,),
        in_specs=[prev(256, cc), cur(256, cc), nxt(256, cc), prev(768, qc), cur(768, qc), nxt(768, qc),
                  full(convw8), full(wpool_bd), full(cscale)],
        out_specs=[pl.BlockSpec((tm, 256), lambda i: (i, 0)), pl.BlockSpec((tm, 768), lambda i: (i, 0))],
        out_shape=[jax.ShapeDtypeStruct((n, 256), F32), jax.ShapeDtypeStruct((n, 768), F32)],
        compiler_params=_cparams(("parallel",)),
        name="local",
    )(proj, proj, proj, proj, proj, proj, convw8.arr, wpool_bd.arr, cscale.arr)


def _gdn_kernel(n_steps, g_chunks, has_s0, xf_ref, xb_ref, abf_ref, abb_ref, prow_ref, pcol_ref, *refs):
    if has_s0:
        s0f_ref, s0b_ref, of_ref, ob_ref, sfin_ref, s_ref = refs
    else:
        of_ref, ob_ref, sfin_ref, s_ref = refs
    i = pl.program_id(1)
    c = D_CHUNK
    nh = D_HEADS
    nb = 2 * nh

    @pl.when(i == 0)
    def _():
        if has_s0:
            s_ref[0:nh] = s0f_ref[0]
            s_ref[nh:2 * nh] = s0b_ref[0]
        else:
            s_ref[...] = jnp.zeros(s_ref.shape, F32)

    r = lax.broadcasted_iota(jnp.int32, (c, c), 0)
    cidx = lax.broadcasted_iota(jnp.int32, (c, c), 1)
    lower = jnp.where(cidx <= r, 1.0, 0.0).astype(BF16)
    upper = jnp.where(cidx >= r, 1.0, 0.0).astype(BF16)
    prow = prow_ref[...]
    pcol = pcol_ref[...]

    eye = jnp.where(cidx == r, 1.0, 0.0).astype(BF16)
    sel_lower = jnp.concatenate([lower] * 3, axis=1)
    sel_upper = jnp.concatenate([upper] * 3, axis=1)
    pick = jnp.where(lax.broadcasted_iota(jnp.int32, (16, LANES), 0) == lax.broadcasted_iota(jnp.int32, (16, LANES), 1),
                     1.0, 0.0).astype(BF16)
    sel_rows = jnp.concatenate([pick] * 3, axis=1)
    sel_lower_t = jnp.concatenate([lower] * 3, axis=0)
    sel_upper_t = jnp.concatenate([upper] * 3, axis=0)

    bdot = lambda a, b: jnp.einsum('bik,bkj->bij', a, b, preferred_element_type=F32)
    bdot_nt = lambda a, b: jnp.einsum('bik,bjk->bij', a, b, preferred_element_type=F32)

    def prepare(steps):
        nb = 2 * len(steps) * nh
        is_fwd = lax.broadcasted_iota(jnp.int32, (nb, c, c), 0) < nb // 2
        r3 = lax.broadcasted_iota(jnp.int32, (nb, c, c), 1)
        c3 = lax.broadcasted_iota(jnp.int32, (nb, c, c), 2)
        ahead = jnp.where(is_fwd, r3 - c3, c3 - r3)
        incl = ahead >= 0
        strict = ahead > 0
        q_l, k_l, v_l, b_l, gcol_l, grow_l = [], [], [], [], [], []
        for d, step in [(d, step) for d in range(2) for step in steps]:
            x_ref, ab_ref = ((xf_ref, abf_ref), (xb_ref, abb_ref))[d]
            gi = step if d == 0 else g_chunks - 1 - step
            rows = slice(gi * c, (gi + 1) * c)
            x = x_ref[rows, :]
            ab = ab_ref[rows, :]
            abr = _dot_nt(sel_rows, jnp.concatenate(_split3(ab), axis=1))
            beta_c = jax.nn.sigmoid(ab)
            g_c = -jnp.exp(prow[0:1, :]) * _softplus(ab + prow[1:2, :])
            g_r = -jnp.exp(pcol[:, 0:1]) * _softplus(abr + pcol[:, 1:2])
            if d == 0:
                gc_c = _dot(sel_lower, jnp.concatenate(_split3(g_c), axis=0))
                gc_r = _dot(jnp.concatenate(_split3(g_r), axis=1), sel_upper_t)
            else:
                gc_c = _dot(sel_upper, jnp.concatenate(_split3(g_c), axis=0))
                gc_r = _dot(jnp.concatenate(_split3(g_r), axis=1), sel_lower_t)
            for h in range(nh):
                hs = slice(h * D_DK, (h + 1) * D_DK)
                lane = nh * d + h
                q_l.append(x[:, 0:256][:, hs])
                k_l.append(x[:, 256:512][:, hs])
                v_l.append(x[:, 512:768][:, hs])
                b_l.append(beta_c[:, lane:lane + 1])
                gcol_l.append(gc_c[:, 8 + lane:9 + lane])
                grow_l.append(gc_r[8 + lane:9 + lane, :])
        qh, kh, vh = jnp.stack(q_l), jnp.stack(k_l), jnp.stack(v_l)
        bcol, gcol, grow = jnp.stack(b_l), jnp.stack(gcol_l), jnp.stack(grow_l)
        glast = jnp.concatenate([gcol[:nb // 2, c - 1:c, :], gcol[nb // 2:, 0:1, :]], axis=0)
        decay = jnp.exp(jnp.where(incl, gcol - grow, NEG_INF))
        kb = kh * bcol
        kq = bdot_nt(jnp.concatenate([kb, qh], axis=1).astype(BF16), kh.astype(BF16))
        p = -jnp.where(strict, kq[:, 0:c] * decay, 0.0)
        egc = jnp.exp(gcol)
        xs = jnp.concatenate([vh * bcol, kb * egc], axis=2)
        for it in range(6):
            ph = p.astype(BF16)
            pl_ = (p - ph.astype(F32)).astype(BF16)
            lhs = jnp.concatenate([ph, pl_, ph], axis=2)
            yh = xs.astype(BF16)
            yl = (xs - yh.astype(F32)).astype(BF16)
            if it < 5:
                yh = jnp.concatenate([yh, ph], axis=2)
                yl = jnp.concatenate([yl, pl_], axis=2)
            m = bdot(lhs, jnp.concatenate([yh, yh, yl], axis=1))
            xs = xs + m[:, :, 0:2 * D_DK]
            if it < 5:
                p = m[:, :, 2 * D_DK:3 * D_DK]
        u = xs[:, :, 0:D_DV]
        w = xs[:, :, D_DV:2 * D_DV]
        a = kq[:, c:2 * c] * decay
        kd = (kh * jnp.exp(glast - gcol)).astype(BF16)
        kdt = bdot_nt(jnp.broadcast_to(eye, (nb, c, c)), kd).astype(BF16)
        lhs_s = jnp.concatenate([w, qh * egc], axis=1).astype(BF16)
        lhs_v = jnp.concatenate([a.astype(BF16), kdt], axis=1)
        return u, lhs_s, lhs_v, jnp.exp(glast)

    prep = prepare(list(range(g_chunks)))

    def chains(arr, step):
        lo_f = step * nh
        lo_b = (g_chunks + step) * nh
        return jnp.concatenate([arr[lo_f:lo_f + nh], arr[lo_b:lo_b + nh]], axis=0)

    s = s_ref[...]
    outs_f, outs_b = [None] * g_chunks, [None] * g_chunks
    for step in range(g_chunks):
        u, lhs_s, lhs_v, eg = (chains(arr, step) for arr in prep)
        ws_qs = bdot(lhs_s, s.astype(BF16))
        vb = (u - ws_qs[:, 0:c]).astype(BF16)
        av = bdot(lhs_v, vb)
        o = ws_qs[:, c:2 * c] + av[:, 0:c]
        s = s * eg + av[:, c:2 * c]
        outs_f[step] = jnp.concatenate([o[h] for h in range(nh)], axis=1)
        outs_b[g_chunks - 1 - step] = jnp.concatenate([o[nh + h] for h in range(nh)], axis=1)
    of_ref[...] = jnp.concatenate(outs_f, axis=0)
    ob_ref[...] = jnp.concatenate(outs_b, axis=0)
    s_ref[...] = s

    @pl.when(i == n_steps - 1)
    def _():
        sfin_ref[0] = s.reshape(2, nh, D_DK, D_DV)


def _gdn_call(qkvn, proj, prow, pcol, s0, n_seq, t):
    n = qkvn.shape[0]
    g = GDN_CHUNKS_PER_STEP
    ns = t // (D_CHUNK * g)
    rows = D_CHUNK * g
    fwd = lambda b, i: b * ns + i
    bwd = lambda b, i: b * ns + ns - 1 - i
    xblk = lambda f: pl.BlockSpec((rows, 768), lambda b, i: (f(b, i), 0))
    abblk = lambda f: pl.BlockSpec((rows, 128), lambda b, i: (f(b, i), C_AB // 128))
    oblk = lambda f: pl.BlockSpec((rows, 256), lambda b, i: (f(b, i), 0))
    sblk = pl.BlockSpec((1, 2, D_HEADS, D_DK, D_DV), lambda b, i: (b, 0, 0, 0, 0))
    osd = jax.ShapeDtypeStruct((n, 256), F32)
    in_specs = [xblk(fwd), xblk(bwd), abblk(fwd), abblk(bwd), _resident(prow), _resident(pcol)]
    args = [qkvn, qkvn, proj, proj, prow.arr, pcol.arr]
    if s0 is not None:
        for st in s0:
            in_specs.append(pl.BlockSpec((1, None, D_HEADS, D_DK, D_DV), lambda b, i, l=st.l: (b, l, 0, 0, 0)))
            args.append(st.arr)
    return pl.pallas_call(
        functools.partial(_gdn_kernel, ns, g, s0 is not None),
        grid=(n_seq, ns),
        in_specs=in_specs,
        out_specs=[oblk(fwd), oblk(bwd), sblk],
        out_shape=[osd, osd, jax.ShapeDtypeStruct((n_seq, 2, D_HEADS, D_DK, D_DV), F32)],
        scratch_shapes=[pltpu.VMEM((2 * D_HEADS, D_DK, D_DV), F32)],
        compiler_params=_cparams(("parallel", "arbitrary")),
        name="gdn",
    )(*args)


def _merge_kernel(x_ref, mod_ref, gpre_ref, oa_ref, ob_ref, oc_ref, of_ref, obw_ref, z_ref, gnorm_ref,
                  wg_ref, wbr_ref, wo_ref, gpost_ref, o_ref):
    d = D_MODEL
    x = x_ref[...]
    mod = mod_ref[0]
    sh1, sc1, ga1 = mod[:, 0:d], mod[:, d:2 * d], mod[:, 2 * d:3 * d]
    h = (_rmsnorm_rows(x, gpre_ref[...], d) * (1.0 + sc1) + sh1).astype(BF16)
    o = of_ref[...] + obw_ref[...]
    ms = _dot_sel_right(o * o, _group_ones(256, D_DV)) * (1.0 / D_DV)
    od = o * lax.rsqrt(ms + EPS) * gnorm_ref[...] * _silu(z_ref[...])
    acc = None
    for m, br in enumerate((oa_ref[...], ob_ref[...], oc_ref[...], od)):
        gate = jax.nn.sigmoid(_dot(h, wg_ref[:, m * d:(m + 1) * d]))
        term = gate * _dot(br.astype(BF16), wbr_ref[m])
        acc = term if acc is None else acc + term
    mix = _dot(acc.astype(BF16), wo_ref[...])
    o_ref[...] = x + ga1 * _rmsnorm_rows(mix, gpost_ref[...], d)


def _merge_call(x2d, mod3, gpre, oa, ob, oc, of, obw, proj, gnorm, wg, wbr, wo, gpost, seq_len, tm):
    n, d = x2d.shape
    tiles_per_seq = seq_len // tm
    row = lambda w, c=0: pl.BlockSpec((tm, w), lambda i: (i, c))
    full = _resident
    return pl.pallas_call(
        _merge_kernel,
        grid=(n // tm,),
        in_specs=[row(d), _mod_block(mod3, tiles_per_seq), full(gpre),
                  row(256), row(256), row(256), row(256), row(256), row(256, C_Z // 256), full(gnorm),
                  full(wg), full(wbr), full(wo), full(gpost)],
        out_specs=row(d),
        out_shape=jax.ShapeDtypeStruct((n, d), F32),
        compiler_params=_cparams(("parallel",)),
        name="merge",
    )(*_operands([x2d, mod3, gpre, oa, ob, oc, of, obw, proj, gnorm, wg, wbr, wo, gpost]))


def _ffn_kernel(x_ref, mod_ref, gpre_ref, wup_ref, wdown_ref, gpost_ref, o_ref):
    d = D_MODEL
    x = x_ref[...]
    mod = mod_ref[0]
    sh2, sc2, ga2 = mod[:, 3 * d:4 * d], mod[:, 4 * d:5 * d], mod[:, 5 * d:6 * d]
    h = (_rmsnorm_rows(x, gpre_ref[...], d) * (1.0 + sc2) + sh2).astype(BF16)
    f = None
    for c in range(D_FF // FF_CHUNK):
        up = _dot(h, wup_ref[:, 2 * c * FF_CHUNK:2 * (c + 1) * FF_CHUNK])
        act = (_silu(up[:, 0:FF_CHUNK]) * up[:, FF_CHUNK:2 * FF_CHUNK]).astype(BF16)
        part = _dot(act, wdown_ref[c * FF_CHUNK:(c + 1) * FF_CHUNK, :])
        f = part if f is None else f + part
    o_ref[...] = x + ga2 * _rmsnorm_rows(f, gpost_ref[...], d)


def _ffn_call(x2d, mod3, gpre, wup, wdown, gpost, seq_len, tm):
    n, d = x2d.shape
    tiles_per_seq = seq_len // tm
    row = lambda w: pl.BlockSpec((tm, w), lambda i: (i, 0))
    full = _resident
    return pl.pallas_call(
        _ffn_kernel,
        grid=(n // tm,),
        in_specs=[row(d), _mod_block(mod3, tiles_per_seq), full(gpre), full(wup), full(wdown), full(gpost)],
        out_specs=row(d),
        out_shape=jax.ShapeDtypeStruct((n, d), F32),
        compiler_params=_cparams(("parallel",)),
        name="ffn",
    )(*_operands([x2d, mod3, gpre, wup, wdown, gpost]))


def _rope_tables(n_tokens, dim, reps):
    n_rows = n_tokens // GRID_W
    row = jnp.repeat(jnp.arange(n_rows), GRID_W).astype(F32)
    col = jnp.tile(jnp.arange(GRID_W), n_rows).astype(F32)
    nfreq = dim // 4
    inv = ROPE_BASE ** (-jnp.arange(nfreq, dtype=F32) / nfreq)
    ang_r = row[:, None] * inv
    ang_c = col[:, None] * inv
    ang = jnp.concatenate([ang_r, ang_r, ang_c, ang_c], axis=-1)
    cos, sin = jnp.cos(ang), jnp.sin(ang)
    even = ((jnp.arange(dim) // nfreq) % 2 == 0)[None, :]
    sin_next = jnp.where(even, -sin, 0.0)
    sin_prev = jnp.where(even, 0.0, sin)
    return tuple(jnp.tile(a, (1, reps)) for a in (cos, sin_next, sin_prev))


def _rope_tables_mla_wide(n_tokens):
    cos, sin_next, sin_prev = _rope_tables(n_tokens, B_ROPE, 1)
    tail = MLA_HEAD_LANES - B_NOPE - B_ROPE
    wide = lambda a, fill: jnp.tile(jnp.pad(a, ((0, 0), (B_NOPE, tail)), constant_values=fill), (1, B_HEADS))
    return wide(cos, 1.0), wide(sin_next, 0.0), wide(sin_prev, 0.0)


def _split_cols(w):
    out, acc = [], 0
    for s in IN_SIZES:
        out.append(w[..., acc:acc + s])
        acc += s
    return out


def _pack_w_in(w):
    (a_q, a_k, a_v, b_cq, b_ckv, b_kr, c_in, d_qkv, d_z, d_beta, d_alpha, gate) = _split_cols(w)
    z = lambda n: jnp.zeros(w.shape[:-1] + (n,), w.dtype)
    k0, k1 = a_k[..., :A_HD], a_k[..., A_HD:]
    v0, v1 = a_v[..., :A_HD], a_v[..., A_HD:]
    w1 = jnp.concatenate([a_q, k0, k0, k1, k1, v0, v0, v1, v1, b_cq, z(256 - B_Q_RANK), b_ckv,
                          b_kr, b_kr, b_kr, b_kr, c_in, d_qkv, d_z, d_beta, d_alpha, z(112)], axis=-1)
    return w1.astype(BF16), gate.astype(BF16)


def _expand_kv_heads(a):
    return jnp.repeat(a, A_HEADS // A_KV_HEADS, axis=-2).reshape(a.shape[:-2] + (A_HEADS * A_HD,))


def _stacked_weights(g_pre1, g_post1, g_pre2, g_post2, w_in, a_sink, b_g_cq, b_g_ckv, b_w_uq, b_w_ukv,
                     c_w_pool, c_scale, d_conv, d_a_log, d_dt_bias, d_g_norm, w_br, w_o, w_up, w_down):
    depth = w_in.shape[0]
    w1, wg = _pack_w_in(w_in)
    wuq = b_w_uq.reshape(depth, B_Q_RANK, B_HEADS, B_NOPE + B_ROPE)
    wuq = jnp.pad(wuq, ((0, 0), (0, 256 - B_Q_RANK), (0, 0), (0, MLA_HEAD_LANES - B_NOPE - B_ROPE)))
    wuq = wuq.reshape(depth, 256, MLA_WIDE).astype(BF16)
    wukv = b_w_ukv.reshape(depth, B_KV_RANK, B_HEADS, B_NOPE + B_VD)
    wk = jnp.pad(wukv[..., :B_NOPE], ((0, 0), (0, 0), (0, 0), (0, MLA_HEAD_LANES - B_NOPE)))
    wukv = jnp.concatenate([wk.reshape(depth, B_KV_RANK, MLA_WIDE),
                            wukv[..., B_NOPE:].reshape(depth, B_KV_RANK, B_HEADS * B_VD)], -1).astype(BF16)
    same = jnp.eye(C_GROUPS, dtype=F32)[None, :, None, :, None]
    wpool = (c_w_pool[:, :, :, None, :] * same).reshape(depth, C_GROUPS * C_GW, C_GROUPS * C_GW).astype(BF16)
    dm = w_up.shape[1]
    wup = w_up.reshape(depth, dm, 2, D_FF // FF_CHUNK, FF_CHUNK).transpose(0, 1, 3, 2, 4).reshape(depth, dm, 2 * D_FF)
    gdn = jnp.stack([d_a_log.reshape(depth, 2 * D_HEADS), d_dt_bias.reshape(depth, 2 * D_HEADS)], axis=1)
    prow = jnp.pad(gdn, ((0, 0), (0, 0), (8, LANES - 16)))
    pcol = jnp.pad(gdn.transpose(0, 2, 1), ((0, 0), (8, 0), (0, 0)))
    row = lambda a: a[:, None, :]
    return dict(
        w1=w1, wg=wg, wuq=wuq, wukv=wukv,
        gpre1=row(g_pre1), gpost1=row(g_post1), gpre2=row(g_pre2), gpost2=row(g_post2),
        gcq=row(jnp.pad(b_g_cq, ((0, 0), (0, 256 - B_Q_RANK)))), gckv=row(b_g_ckv),
        sink=a_sink, wpool=wpool, cscale=row(c_scale),
        convw=jnp.pad(d_conv, ((0, 0), (0, SUBLANES - D_CONV), (0, 0))),
        prow=prow, pcol=pcol, gnorm=row(jnp.tile(d_g_norm, (1, D_HEADS))),
        wbr=w_br.astype(BF16), wo=w_o.astype(BF16), wup=wup.astype(BF16), wdown=w_down.astype(BF16))


def _trunk_layer(x2d, mod3, lw, n_seq, t, ctx):
    latent = ctx is not None
    n = x2d.shape[0]
    tm = min(512, t) if latent else 256
    tm_dense = min(512, n)
    if latent:
        tables = _rope_tables(t, A_HD, A_HEADS) + _rope_tables_mla_wide(t)
        proj, ckvn, qw, kwr, vb, aqr, akr, qwr = _inproj_call(
            x2d, mod3, lw["gpre1"], lw["w1"], lw["gcq"], lw["wuq"], lw["gckv"], lw["wukv"], tables, t, tm_dense)
        kctx, vctx, ctx_ckv, ckr_wide, s0f, s0b = ctx
        out_a = _win_attn_call(lw["sink"], aqr, akr, proj, kctx, vctx, n_seq, t)
        out_b = _mla_lat_call(qw, qwr, kwr, vb, ctx_ckv, ckr_wide, lw["wukv"], n_seq, t, min(256, t))
        s0 = (s0f, s0b)
    else:
        proj, ckvn, qw, kw, vb = _inproj_call(
            x2d, mod3, lw["gpre1"], lw["w1"], lw["gcq"], lw["wuq"], lw["gckv"], lw["wukv"], None, t, tm_dense)
        out_a, out_b = _attn_ctx_call(lw["sink"], proj, qw, kw, vb, n_seq, t)
        s0 = None
    out_c, qkvn = _local_call(proj, lw["convw"], lw["wpool"], lw["cscale"], t, tm)
    o_f, o_b, s_fin = _gdn_call(qkvn, proj, lw["prow"], lw["pcol"], s0, n_seq, t)
    x1 = _merge_call(x2d, mod3, lw["gpre1"], out_a, out_b, out_c, o_f, o_b, proj, lw["gnorm"],
                     lw["wg"], lw["wbr"], lw["wo"], lw["gpost1"], t, tm_dense)
    x2 = _ffn_call(x1, mod3, lw["gpre2"], lw["wup"], lw["wdown"], lw["gpost2"], t, tm_dense)
    new_ctx = None
    if not latent:
        p3 = proj.reshape(n_seq, t, P1)
        ka = jnp.stack([p3[:, :, C_AK:C_AK + A_HD], p3[:, :, C_AK + 2 * A_HD:C_AK + 3 * A_HD]], axis=2)
        va = jnp.stack([p3[:, :, C_AV:C_AV + A_HD], p3[:, :, C_AV + 2 * A_HD:C_AV + 3 * A_HD]], axis=2)
        new_ctx = (ka, va, ckvn.reshape(n_seq, t, B_KV_RANK), p3[:, :, C_KR:C_KR + B_ROPE],
                   s_fin[:, 0], s_fin[:, 1])
    return x2, new_ctx


def kernel(x_prompt, x_sample, cache_a_k, cache_a_v, cache_b_ckv, cache_b_krope, state_d_fwd, state_d_bwd, c, c_ctx, w_mod, b_mod, g_pre1, g_post1, g_pre2, g_post2, w_in, a_sink, b_g_cq, b_g_ckv, b_w_uq, b_w_ukv, c_w_pool, c_scale, d_conv, d_a_log, d_dt_bias, d_g_norm, w_br, w_o, w_up, w_down):
    depth = w_mod.shape[0]
    bp, tp, d = x_prompt.shape
    bs, ts, _ = x_sample.shape
    assert bs + 1 <= SUBLANES
    cond8 = jnp.zeros((SUBLANES, d), F32).at[0].set(c_ctx).at[1:1 + bs].set(c)
    mod_all = _mod_call(cond8, w_mod, b_mod).reshape(depth, SUBLANES, 1, 6 * d)
    yp = x_prompt.reshape(bp * tp, d)
    ys = x_sample.reshape(bs * ts, d)
    new = [[] for _ in range(6)]
    kctx_all, vctx_all = _expand_kv_heads(cache_a_k), _expand_kv_heads(cache_a_v)
    ckr_wide_all = jnp.tile(jnp.pad(cache_b_krope, ((0, 0), (0, 0), (0, 0),
                                                    (B_NOPE, MLA_HEAD_LANES - B_NOPE - B_ROPE))), (1, 1, 1, B_HEADS))
    stacked = _stacked_weights(g_pre1, g_post1, g_pre2, g_post2, w_in, a_sink, b_g_cq, b_g_ckv, b_w_uq, b_w_ukv,
                               c_w_pool, c_scale, d_conv, d_a_log, d_dt_bias, d_g_norm, w_br, w_o, w_up, w_down)
    for l in range(depth):
        lw = {k: _Layer(v, l) for k, v in stacked.items()}
        mod_ctx = _Mod(mod_all, l, 0, 1)
        mod_lat = _Mod(mod_all, l, 1, bs)
        yp, nctx = _trunk_layer(yp, mod_ctx, lw, bp, tp, None)
        for acc, val in zip(new, nctx):
            acc.append(val)
        ctx = tuple(_Layer(a, l) for a in (kctx_all, vctx_all, cache_b_ckv, ckr_wide_all, state_d_fwd, state_d_bwd))
        ys, _ = _trunk_layer(ys, mod_lat, lw, bs, ts, ctx)
    return (yp.reshape(bp, tp, d), ys.reshape(bs, ts, d)) + tuple(jnp.stack(v, axis=1) for v in new)
```

```python
import functools
import math
from typing import NamedTuple

import jax
import jax.numpy as jnp
from jax import lax
from jax.experimental import pallas as pl
from jax.experimental.pallas import tpu as pltpu

F32 = jnp.float32
BF16 = jnp.bfloat16

D_MODEL = 1024
GRID_W = 64
ROPE_BASE = 10000.0
EPS = 1e-6
NEG_INF = -1e30
Q_BLOCK = 128
N_BRANCH = 4
BRANCH_W = 256
A_HEADS = 4
A_KV_HEADS = 2
A_HD = 64
A_WINDOW = 128
A_SCALE = A_HD ** -0.5
B_HEADS = 4
B_Q_RANK = 192
B_KV_RANK = 128
B_NOPE = 64
B_ROPE = 32
B_VD = 64
B_SCALE = (B_NOPE + B_ROPE) ** -0.5
C_GROUPS = 4
C_GW = 64
C_WINDOWS = (2, 4, 8, 16)
D_HEADS = 4
D_DK = 64
D_DV = 64
D_CONV = 5
D_CHUNK = 64
FF_RAW = -(-8 * D_MODEL // 3)
D_FF = -(-FF_RAW // 256) * 256
IN_SIZES = (A_HEADS * A_HD, A_KV_HEADS * A_HD, A_KV_HEADS * A_HD, B_Q_RANK, B_KV_RANK, B_ROPE,
            C_GROUPS * C_GW, D_HEADS * (2 * D_DK + D_DV), D_HEADS * D_DV, 2 * D_HEADS, 2 * D_HEADS,
            N_BRANCH * D_MODEL)

LANES = 128
SUBLANES = 8
VMEM_LIMIT_BYTES = 56 * 1024 * 1024
HALO = SUBLANES
GDN_SEQS_PER_STEP = 2
GDN_CHUNKS_PER_STEP = 4
MLA_HEAD_LANES = LANES
MLA_WIDE = B_HEADS * MLA_HEAD_LANES
LOG2E = math.log2(math.e)
WIN_ATTN_TQ = 512

C_AQ = 0
C_AK = 256
C_AV = 512
C_CQ = 768
C_CKV = 1024
C_KR = 1152
C_CIN = 1280
C_QKV = 1536
C_Z = 2304
C_AB = 2560
P1 = 2688


def _cparams(sem):
    return pltpu.CompilerParams(dimension_semantics=sem, vmem_limit_bytes=VMEM_LIMIT_BYTES)


class _Layer(NamedTuple):
    arr: jax.Array
    l: int


def _resident(p):
    shape = p.arr.shape[1:]
    return pl.BlockSpec((None,) + shape, lambda *_: (p.l,) + (0,) * len(shape), pipeline_mode=pl.Buffered(1))


class _Mod(NamedTuple):
    arr: jax.Array
    l: int
    first: int
    count: int


def _mod_block(m, tiles_per_seq):
    width = m.arr.shape[-1]
    if m.count == 1:
        return pl.BlockSpec((None, 1, 1, width), lambda i: (m.l, m.first, 0, 0))
    return pl.BlockSpec((None, 1, 1, width), lambda i: (m.l, m.first + i // tiles_per_seq, 0, 0))


def _cache_block(p):
    return pl.BlockSpec((1, None) + p.arr.shape[2:], lambda b, i: (b, p.l, 0, 0))


def _operands(args):
    return [a.arr if isinstance(a, (_Layer, _Mod)) else a for a in args]


def _dot(a, b):
    return jnp.dot(a, b, preferred_element_type=F32)


def _dot_nt(a, b):
    return lax.dot_general(a, b, (((1,), (1,)), ((), ())), preferred_element_type=F32)


def _split3(x):
    hi = x.astype(BF16)
    r1 = x - hi.astype(F32)
    mid = r1.astype(BF16)
    lo = (r1 - mid.astype(F32)).astype(BF16)
    return hi, mid, lo


def _dot_sel_right(x, sel):
    hi, mid, lo = _split3(x)
    return _dot(hi, sel) + _dot(mid, sel) + _dot(lo, sel)


def _silu(x):
    return x * jax.nn.sigmoid(x)


def _softplus(x):
    return jnp.maximum(x, 0.0) + jnp.log1p(jnp.exp(-jnp.abs(x)))


def _head_mask(width, n_lanes, h):
    lane = lax.broadcasted_iota(jnp.int32, (1, n_lanes), 1)
    return (lane // width) == h


def _group_ones(n, width):
    r = lax.broadcasted_iota(jnp.int32, (n, n), 0) // width
    c = lax.broadcasted_iota(jnp.int32, (n, n), 1) // width
    return jnp.where(r == c, 1.0, 0.0).astype(BF16)


def _mod_kernel(c_ref, w_ref, b_ref, o_ref):
    s = _silu(c_ref[...])
    o_ref[0] = _dot(s.astype(BF16), w_ref[0].astype(BF16)) + b_ref[0]


def _mod_call(cond8, w_mod, b_mod):
    depth, d, n = w_mod.shape
    tn = 1536
    return pl.pallas_call(
        _mod_kernel,
        grid=(depth, n // tn),
        in_specs=[pl.BlockSpec((SUBLANES, d), lambda l, j: (0, 0)),
                  pl.BlockSpec((1, d, tn), lambda l, j: (l, 0, j)),
                  pl.BlockSpec((1, 1, tn), lambda l, j: (l, 0, j))],
        out_specs=pl.BlockSpec((1, SUBLANES, tn), lambda l, j: (l, 0, j)),
        out_shape=jax.ShapeDtypeStruct((depth, SUBLANES, n), F32),
        compiler_params=_cparams(("parallel", "parallel")),
        name="mod",
    )(cond8, w_mod, b_mod.reshape(depth, 1, n))


def _rmsnorm_rows(x, g, n):
    ms = jnp.sum(x * x, axis=-1, keepdims=True) * (1.0 / n)
    return x * lax.rsqrt(ms + EPS) * g


def _rope(x, cos, sin_next, sin_prev, quarter):
    n = x.shape[-1]
    return x * cos + pltpu.roll(x, n - quarter, 1) * sin_next + pltpu.roll(x, quarter, 1) * sin_prev


def _mla_wide_kr(kr_rep):
    lane = lax.broadcasted_iota(jnp.int32, (1, MLA_HEAD_LANES), 1)
    keep = (lane >= B_NOPE) & (lane < B_NOPE + B_ROPE)
    return jnp.concatenate([jnp.where(keep, kr_rep, 0.0)] * B_HEADS, axis=1)


def _inproj_kernel(latent, *refs):
    if latent:
        (x_ref, mod_ref, gpre_ref, w1_ref, gcq_ref, wuq_ref, gckv_ref, wukv_ref,
         ca_ref, sna_ref, spa_ref, cb_ref, snb_ref, spb_ref,
         proj_ref, ckv_ref, qw_ref, kw_ref, vb_ref, aqr_ref, akr_ref, qwr_ref) = refs
    else:
        (x_ref, mod_ref, gpre_ref, w1_ref, gcq_ref, wuq_ref, gckv_ref, wukv_ref,
         proj_ref, ckv_ref, qw_ref, kw_ref, vb_ref) = refs
    d = D_MODEL
    x = x_ref[...]
    mod = mod_ref[0]
    sh1 = mod[:, 0:d]
    sc1 = mod[:, d:2 * d]
    h = _rmsnorm_rows(x, gpre_ref[...], d) * (1.0 + sc1) + sh1
    proj = _dot(h.astype(BF16), w1_ref[...])
    proj_ref[...] = proj
    cqn = _rmsnorm_rows(proj[:, C_CQ:C_CQ + 256], gcq_ref[...], B_Q_RANK)
    qw = _dot(cqn.astype(BF16), wuq_ref[...])
    qw_ref[...] = qw.astype(BF16)
    ckvn = _rmsnorm_rows(proj[:, C_CKV:C_CKV + B_KV_RANK], gckv_ref[...], B_KV_RANK)
    ckv_ref[...] = ckvn
    knv = _dot(ckvn.astype(BF16), wukv_ref[...])
    vb_ref[...] = knv[:, MLA_WIDE:].astype(BF16)
    kr_wide = _mla_wide_kr(proj[:, C_KR:C_KR + 128])
    if latent:
        ca, sna, spa = ca_ref[...], sna_ref[...], spa_ref[...]
        cb, snb, spb = cb_ref[...], snb_ref[...], spb_ref[...]
        aqr_ref[...] = _rope(proj[:, C_AQ:C_AQ + 256], ca, sna, spa, A_HD // 4).astype(BF16)
        akr_ref[...] = _rope(proj[:, C_AK:C_AK + 256], ca, sna, spa, A_HD // 4).astype(BF16)
        qwr_ref[...] = _rope(qw, cb, snb, spb, B_ROPE // 4).astype(BF16)
        kr_wide = _rope(kr_wide, cb, snb, spb, B_ROPE // 4)
    kw_ref[...] = (knv[:, :MLA_WIDE] + kr_wide).astype(BF16)


def _inproj_call(x2d, mod3, gpre, w1, gcq, wuq, gckv, wukv, tables, seq_len, tm):
    n, d = x2d.shape
    latent = tables is not None
    tiles_per_seq = seq_len // tm if latent else 1
    row = lambda w, dt=F32: (pl.BlockSpec((tm, w), lambda i: (i, 0)), jax.ShapeDtypeStruct((n, w), dt))
    params = [gpre, w1, gcq, wuq, gckv, wukv]
    in_specs = [pl.BlockSpec((tm, d), lambda i: (i, 0)),
                _mod_block(mod3, seq_len // tm)] + [_resident(p) for p in params]
    args = [x2d, mod3] + params
    outs = [row(P1), row(B_KV_RANK), row(MLA_WIDE, BF16), row(MLA_WIDE, BF16), row(256, BF16)]
    if latent:
        for t in tables:
            in_specs.append(pl.BlockSpec((tm, t.shape[1]), lambda i: (i % tiles_per_seq, 0)))
            args.append(t)
        outs += [row(256, BF16), row(256, BF16), row(MLA_WIDE, BF16)]
    return pl.pallas_call(
        functools.partial(_inproj_kernel, latent),
        grid=(n // tm,),
        in_specs=in_specs,
        out_specs=[o[0] for o in outs],
        out_shape=[o[1] for o in outs],
        compiler_params=_cparams(("parallel",)),
        name="inproj_lat" if latent else "inproj_ctx",
    )(*_operands(args))


def _softmax_pv(scores, values, scale, sink, hm):
    c1 = scale * LOG2E
    m = jnp.max(scores[0], axis=-1, keepdims=True)
    for s in scores[1:]:
        m = jnp.maximum(m, jnp.max(s, axis=-1, keepdims=True))
    m2 = m * c1
    if sink is not None:
        m2 = jnp.maximum(m2, sink * LOG2E)
    den = None
    pv = None
    for s, v in zip(scores, values):
        p = jnp.exp2(s * c1 - m2)
        ps = jnp.sum(p, axis=-1, keepdims=True)
        den = ps if den is None else den + ps
        t = _dot(p.astype(BF16), v)
        pv = t if pv is None else pv + t
    if sink is not None:
        den = den + jnp.exp2(sink * LOG2E - m2)
    return jnp.where(hm, pv / den, 0.0)


def _attn_ctx_kernel(layer, sink_ref, q_ref, k_ref, v_ref, qw_ref, kw_ref, vb_ref, oa_ref, ob_ref):
    q = q_ref[...]
    k = k_ref[...].astype(BF16)
    v = v_ref[...].astype(BF16)
    acc = jnp.zeros(q.shape, F32)
    for h in range(A_HEADS):
        hm = _head_mask(A_HD, 256, h)
        s = _dot_nt(jnp.where(hm, q, 0.0).astype(BF16), k)
        acc = acc + _softmax_pv([s], [v], A_SCALE, sink_ref[layer, h], hm)
    oa_ref[...] = acc
    vb = vb_ref[...]
    acc = jnp.zeros((q.shape[0], B_HEADS * B_VD), F32)
    for h in range(B_HEADS):
        hs = slice(h * MLA_HEAD_LANES, (h + 1) * MLA_HEAD_LANES)
        s = _dot_nt(qw_ref[:, hs], kw_ref[:, hs])
        acc = acc + _softmax_pv([s], [vb], B_SCALE, None, _head_mask(B_VD, 256, h))
    ob_ref[...] = acc


def _attn_ctx_call(sink, proj, qw, kw, vb, n_seq, t):
    n = proj.shape[0]
    col = lambda w, j: pl.BlockSpec((t, w), lambda b: (b, j))
    out = (pl.BlockSpec((t, 256), lambda b: (b, 0)), jax.ShapeDtypeStruct((n, 256), F32))
    return pl.pallas_call(
        functools.partial(_attn_ctx_kernel, sink.l),
        grid=(n_seq,),
        in_specs=[pl.BlockSpec(memory_space=pltpu.SMEM),
                  col(256, C_AQ // 256), col(256, C_AK // 256), col(256, C_AV // 256),
                  col(MLA_WIDE, 0), col(MLA_WIDE, 0), col(256, 0)],
        out_specs=[out[0], out[0]],
        out_shape=[out[1], out[1]],
        compiler_params=_cparams(("parallel",)),
        name="attn_ctx",
    )(sink.arr, proj, proj, proj, qw, kw, vb)


def _win_attn_kernel(t, layer, sink_ref, qr_ref, q_ref, kp_ref, kc_ref, kn_ref, vp_ref, vc_ref, vn_ref,
                     kctx_ref, vctx_ref, o_ref):
    i = pl.program_id(1)
    qr = qr_ref[...]
    q = q_ref[...]
    tq = q.shape[0]
    kl = jnp.concatenate([kp_ref[...], kc_ref[...], kn_ref[...]], axis=0)
    vl = jnp.concatenate([vp_ref[...], vc_ref[...], vn_ref[...]], axis=0).astype(BF16)
    kctx = kctx_ref[0].astype(BF16)
    vctx = vctx_ref[0].astype(BF16)
    q_pos = i * tq + lax.broadcasted_iota(jnp.int32, (tq, tq + 2 * A_WINDOW), 0)
    k_pos = i * tq - A_WINDOW + lax.broadcasted_iota(jnp.int32, (tq, tq + 2 * A_WINDOW), 1)
    valid = (jnp.abs(q_pos - k_pos) <= A_WINDOW) & (k_pos >= 0) & (k_pos < t)
    acc = jnp.zeros(q.shape, F32)
    for h in range(A_HEADS):
        hm = _head_mask(A_HD, 256, h)
        s_loc = jnp.where(valid, _dot_nt(jnp.where(hm, qr, 0.0).astype(BF16), kl), NEG_INF)
        s_ctx = _dot_nt(jnp.where(hm, q, 0.0).astype(BF16), kctx)
        acc = acc + _softmax_pv([s_loc, s_ctx], [vl, vctx], A_SCALE, sink_ref[layer, h], hm)
    o_ref[...] = acc


def _win_attn_call(sink, aqr, akr, proj, kctx, vctx, n_seq, t):
    n = proj.shape[0]
    tq = min(WIN_ATTN_TQ, t)
    nq = t // tq
    r = tq // A_WINDOW
    nh = t // A_WINDOW
    cur = lambda j: pl.BlockSpec((tq, 256), lambda b, i: (b * nq + i, j))
    prev = lambda j: pl.BlockSpec((A_WINDOW, 256), lambda b, i: (b * nh + jnp.maximum(i * r - 1, 0), j))
    nxt = lambda j: pl.BlockSpec((A_WINDOW, 256), lambda b, i: (b * nh + jnp.minimum((i + 1) * r, nh - 1), j))
    ctx = _cache_block(kctx)
    return pl.pallas_call(
        functools.partial(_win_attn_kernel, t, sink.l),
        grid=(n_seq, nq),
        in_specs=[pl.BlockSpec(memory_space=pltpu.SMEM),
                  cur(0), cur(C_AQ // 256),
                  prev(0), cur(0), nxt(0),
                  prev(C_AV // 256), cur(C_AV // 256), nxt(C_AV // 256),
                  ctx, ctx],
        out_specs=pl.BlockSpec((tq, 256), lambda b, i: (b * nq + i, 0)),
        out_shape=jax.ShapeDtypeStruct((n, 256), F32),
        compiler_params=_cparams(("parallel", "parallel")),
        name="win_attn",
    )(sink.arr, aqr, proj, akr, akr, akr, proj, proj, proj, kctx.arr, vctx.arr)


def _mla_lat_kernel(qw_ref, qwr_ref, kwr_ref, vb_ref, cckv_ref, ckr_ref, wukv_ref, o_ref):
    kvc = _dot(cckv_ref[0].astype(BF16), wukv_ref[...])
    kwc = (kvc[:, :MLA_WIDE] + ckr_ref[0]).astype(BF16)
    vc = kvc[:, MLA_WIDE:].astype(BF16)
    v = vb_ref[...]
    acc = jnp.zeros((qw_ref.shape[0], B_HEADS * B_VD), F32)
    for h in range(B_HEADS):
        hs = slice(h * MLA_HEAD_LANES, (h + 1) * MLA_HEAD_LANES)
        s_lat = _dot_nt(qwr_ref[:, hs], kwr_ref[:, hs])
        s_ctx = _dot_nt(qw_ref[:, hs], kwc[:, hs])
        acc = acc + _softmax_pv([s_lat, s_ctx], [v, vc], B_SCALE, None, _head_mask(B_VD, 256, h))
    o_ref[...] = acc


def _mla_lat_call(qw, qwr, kwr, vb, cckv, ckr_wide, wukv, n_seq, t, tq):
    n = qw.shape[0]
    nq = t // tq
    qblk = lambda w: pl.BlockSpec((tq, w), lambda b, i: (b * nq + i, 0))
    seq = lambda w: pl.BlockSpec((t, w), lambda b, i: (b, 0))
    ctx = _cache_block
    return pl.pallas_call(
        _mla_lat_kernel,
        grid=(n_seq, nq),
        in_specs=[qblk(MLA_WIDE), qblk(MLA_WIDE), seq(MLA_WIDE), seq(256), ctx(cckv), ctx(ckr_wide),
                  _resident(wukv)],
        out_specs=pl.BlockSpec((tq, 256), lambda b, i: (b * nq + i, 0)),
        out_shape=jax.ShapeDtypeStruct((n, 256), F32),
        compiler_params=_cparams(("parallel", "parallel")),
        name="mla_lat",
    )(qw, qwr, kwr, vb, cckv.arr, ckr_wide.arr, wukv.arr)


def _with_halo(prev_ref, cur_ref, next_ref, has_prev, has_next):
    prev = jnp.where(has_prev, prev_ref[...], 0.0)
    nxt = jnp.where(has_next, next_ref[...], 0.0)
    return jnp.concatenate([prev, cur_ref[...], nxt], axis=0)


def _shift_rows(x, k):
    n = x.shape[0]
    return pltpu.roll(x, (-k) % n, 0)


def _local_kernel(t, tm, cp_ref, cc_ref, cn_ref, qp_ref, qc_ref, qn_ref, convw_ref, wpool_ref, cscale_ref,
                  oc_ref, qkv_ref):
    i = pl.program_id(0)
    tiles_per_seq = t // tm
    j = i % tiles_per_seq
    has_prev = j > 0
    has_next = j < tiles_per_seq - 1
    x = _with_halo(cp_ref, cc_ref, cn_ref, has_prev, has_next)
    p2 = x + _shift_rows(x, -1)
    p4 = _shift_rows(p2, 1) + _shift_rows(p2, -1)
    p8 = _shift_rows(p4, 2) + _shift_rows(p4, -2)
    p16 = _shift_rows(p8, 4) + _shift_rows(p8, -4)
    grp = lax.broadcasted_iota(jnp.int32, (1, 256), 1) // C_GW
    win = jnp.where(grp == 0, p2, jnp.where(grp == 1, p4, jnp.where(grp == 2, p8, p16)))[HALO:HALO + tm]
    pos = j * tm + lax.broadcasted_iota(jnp.int32, (tm, 256), 0)
    half = jnp.where(grp == 0, 1, jnp.where(grp == 1, 2, jnp.where(grp == 2, 4, 8)))
    cnt = (jnp.minimum(pos + half, t) - jnp.maximum(pos - half, 0)).astype(F32)
    y = win / cnt - cc_ref[...]
    oc_ref[...] = _dot(y.astype(BF16), wpool_ref[...]) * cscale_ref[...]
    xq = _with_halo(qp_ref, qc_ref, qn_ref, has_prev, has_next)
    w = convw_ref[...]
    pad = D_CONV // 2
    acc = None
    for tap in range(D_CONV):
        term = _shift_rows(xq, tap - pad) * w[tap:tap + 1, :]
        acc = term if acc is None else acc + term
    u = _silu(acc[HALO:HALO + tm])
    ones = _group_ones(256, D_DK)
    q = u[:, 0:256]
    k = u[:, 256:512]
    q = q * lax.rsqrt(_dot_sel_right(q * q, ones) + EPS) * (D_DK ** -0.5)
    k = k * lax.rsqrt(_dot_sel_right(k * k, ones) + EPS)
    qkv_ref[:, 0:256] = q
    qkv_ref[:, 256:512] = k
    qkv_ref[:, 512:768] = u[:, 512:768]


def _local_call(proj, convw8, wpool_bd, cscale, t, tm):
    n = proj.shape[0]
    hb = tm // HALO
    nblk = n // HALO
    cur = lambda w, c: pl.BlockSpec((tm, w), lambda i: (i, c))
    prev = lambda w, c: pl.BlockSpec((HALO, w), lambda i: (jnp.maximum(i * hb - 1, 0), c))
    nxt = lambda w, c: pl.BlockSpec((HALO, w), lambda i: (jnp.minimum((i + 1) * hb, nblk - 1), c))
    full = _resident
    cc, qc = C_CIN // 256, C_QKV // 768
    return pl.pallas_call(
        functools.partial(_local_kernel, t, tm),
        grid=(n // tm,),
        in_specs=[prev(256, cc), cur(256, cc), nxt(256, cc), prev(768, qc), cur(768, qc), nxt(768, qc),
                  full(convw8), full(wpool_bd), full(cscale)],
        out_specs=[pl.BlockSpec((tm, 256), lambda i: (i, 0)), pl.BlockSpec((tm, 768), lambda i: (i, 0))],
        out_shape=[jax.ShapeDtypeStruct((n, 256), F32), jax.ShapeDtypeStruct((n, 768), F32)],
        compiler_params=_cparams(("parallel",)),
        name="local",
    )(proj, proj, proj, proj, proj, proj, convw8.arr, wpool_bd.arr, cscale.arr)


def _gdn_kernel(n_steps, g_chunks, n_sq, has_s0, xf_ref, xb_ref, abf_ref, abb_ref, prow_ref, pcol_ref, *refs):
    if has_s0:
        s0f_ref, s0b_ref, of_ref, ob_ref, sfin_ref, s_ref = refs
    else:
        of_ref, ob_ref, sfin_ref, s_ref = refs
    i = pl.program_id(1)
    c = D_CHUNK
    nh = D_HEADS

    @pl.when(i == 0)
    def _():
        if has_s0:
            for sq in range(n_sq):
                s_ref[sq * nh:(sq + 1) * nh] = s0f_ref[sq]
                s_ref[(n_sq + sq) * nh:(n_sq + sq + 1) * nh] = s0b_ref[sq]
        else:
            s_ref[...] = jnp.zeros(s_ref.shape, F32)

    r = lax.broadcasted_iota(jnp.int32, (c, c), 0)
    cidx = lax.broadcasted_iota(jnp.int32, (c, c), 1)
    lower = jnp.where(cidx <= r, 1.0, 0.0).astype(BF16)
    upper = jnp.where(cidx >= r, 1.0, 0.0).astype(BF16)
    prow = prow_ref[...]
    pcol = pcol_ref[...]

    eye = jnp.where(cidx == r, 1.0, 0.0).astype(BF16)
    sel_lower = jnp.concatenate([lower] * 3, axis=1)
    sel_upper = jnp.concatenate([upper] * 3, axis=1)
    pick = jnp.where(lax.broadcasted_iota(jnp.int32, (16, LANES), 0) == lax.broadcasted_iota(jnp.int32, (16, LANES), 1),
                     1.0, 0.0).astype(BF16)
    sel_rows = jnp.concatenate([pick] * 3, axis=1)
    sel_lower_t = jnp.concatenate([lower] * 3, axis=0)
    sel_upper_t = jnp.concatenate([upper] * 3, axis=0)

    bdot = lambda a, b: jnp.einsum('bik,bkj->bij', a, b, preferred_element_type=F32)
    bdot_nt = lambda a, b: jnp.einsum('bik,bjk->bij', a, b, preferred_element_type=F32)

    def prepare(steps):
        nb = 2 * n_sq * len(steps) * nh
        is_fwd = lax.broadcasted_iota(jnp.int32, (nb, c, c), 0) < nb // 2
        r3 = lax.broadcasted_iota(jnp.int32, (nb, c, c), 1)
        c3 = lax.broadcasted_iota(jnp.int32, (nb, c, c), 2)
        ahead = jnp.where(is_fwd, r3 - c3, c3 - r3)
        incl = ahead >= 0
        strict = ahead > 0
        q_l, k_l, v_l, b_l, gcol_l, grow_l = [], [], [], [], [], []
        for d, sq, step in [(d, sq, step) for d in range(2) for sq in range(n_sq) for step in steps]:
            x_ref, ab_ref = ((xf_ref, abf_ref), (xb_ref, abb_ref))[d]
            gi = step if d == 0 else g_chunks - 1 - step
            rows = slice(gi * c, (gi + 1) * c)
            x = x_ref[sq, rows, :]
            ab = ab_ref[sq, rows, :]
            abr = _dot_nt(sel_rows, jnp.concatenate(_split3(ab), axis=1))
            beta_c = jax.nn.sigmoid(ab)
            g_c = -jnp.exp(prow[0:1, :]) * _softplus(ab + prow[1:2, :])
            g_r = -jnp.exp(pcol[:, 0:1]) * _softplus(abr + pcol[:, 1:2])
            if d == 0:
                gc_c = _dot(sel_lower, jnp.concatenate(_split3(g_c), axis=0))
                gc_r = _dot(jnp.concatenate(_split3(g_r), axis=1), sel_upper_t)
            else:
                gc_c = _dot(sel_upper, jnp.concatenate(_split3(g_c), axis=0))
                gc_r = _dot(jnp.concatenate(_split3(g_r), axis=1), sel_lower_t)
            for h in range(nh):
                hs = slice(h * D_DK, (h + 1) * D_DK)
                lane = nh * d + h
                q_l.append(x[:, 0:256][:, hs])
                k_l.append(x[:, 256:512][:, hs])
                v_l.append(x[:, 512:768][:, hs])
                b_l.append(beta_c[:, lane:lane + 1])
                gcol_l.append(gc_c[:, 8 + lane:9 + lane])
                grow_l.append(gc_r[8 + lane:9 + lane, :])
        qh, kh, vh = jnp.stack(q_l), jnp.stack(k_l), jnp.stack(v_l)
        bcol, gcol, grow = jnp.stack(b_l), jnp.stack(gcol_l), jnp.stack(grow_l)
        glast = jnp.concatenate([gcol[:nb // 2, c - 1:c, :], gcol[nb // 2:, 0:1, :]], axis=0)
        decay = jnp.exp(jnp.where(incl, gcol - grow, NEG_INF))
        kb = kh * bcol
        kq = bdot_nt(jnp.concatenate([kb, qh], axis=1).astype(BF16), kh.astype(BF16))
        p = -jnp.where(strict, kq[:, 0:c] * decay, 0.0)
        egc = jnp.exp(gcol)
        xs = jnp.concatenate([vh * bcol, kb * egc], axis=2)
        for it in range(6):
            ph = p.astype(BF16)
            pl_ = (p - ph.astype(F32)).astype(BF16)
            lhs = jnp.concatenate([ph, pl_, ph], axis=2)
            yh = xs.astype(BF16)
            yl = (xs - yh.astype(F32)).astype(BF16)
            if it < 5:
                yh = jnp.concatenate([yh, ph], axis=2)
                yl = jnp.concatenate([yl, pl_], axis=2)
            m = bdot(lhs, jnp.concatenate([yh, yh, yl], axis=1))
            xs = xs + m[:, :, 0:2 * D_DK]
            if it < 5:
                p = m[:, :, 2 * D_DK:3 * D_DK]
        u = xs[:, :, 0:D_DV]
        w = xs[:, :, D_DV:2 * D_DV]
        a = kq[:, c:2 * c] * decay
        kd = (kh * jnp.exp(glast - gcol)).astype(BF16)
        kdt = bdot_nt(jnp.broadcast_to(eye, (nb, c, c)), kd).astype(BF16)
        lhs_s = jnp.concatenate([w, qh * egc], axis=1).astype(BF16)
        lhs_v = jnp.concatenate([a.astype(BF16), kdt], axis=1)
        return u, lhs_s, lhs_v, jnp.exp(glast)

    prep = prepare(list(range(g_chunks)))

    def chains(arr, step):
        los = [((d * n_sq + sq) * g_chunks + step) * nh for d in range(2) for sq in range(n_sq)]
        return jnp.concatenate([arr[lo:lo + nh] for lo in los], axis=0)

    s = s_ref[...]
    outs = [[[None] * g_chunks for _ in range(n_sq)] for _ in range(2)]
    for step in range(g_chunks):
        u, lhs_s, lhs_v, eg = (chains(arr, step) for arr in prep)
        ws_qs = bdot(lhs_s, s.astype(BF16))
        vb = (u - ws_qs[:, 0:c]).astype(BF16)
        av = bdot(lhs_v, vb)
        o = ws_qs[:, c:2 * c] + av[:, 0:c]
        s = s * eg + av[:, c:2 * c]
        for d in range(2):
            gi = step if d == 0 else g_chunks - 1 - step
            for sq in range(n_sq):
                base = (d * n_sq + sq) * nh
                outs[d][sq][gi] = jnp.concatenate([o[base + h] for h in range(nh)], axis=1)
    for sq in range(n_sq):
        of_ref[sq] = jnp.concatenate(outs[0][sq], axis=0)
        ob_ref[sq] = jnp.concatenate(outs[1][sq], axis=0)
    s_ref[...] = s

    @pl.when(i == n_steps - 1)
    def _():
        for d in range(2):
            for sq in range(n_sq):
                sfin_ref[sq, d] = s[(d * n_sq + sq) * nh:(d * n_sq + sq + 1) * nh]


def _gdn_call(qkvn, proj, prow, pcol, s0, n_seq, t):
    g = GDN_CHUNKS_PER_STEP
    n_sq = GDN_SEQS_PER_STEP if n_seq % GDN_SEQS_PER_STEP == 0 else 1
    ns = t // (D_CHUNK * g)
    rows = D_CHUNK * g
    fwd = lambda i: i
    bwd = lambda i: ns - 1 - i
    x3 = qkvn.reshape(n_seq, t, 768)
    p3 = proj.reshape(n_seq, t, P1)
    xblk = lambda f: pl.BlockSpec((n_sq, rows, 768), lambda b, i: (b, f(i), 0))
    abblk = lambda f: pl.BlockSpec((n_sq, rows, 128), lambda b, i: (b, f(i), C_AB // 128))
    oblk = lambda f: pl.BlockSpec((n_sq, rows, 256), lambda b, i: (b, f(i), 0))
    sblk = pl.BlockSpec((n_sq, 2, D_HEADS, D_DK, D_DV), lambda b, i: (b, 0, 0, 0, 0))
    osd = jax.ShapeDtypeStruct((n_seq, t, 256), F32)
    in_specs = [xblk(fwd), xblk(bwd), abblk(fwd), abblk(bwd), _resident(prow), _resident(pcol)]
    args = [x3, x3, p3, p3, prow.arr, pcol.arr]
    if s0 is not None:
        for st in s0:
            in_specs.append(pl.BlockSpec((n_sq, None, D_HEADS, D_DK, D_DV), lambda b, i, l=st.l: (b, l, 0, 0, 0)))
            args.append(st.arr)
    o_f, o_b, s_fin = pl.pallas_call(
        functools.partial(_gdn_kernel, ns, g, n_sq, s0 is not None),
        grid=(n_seq // n_sq, ns),
        in_specs=in_specs,
        out_specs=[oblk(fwd), oblk(bwd), sblk],
        out_shape=[osd, osd, jax.ShapeDtypeStruct((n_seq, 2, D_HEADS, D_DK, D_DV), F32)],
        scratch_shapes=[pltpu.VMEM((2 * n_sq * D_HEADS, D_DK, D_DV), F32)],
        compiler_params=_cparams(("parallel", "arbitrary")),
        name="gdn",
    )(*args)
    return o_f.reshape(n_seq * t, 256), o_b.reshape(n_seq * t, 256), s_fin


def _merge_kernel(x_ref, mod_ref, gpre_ref, oa_ref, ob_ref, oc_ref, of_ref, obw_ref, z_ref, gnorm_ref,
                  wg_ref, wbr_ref, wo_ref, gpost_ref, o_ref):
    d = D_MODEL
    x = x_ref[...]
    mod = mod_ref[0]
    sh1, sc1, ga1 = mod[:, 0:d], mod[:, d:2 * d], mod[:, 2 * d:3 * d]
    h = (_rmsnorm_rows(x, gpre_ref[...], d) * (1.0 + sc1) + sh1).astype(BF16)
    o = of_ref[...] + obw_ref[...]
    ms = _dot_sel_right(o * o, _group_ones(256, D_DV)) * (1.0 / D_DV)
    od = o * lax.rsqrt(ms + EPS) * gnorm_ref[...] * _silu(z_ref[...])
    acc = None
    for m, br in enumerate((oa_ref[...], ob_ref[...], oc_ref[...], od)):
        gate = jax.nn.sigmoid(_dot(h, wg_ref[:, m * d:(m + 1) * d]))
        term = gate * _dot(br.astype(BF16), wbr_ref[m])
        acc = term if acc is None else acc + term
    mix = _dot(acc.astype(BF16), wo_ref[...])
    o_ref[...] = x + ga1 * _rmsnorm_rows(mix, gpost_ref[...], d)


def _merge_call(x2d, mod3, gpre, oa, ob, oc, of, obw, proj, gnorm, wg, wbr, wo, gpost, seq_len, tm):
    n, d = x2d.shape
    tiles_per_seq = seq_len // tm
    row = lambda w, c=0: pl.BlockSpec((tm, w), lambda i: (i, c))
    full = _resident
    return pl.pallas_call(
        _merge_kernel,
        grid=(n // tm,),
        in_specs=[row(d), _mod_block(mod3, tiles_per_seq), full(gpre),
                  row(256), row(256), row(256), row(256), row(256), row(256, C_Z // 256), full(gnorm),
                  full(wg), full(wbr), full(wo), full(gpost)],
        out_specs=row(d),
        out_shape=jax.ShapeDtypeStruct((n, d), F32),
        compiler_params=_cparams(("parallel",)),
        name="merge",
    )(*_operands([x2d, mod3, gpre, oa, ob, oc, of, obw, proj, gnorm, wg, wbr, wo, gpost]))


def _ffn_kernel(x_ref, mod_ref, gpre_ref, wup_ref, wdown_ref, gpost_ref, o_ref):
    d = D_MODEL
    x = x_ref[...]
    mod = mod_ref[0]
    sh2, sc2, ga2 = mod[:, 3 * d:4 * d], mod[:, 4 * d:5 * d], mod[:, 5 * d:6 * d]
    h = (_rmsnorm_rows(x, gpre_ref[...], d) * (1.0 + sc2) + sh2).astype(BF16)
    up = _dot(h, wup_ref[...])
    act = _silu(up[:, 0:D_FF]) * up[:, D_FF:2 * D_FF]
    f = _dot(act.astype(BF16), wdown_ref[...])
    o_ref[...] = x + ga2 * _rmsnorm_rows(f, gpost_ref[...], d)


def _ffn_call(x2d, mod3, gpre, wup, wdown, gpost, seq_len, tm):
    n, d = x2d.shape
    tiles_per_seq = seq_len // tm
    row = lambda w: pl.BlockSpec((tm, w), lambda i: (i, 0))
    full = _resident
    return pl.pallas_call(
        _ffn_kernel,
        grid=(n // tm,),
        in_specs=[row(d), _mod_block(mod3, tiles_per_seq), full(gpre), full(wup), full(wdown), full(gpost)],
        out_specs=row(d),
        out_shape=jax.ShapeDtypeStruct((n, d), F32),
        compiler_params=_cparams(("parallel",)),
        name="ffn",
    )(*_operands([x2d, mod3, gpre, wup, wdown, gpost]))


def _rope_tables(n_tokens, dim, reps):
    n_rows = n_tokens // GRID_W
    row = jnp.repeat(jnp.arange(n_rows), GRID_W).astype(F32)
    col = jnp.tile(jnp.arange(GRID_W), n_rows).astype(F32)
    nfreq = dim // 4
    inv = ROPE_BASE ** (-jnp.arange(nfreq, dtype=F32) / nfreq)
    ang_r = row[:, None] * inv
    ang_c = col[:, None] * inv
    ang = jnp.concatenate([ang_r, ang_r, ang_c, ang_c], axis=-1)
    cos, sin = jnp.cos(ang), jnp.sin(ang)
    even = ((jnp.arange(dim) // nfreq) % 2 == 0)[None, :]
    sin_next = jnp.where(even, -sin, 0.0)
    sin_prev = jnp.where(even, 0.0, sin)
    return tuple(jnp.tile(a, (1, reps)) for a in (cos, sin_next, sin_prev))


def _rope_tables_mla_wide(n_tokens):
    cos, sin_next, sin_prev = _rope_tables(n_tokens, B_ROPE, 1)
    tail = MLA_HEAD_LANES - B_NOPE - B_ROPE
    wide = lambda a, fill: jnp.tile(jnp.pad(a, ((0, 0), (B_NOPE, tail)), constant_values=fill), (1, B_HEADS))
    return wide(cos, 1.0), wide(sin_next, 0.0), wide(sin_prev, 0.0)


def _split_cols(w):
    out, acc = [], 0
    for s in IN_SIZES:
        out.append(w[..., acc:acc + s])
        acc += s
    return out


def _pack_w_in(w):
    (a_q, a_k, a_v, b_cq, b_ckv, b_kr, c_in, d_qkv, d_z, d_beta, d_alpha, gate) = _split_cols(w)
    z = lambda n: jnp.zeros(w.shape[:-1] + (n,), w.dtype)
    k0, k1 = a_k[..., :A_HD], a_k[..., A_HD:]
    v0, v1 = a_v[..., :A_HD], a_v[..., A_HD:]
    w1 = jnp.concatenate([a_q, k0, k0, k1, k1, v0, v0, v1, v1, b_cq, z(256 - B_Q_RANK), b_ckv,
                          b_kr, b_kr, b_kr, b_kr, c_in, d_qkv, d_z, d_beta, d_alpha, z(112)], axis=-1)
    return w1.astype(BF16), gate.astype(BF16)


def _expand_kv_heads(a):
    return jnp.repeat(a, A_HEADS // A_KV_HEADS, axis=-2).reshape(a.shape[:-2] + (A_HEADS * A_HD,))


def _stacked_weights(g_pre1, g_post1, g_pre2, g_post2, w_in, a_sink, b_g_cq, b_g_ckv, b_w_uq, b_w_ukv,
                     c_w_pool, c_scale, d_conv, d_a_log, d_dt_bias, d_g_norm, w_br, w_o, w_up, w_down):
    depth = w_in.shape[0]
    w1, wg = _pack_w_in(w_in)
    wuq = b_w_uq.reshape(depth, B_Q_RANK, B_HEADS, B_NOPE + B_ROPE)
    wuq = jnp.pad(wuq, ((0, 0), (0, 256 - B_Q_RANK), (0, 0), (0, MLA_HEAD_LANES - B_NOPE - B_ROPE)))
    wuq = wuq.reshape(depth, 256, MLA_WIDE).astype(BF16)
    wukv = b_w_ukv.reshape(depth, B_KV_RANK, B_HEADS, B_NOPE + B_VD)
    wk = jnp.pad(wukv[..., :B_NOPE], ((0, 0), (0, 0), (0, 0), (0, MLA_HEAD_LANES - B_NOPE)))
    wukv = jnp.concatenate([wk.reshape(depth, B_KV_RANK, MLA_WIDE),
                            wukv[..., B_NOPE:].reshape(depth, B_KV_RANK, B_HEADS * B_VD)], -1).astype(BF16)
    same = jnp.eye(C_GROUPS, dtype=F32)[None, :, None, :, None]
    wpool = (c_w_pool[:, :, :, None, :] * same).reshape(depth, C_GROUPS * C_GW, C_GROUPS * C_GW).astype(BF16)
    gdn =jnp.stack([d_a_log.reshape(depth, 2 * D_HEADS), d_dt_bias.reshape(depth, 2 * D_HEADS)], axis=1)
    prow = jnp.pad(gdn, ((0, 0), (0, 0), (8, LANES - 16)))
    pcol = jnp.pad(gdn.transpose(0, 2, 1), ((0, 0), (8, 0), (0, 0)))
    row = lambda a: a[:, None, :]
    return dict(
        w1=w1, wg=wg, wuq=wuq, wukv=wukv,
        gpre1=row(g_pre1), gpost1=row(g_post1), gpre2=row(g_pre2), gpost2=row(g_post2),
        gcq=row(jnp.pad(b_g_cq, ((0, 0), (0, 256 - B_Q_RANK)))), gckv=row(b_g_ckv),
        sink=a_sink, wpool=wpool, cscale=row(c_scale),
        convw=jnp.pad(d_conv, ((0, 0), (0, SUBLANES - D_CONV), (0, 0))),
        prow=prow, pcol=pcol, gnorm=row(jnp.tile(d_g_norm, (1, D_HEADS))),
        wbr=w_br.astype(BF16), wo=w_o.astype(BF16), wup=w_up.astype(BF16), wdown=w_down.astype(BF16))


def _trunk_layer(x2d, mod3, lw, n_seq, t, ctx):
    latent = ctx is not None
    n = x2d.shape[0]
    tm = min(512, t) if latent else 256
    tm_dense = min(512, n)
    if latent:
        tables = _rope_tables(t, A_HD, A_HEADS) + _rope_tables_mla_wide(t)
        proj, ckvn, qw, kwr, vb, aqr, akr, qwr = _inproj_call(
            x2d, mod3, lw["gpre1"], lw["w1"], lw["gcq"], lw["wuq"], lw["gckv"], lw["wukv"], tables, t, tm_dense)
        kctx, vctx, ctx_ckv, ckr_wide, s0f, s0b = ctx
        out_a = _win_attn_call(lw["sink"], aqr, akr, proj, kctx, vctx, n_seq, t)
        out_b = _mla_lat_call(qw, qwr, kwr, vb, ctx_ckv, ckr_wide, lw["wukv"], n_seq, t, min(256, t))
        s0 = (s0f, s0b)
    else:
        proj, ckvn, qw, kw, vb = _inproj_call(
            x2d, mod3, lw["gpre1"], lw["w1"], lw["gcq"], lw["wuq"], lw["gckv"], lw["wukv"], None, t, tm_dense)
        out_a, out_b = _attn_ctx_call(lw["sink"], proj, qw, kw, vb, n_seq, t)
        s0 = None
    out_c, qkvn = _local_call(proj, lw["convw"], lw["wpool"], lw["cscale"], t, tm)
    o_f, o_b, s_fin = _gdn_call(qkvn, proj, lw["prow"], lw["pcol"], s0, n_seq, t)
    x1 = _merge_call(x2d, mod3, lw["gpre1"], out_a, out_b, out_c, o_f, o_b, proj, lw["gnorm"],
                     lw["wg"], lw["wbr"], lw["wo"], lw["gpost1"], t, tm_dense)
    x2 = _ffn_call(x1, mod3, lw["gpre2"], lw["wup"], lw["wdown"], lw["gpost2"], t, tm_dense)
    new_ctx = None
    if not latent:
        p3 = proj.reshape(n_seq, t, P1)
        ka = jnp.stack([p3[:, :, C_AK:C_AK + A_HD], p3[:, :, C_AK + 2 * A_HD:C_AK + 3 * A_HD]], axis=2)
        va = jnp.stack([p3[:, :, C_AV:C_AV + A_HD], p3[:, :, C_AV + 2 * A_HD:C_AV + 3 * A_HD]], axis=2)
        new_ctx = (ka, va, ckvn.reshape(n_seq, t, B_KV_RANK), p3[:, :, C_KR:C_KR + B_ROPE],
                   s_fin[:, 0], s_fin[:, 1])
    return x2, new_ctx


def kernel(x_prompt, x_sample, cache_a_k, cache_a_v, cache_b_ckv, cache_b_krope, state_d_fwd, state_d_bwd, c, c_ctx, w_mod, b_mod, g_pre1, g_post1, g_pre2, g_post2, w_in, a_sink, b_g_cq, b_g_ckv, b_w_uq, b_w_ukv, c_w_pool, c_scale, d_conv, d_a_log, d_dt_bias, d_g_norm, w_br, w_o, w_up, w_down):
    depth = w_mod.shape[0]
    bp, tp, d = x_prompt.shape
    bs, ts, _ = x_sample.shape
    assert bs + 1 <= SUBLANES
    cond8 = jnp.zeros((SUBLANES, d), F32).at[0].set(c_ctx).at[1:1 + bs].set(c)
    mod_all = _mod_call(cond8, w_mod, b_mod).reshape(depth, SUBLANES, 1, 6 * d)
    yp = x_prompt.reshape(bp * tp, d)
    ys = x_sample.reshape(bs * ts, d)
    new = [[] for _ in range(6)]
    kctx_all, vctx_all = _expand_kv_heads(cache_a_k), _expand_kv_heads(cache_a_v)
    ckr_wide_all = jnp.tile(jnp.pad(cache_b_krope, ((0, 0), (0, 0), (0, 0),
                                                    (B_NOPE, MLA_HEAD_LANES - B_NOPE - B_ROPE))), (1, 1, 1, B_HEADS))
    stacked = _stacked_weights(g_pre1, g_post1, g_pre2, g_post2, w_in, a_sink, b_g_cq, b_g_ckv, b_w_uq, b_w_ukv,
                               c_w_pool, c_scale, d_conv, d_a_log, d_dt_bias, d_g_norm, w_br, w_o, w_up, w_down)
    for l in range(depth):
        lw = {k: _Layer(v, l) for k, v in stacked.items()}
        mod_ctx = _Mod(mod_all, l, 0, 1)
        mod_lat = _Mod(mod_all, l, 1, bs)
        yp, nctx = _trunk_layer(yp, mod_ctx, lw, bp, tp, None)
        for acc, val in zip(new, nctx):
            acc.append(val)
        ctx = tuple(_Layer(a, l) for a in (kctx_all, vctx_all, cache_b_ckv, ckr_wide_all, state_d_fwd, state_d_bwd))
        ys, _ = _trunk_layer(ys, mod_lat, lw, bs, ts, ctx)
    return (yp.reshape(bp, tp, d), ys.reshape(bs, ts, d)) + tuple(jnp.stack(v, axis=1) for v in new)
```

```python
import functools
import math
from typing import NamedTuple

import jax
import jax.numpy as jnp
from jax import lax
from jax.experimental import pallas as pl
from jax.experimental.pallas import tpu as pltpu

F32 = jnp.float32
BF16 = jnp.bfloat16

D_MODEL = 1024
GRID_W = 64
ROPE_BASE = 10000.0
EPS = 1e-6
NEG_INF = -1e30
Q_BLOCK = 128
N_BRANCH = 4
BRANCH_W = 256
A_HEADS = 4
A_KV_HEADS = 2
A_HD = 64
A_WINDOW = 128
A_SCALE = A_HD ** -0.5
B_HEADS = 4
B_Q_RANK = 192
B_KV_RANK = 128
B_NOPE = 64
B_ROPE = 32
B_VD = 64
B_SCALE = (B_NOPE + B_ROPE) ** -0.5
C_GROUPS = 4
C_GW = 64
C_WINDOWS = (2, 4, 8, 16)
D_HEADS = 4
D_DK = 64
D_DV = 64
D_CONV = 5
D_CHUNK = 64
FF_RAW = -(-8 * D_MODEL // 3)
D_FF = -(-FF_RAW // 256) * 256
IN_SIZES = (A_HEADS * A_HD, A_KV_HEADS * A_HD, A_KV_HEADS * A_HD, B_Q_RANK, B_KV_RANK, B_ROPE,
            C_GROUPS * C_GW, D_HEADS * (2 * D_DK + D_DV), D_HEADS * D_DV, 2 * D_HEADS, 2 * D_HEADS,
            N_BRANCH * D_MODEL)

LANES = 128
SUBLANES = 8
VMEM_LIMIT_BYTES = 56 * 1024 * 1024
HALO = SUBLANES
GDN_SEQS_PER_STEP = 2
GDN_CHUNKS_PER_STEP = 4
MLA_HEAD_LANES = LANES
MLA_WIDE = B_HEADS * MLA_HEAD_LANES
LOG2E = math.log2(math.e)
MLA_TQ = 512
MLA_ROW_GROUP = 32
MLA_COL_BLOCK = 256
WIN_ATTN_TQ = 512

C_AQ = 0
C_AK = 256
C_AV = 512
C_CQ = 768
C_CKV = 1024
C_KR = 1152
C_CIN = 1280
C_QKV = 1536
C_Z = 2304
C_AB = 2560
P1 = 2688


def _cparams(sem):
    return pltpu.CompilerParams(dimension_semantics=sem, vmem_limit_bytes=VMEM_LIMIT_BYTES)


class _Layer(NamedTuple):
    arr: jax.Array
    l: int


def _resident(p):
    shape = p.arr.shape[1:]
    return pl.BlockSpec((None,) + shape, lambda *_: (p.l,) + (0,) * len(shape), pipeline_mode=pl.Buffered(1))


class _Mod(NamedTuple):
    arr: jax.Array
    l: int
    first: int
    count: int


def _mod_block(m, tiles_per_seq):
    width = m.arr.shape[-1]
    if m.count == 1:
        return pl.BlockSpec((None, 1, 1, width), lambda i: (m.l, m.first, 0, 0))
    return pl.BlockSpec((None, 1, 1, width), lambda i: (m.l, m.first + i // tiles_per_seq, 0, 0))


def _cache_block(p):
    return pl.BlockSpec((1, None) + p.arr.shape[2:], lambda b, i: (b, p.l, 0, 0))


def _operands(args):
    return [a.arr if isinstance(a, (_Layer, _Mod)) else a for a in args]


def _dot(a, b):
    return jnp.dot(a, b, preferred_element_type=F32)


def _dot_nt(a, b):
    return lax.dot_general(a, b, (((1,), (1,)), ((), ())), preferred_element_type=F32)


def _split3(x):
    hi = x.astype(BF16)
    r1 = x - hi.astype(F32)
    mid = r1.astype(BF16)
    lo = (r1 - mid.astype(F32)).astype(BF16)
    return hi, mid, lo


def _dot_sel_right(x, sel):
    hi, mid, lo = _split3(x)
    return _dot(hi, sel) + _dot(mid, sel) + _dot(lo, sel)


def _silu(x):
    return x * jax.nn.sigmoid(x)


def _softplus(x):
    return jnp.maximum(x, 0.0) + jnp.log1p(jnp.exp(-jnp.abs(x)))


def _head_mask(width, n_lanes, h):
    lane = lax.broadcasted_iota(jnp.int32, (1, n_lanes), 1)
    return (lane // width) == h


def _group_ones(n, width):
    r = lax.broadcasted_iota(jnp.int32, (n, n), 0) // width
    c = lax.broadcasted_iota(jnp.int32, (n, n), 1) // width
    return jnp.where(r == c, 1.0, 0.0).astype(BF16)


def _mod_kernel(c_ref, w_ref, b_ref, o_ref):
    s = _silu(c_ref[...])
    o_ref[0] = _dot(s.astype(BF16), w_ref[0].astype(BF16)) + b_ref[0]


def _mod_call(cond8, w_mod, b_mod):
    depth, d, n = w_mod.shape
    tn = 1536
    return pl.pallas_call(
        _mod_kernel,
        grid=(depth, n // tn),
        in_specs=[pl.BlockSpec((SUBLANES, d), lambda l, j: (0, 0)),
                  pl.BlockSpec((1, d, tn), lambda l, j: (l, 0, j)),
                  pl.BlockSpec((1, 1, tn), lambda l, j: (l, 0, j))],
        out_specs=pl.BlockSpec((1, SUBLANES, tn), lambda l, j: (l, 0, j)),
        out_shape=jax.ShapeDtypeStruct((depth, SUBLANES, n), F32),
        compiler_params=_cparams(("parallel", "parallel")),
        name="mod",
    )(cond8, w_mod, b_mod.reshape(depth, 1, n))


def _rmsnorm_rows(x, g, n):
    ms = jnp.sum(x * x, axis=-1, keepdims=True) * (1.0 / n)
    return x * lax.rsqrt(ms + EPS) * g


def _rope(x, cos, sin_next, sin_prev, quarter):
    n = x.shape[-1]
    return x * cos + pltpu.roll(x, n - quarter, 1) * sin_next + pltpu.roll(x, quarter, 1) * sin_prev


def _mla_wide_kr(kr_rep):
    lane = lax.broadcasted_iota(jnp.int32, (1, MLA_HEAD_LANES), 1)
    keep = (lane >= B_NOPE) & (lane < B_NOPE + B_ROPE)
    return jnp.concatenate([jnp.where(keep, kr_rep, 0.0)] * B_HEADS, axis=1)


def _inproj_kernel(latent, *refs):
    if latent:
        (x_ref, mod_ref, gpre_ref, w1_ref, gcq_ref, wuq_ref, gckv_ref, wukv_ref,
         ca_ref, sna_ref, spa_ref, cb_ref, snb_ref, spb_ref,
         proj_ref, ckv_ref, qw_ref, kw_ref, vb_ref, aqr_ref, akr_ref, qwr_ref) = refs
    else:
        (x_ref, mod_ref, gpre_ref, w1_ref, gcq_ref, wuq_ref, gckv_ref, wukv_ref,
         proj_ref, ckv_ref, qw_ref, kw_ref, vb_ref) = refs
    d = D_MODEL
    x = x_ref[...]
    mod = mod_ref[0]
    sh1 = mod[:, 0:d]
    sc1 = mod[:, d:2 * d]
    h = _rmsnorm_rows(x, gpre_ref[...], d) * (1.0 + sc1) + sh1
    proj = _dot(h.astype(BF16), w1_ref[...])
    proj_ref[...] = proj
    cqn = _rmsnorm_rows(proj[:, C_CQ:C_CQ + 256], gcq_ref[...], B_Q_RANK)
    qw = _dot(cqn.astype(BF16), wuq_ref[...])
    qw_ref[...] = qw.astype(BF16)
    ckvn = _rmsnorm_rows(proj[:, C_CKV:C_CKV + B_KV_RANK], gckv_ref[...], B_KV_RANK)
    ckv_ref[...] = ckvn
    knv = _dot(ckvn.astype(BF16), wukv_ref[...])
    vb_ref[...] = knv[:, MLA_WIDE:].astype(BF16)
    kr_wide = _mla_wide_kr(proj[:, C_KR:C_KR + 128])
    if latent:
        ca, sna, spa = ca_ref[...], sna_ref[...], spa_ref[...]
        cb, snb, spb = cb_ref[...], snb_ref[...], spb_ref[...]
        aqr_ref[...] = _rope(proj[:, C_AQ:C_AQ + 256], ca, sna, spa, A_HD // 4).astype(BF16)
        akr_ref[...] = _rope(proj[:, C_AK:C_AK + 256], ca, sna, spa, A_HD // 4).astype(BF16)
        qwr_ref[...] = _rope(qw, cb, snb, spb, B_ROPE // 4).astype(BF16)
        kr_wide = _rope(kr_wide, cb, snb, spb, B_ROPE // 4)
    kw_ref[...] = (knv[:, :MLA_WIDE] + kr_wide).astype(BF16)


def _inproj_call(x2d, mod3, gpre, w1, gcq, wuq, gckv, wukv, tables, seq_len, tm):
    n, d = x2d.shape
    latent = tables is not None
    tiles_per_seq = seq_len // tm if latent else 1
    row = lambda w, dt=F32: (pl.BlockSpec((tm, w), lambda i: (i, 0)), jax.ShapeDtypeStruct((n, w), dt))
    params = [gpre, w1, gcq, wuq, gckv, wukv]
    in_specs = [pl.BlockSpec((tm, d), lambda i: (i, 0)),
                _mod_block(mod3, seq_len // tm)] + [_resident(p) for p in params]
    args = [x2d, mod3] + params
    outs = [row(P1), row(B_KV_RANK), row(MLA_WIDE, BF16), row(MLA_WIDE, BF16), row(256, BF16)]
    if latent:
        for t in tables:
            in_specs.append(pl.BlockSpec((tm, t.shape[1]), lambda i: (i % tiles_per_seq, 0)))
            args.append(t)
        outs += [row(256, BF16), row(256, BF16), row(MLA_WIDE, BF16)]
    return pl.pallas_call(
        functools.partial(_inproj_kernel, latent),
        grid=(n // tm,),
        in_specs=in_specs,
        out_specs=[o[0] for o in outs],
        out_shape=[o[1] for o in outs],
        compiler_params=_cparams(("parallel",)),
        name="inproj_lat" if latent else "inproj_ctx",
    )(*_operands(args))


def _softmax_pv(scores, values, scale, sink, hm):
    c1 = scale * LOG2E
    m = jnp.max(scores[0], axis=-1, keepdims=True)
    for s in scores[1:]:
        m = jnp.maximum(m, jnp.max(s, axis=-1, keepdims=True))
    m2 = m * c1
    if sink is not None:
        m2 = jnp.maximum(m2, sink * LOG2E)
    den = None
    pv = None
    for s, v in zip(scores, values):
        p = jnp.exp2(s * c1 - m2)
        ps = jnp.sum(p, axis=-1, keepdims=True)
        den = ps if den is None else den + ps
        t = _dot(p.astype(BF16), v)
        pv = t if pv is None else pv + t
    if sink is not None:
        den = den + jnp.exp2(sink * LOG2E - m2)
    return jnp.where(hm, pv / den, 0.0)


def _attn_ctx_kernel(layer, sink_ref, q_ref, k_ref, v_ref, qw_ref, kw_ref, vb_ref, oa_ref, ob_ref):
    q = q_ref[...]
    k = k_ref[...].astype(BF16)
    v = v_ref[...].astype(BF16)
    acc = jnp.zeros(q.shape, F32)
    for h in range(A_HEADS):
        hm = _head_mask(A_HD, 256, h)
        s = _dot_nt(jnp.where(hm, q, 0.0).astype(BF16), k)
        acc = acc + _softmax_pv([s], [v], A_SCALE, sink_ref[layer, h], hm)
    oa_ref[...] = acc
    vb = vb_ref[...]
    acc = jnp.zeros((q.shape[0], B_HEADS * B_VD), F32)
    for h in range(B_HEADS):
        hs = slice(h * MLA_HEAD_LANES, (h + 1) * MLA_HEAD_LANES)
        s = _dot_nt(qw_ref[:, hs], kw_ref[:, hs])
        acc = acc + _softmax_pv([s], [vb], B_SCALE, None, _head_mask(B_VD, 256, h))
    ob_ref[...] = acc


def _attn_ctx_call(sink, proj, qw, kw, vb, n_seq, t):
    n = proj.shape[0]
    col = lambda w, j: pl.BlockSpec((t, w), lambda b: (b, j))
    out = (pl.BlockSpec((t, 256), lambda b: (b, 0)), jax.ShapeDtypeStruct((n, 256), F32))
    return pl.pallas_call(
        functools.partial(_attn_ctx_kernel, sink.l),
        grid=(n_seq,),
        in_specs=[pl.BlockSpec(memory_space=pltpu.SMEM),
                  col(256, C_AQ // 256), col(256, C_AK // 256), col(256, C_AV // 256),
                  col(MLA_WIDE, 0), col(MLA_WIDE, 0), col(256, 0)],
        out_specs=[out[0], out[0]],
        out_shape=[out[1], out[1]],
        compiler_params=_cparams(("parallel",)),
        name="attn_ctx",
    )(sink.arr, proj, proj, proj, qw, kw, vb)


def _win_attn_kernel(t, layer, sink_ref, qr_ref, q_ref, kp_ref, kc_ref, kn_ref, vp_ref, vc_ref, vn_ref,
                     kctx_ref, vctx_ref, o_ref):
    i = pl.program_id(1)
    qr = qr_ref[...]
    q = q_ref[...]
    tq = q.shape[0]
    kl = jnp.concatenate([kp_ref[...], kc_ref[...], kn_ref[...]], axis=0)
    vl = jnp.concatenate([vp_ref[...], vc_ref[...], vn_ref[...]], axis=0).astype(BF16)
    kctx = kctx_ref[0].astype(BF16)
    vctx = vctx_ref[0].astype(BF16)
    q_pos = i * tq + lax.broadcasted_iota(jnp.int32, (tq, tq + 2 * A_WINDOW), 0)
    k_pos = i * tq - A_WINDOW + lax.broadcasted_iota(jnp.int32, (tq, tq + 2 * A_WINDOW), 1)
    valid = (jnp.abs(q_pos - k_pos) <= A_WINDOW) & (k_pos >= 0) & (k_pos < t)
    acc = jnp.zeros(q.shape, F32)
    for h in range(A_HEADS):
        hm = _head_mask(A_HD, 256, h)
        s_loc = jnp.where(valid, _dot_nt(jnp.where(hm, qr, 0.0).astype(BF16), kl), NEG_INF)
        s_ctx = _dot_nt(jnp.where(hm, q, 0.0).astype(BF16), kctx)
        acc = acc + _softmax_pv([s_loc, s_ctx], [vl, vctx], A_SCALE, sink_ref[layer, h], hm)
    o_ref[...] = acc


def _win_attn_call(sink, aqr, akr, proj, kctx, vctx, n_seq, t):
    n = proj.shape[0]
    tq = min(WIN_ATTN_TQ, t)
    nq = t // tq
    r = tq // A_WINDOW
    nh = t // A_WINDOW
    cur = lambda j: pl.BlockSpec((tq, 256), lambda b, i: (b * nq + i, j))
    prev = lambda j: pl.BlockSpec((A_WINDOW, 256), lambda b, i: (b * nh + jnp.maximum(i * r - 1, 0), j))
    nxt = lambda j: pl.BlockSpec((A_WINDOW, 256), lambda b, i: (b * nh + jnp.minimum((i + 1) * r, nh - 1), j))
    ctx = _cache_block(kctx)
    return pl.pallas_call(
        functools.partial(_win_attn_kernel, t, sink.l),
        grid=(n_seq, nq),
        in_specs=[pl.BlockSpec(memory_space=pltpu.SMEM),
                  cur(0), cur(C_AQ // 256),
                  prev(0), cur(0), nxt(0),
                  prev(C_AV // 256), cur(C_AV // 256), nxt(C_AV // 256),
                  ctx, ctx],
        out_specs=pl.BlockSpec((tq, 256), lambda b, i: (b * nq + i, 0)),
        out_shape=jax.ShapeDtypeStruct((n, 256), F32),
        compiler_params=_cparams(("parallel", "parallel")),
        name="win_attn",
    )(sink.arr, aqr, proj, akr, akr, akr, proj, proj, proj, kctx.arr, vctx.arr)


def _mla_lat_kernel(qw_ref, qwr_ref, kwr_ref, vb_ref, cckv_ref, ckr_ref, wukv_ref, o_ref, s_ref, p_ref, den_ref):
    kvc = _dot(cckv_ref[0].astype(BF16), wukv_ref[...])
    kwc = (kvc[:, :MLA_WIDE] + ckr_ref[0]).astype(BF16)
    vc = kvc[:, MLA_WIDE:].astype(BF16)
    tq = qw_ref.shape[0]
    t = kwr_ref.shape[0]
    n_keys = s_ref.shape[2]
    c1 = B_SCALE * LOG2E
    cb = MLA_COL_BLOCK
    rg = MLA_ROW_GROUP

    def scores_into(slot, h):
        hs = slice(h * MLA_HEAD_LANES, (h + 1) * MLA_HEAD_LANES)
        s_ref[slot, :, 0:t] = _dot_nt(qwr_ref[:, hs], kwr_ref[:, hs])
        s_ref[slot, :, t:n_keys] = _dot_nt(qw_ref[:, hs], kwc[:, hs])

    scores_into(0, 0)
    acc = jnp.zeros((tq, B_HEADS * B_VD), F32)
    for h in range(B_HEADS):
        slot = h % 2
        if h + 1 < B_HEADS:
            scores_into(1 - slot, h + 1)
        for r0 in range(0, tq, rg):
            rows = slice(r0, r0 + rg)
            m = s_ref[slot, rows, 0:cb]
            for c0 in range(cb, n_keys, cb):
                m = jnp.maximum(m, s_ref[slot, rows, c0:c0 + cb])
            m2 = jnp.max(m, axis=-1, keepdims=True) * c1
            part = None
            for c0 in range(0, n_keys, cb):
                p = jnp.exp2(s_ref[slot, rows, c0:c0 + cb] * c1 - m2)
                part = p if part is None else part + p
                p_ref[rows, c0:c0 + cb] = p.astype(BF16)
            den_ref[rows, :] = jnp.broadcast_to(jnp.sum(part, axis=-1, keepdims=True), (rg, LANES))
        pv = _dot(p_ref[:, 0:t], vb_ref[...]) + _dot(p_ref[:, t:n_keys], vc)
        acc = acc + jnp.where(_head_mask(B_VD, 256, h), pv / den_ref[:, 0:1], 0.0)
    o_ref[...] = acc


def _mla_lat_call(qw, qwr, kwr, vb, cckv, ckr_wide, wukv, n_seq, t, tq):
    n = qw.shape[0]
    nq = t // tq
    qblk = lambda w: pl.BlockSpec((tq, w), lambda b, i: (b * nq + i, 0))
    seq = lambda w: pl.BlockSpec((t, w), lambda b, i: (b, 0))
    ctx = _cache_block
    return pl.pallas_call(
        _mla_lat_kernel,
        grid=(n_seq, nq),
        in_specs=[qblk(MLA_WIDE), qblk(MLA_WIDE), seq(MLA_WIDE), seq(256), ctx(cckv), ctx(ckr_wide),
                  _resident(wukv)],
        out_specs=pl.BlockSpec((tq, 256), lambda b, i: (b * nq + i, 0)),
        out_shape=jax.ShapeDtypeStruct((n, 256), F32),
        scratch_shapes=[pltpu.VMEM((2, tq, t + cckv.arr.shape[2]), F32), pltpu.VMEM((tq, t + cckv.arr.shape[2]), BF16),
                        pltpu.VMEM((tq, LANES), F32)],
        compiler_params=_cparams(("parallel", "parallel")),
        name="mla_lat",
    )(qw, qwr, kwr, vb, cckv.arr, ckr_wide.arr, wukv.arr)


def _with_halo(prev_ref, cur_ref, next_ref, has_prev, has_next):
    prev = jnp.where(has_prev, prev_ref[...], 0.0)
    nxt = jnp.where(has_next, next_ref[...], 0.0)
    return jnp.concatenate([prev, cur_ref[...], nxt], axis=0)


def _shift_rows(x, k):
    n = x.shape[0]
    return pltpu.roll(x, (-k) % n, 0)


def _local_kernel(t, tm, cp_ref, cc_ref, cn_ref, qp_ref, qc_ref, qn_ref, convw_ref, wpool_ref, cscale_ref,
                  oc_ref, qkv_ref):
    i = pl.program_id(0)
    tiles_per_seq = t // tm
    j = i % tiles_per_seq
    has_prev = j > 0
    has_next = j < tiles_per_seq - 1
    x = _with_halo(cp_ref, cc_ref, cn_ref, has_prev, has_next)
    p2 = x + _shift_rows(x, -1)
    p4 = _shift_rows(p2, 1) + _shift_rows(p2, -1)
    p8 = _shift_rows(p4, 2) + _shift_rows(p4, -2)
    p16 = _shift_rows(p8, 4) + _shift_rows(p8, -4)
    grp = lax.broadcasted_iota(jnp.int32, (1, 256), 1) // C_GW
    win = jnp.where(grp == 0, p2, jnp.where(grp == 1, p4, jnp.where(grp == 2, p8, p16)))[HALO:HALO + tm]
    pos = j * tm + lax.broadcasted_iota(jnp.int32, (tm, 256), 0)
    half = jnp.where(grp == 0, 1, jnp.where(grp == 1, 2, jnp.where(grp == 2, 4, 8)))
    cnt = (jnp.minimum(pos + half, t) - jnp.maximum(pos - half, 0)).astype(F32)
    y = win / cnt - cc_ref[...]
    oc_ref[...] = _dot(y.astype(BF16), wpool_ref[...]) * cscale_ref[...]
    xq = _with_halo(qp_ref, qc_ref, qn_ref, has_prev, has_next)
    w = convw_ref[...]
    pad = D_CONV // 2
    acc = None
    for tap in range(D_CONV):
        term = _shift_rows(xq, tap - pad) * w[tap:tap + 1, :]
        acc = term if acc is None else acc + term
    u = _silu(acc[HALO:HALO + tm])
    ones = _group_ones(256, D_DK)
    q = u[:, 0:256]
    k = u[:, 256:512]
    q = q * lax.rsqrt(_dot_sel_right(q * q, ones) + EPS) * (D_DK ** -0.5)
    k = k * lax.rsqrt(_dot_sel_right(k * k, ones) + EPS)
    qkv_ref[:, 0:256] = q
    qkv_ref[:, 256:512] = k
    qkv_ref[:, 512:768] = u[:, 512:768]


def _local_call(proj, convw8, wpool_bd, cscale, t, tm):
    n = proj.shape[0]
    hb = tm // HALO
    nblk = n // HALO
    cur = lambda w, c: pl.BlockSpec((tm, w), lambda i: (i, c))
    prev = lambda w, c: pl.BlockSpec((HALO, w), lambda i: (jnp.maximum(i * hb - 1, 0), c))
    nxt = lambda w, c: pl.BlockSpec((HALO, w), lambda i: (jnp.minimum((i + 1) * hb, nblk - 1), c))
    full = _resident
    cc, qc = C_CIN // 256, C_QKV // 768
    return pl.pallas_call(
        functools.partial(_local_kernel, t, tm),
        grid=(n // tm,),
        in_specs=[prev(256, cc), cur(256, cc), nxt(256, cc), prev(768, qc), cur(768, qc), nxt(768, qc),
                  full(convw8), full(wpool_bd), full(cscale)],
        out_specs=[pl.BlockSpec((tm, 256), lambda i: (i, 0)), pl.BlockSpec((tm, 768), lambda i: (i, 0))],
        out_shape=[jax.ShapeDtypeStruct((n, 256), F32), jax.ShapeDtypeStruct((n, 768), F32)],
        compiler_params=_cparams(("parallel",)),
        name="local",
    )(proj, proj, proj, proj, proj, proj, convw8.arr, wpool_bd.arr, cscale.arr)


def _gdn_kernel(n_steps, g_chunks, n_sq, has_s0, xf_ref, xb_ref, abf_ref, abb_ref, prow_ref, pcol_ref, *refs):
    if has_s0:
        s0f_ref, s0b_ref, of_ref, ob_ref, sfin_ref, s_ref = refs
    else:
        of_ref, ob_ref, sfin_ref, s_ref = refs
    i = pl.program_id(1)
    c = D_CHUNK
    nh = D_HEADS

    @pl.when(i == 0)
    def _():
        if has_s0:
            for sq in range(n_sq):
                s_ref[sq * nh:(sq + 1) * nh] = s0f_ref[sq]
                s_ref[(n_sq + sq) * nh:(n_sq + sq + 1) * nh] = s0b_ref[sq]
        else:
            s_ref[...] = jnp.zeros(s_ref.shape, F32)

    r = lax.broadcasted_iota(jnp.int32, (c, c), 0)
    cidx = lax.broadcasted_iota(jnp.int32, (c, c), 1)
    lower = jnp.where(cidx <= r, 1.0, 0.0).astype(BF16)
    upper = jnp.where(cidx >= r, 1.0, 0.0).astype(BF16)
    prow = prow_ref[...]
    pcol = pcol_ref[...]

    eye = jnp.where(cidx == r, 1.0, 0.0).astype(BF16)
    sel_lower = jnp.concatenate([lower] * 3, axis=1)
    sel_upper = jnp.concatenate([upper] * 3, axis=1)
    pick = jnp.where(lax.broadcasted_iota(jnp.int32, (16, LANES), 0) == lax.broadcasted_iota(jnp.int32, (16, LANES), 1),
                     1.0, 0.0).astype(BF16)
    sel_rows = jnp.concatenate([pick] * 3, axis=1)
    sel_lower_t = jnp.concatenate([lower] * 3, axis=0)
    sel_upper_t = jnp.concatenate([upper] * 3, axis=0)

    bdot = lambda a, b: jnp.einsum('bik,bkj->bij', a, b, preferred_element_type=F32)
    bdot_nt = lambda a, b: jnp.einsum('bik,bjk->bij', a, b, preferred_element_type=F32)

    def prepare(steps):
        nb = 2 * n_sq * len(steps) * nh
        is_fwd = lax.broadcasted_iota(jnp.int32, (nb, c, c), 0) < nb // 2
        r3 = lax.broadcasted_iota(jnp.int32, (nb, c, c), 1)
        c3 = lax.broadcasted_iota(jnp.int32, (nb, c, c), 2)
        ahead = jnp.where(is_fwd, r3 - c3, c3 - r3)
        incl = ahead >= 0
        strict = ahead > 0
        q_l, k_l, v_l, b_l, gcol_l, grow_l = [], [], [], [], [], []
        for d, sq, step in [(d, sq, step) for d in range(2) for sq in range(n_sq) for step in steps]:
            x_ref, ab_ref = ((xf_ref, abf_ref), (xb_ref, abb_ref))[d]
            gi = step if d == 0 else g_chunks - 1 - step
            rows = slice(gi * c, (gi + 1) * c)
            x = x_ref[sq, rows, :]
            ab = ab_ref[sq, rows, :]
            abr = _dot_nt(sel_rows, jnp.concatenate(_split3(ab), axis=1))
            beta_c = jax.nn.sigmoid(ab)
            g_c = -jnp.exp(prow[0:1, :]) * _softplus(ab + prow[1:2, :])
            g_r = -jnp.exp(pcol[:, 0:1]) * _softplus(abr + pcol[:, 1:2])
            if d == 0:
                gc_c = _dot(sel_lower, jnp.concatenate(_split3(g_c), axis=0))
                gc_r = _dot(jnp.concatenate(_split3(g_r), axis=1), sel_upper_t)
            else:
                gc_c = _dot(sel_upper, jnp.concatenate(_split3(g_c), axis=0))
                gc_r = _dot(jnp.concatenate(_split3(g_r), axis=1), sel_lower_t)
            for h in range(nh):
                hs = slice(h * D_DK, (h + 1) * D_DK)
                lane = nh * d + h
                q_l.append(x[:, 0:256][:, hs])
                k_l.append(x[:, 256:512][:, hs])
                v_l.append(x[:, 512:768][:, hs])
                b_l.append(beta_c[:, lane:lane + 1])
                gcol_l.append(gc_c[:, 8 + lane:9 + lane])
                grow_l.append(gc_r[8 + lane:9 + lane, :])
        qh, kh, vh = jnp.stack(q_l), jnp.stack(k_l), jnp.stack(v_l)
        bcol, gcol, grow = jnp.stack(b_l), jnp.stack(gcol_l), jnp.stack(grow_l)
        glast = jnp.concatenate([gcol[:nb // 2, c - 1:c, :], gcol[nb // 2:, 0:1, :]], axis=0)
        decay = jnp.exp(jnp.where(incl, gcol - grow, NEG_INF))
        kb = kh * bcol
        kq = bdot_nt(jnp.concatenate([kb, qh], axis=1).astype(BF16), kh.astype(BF16))
        p = -jnp.where(strict, kq[:, 0:c] * decay, 0.0)
        egc = jnp.exp(gcol)
        xs = jnp.concatenate([vh * bcol, kb * egc], axis=2)
        for it in range(6):
            ph = p.astype(BF16)
            pl_ = (p - ph.astype(F32)).astype(BF16)
            lhs = jnp.concatenate([ph, pl_, ph], axis=2)
            yh = xs.astype(BF16)
            yl = (xs - yh.astype(F32)).astype(BF16)
            if it < 5:
                yh = jnp.concatenate([yh, ph], axis=2)
                yl = jnp.concatenate([yl, pl_], axis=2)
            m = bdot(lhs, jnp.concatenate([yh, yh, yl], axis=1))
            xs = xs + m[:, :, 0:2 * D_DK]
            if it < 5:
                p = m[:, :, 2 * D_DK:3 * D_DK]
        u = xs[:, :, 0:D_DV]
        w = xs[:, :, D_DV:2 * D_DV]
        a = kq[:, c:2 * c] * decay
        kd = (kh * jnp.exp(glast - gcol)).astype(BF16)
        kdt = bdot_nt(jnp.broadcast_to(eye, (nb, c, c)), kd).astype(BF16)
        lhs_s = jnp.concatenate([w, qh * egc], axis=1).astype(BF16)
        lhs_v = jnp.concatenate([a.astype(BF16), kdt], axis=1)
        return u, lhs_s, lhs_v, jnp.exp(glast)

    prep = prepare(list(range(g_chunks)))

    def chains(arr, step):
        los = [((d * n_sq + sq) * g_chunks + step) * nh for d in range(2) for sq in range(n_sq)]
        return jnp.concatenate([arr[lo:lo + nh] for lo in los], axis=0)

    s = s_ref[...]
    outs = [[[None] * g_chunks for _ in range(n_sq)] for _ in range(2)]
    for step in range(g_chunks):
        u, lhs_s, lhs_v, eg = (chains(arr, step) for arr in prep)
        ws_qs = bdot(lhs_s, s.astype(BF16))
        vb = (u - ws_qs[:, 0:c]).astype(BF16)
        av = bdot(lhs_v, vb)
        o = ws_qs[:, c:2 * c] + av[:, 0:c]
        s = s * eg + av[:, c:2 * c]
        for d in range(2):
            gi = step if d == 0 else g_chunks - 1 - step
            for sq in range(n_sq):
                base = (d * n_sq + sq) * nh
                outs[d][sq][gi] = jnp.concatenate([o[base + h] for h in range(nh)], axis=1)
    for sq in range(n_sq):
        of_ref[sq] = jnp.concatenate(outs[0][sq], axis=0)
        ob_ref[sq] = jnp.concatenate(outs[1][sq], axis=0)
    s_ref[...] = s

    @pl.when(i == n_steps - 1)
    def _():
        for d in range(2):
            for sq in range(n_sq):
                sfin_ref[sq, d] = s[(d * n_sq + sq) * nh:(d * n_sq + sq + 1) * nh]


def _gdn_call(qkvn, proj, prow, pcol, s0, n_seq, t):
    g = GDN_CHUNKS_PER_STEP
    n_sq = GDN_SEQS_PER_STEP if n_seq % GDN_SEQS_PER_STEP == 0 else 1
    ns = t // (D_CHUNK * g)
    rows = D_CHUNK * g
    fwd = lambda i: i
    bwd = lambda i: ns - 1 - i
    x3 = qkvn.reshape(n_seq, t, 768)
    p3 = proj.reshape(n_seq, t, P1)
    xblk = lambda f: pl.BlockSpec((n_sq, rows, 768), lambda b, i: (b, f(i), 0))
    abblk = lambda f: pl.BlockSpec((n_sq, rows, 128), lambda b, i: (b, f(i), C_AB // 128))
    oblk = lambda f: pl.BlockSpec((n_sq, rows, 256), lambda b, i: (b, f(i), 0))
    sblk = pl.BlockSpec((n_sq, 2, D_HEADS, D_DK, D_DV), lambda b, i: (b, 0, 0, 0, 0))
    osd = jax.ShapeDtypeStruct((n_seq, t, 256), F32)
    in_specs = [xblk(fwd), xblk(bwd), abblk(fwd), abblk(bwd), _resident(prow), _resident(pcol)]
    args = [x3, x3, p3, p3, prow.arr, pcol.arr]
    if s0 is not None:
        for st in s0:
            in_specs.append(pl.BlockSpec((n_sq, None, D_HEADS, D_DK, D_DV), lambda b, i, l=st.l: (b, l, 0, 0, 0)))
            args.append(st.arr)
    o_f, o_b, s_fin = pl.pallas_call(
        functools.partial(_gdn_kernel, ns, g, n_sq, s0 is not None),
        grid=(n_seq // n_sq, ns),
        in_specs=in_specs,
        out_specs=[oblk(fwd), oblk(bwd), sblk],
        out_shape=[osd, osd, jax.ShapeDtypeStruct((n_seq, 2, D_HEADS, D_DK, D_DV), F32)],
        scratch_shapes=[pltpu.VMEM((2 * n_sq * D_HEADS, D_DK, D_DV), F32)],
        compiler_params=_cparams(("parallel", "arbitrary")),
        name="gdn",
    )(*args)
    return o_f.reshape(n_seq * t, 256), o_b.reshape(n_seq * t, 256), s_fin


def _merge_kernel(x_ref, mod_ref, gpre_ref, oa_ref, ob_ref, oc_ref, of_ref, obw_ref, z_ref, gnorm_ref,
                  wg_ref, wbr_ref, wo_ref, gpost_ref, o_ref):
    d = D_MODEL
    x = x_ref[...]
    mod = mod_ref[0]
    sh1, sc1, ga1 = mod[:, 0:d], mod[:, d:2 * d], mod[:, 2 * d:3 * d]
    h = (_rmsnorm_rows(x, gpre_ref[...], d) * (1.0 + sc1) + sh1).astype(BF16)
    o = of_ref[...] + obw_ref[...]
    ms = _dot_sel_right(o * o, _group_ones(256, D_DV)) * (1.0 / D_DV)
    od = o * lax.rsqrt(ms + EPS) * gnorm_ref[...] * _silu(z_ref[...])
    acc = None
    for m, br in enumerate((oa_ref[...], ob_ref[...], oc_ref[...], od)):
        gate = jax.nn.sigmoid(_dot(h, wg_ref[:, m * d:(m + 1) * d]))
        term = gate * _dot(br.astype(BF16), wbr_ref[m])
        acc = term if acc is None else acc + term
    mix = _dot(acc.astype(BF16), wo_ref[...])
    o_ref[...] = x + ga1 * _rmsnorm_rows(mix, gpost_ref[...], d)


def _merge_call(x2d, mod3, gpre, oa, ob, oc, of, obw, proj, gnorm, wg, wbr, wo, gpost, seq_len, tm):
    n, d = x2d.shape
    tiles_per_seq = seq_len // tm
    row = lambda w, c=0: pl.BlockSpec((tm, w), lambda i: (i, c))
    full = _resident
    return pl.pallas_call(
        _merge_kernel,
        grid=(n // tm,),
        in_specs=[row(d), _mod_block(mod3, tiles_per_seq), full(gpre),
                  row(256), row(256), row(256), row(256), row(256), row(256, C_Z // 256), full(gnorm),
                  full(wg), full(wbr), full(wo), full(gpost)],
        out_specs=row(d),
        out_shape=jax.ShapeDtypeStruct((n, d), F32),
        compiler_params=_cparams(("parallel",)),
        name="merge",
    )(*_operands([x2d, mod3, gpre, oa, ob, oc, of, obw, proj, gnorm, wg, wbr, wo, gpost]))


def _ffn_kernel(x_ref, mod_ref, gpre_ref, wup_ref, wdown_ref, gpost_ref, o_ref):
    d = D_MODEL
    x = x_ref[...]
    mod = mod_ref[0]
    sh2, sc2, ga2 = mod[:, 3 * d:4 * d], mod[:, 4 * d:5 * d], mod[:, 5 * d:6 * d]
    h = (_rmsnorm_rows(x, gpre_ref[...], d) * (1.0 + sc2) + sh2).astype(BF16)
    up = _dot(h, wup_ref[...])
    act = _silu(up[:, 0:D_FF]) * up[:, D_FF:2 * D_FF]
    f = _dot(act.astype(BF16), wdown_ref[...])
    o_ref[...] = x + ga2 * _rmsnorm_rows(f, gpost_ref[...], d)


def _ffn_call(x2d, mod3, gpre, wup, wdown, gpost, seq_len, tm):
    n, d = x2d.shape
    tiles_per_seq = seq_len // tm
    row = lambda w: pl.BlockSpec((tm, w), lambda i: (i, 0))
    full = _resident
    return pl.pallas_call(
        _ffn_kernel,
        grid=(n // tm,),
        in_specs=[row(d), _mod_block(mod3, tiles_per_seq), full(gpre), full(wup), full(wdown), full(gpost)],
        out_specs=row(d),
        out_shape=jax.ShapeDtypeStruct((n, d), F32),
        compiler_params=_cparams(("parallel",)),
        name="ffn",
    )(*_operands([x2d, mod3, gpre, wup, wdown, gpost]))


def _rope_tables(n_tokens, dim, reps):
    n_rows = n_tokens // GRID_W
    row = jnp.repeat(jnp.arange(n_rows), GRID_W).astype(F32)
    col = jnp.tile(jnp.arange(GRID_W), n_rows).astype(F32)
    nfreq = dim // 4
    inv = ROPE_BASE ** (-jnp.arange(nfreq, dtype=F32) / nfreq)
    ang_r = row[:, None] * inv
    ang_c = col[:, None] * inv
    ang = jnp.concatenate([ang_r, ang_r, ang_c, ang_c], axis=-1)
    cos, sin = jnp.cos(ang), jnp.sin(ang)
    even = ((jnp.arange(dim) // nfreq) % 2 == 0)[None, :]
    sin_next = jnp.where(even, -sin, 0.0)
    sin_prev = jnp.where(even, 0.0, sin)
    return tuple(jnp.tile(a, (1, reps)) for a in (cos, sin_next, sin_prev))


def _rope_tables_mla_wide(n_tokens):
    cos, sin_next, sin_prev = _rope_tables(n_tokens, B_ROPE, 1)
    tail = MLA_HEAD_LANES - B_NOPE - B_ROPE
    wide = lambda a, fill: jnp.tile(jnp.pad(a, ((0, 0), (B_NOPE, tail)), constant_values=fill), (1, B_HEADS))
    return wide(cos, 1.0), wide(sin_next, 0.0), wide(sin_prev, 0.0)


def _split_cols(w):
    out, acc = [], 0
    for s in IN_SIZES:
        out.append(w[..., acc:acc + s])
        acc += s
    return out


def _pack_w_in(w):
    (a_q, a_k, a_v, b_cq, b_ckv, b_kr, c_in, d_qkv, d_z, d_beta, d_alpha, gate) = _split_cols(w)
    z = lambda n: jnp.zeros(w.shape[:-1] + (n,), w.dtype)
    k0, k1 = a_k[..., :A_HD], a_k[..., A_HD:]
    v0, v1 = a_v[..., :A_HD], a_v[..., A_HD:]
    w1 = jnp.concatenate([a_q, k0, k0, k1, k1, v0, v0, v1, v1, b_cq, z(256 - B_Q_RANK), b_ckv,
                          b_kr, b_kr, b_kr, b_kr, c_in, d_qkv, d_z, d_beta, d_alpha, z(112)], axis=-1)
    return w1.astype(BF16), gate.astype(BF16)


def _expand_kv_heads(a):
    return jnp.repeat(a, A_HEADS // A_KV_HEADS, axis=-2).reshape(a.shape[:-2] + (A_HEADS * A_HD,))


def _stacked_weights(g_pre1, g_post1, g_pre2, g_post2, w_in, a_sink, b_g_cq, b_g_ckv, b_w_uq, b_w_ukv,
                     c_w_pool, c_scale, d_conv, d_a_log, d_dt_bias, d_g_norm, w_br, w_o, w_up, w_down):
    depth = w_in.shape[0]
    w1, wg = _pack_w_in(w_in)
    wuq = b_w_uq.reshape(depth, B_Q_RANK, B_HEADS, B_NOPE + B_ROPE)
    wuq = jnp.pad(wuq, ((0, 0), (0, 256 - B_Q_RANK), (0, 0), (0, MLA_HEAD_LANES - B_NOPE - B_ROPE)))
    wuq = wuq.reshape(depth, 256, MLA_WIDE).astype(BF16)
    wukv = b_w_ukv.reshape(depth, B_KV_RANK, B_HEADS, B_NOPE + B_VD)
    wk = jnp.pad(wukv[..., :B_NOPE], ((0, 0), (0, 0), (0, 0), (0, MLA_HEAD_LANES - B_NOPE)))
    wukv = jnp.concatenate([wk.reshape(depth, B_KV_RANK, MLA_WIDE),
                            wukv[..., B_NOPE:].reshape(depth, B_KV_RANK, B_HEADS * B_VD)], -1).astype(BF16)
    same = jnp.eye(C_GROUPS, dtype=F32)[None, :, None, :, None]
    wpool = (c_w_pool[:, :, :, None, :] * same).reshape(depth, C_GROUPS * C_GW, C_GROUPS * C_GW).astype(BF16)
    gdn =jnp.stack([d_a_log.reshape(depth, 2 * D_HEADS), d_dt_bias.reshape(depth, 2 * D_HEADS)], axis=1)
    prow = jnp.pad(gdn, ((0, 0), (0, 0), (8, LANES - 16)))
    pcol = jnp.pad(gdn.transpose(0, 2, 1), ((0, 0), (8, 0), (0, 0)))
    row = lambda a: a[:, None, :]
    return dict(
        w1=w1, wg=wg, wuq=wuq, wukv=wukv,
        gpre1=row(g_pre1), gpost1=row(g_post1), gpre2=row(g_pre2), gpost2=row(g_post2),
        gcq=row(jnp.pad(b_g_cq, ((0, 0), (0, 256 - B_Q_RANK)))), gckv=row(b_g_ckv),
        sink=a_sink, wpool=wpool, cscale=row(c_scale),
        convw=jnp.pad(d_conv, ((0, 0), (0, SUBLANES - D_CONV), (0, 0))),
        prow=prow, pcol=pcol, gnorm=row(jnp.tile(d_g_norm, (1, D_HEADS))),
        wbr=w_br.astype(BF16), wo=w_o.astype(BF16), wup=w_up.astype(BF16), wdown=w_down.astype(BF16))


def _trunk_layer(x2d, mod3, lw, n_seq, t, ctx):
    latent = ctx is not None
    n = x2d.shape[0]
    tm = min(512, t) if latent else 256
    tm_dense = min(512, n)
    if latent:
        tables = _rope_tables(t, A_HD, A_HEADS) + _rope_tables_mla_wide(t)
        proj, ckvn, qw, kwr, vb, aqr, akr, qwr = _inproj_call(
            x2d, mod3, lw["gpre1"], lw["w1"], lw["gcq"], lw["wuq"], lw["gckv"], lw["wukv"], tables, t, tm_dense)
        kctx, vctx, ctx_ckv, ckr_wide, s0f, s0b = ctx
        out_a = _win_attn_call(lw["sink"], aqr, akr, proj, kctx, vctx, n_seq, t)
        out_b = _mla_lat_call(qw, qwr, kwr, vb, ctx_ckv, ckr_wide, lw["wukv"], n_seq, t, min(MLA_TQ, t))
        s0 = (s0f, s0b)
    else:
        proj, ckvn, qw, kw, vb = _inproj_call(
            x2d, mod3, lw["gpre1"], lw["w1"], lw["gcq"], lw["wuq"], lw["gckv"], lw["wukv"], None, t, tm_dense)
        out_a, out_b = _attn_ctx_call(lw["sink"], proj, qw, kw, vb, n_seq, t)
        s0 = None
    out_c, qkvn = _local_call(proj, lw["convw"], lw["wpool"], lw["cscale"], t, tm)
    o_f, o_b, s_fin = _gdn_call(qkvn, proj, lw["prow"], lw["pcol"], s0, n_seq, t)
    x1 = _merge_call(x2d, mod3, lw["gpre1"], out_a, out_b, out_c, o_f, o_b, proj, lw["gnorm"],
                     lw["wg"], lw["wbr"], lw["wo"], lw["gpost1"], t, tm_dense)
    x2 = _ffn_call(x1, mod3, lw["gpre2"], lw["wup"], lw["wdown"], lw["gpost2"], t, tm_dense)
    new_ctx = None
    if not latent:
        p3 = proj.reshape(n_seq, t, P1)
        ka = jnp.stack([p3[:, :, C_AK:C_AK + A_HD], p3[:, :, C_AK + 2 * A_HD:C_AK + 3 * A_HD]], axis=2)
        va = jnp.stack([p3[:, :, C_AV:C_AV + A_HD], p3[:, :, C_AV + 2 * A_HD:C_AV + 3 * A_HD]], axis=2)
        new_ctx = (ka, va, ckvn.reshape(n_seq, t, B_KV_RANK), p3[:, :, C_KR:C_KR + B_ROPE],
                   s_fin[:, 0], s_fin[:, 1])
    return x2, new_ctx


def kernel(x_prompt, x_sample, cache_a_k, cache_a_v, cache_b_ckv, cache_b_krope, state_d_fwd, state_d_bwd, c, c_ctx, w_mod, b_mod, g_pre1, g_post1, g_pre2, g_post2, w_in, a_sink, b_g_cq, b_g_ckv, b_w_uq, b_w_ukv, c_w_pool, c_scale, d_conv, d_a_log, d_dt_bias, d_g_norm, w_br, w_o, w_up, w_down):
    depth = w_mod.shape[0]
    bp, tp, d = x_prompt.shape
    bs, ts, _ = x_sample.shape
    assert bs + 1 <= SUBLANES
    cond8 = jnp.zeros((SUBLANES, d), F32).at[0].set(c_ctx).at[1:1 + bs].set(c)
    mod_all = _mod_call(cond8, w_mod, b_mod).reshape(depth, SUBLANES, 1, 6 * d)
    yp = x_prompt.reshape(bp * tp, d)
    ys = x_sample.reshape(bs * ts, d)
    new = [[] for _ in range(6)]
    kctx_all, vctx_all = _expand_kv_heads(cache_a_k), _expand_kv_heads(cache_a_v)
    ckr_wide_all = jnp.tile(jnp.pad(cache_b_krope, ((0, 0), (0, 0), (0, 0),
                                                    (B_NOPE, MLA_HEAD_LANES - B_NOPE - B_ROPE))), (1, 1, 1, B_HEADS))
    stacked = _stacked_weights(g_pre1, g_post1, g_pre2, g_post2, w_in, a_sink, b_g_cq, b_g_ckv, b_w_uq, b_w_ukv,
                               c_w_pool, c_scale, d_conv, d_a_log, d_dt_bias, d_g_norm, w_br, w_o, w_up, w_down)
    for l in range(depth):
        lw = {k: _Layer(v, l) for k, v in stacked.items()}
        mod_ctx = _Mod(mod_all, l, 0, 1)
        mod_lat = _Mod(mod_all, l, 1, bs)
        yp, nctx = _trunk_layer(yp, mod_ctx, lw, bp, tp, None)
        for acc, val in zip(new, nctx):
            acc.append(val)
        ctx = tuple(_Layer(a, l) for a in (kctx_all, vctx_all, cache_b_ckv, ckr_wide_all, state_d_fwd, state_d_bwd))
        ys, _ = _trunk_layer(ys, mod_lat, lw, bs, ts, ctx)
    return (yp.reshape(bp, tp, d), ys.reshape(bs, ts, d)) + tuple(jnp.stack(v, axis=1) for v in new)
```

```python
import functools
import math
from typing import NamedTuple

import jax
import jax.numpy as jnp
from jax import lax
from jax.experimental import pallas as pl
from jax.experimental.pallas import tpu as pltpu

F32 = jnp.float32
BF16 = jnp.bfloat16

D_MODEL = 1024
GRID_W = 64
ROPE_BASE = 10000.0
EPS = 1e-6
NEG_INF = -1e30
Q_BLOCK = 128
N_BRANCH = 4
BRANCH_W = 256
A_HEADS = 4
A_KV_HEADS = 2
A_HD = 64
A_WINDOW = 128
A_SCALE = A_HD ** -0.5
B_HEADS = 4
B_Q_RANK = 192
B_KV_RANK = 128
B_NOPE = 64
B_ROPE = 32
B_VD = 64
B_SCALE = (B_NOPE + B_ROPE) ** -0.5
C_GROUPS = 4
C_GW = 64
C_WINDOWS = (2, 4, 8, 16)
D_HEADS = 4
D_DK = 64
D_DV = 64
D_CONV = 5
D_CHUNK = 64
FF_RAW = -(-8 * D_MODEL // 3)
D_FF = -(-FF_RAW // 256) * 256
IN_SIZES = (A_HEADS * A_HD, A_KV_HEADS * A_HD, A_KV_HEADS * A_HD, B_Q_RANK, B_KV_RANK, B_ROPE,
            C_GROUPS * C_GW, D_HEADS * (2 * D_DK + D_DV), D_HEADS * D_DV, 2 * D_HEADS, 2 * D_HEADS,
            N_BRANCH * D_MODEL)

LANES = 128
SUBLANES = 8
VMEM_LIMIT_BYTES = 56 * 1024 * 1024
HALO = SUBLANES
GDN_SEQS_PER_STEP = 2
GDN_CHUNKS_PER_STEP = 4
MLA_HEAD_LANES = LANES
MLA_WIDE = B_HEADS * MLA_HEAD_LANES
LOG2E = math.log2(math.e)
MLA_TQ = 512
MLA_ROW_GROUP = 32
MLA_COL_BLOCK = 256
WIN_ATTN_TQ = 512

C_AQ = 0
C_AK = 256
C_AV = 512
C_CQ = 768
C_CKV = 1024
C_KR = 1152
C_CIN = 1280
C_QKV = 1536
C_Z = 2304
C_AB = 2560
P1 = 2688


def _cparams(sem):
    return pltpu.CompilerParams(dimension_semantics=sem, vmem_limit_bytes=VMEM_LIMIT_BYTES)


class _Layer(NamedTuple):
    arr: jax.Array
    l: int


def _resident(p):
    shape = p.arr.shape[1:]
    return pl.BlockSpec((None,) + shape, lambda *_: (p.l,) + (0,) * len(shape), pipeline_mode=pl.Buffered(1))


class _Mod(NamedTuple):
    arr: jax.Array
    l: int
    first: int
    count: int


def _mod_block(m, tiles_per_seq):
    width = m.arr.shape[-1]
    if m.count == 1:
        return pl.BlockSpec((None, 1, 1, width), lambda i: (m.l, m.first, 0, 0))
    return pl.BlockSpec((None, 1, 1, width), lambda i: (m.l, m.first + i // tiles_per_seq, 0, 0))


def _cache_block(p):
    return pl.BlockSpec((1, None) + p.arr.shape[2:], lambda b, i: (b, p.l, 0, 0))


def _operands(args):
    return [a.arr if isinstance(a, (_Layer, _Mod)) else a for a in args]


def _dot(a, b):
    return jnp.dot(a, b, preferred_element_type=F32)


def _dot_nt(a, b):
    return lax.dot_general(a, b, (((1,), (1,)), ((), ())), preferred_element_type=F32)


def _split3(x):
    hi = x.astype(BF16)
    r1 = x - hi.astype(F32)
    mid = r1.astype(BF16)
    lo = (r1 - mid.astype(F32)).astype(BF16)
    return hi, mid, lo


def _dot_sel_right(x, sel):
    hi, mid, lo = _split3(x)
    return _dot(hi, sel) + _dot(mid, sel) + _dot(lo, sel)


def _silu(x):
    return x * jax.nn.sigmoid(x)


def _softplus(x):
    return jnp.maximum(x, 0.0) + jnp.log1p(jnp.exp(-jnp.abs(x)))


def _head_mask(width, n_lanes, h):
    lane = lax.broadcasted_iota(jnp.int32, (1, n_lanes), 1)
    return (lane // width) == h


def _group_ones(n, width):
    r = lax.broadcasted_iota(jnp.int32, (n, n), 0) // width
    c = lax.broadcasted_iota(jnp.int32, (n, n), 1) // width
    return jnp.where(r == c, 1.0, 0.0).astype(BF16)


def _mod_kernel(c_ref, w_ref, b_ref, o_ref):
    s = _silu(c_ref[...])
    o_ref[0] = _dot(s.astype(BF16), w_ref[0].astype(BF16)) + b_ref[0]


def _mod_call(cond8, w_mod, b_mod):
    depth, d, n = w_mod.shape
    tn = 1536
    return pl.pallas_call(
        _mod_kernel,
        grid=(depth, n // tn),
        in_specs=[pl.BlockSpec((SUBLANES, d), lambda l, j: (0, 0)),
                  pl.BlockSpec((1, d, tn), lambda l, j: (l, 0, j)),
                  pl.BlockSpec((1, 1, tn), lambda l, j: (l, 0, j))],
        out_specs=pl.BlockSpec((1, SUBLANES, tn), lambda l, j: (l, 0, j)),
        out_shape=jax.ShapeDtypeStruct((depth, SUBLANES, n), F32),
        compiler_params=_cparams(("parallel", "parallel")),
        name="mod",
    )(cond8, w_mod, b_mod.reshape(depth, 1, n))


def _rmsnorm_rows(x, g, n):
    ms = jnp.sum(x * x, axis=-1, keepdims=True) * (1.0 / n)
    return x * lax.rsqrt(ms + EPS) * g


def _rope(x, cos, sin_next, sin_prev, quarter):
    n = x.shape[-1]
    return x * cos + pltpu.roll(x, n - quarter, 1) * sin_next + pltpu.roll(x, quarter, 1) * sin_prev


def _mla_wide_kr(kr_rep):
    lane = lax.broadcasted_iota(jnp.int32, (1, MLA_HEAD_LANES), 1)
    keep = (lane >= B_NOPE) & (lane < B_NOPE + B_ROPE)
    return jnp.concatenate([jnp.where(keep, kr_rep, 0.0)] * B_HEADS, axis=1)


def _inproj_kernel(latent, *refs):
    if latent:
        (x_ref, mod_ref, gpre_ref, w1_ref, gcq_ref, wuq_ref, gckv_ref, wukv_ref,
         ca_ref, sna_ref, spa_ref, cb_ref, snb_ref, spb_ref,
         proj_ref, ckv_ref, qw_ref, kw_ref, vb_ref, aqr_ref, akr_ref, qwr_ref) = refs
    else:
        (x_ref, mod_ref, gpre_ref, w1_ref, gcq_ref, wuq_ref, gckv_ref, wukv_ref,
         proj_ref, ckv_ref, qw_ref, kw_ref, vb_ref) = refs
    d = D_MODEL
    x = x_ref[...]
    mod = mod_ref[0]
    sh1 = mod[:, 0:d]
    sc1 = mod[:, d:2 * d]
    h = _rmsnorm_rows(x, gpre_ref[...], d) * (1.0 + sc1) + sh1
    proj = _dot(h.astype(BF16), w1_ref[...])
    proj_ref[...] = proj
    cqn = _rmsnorm_rows(proj[:, C_CQ:C_CQ + 256], gcq_ref[...], B_Q_RANK)
    qw = _dot(cqn.astype(BF16), wuq_ref[...])
    qw_ref[...] = qw.astype(BF16)
    ckvn = _rmsnorm_rows(proj[:, C_CKV:C_CKV + B_KV_RANK], gckv_ref[...], B_KV_RANK)
    ckv_ref[...] = ckvn
    knv = _dot(ckvn.astype(BF16), wukv_ref[...])
    vb_ref[...] = knv[:, MLA_WIDE:].astype(BF16)
    kr_wide = _mla_wide_kr(proj[:, C_KR:C_KR + 128])
    if latent:
        ca, sna, spa = (jnp.concatenate([r[...]] * (256 // LANES), axis=1) for r in (ca_ref, sna_ref, spa_ref))
        cb, snb, spb = (jnp.concatenate([r[...]] * B_HEADS, axis=1) for r in (cb_ref, snb_ref, spb_ref))
        aqr_ref[...] = _rope(proj[:, C_AQ:C_AQ + 256], ca, sna, spa, A_HD // 4).astype(BF16)
        akr_ref[...] = _rope(proj[:, C_AK:C_AK + 256], ca, sna, spa, A_HD // 4).astype(BF16)
        qwr_ref[...] = _rope(qw, cb, snb, spb, B_ROPE // 4).astype(BF16)
        kr_wide = _rope(kr_wide, cb, snb, spb, B_ROPE // 4)
    kw_ref[...] = (knv[:, :MLA_WIDE] + kr_wide).astype(BF16)


def _inproj_call(x2d, mod3, gpre, w1, gcq, wuq, gckv, wukv, tables, seq_len, tm):
    n, d = x2d.shape
    latent = tables is not None
    tiles_per_seq = seq_len // tm if latent else 1
    row = lambda w, dt=F32: (pl.BlockSpec((tm, w), lambda i: (i, 0)), jax.ShapeDtypeStruct((n, w), dt))
    params = [gpre, w1, gcq, wuq, gckv, wukv]
    in_specs = [pl.BlockSpec((tm, d), lambda i: (i, 0)),
                _mod_block(mod3, seq_len // tm)] + [_resident(p) for p in params]
    args = [x2d, mod3] + params
    outs = [row(P1), row(B_KV_RANK), row(MLA_WIDE, BF16), row(MLA_WIDE, BF16), row(256, BF16)]
    if latent:
        for t in tables:
            in_specs.append(pl.BlockSpec((tm, t.shape[1]), lambda i: (i % tiles_per_seq, 0)))
            args.append(t)
        outs += [row(256, BF16), row(256, BF16), row(MLA_WIDE, BF16)]
    return pl.pallas_call(
        functools.partial(_inproj_kernel, latent),
        grid=(n // tm,),
        in_specs=in_specs,
        out_specs=[o[0] for o in outs],
        out_shape=[o[1] for o in outs],
        compiler_params=_cparams(("parallel",)),
        name="inproj_lat" if latent else "inproj_ctx",
    )(*_operands(args))


def _softmax_pv(scores, values, scale, sink, hm):
    c1 = scale * LOG2E
    m = jnp.max(scores[0], axis=-1, keepdims=True)
    for s in scores[1:]:
        m = jnp.maximum(m, jnp.max(s, axis=-1, keepdims=True))
    m2 = m * c1
    if sink is not None:
        m2 = jnp.maximum(m2, sink * LOG2E)
    den = None
    pv = None
    for s, v in zip(scores, values):
        p = jnp.exp2(s * c1 - m2)
        ps = jnp.sum(p, axis=-1, keepdims=True)
        den = ps if den is None else den + ps
        t = _dot(p.astype(BF16), v)
        pv = t if pv is None else pv + t
    if sink is not None:
        den = den + jnp.exp2(sink * LOG2E - m2)
    return jnp.where(hm, pv / den, 0.0)


def _attn_ctx_kernel(layer, sink_ref, q_ref, k_ref, v_ref, qw_ref, kw_ref, vb_ref, oa_ref, ob_ref):
    q = q_ref[...]
    k = k_ref[...].astype(BF16)
    v = v_ref[...].astype(BF16)
    acc = jnp.zeros(q.shape, F32)
    for h in range(A_HEADS):
        hm = _head_mask(A_HD, 256, h)
        s = _dot_nt(jnp.where(hm, q, 0.0).astype(BF16), k)
        acc = acc + _softmax_pv([s], [v], A_SCALE, sink_ref[layer, h], hm)
    oa_ref[...] = acc
    vb = vb_ref[...]
    acc = jnp.zeros((q.shape[0], B_HEADS * B_VD), F32)
    for h in range(B_HEADS):
        hs = slice(h * MLA_HEAD_LANES, (h + 1) * MLA_HEAD_LANES)
        s = _dot_nt(qw_ref[:, hs], kw_ref[:, hs])
        acc = acc + _softmax_pv([s], [vb], B_SCALE, None, _head_mask(B_VD, 256, h))
    ob_ref[...] = acc


def _attn_ctx_call(sink, proj, qw, kw, vb, n_seq, t):
    n = proj.shape[0]
    col = lambda w, j: pl.BlockSpec((t, w), lambda b: (b, j))
    out = (pl.BlockSpec((t, 256), lambda b: (b, 0)), jax.ShapeDtypeStruct((n, 256), F32))
    return pl.pallas_call(
        functools.partial(_attn_ctx_kernel, sink.l),
        grid=(n_seq,),
        in_specs=[pl.BlockSpec(memory_space=pltpu.SMEM),
                  col(256, C_AQ // 256), col(256, C_AK // 256), col(256, C_AV // 256),
                  col(MLA_WIDE, 0), col(MLA_WIDE, 0), col(256, 0)],
        out_specs=[out[0], out[0]],
        out_shape=[out[1], out[1]],
        compiler_params=_cparams(("parallel",)),
        name="attn_ctx",
    )(sink.arr, proj, proj, proj, qw, kw, vb)


def _win_attn_kernel(t, layer, sink_ref, qr_ref, q_ref, kp_ref, kc_ref, kn_ref, vp_ref, vc_ref, vn_ref,
                     kctx_ref, vctx_ref, o_ref):
    i = pl.program_id(1)
    qr = qr_ref[...]
    q = q_ref[...]
    tq = q.shape[0]
    kl = jnp.concatenate([kp_ref[...], kc_ref[...], kn_ref[...]], axis=0)
    vl = jnp.concatenate([vp_ref[...], vc_ref[...], vn_ref[...]], axis=0).astype(BF16)
    kctx = kctx_ref[0].astype(BF16)
    vctx = vctx_ref[0].astype(BF16)
    q_pos = i * tq + lax.broadcasted_iota(jnp.int32, (tq, tq + 2 * A_WINDOW), 0)
    k_pos = i * tq - A_WINDOW + lax.broadcasted_iota(jnp.int32, (tq, tq + 2 * A_WINDOW), 1)
    valid = (jnp.abs(q_pos - k_pos) <= A_WINDOW) & (k_pos >= 0) & (k_pos < t)
    acc = jnp.zeros(q.shape, F32)
    for h in range(A_HEADS):
        hm = _head_mask(A_HD, 256, h)
        s_loc = jnp.where(valid, _dot_nt(jnp.where(hm, qr, 0.0).astype(BF16), kl), NEG_INF)
        s_ctx = _dot_nt(jnp.where(hm, q, 0.0).astype(BF16), kctx)
        acc = acc + _softmax_pv([s_loc, s_ctx], [vl, vctx], A_SCALE, sink_ref[layer, h], hm)
    o_ref[...] = acc


def _win_attn_call(sink, aqr, akr, proj, kctx, vctx, n_seq, t):
    n = proj.shape[0]
    tq = min(WIN_ATTN_TQ, t)
    nq = t // tq
    r = tq // A_WINDOW
    nh = t // A_WINDOW
    cur = lambda j: pl.BlockSpec((tq, 256), lambda b, i: (b * nq + i, j))
    prev = lambda j: pl.BlockSpec((A_WINDOW, 256), lambda b, i: (b * nh + jnp.maximum(i * r - 1, 0), j))
    nxt = lambda j: pl.BlockSpec((A_WINDOW, 256), lambda b, i: (b * nh + jnp.minimum((i + 1) * r, nh - 1), j))
    ctx = _cache_block(kctx)
    return pl.pallas_call(
        functools.partial(_win_attn_kernel, t, sink.l),
        grid=(n_seq, nq),
        in_specs=[pl.BlockSpec(memory_space=pltpu.SMEM),
                  cur(0), cur(C_AQ // 256),
                  prev(0), cur(0), nxt(0),
                  prev(C_AV // 256), cur(C_AV // 256), nxt(C_AV // 256),
                  ctx, ctx],
        out_specs=pl.BlockSpec((tq, 256), lambda b, i: (b * nq + i, 0)),
        out_shape=jax.ShapeDtypeStruct((n, 256), F32),
        compiler_params=_cparams(("parallel", "parallel")),
        name="win_attn",
    )(sink.arr, aqr, proj, akr, akr, akr, proj, proj, proj, kctx.arr, vctx.arr)


def _mla_lat_kernel(qw_ref, qwr_ref, kwr_ref, vb_ref, cckv_ref, ckr_ref, wukv_ref, o_ref, s_ref, p_ref, den_ref):
    kvc = _dot(cckv_ref[0].astype(BF16), wukv_ref[...])
    kwc = (kvc[:, :MLA_WIDE] + ckr_ref[0]).astype(BF16)
    vc = kvc[:, MLA_WIDE:].astype(BF16)
    tq = qw_ref.shape[0]
    t = kwr_ref.shape[0]
    n_keys = s_ref.shape[2]
    c1 = B_SCALE * LOG2E
    cb = MLA_COL_BLOCK
    rg = MLA_ROW_GROUP

    def scores_into(slot, h):
        hs = slice(h * MLA_HEAD_LANES, (h + 1) * MLA_HEAD_LANES)
        s_ref[slot, :, 0:t] = _dot_nt(qwr_ref[:, hs], kwr_ref[:, hs])
        s_ref[slot, :, t:n_keys] = _dot_nt(qw_ref[:, hs], kwc[:, hs])

    scores_into(0, 0)
    acc = jnp.zeros((tq, B_HEADS * B_VD), F32)
    for h in range(B_HEADS):
        slot = h % 2
        if h + 1 < B_HEADS:
            scores_into(1 - slot, h + 1)
        for r0 in range(0, tq, rg):
            rows = slice(r0, r0 + rg)
            m = s_ref[slot, rows, 0:cb]
            for c0 in range(cb, n_keys, cb):
                m = jnp.maximum(m, s_ref[slot, rows, c0:c0 + cb])
            m2 = jnp.max(m, axis=-1, keepdims=True) * c1
            part = None
            for c0 in range(0, n_keys, cb):
                p = jnp.exp2(s_ref[slot, rows, c0:c0 + cb] * c1 - m2)
                part = p if part is None else part + p
                p_ref[rows, c0:c0 + cb] = p.astype(BF16)
            den_ref[rows, :] = jnp.broadcast_to(jnp.sum(part, axis=-1, keepdims=True), (rg, LANES))
        pv = _dot(p_ref[:, 0:t], vb_ref[...]) + _dot(p_ref[:, t:n_keys], vc)
        acc = acc + jnp.where(_head_mask(B_VD, 256, h), pv / den_ref[:, 0:1], 0.0)
    o_ref[...] = acc


def _mla_lat_call(qw, qwr, kwr, vb, cckv, ckr_wide, wukv, n_seq, t, tq):
    n = qw.shape[0]
    nq = t // tq
    qblk = lambda w: pl.BlockSpec((tq, w), lambda b, i: (b * nq + i, 0))
    seq = lambda w: pl.BlockSpec((t, w), lambda b, i: (b, 0))
    ctx = _cache_block
    return pl.pallas_call(
        _mla_lat_kernel,
        grid=(n_seq, nq),
        in_specs=[qblk(MLA_WIDE), qblk(MLA_WIDE), seq(MLA_WIDE), seq(256), ctx(cckv), ctx(ckr_wide),
                  _resident(wukv)],
        out_specs=pl.BlockSpec((tq, 256), lambda b, i: (b * nq + i, 0)),
        out_shape=jax.ShapeDtypeStruct((n, 256), F32),
        scratch_shapes=[pltpu.VMEM((2, tq, t + cckv.arr.shape[2]), F32), pltpu.VMEM((tq, t + cckv.arr.shape[2]), BF16),
                        pltpu.VMEM((tq, LANES), F32)],
        compiler_params=_cparams(("parallel", "parallel")),
        name="mla_lat",
    )(qw, qwr, kwr, vb, cckv.arr, ckr_wide.arr, wukv.arr)


def _with_halo(prev_ref, cur_ref, next_ref, has_prev, has_next):
    prev = jnp.where(has_prev, prev_ref[...], 0.0)
    nxt = jnp.where(has_next, next_ref[...], 0.0)
    return jnp.concatenate([prev, cur_ref[...], nxt], axis=0)


def _shift_rows(x, k):
    n = x.shape[0]
    return pltpu.roll(x, (-k) % n, 0)


def _local_kernel(t, tm, cp_ref, cc_ref, cn_ref, qp_ref, qc_ref, qn_ref, convw_ref, wpool_ref, cscale_ref,
                  oc_ref, qkv_ref):
    i = pl.program_id(0)
    tiles_per_seq = t // tm
    j = i % tiles_per_seq
    has_prev = j > 0
    has_next = j < tiles_per_seq - 1
    x = _with_halo(cp_ref, cc_ref, cn_ref, has_prev, has_next)
    p2 = x + _shift_rows(x, -1)
    p4 = _shift_rows(p2, 1) + _shift_rows(p2, -1)
    p8 = _shift_rows(p4, 2) + _shift_rows(p4, -2)
    p16 = _shift_rows(p8, 4) + _shift_rows(p8, -4)
    grp = lax.broadcasted_iota(jnp.int32, (1, 256), 1) // C_GW
    win = jnp.where(grp == 0, p2, jnp.where(grp == 1, p4, jnp.where(grp == 2, p8, p16)))[HALO:HALO + tm]
    pos = j * tm + lax.broadcasted_iota(jnp.int32, (tm, 256), 0)
    half = jnp.where(grp == 0, 1, jnp.where(grp == 1, 2, jnp.where(grp == 2, 4, 8)))
    cnt = (jnp.minimum(pos + half, t) - jnp.maximum(pos - half, 0)).astype(F32)
    y = win / cnt - cc_ref[...]
    oc_ref[...] = _dot(y.astype(BF16), wpool_ref[...]) * cscale_ref[...]
    xq = _with_halo(qp_ref, qc_ref, qn_ref, has_prev, has_next)
    w = convw_ref[...]
    pad = D_CONV // 2
    acc = None
    for tap in range(D_CONV):
        term = _shift_rows(xq, tap - pad) * w[tap:tap + 1, :]
        acc = term if acc is None else acc + term
    u = _silu(acc[HALO:HALO + tm])
    ones = _group_ones(256, D_DK)
    q = u[:, 0:256]
    k = u[:, 256:512]
    q = q * lax.rsqrt(_dot_sel_right(q * q, ones) + EPS) * (D_DK ** -0.5)
    k = k * lax.rsqrt(_dot_sel_right(k * k, ones) + EPS)
    qkv_ref[:, 0:256] = q
    qkv_ref[:, 256:512] = k
    qkv_ref[:, 512:768] = u[:, 512:768]


def _local_call(proj, convw8, wpool_bd, cscale, t, tm):
    n = proj.shape[0]
    hb = tm // HALO
    nblk = n // HALO
    cur = lambda w, c: pl.BlockSpec((tm, w), lambda i: (i, c))
    prev = lambda w, c: pl.BlockSpec((HALO, w), lambda i: (jnp.maximum(i * hb - 1, 0), c))
    nxt = lambda w, c: pl.BlockSpec((HALO, w), lambda i: (jnp.minimum((i + 1) * hb, nblk - 1), c))
    full = _resident
    cc, qc = C_CIN // 256, C_QKV // 768
    return pl.pallas_call(
        functools.partial(_local_kernel, t, tm),
        grid=(n // tm,),
        in_specs=[prev(256, cc), cur(256, cc), nxt(256, cc), prev(768, qc), cur(768, qc), nxt(768, qc),
                  full(convw8), full(wpool_bd), full(cscale)],
        out_specs=[pl.BlockSpec((tm, 256), lambda i: (i, 0)), pl.BlockSpec((tm, 768), lambda i: (i, 0))],
        out_shape=[jax.ShapeDtypeStruct((n, 256), F32), jax.ShapeDtypeStruct((n, 768), F32)],
        compiler_params=_cparams(("parallel",)),
        name="local",
    )(proj, proj, proj, proj, proj, proj, convw8.arr, wpool_bd.arr, cscale.arr)


def _gdn_kernel(n_steps, g_chunks, n_sq, has_s0, xf_ref, xb_ref, abf_ref, abb_ref, prow_ref, pcol_ref, *refs):
    if has_s0:
        s0f_ref, s0b_ref, of_ref, ob_ref, sfin_ref, s_ref = refs
    else:
        of_ref, ob_ref, sfin_ref, s_ref = refs
    i = pl.program_id(1)
    c = D_CHUNK
    nh = D_HEADS

    @pl.when(i == 0)
    def _():
        if has_s0:
            for sq in range(n_sq):
                s_ref[sq * nh:(sq + 1) * nh] = s0f_ref[sq]
                s_ref[(n_sq + sq) * nh:(n_sq + sq + 1) * nh] = s0b_ref[sq]
        else:
            s_ref[...] = jnp.zeros(s_ref.shape, F32)

    r = lax.broadcasted_iota(jnp.int32, (c, c), 0)
    cidx = lax.broadcasted_iota(jnp.int32, (c, c), 1)
    lower = jnp.where(cidx <= r, 1.0, 0.0).astype(BF16)
    upper = jnp.where(cidx >= r, 1.0, 0.0).astype(BF16)
    prow = prow_ref[...]
    pcol = pcol_ref[...]

    eye = jnp.where(cidx == r, 1.0, 0.0).astype(BF16)
    sel_lower = jnp.concatenate([lower] * 3, axis=1)
    sel_upper = jnp.concatenate([upper] * 3, axis=1)
    pick = jnp.where(lax.broadcasted_iota(jnp.int32, (16, LANES), 0) == lax.broadcasted_iota(jnp.int32, (16, LANES), 1),
                     1.0, 0.0).astype(BF16)
    sel_rows = jnp.concatenate([pick] * 3, axis=1)
    sel_lower_t = jnp.concatenate([lower] * 3, axis=0)
    sel_upper_t = jnp.concatenate([upper] * 3, axis=0)

    bdot = lambda a, b: jnp.einsum('bik,bkj->bij', a, b, preferred_element_type=F32)
    bdot_nt = lambda a, b: jnp.einsum('bik,bjk->bij', a, b, preferred_element_type=F32)

    def prepare(steps):
        nb = 2 * n_sq * len(steps) * nh
        is_fwd = lax.broadcasted_iota(jnp.int32, (nb, c, c), 0) < nb // 2
        r3 = lax.broadcasted_iota(jnp.int32, (nb, c, c), 1)
        c3 = lax.broadcasted_iota(jnp.int32, (nb, c, c), 2)
        ahead = jnp.where(is_fwd, r3 - c3, c3 - r3)
        incl = ahead >= 0
        strict = ahead > 0
        q_l, k_l, v_l, b_l, gcol_l, grow_l = [], [], [], [], [], []
        for d, sq, step in [(d, sq, step) for d in range(2) for sq in range(n_sq) for step in steps]:
            x_ref, ab_ref = ((xf_ref, abf_ref), (xb_ref, abb_ref))[d]
            gi = step if d == 0 else g_chunks - 1 - step
            rows = slice(gi * c, (gi + 1) * c)
            x = x_ref[sq, rows, :]
            ab = ab_ref[sq, rows, :]
            abr = _dot_nt(sel_rows, jnp.concatenate(_split3(ab), axis=1))
            beta_c = jax.nn.sigmoid(ab)
            g_c = -jnp.exp(prow[0:1, :]) * _softplus(ab + prow[1:2, :])
            g_r = -jnp.exp(pcol[:, 0:1]) * _softplus(abr + pcol[:, 1:2])
            if d == 0:
                gc_c = _dot(sel_lower, jnp.concatenate(_split3(g_c), axis=0))
                gc_r = _dot(jnp.concatenate(_split3(g_r), axis=1), sel_upper_t)
            else:
                gc_c = _dot(sel_upper, jnp.concatenate(_split3(g_c), axis=0))
                gc_r = _dot(jnp.concatenate(_split3(g_r), axis=1), sel_lower_t)
            for h in range(nh):
                hs = slice(h * D_DK, (h + 1) * D_DK)
                lane = nh * d + h
                q_l.append(x[:, 0:256][:, hs])
                k_l.append(x[:, 256:512][:, hs])
                v_l.append(x[:, 512:768][:, hs])
                b_l.append(beta_c[:, lane:lane + 1])
                gcol_l.append(gc_c[:, 8 + lane:9 + lane])
                grow_l.append(gc_r[8 + lane:9 + lane, :])
        qh, kh, vh = jnp.stack(q_l), jnp.stack(k_l), jnp.stack(v_l)
        bcol, gcol, grow = jnp.stack(b_l), jnp.stack(gcol_l), jnp.stack(grow_l)
        glast = jnp.concatenate([gcol[:nb // 2, c - 1:c, :], gcol[nb // 2:, 0:1, :]], axis=0)
        decay = jnp.exp(jnp.where(incl, gcol - grow, NEG_INF))
        kb = kh * bcol
        kq = bdot_nt(jnp.concatenate([kb, qh], axis=1).astype(BF16), kh.astype(BF16))
        p = -jnp.where(strict, kq[:, 0:c] * decay, 0.0)
        egc = jnp.exp(gcol)
        xs = jnp.concatenate([vh * bcol, kb * egc], axis=2)
        for it in range(6):
            ph = p.astype(BF16)
            pl_ = (p - ph.astype(F32)).astype(BF16)
            lhs = jnp.concatenate([ph, pl_, ph], axis=2)
            yh = xs.astype(BF16)
            yl = (xs - yh.astype(F32)).astype(BF16)
            if it < 5:
                yh = jnp.concatenate([yh, ph], axis=2)
                yl = jnp.concatenate([yl, pl_], axis=2)
            m = bdot(lhs, jnp.concatenate([yh, yh, yl], axis=1))
            xs = xs + m[:, :, 0:2 * D_DK]
            if it < 5:
                p = m[:, :, 2 * D_DK:3 * D_DK]
        u = xs[:, :, 0:D_DV]
        w = xs[:, :, D_DV:2 * D_DV]
        a = kq[:, c:2 * c] * decay
        kd = (kh * jnp.exp(glast - gcol)).astype(BF16)
        kdt = bdot_nt(jnp.broadcast_to(eye, (nb, c, c)), kd).astype(BF16)
        lhs_s = jnp.concatenate([w, qh * egc], axis=1).astype(BF16)
        lhs_v = jnp.concatenate([a.astype(BF16), kdt], axis=1)
        return u, lhs_s, lhs_v, jnp.exp(glast)

    prep = prepare(list(range(g_chunks)))

    def chains(arr, step):
        los = [((d * n_sq + sq) * g_chunks + step) * nh for d in range(2) for sq in range(n_sq)]
        return jnp.concatenate([arr[lo:lo + nh] for lo in los], axis=0)

    s = s_ref[...]
    outs = [[[None] * g_chunks for _ in range(n_sq)] for _ in range(2)]
    for step in range(g_chunks):
        u, lhs_s, lhs_v, eg = (chains(arr, step) for arr in prep)
        ws_qs = bdot(lhs_s, s.astype(BF16))
        vb = (u - ws_qs[:, 0:c]).astype(BF16)
        av = bdot(lhs_v, vb)
        o = ws_qs[:, c:2 * c] + av[:, 0:c]
        s = s * eg + av[:, c:2 * c]
        for d in range(2):
            gi = step if d == 0 else g_chunks - 1 - step
            for sq in range(n_sq):
                base = (d * n_sq + sq) * nh
                outs[d][sq][gi] = jnp.concatenate([o[base + h] for h in range(nh)], axis=1)
    for sq in range(n_sq):
        of_ref[sq] = jnp.concatenate(outs[0][sq], axis=0)
        ob_ref[sq] = jnp.concatenate(outs[1][sq], axis=0)
    s_ref[...] = s

    @pl.when(i == n_steps - 1)
    def _():
        for d in range(2):
            for sq in range(n_sq):
                sfin_ref[sq, d] = s[(d * n_sq + sq) * nh:(d * n_sq + sq + 1) * nh]


def _gdn_call(qkvn, proj, prow, pcol, s0, n_seq, t):
    g = GDN_CHUNKS_PER_STEP
    n_sq = math.gcd(n_seq, GDN_SEQS_PER_STEP)
    ns = t // (D_CHUNK * g)
    rows = D_CHUNK * g
    fwd = lambda i: i
    bwd = lambda i: ns - 1 - i
    x3 = qkvn.reshape(n_seq, t, 768)
    p3 = proj.reshape(n_seq, t, P1)
    xblk = lambda f: pl.BlockSpec((n_sq, rows, 768), lambda b, i: (b, f(i), 0))
    abblk = lambda f: pl.BlockSpec((n_sq, rows, 128), lambda b, i: (b, f(i), C_AB // 128))
    oblk = lambda f: pl.BlockSpec((n_sq, rows, 256), lambda b, i: (b, f(i), 0))
    sblk = pl.BlockSpec((n_sq, 2, D_HEADS, D_DK, D_DV), lambda b, i: (b, 0, 0, 0, 0))
    osd = jax.ShapeDtypeStruct((n_seq, t, 256), F32)
    in_specs = [xblk(fwd), xblk(bwd), abblk(fwd), abblk(bwd), _resident(prow), _resident(pcol)]
    args = [x3, x3, p3, p3, prow.arr, pcol.arr]
    if s0 is not None:
        for st in s0:
            in_specs.append(pl.BlockSpec((n_sq, None, D_HEADS, D_DK, D_DV), lambda b, i, l=st.l: (b, l, 0, 0, 0)))
            args.append(st.arr)
    o_f, o_b, s_fin = pl.pallas_call(
        functools.partial(_gdn_kernel, ns, g, n_sq, s0 is not None),
        grid=(n_seq // n_sq, ns),
        in_specs=in_specs,
        out_specs=[oblk(fwd), oblk(bwd), sblk],
        out_shape=[osd, osd, jax.ShapeDtypeStruct((n_seq, 2, D_HEADS, D_DK, D_DV), F32)],
        scratch_shapes=[pltpu.VMEM((2 * n_sq * D_HEADS, D_DK, D_DV), F32)],
        compiler_params=_cparams(("parallel", "arbitrary")),
        name="gdn",
    )(*args)
    return o_f.reshape(n_seq * t, 256), o_b.reshape(n_seq * t, 256), s_fin


def _merge_kernel(x_ref, mod_ref, gpre_ref, oa_ref, ob_ref, oc_ref, of_ref, obw_ref, z_ref, gnorm_ref,
                  wg_ref, wbr_ref, wo_ref, gpost_ref, o_ref):
    d = D_MODEL
    x = x_ref[...]
    mod = mod_ref[0]
    sh1, sc1, ga1 = mod[:, 0:d], mod[:, d:2 * d], mod[:, 2 * d:3 * d]
    h = (_rmsnorm_rows(x, gpre_ref[...], d) * (1.0 + sc1) + sh1).astype(BF16)
    o = of_ref[...] + obw_ref[...]
    ms = _dot_sel_right(o * o, _group_ones(256, D_DV)) * (1.0 / D_DV)
    od = o * lax.rsqrt(ms + EPS) * gnorm_ref[...] * _silu(z_ref[...])
    acc = None
    for m, br in enumerate((oa_ref[...], ob_ref[...], oc_ref[...], od)):
        gate = jax.nn.sigmoid(_dot(h, wg_ref[:, m * d:(m + 1) * d]))
        term = gate * _dot(br.astype(BF16), wbr_ref[m])
        acc = term if acc is None else acc + term
    mix = _dot(acc.astype(BF16), wo_ref[...])
    o_ref[...] = x + ga1 * _rmsnorm_rows(mix, gpost_ref[...], d)


def _merge_call(x2d, mod3, gpre, oa, ob, oc, of, obw, proj, gnorm, wg, wbr, wo, gpost, seq_len, tm):
    n, d = x2d.shape
    tiles_per_seq = seq_len // tm
    row = lambda w, c=0: pl.BlockSpec((tm, w), lambda i: (i, c))
    full = _resident
    return pl.pallas_call(
        _merge_kernel,
        grid=(n // tm,),
        in_specs=[row(d), _mod_block(mod3, tiles_per_seq), full(gpre),
                  row(256), row(256), row(256), row(256), row(256), row(256, C_Z // 256), full(gnorm),
                  full(wg), full(wbr), full(wo), full(gpost)],
        out_specs=row(d),
        out_shape=jax.ShapeDtypeStruct((n, d), F32),
        compiler_params=_cparams(("parallel",)),
        name="merge",
    )(*_operands([x2d, mod3, gpre, oa, ob, oc, of, obw, proj, gnorm, wg, wbr, wo, gpost]))


def _ffn_kernel(x_ref, mod_ref, gpre_ref, wup_ref, wdown_ref, gpost_ref, o_ref):
    d = D_MODEL
    x = x_ref[...]
    mod = mod_ref[0]
    sh2, sc2, ga2 = mod[:, 3 * d:4 * d], mod[:, 4 * d:5 * d], mod[:, 5 * d:6 * d]
    h = (_rmsnorm_rows(x, gpre_ref[...], d) * (1.0 + sc2) + sh2).astype(BF16)
    up = _dot(h, wup_ref[...])
    act = _silu(up[:, 0:D_FF]) * up[:, D_FF:2 * D_FF]
    f = _dot(act.astype(BF16), wdown_ref[...])
    o_ref[...] = x + ga2 * _rmsnorm_rows(f, gpost_ref[...], d)


def _ffn_call(x2d, mod3, gpre, wup, wdown, gpost, seq_len, tm):
    n, d = x2d.shape
    tiles_per_seq = seq_len // tm
    row = lambda w: pl.BlockSpec((tm, w), lambda i: (i, 0))
    full = _resident
    return pl.pallas_call(
        _ffn_kernel,
        grid=(n // tm,),
        in_specs=[row(d), _mod_block(mod3, tiles_per_seq), full(gpre), full(wup), full(wdown), full(gpost)],
        out_specs=row(d),
        out_shape=jax.ShapeDtypeStruct((n, d), F32),
        compiler_params=_cparams(("parallel",)),
        name="ffn",
    )(*_operands([x2d, mod3, gpre, wup, wdown, gpost]))


def _rope_tables(n_tokens, dim, reps):
    n_rows = n_tokens // GRID_W
    row = jnp.repeat(jnp.arange(n_rows), GRID_W).astype(F32)
    col = jnp.tile(jnp.arange(GRID_W), n_rows).astype(F32)
    nfreq = dim // 4
    inv = ROPE_BASE ** (-jnp.arange(nfreq, dtype=F32) / nfreq)
    ang_r = row[:, None] * inv
    ang_c = col[:, None] * inv
    ang = jnp.concatenate([ang_r, ang_r, ang_c, ang_c], axis=-1)
    cos, sin = jnp.cos(ang), jnp.sin(ang)
    even = ((jnp.arange(dim) // nfreq) % 2 == 0)[None, :]
    sin_next = jnp.where(even, -sin, 0.0)
    sin_prev = jnp.where(even, 0.0, sin)
    return tuple(jnp.tile(a, (1, reps)) for a in (cos, sin_next, sin_prev))


def _rope_tables_mla_wide(n_tokens):
    cos, sin_next, sin_prev = _rope_tables(n_tokens, B_ROPE, 1)
    tail = MLA_HEAD_LANES - B_NOPE - B_ROPE
    wide = lambda a, fill: jnp.pad(a, ((0, 0), (B_NOPE, tail)), constant_values=fill)
    return wide(cos, 1.0), wide(sin_next, 0.0), wide(sin_prev, 0.0)


def _split_cols(w):
    out, acc = [], 0
    for s in IN_SIZES:
        out.append(w[..., acc:acc + s])
        acc += s
    return out


def _pack_w_in_kernel(w_ref, w1_ref, wg_ref):
    w = w_ref[...]
    (a_q, a_k, a_v, b_cq, b_ckv, b_kr, c_in, d_qkv, d_z, d_beta, d_alpha, gate) = _split_cols(w)
    z = lambda n: jnp.zeros((w.shape[0], n), w.dtype)
    k0, k1 = a_k[:, :A_HD], a_k[:, A_HD:]
    v0, v1 = a_v[:, :A_HD], a_v[:, A_HD:]
    w1 = jnp.concatenate([a_q, k0, k0, k1, k1, v0, v0, v1, v1, b_cq, z(256 - B_Q_RANK), b_ckv,
                          b_kr, b_kr, b_kr, b_kr, c_in, d_qkv, d_z, d_beta, d_alpha, z(112)], axis=-1)
    w1_ref[...] = w1.astype(BF16)
    wg_ref[...] = gate.astype(BF16)


def _pack_w_in(w):
    depth, d, p = w.shape
    n_gate = IN_SIZES[-1]
    tr = 256
    return pl.pallas_call(
        _pack_w_in_kernel,
        grid=(depth, d // tr),
        in_specs=[pl.BlockSpec((None, tr, p), lambda l, i: (l, i, 0))],
        out_specs=[pl.BlockSpec((None, tr, P1), lambda l, i: (l, i, 0)),
                   pl.BlockSpec((None, tr, n_gate), lambda l, i: (l, i, 0))],
        out_shape=[jax.ShapeDtypeStruct((depth, d, P1), BF16), jax.ShapeDtypeStruct((depth, d, n_gate), BF16)],
        compiler_params=_cparams(("parallel", "parallel")),
        name="pack_w_in",
    )(w)


def _expand_kv_heads(a):
    return jnp.repeat(a, A_HEADS // A_KV_HEADS, axis=-2).reshape(a.shape[:-2] + (A_HEADS * A_HD,))


def _stacked_weights(g_pre1, g_post1, g_pre2, g_post2, w_in, a_sink, b_g_cq, b_g_ckv, b_w_uq, b_w_ukv,
                     c_w_pool, c_scale, d_conv, d_a_log, d_dt_bias, d_g_norm, w_br, w_o, w_up, w_down):
    depth = w_in.shape[0]
    w1, wg = _pack_w_in(w_in)
    wuq = b_w_uq.reshape(depth, B_Q_RANK, B_HEADS, B_NOPE + B_ROPE)
    wuq = jnp.pad(wuq, ((0, 0), (0, 256 - B_Q_RANK), (0, 0), (0, MLA_HEAD_LANES - B_NOPE - B_ROPE)))
    wuq = wuq.reshape(depth, 256, MLA_WIDE).astype(BF16)
    wukv = b_w_ukv.reshape(depth, B_KV_RANK, B_HEADS, B_NOPE + B_VD)
    wk = jnp.pad(wukv[..., :B_NOPE], ((0, 0), (0, 0), (0, 0), (0, MLA_HEAD_LANES - B_NOPE)))
    wukv = jnp.concatenate([wk.reshape(depth, B_KV_RANK, MLA_WIDE),
                            wukv[..., B_NOPE:].reshape(depth, B_KV_RANK, B_HEADS * B_VD)], -1).astype(BF16)
    same = jnp.eye(C_GROUPS, dtype=F32)[None, :, None, :, None]
    wpool = (c_w_pool[:, :, :, None, :] * same).reshape(depth, C_GROUPS * C_GW, C_GROUPS * C_GW).astype(BF16)
    gdn =jnp.stack([d_a_log.reshape(depth, 2 * D_HEADS), d_dt_bias.reshape(depth, 2 * D_HEADS)], axis=1)
    prow = jnp.pad(gdn, ((0, 0), (0, 0), (8, LANES - 16)))
    pcol = jnp.pad(gdn.transpose(0, 2, 1), ((0, 0), (8, 0), (0, 0)))
    row = lambda a: a[:, None, :]
    return dict(
        w1=w1, wg=wg, wuq=wuq, wukv=wukv,
        gpre1=row(g_pre1), gpost1=row(g_post1), gpre2=row(g_pre2), gpost2=row(g_post2),
        gcq=row(jnp.pad(b_g_cq, ((0, 0), (0, 256 - B_Q_RANK)))), gckv=row(b_g_ckv),
        sink=a_sink, wpool=wpool, cscale=row(c_scale),
        convw=jnp.pad(d_conv, ((0, 0), (0, SUBLANES - D_CONV), (0, 0))),
        prow=prow, pcol=pcol, gnorm=row(jnp.tile(d_g_norm, (1, D_HEADS))),
        wbr=w_br.astype(BF16), wo=w_o.astype(BF16), wup=w_up.astype(BF16), wdown=w_down.astype(BF16))


def _trunk_layer(x2d, mod3, lw, n_seq, t, ctx, tables=None):
    latent = ctx is not None
    n = x2d.shape[0]
    tm = min(512, t) if latent else 256
    tm_dense = min(512, n)
    if latent:
        proj, ckvn, qw, kwr, vb, aqr, akr, qwr = _inproj_call(
            x2d, mod3, lw["gpre1"], lw["w1"], lw["gcq"], lw["wuq"], lw["gckv"], lw["wukv"], tables, t, tm_dense)
        kctx, vctx, ctx_ckv, ckr_wide, s0f, s0b = ctx
        out_a = _win_attn_call(lw["sink"], aqr, akr, proj, kctx, vctx, n_seq, t)
        out_b = _mla_lat_call(qw, qwr, kwr, vb, ctx_ckv, ckr_wide, lw["wukv"], n_seq, t, min(MLA_TQ, t))
        s0 = (s0f, s0b)
    else:
        proj, ckvn, qw, kw, vb = _inproj_call(
            x2d, mod3, lw["gpre1"], lw["w1"], lw["gcq"], lw["wuq"], lw["gckv"], lw["wukv"], None, t, tm_dense)
        out_a, out_b = _attn_ctx_call(lw["sink"], proj, qw, kw, vb, n_seq, t)
        s0 = None
    out_c, qkvn = _local_call(proj, lw["convw"], lw["wpool"], lw["cscale"], t, tm)
    o_f, o_b, s_fin = _gdn_call(qkvn, proj, lw["prow"], lw["pcol"], s0, n_seq, t)
    x1 = _merge_call(x2d, mod3, lw["gpre1"], out_a, out_b, out_c, o_f, o_b, proj, lw["gnorm"],
                     lw["wg"], lw["wbr"], lw["wo"], lw["gpost1"], t, tm_dense)
    x2 = _ffn_call(x1, mod3, lw["gpre2"], lw["wup"], lw["wdown"], lw["gpost2"], t, tm_dense)
    new_ctx = None
    if not latent:
        p3 = proj.reshape(n_seq, t, P1)
        ka = jnp.stack([p3[:, :, C_AK:C_AK + A_HD], p3[:, :, C_AK + 2 * A_HD:C_AK + 3 * A_HD]], axis=2)
        va = jnp.stack([p3[:, :, C_AV:C_AV + A_HD], p3[:, :, C_AV + 2 * A_HD:C_AV + 3 * A_HD]], axis=2)
        new_ctx = (ka, va, ckvn.reshape(n_seq, t, B_KV_RANK), p3[:, :, C_KR:C_KR + B_ROPE],
                   s_fin[:, 0], s_fin[:, 1])
    return x2, new_ctx


def kernel(x_prompt, x_sample, cache_a_k, cache_a_v, cache_b_ckv, cache_b_krope, state_d_fwd, state_d_bwd, c, c_ctx, w_mod, b_mod, g_pre1, g_post1, g_pre2, g_post2, w_in, a_sink, b_g_cq, b_g_ckv, b_w_uq, b_w_ukv, c_w_pool, c_scale, d_conv, d_a_log, d_dt_bias, d_g_norm, w_br, w_o, w_up, w_down):
    depth = w_mod.shape[0]
    bp, tp, d = x_prompt.shape
    bs, ts, _ = x_sample.shape
    assert bs + 1 <= SUBLANES
    cond8 = jnp.zeros((SUBLANES, d), F32).at[0].set(c_ctx).at[1:1 + bs].set(c)
    mod_all = _mod_call(cond8, w_mod, b_mod).reshape(depth, SUBLANES, 1, 6 * d)
    yp = x_prompt.reshape(bp * tp, d)
    ys = x_sample.reshape(bs * ts, d)
    new = [[] for _ in range(6)]
    tables = _rope_tables(ts, A_HD, LANES // A_HD) + _rope_tables_mla_wide(ts)
    kctx_all, vctx_all = _expand_kv_heads(cache_a_k), _expand_kv_heads(cache_a_v)
    ckr_wide_all = jnp.tile(jnp.pad(cache_b_krope, ((0, 0), (0, 0), (0, 0),
                                                    (B_NOPE, MLA_HEAD_LANES - B_NOPE - B_ROPE))), (1, 1, 1, B_HEADS))
    stacked = _stacked_weights(g_pre1, g_post1, g_pre2, g_post2, w_in, a_sink, b_g_cq, b_g_ckv, b_w_uq, b_w_ukv,
                               c_w_pool, c_scale, d_conv, d_a_log, d_dt_bias, d_g_norm, w_br, w_o, w_up, w_down)
    for l in range(depth):
        lw = {k: _Layer(v, l) for k, v in stacked.items()}
        mod_ctx = _Mod(mod_all, l, 0, 1)
        mod_lat = _Mod(mod_all, l, 1, bs)
        yp, nctx = _trunk_layer(yp, mod_ctx, lw, bp, tp, None)
        for acc, val in zip(new, nctx):
            acc.append(val)
        ctx = tuple(_Layer(a, l) for a in (kctx_all, vctx_all, cache_b_ckv, ckr_wide_all, state_d_fwd, state_d_bwd))
        ys, _ = _trunk_layer(ys, mod_lat, lw, bs, ts, ctx, tables)
    return (yp.reshape(bp, tp, d), ys.reshape(bs, ts, d)) + tuple(jnp.stack(v, axis=1) for v in new)
```

```python
import functools
import math
from typing import NamedTuple

import jax
import jax.numpy as jnp
from jax import lax
from jax.experimental import pallas as pl
from jax.experimental.pallas import tpu as pltpu

F32 = jnp.float32
BF16 = jnp.bfloat16

D_MODEL = 1024
GRID_W = 64
ROPE_BASE = 10000.0
EPS = 1e-6
NEG_INF = -1e30
Q_BLOCK = 128
N_BRANCH = 4
BRANCH_W = 256
A_HEADS = 4
A_KV_HEADS = 2
A_HD = 64
A_WINDOW = 128
A_SCALE = A_HD ** -0.5
B_HEADS = 4
B_Q_RANK = 192
B_KV_RANK = 128
B_NOPE = 64
B_ROPE = 32
B_VD = 64
B_SCALE = (B_NOPE + B_ROPE) ** -0.5
C_GROUPS = 4
C_GW = 64
C_WINDOWS = (2, 4, 8, 16)
D_HEADS = 4
D_DK = 64
D_DV = 64
D_CONV = 5
D_CHUNK = 64
FF_RAW = -(-8 * D_MODEL // 3)
D_FF = -(-FF_RAW // 256) * 256
IN_SIZES = (A_HEADS * A_HD, A_KV_HEADS * A_HD, A_KV_HEADS * A_HD, B_Q_RANK, B_KV_RANK, B_ROPE,
            C_GROUPS * C_GW, D_HEADS * (2 * D_DK + D_DV), D_HEADS * D_DV, 2 * D_HEADS, 2 * D_HEADS,
            N_BRANCH * D_MODEL)

LANES = 128
SUBLANES = 8
VMEM_LIMIT_BYTES = 56 * 1024 * 1024
HALO = SUBLANES
GDN_SEQS_PER_STEP = 2
GDN_CHUNKS_PER_STEP = 4
MLA_HEAD_LANES = LANES
MLA_WIDE = B_HEADS * MLA_HEAD_LANES
LOG2E = math.log2(math.e)
MLA_TQ = 512
MLA_ROW_GROUP = 32
MLA_COL_BLOCK = 256
WIN_ATTN_TQ = 512

C_AQ = 0
C_AK = 256
C_AV = 512
C_CQ = 768
C_CKV = 1024
C_KR = 1152
C_CIN = 1280
C_QKV = 1536
C_Z = 2304
C_AB = 2560
P1 = 2688


def _cparams(sem):
    return pltpu.CompilerParams(dimension_semantics=sem, vmem_limit_bytes=VMEM_LIMIT_BYTES)


class _Layer(NamedTuple):
    arr: jax.Array
    l: int


def _resident(p):
    shape = p.arr.shape[1:]
    return pl.BlockSpec((None,) + shape, lambda *_: (p.l,) + (0,) * len(shape), pipeline_mode=pl.Buffered(1))


class _Mod(NamedTuple):
    arr: jax.Array
    l: int
    first: int
    count: int


def _mod_block(m, tiles_per_seq):
    width = m.arr.shape[-1]
    if m.count == 1:
        return pl.BlockSpec((None, 1, 1, width), lambda i: (m.l, m.first, 0, 0))
    return pl.BlockSpec((None, 1, 1, width), lambda i: (m.l, m.first + i // tiles_per_seq, 0, 0))


def _cache_block(p):
    return pl.BlockSpec((1, None) + p.arr.shape[2:], lambda b, i: (b, p.l, 0, 0))


def _operands(args):
    return [a.arr if isinstance(a, (_Layer, _Mod)) else a for a in args]


def _dot(a, b):
    return jnp.dot(a, b, preferred_element_type=F32)


def _dot_nt(a, b):
    return lax.dot_general(a, b, (((1,), (1,)), ((), ())), preferred_element_type=F32)


def _split3(x):
    hi = x.astype(BF16)
    r1 = x - hi.astype(F32)
    mid = r1.astype(BF16)
    lo = (r1 - mid.astype(F32)).astype(BF16)
    return hi, mid, lo


def _dot_sel_right(x, sel):
    hi, mid, lo = _split3(x)
    return _dot(hi, sel) + _dot(mid, sel) + _dot(lo, sel)


def _silu(x):
    return x * jax.nn.sigmoid(x)


def _softplus(x):
    return jnp.maximum(x, 0.0) + jnp.log1p(jnp.exp(-jnp.abs(x)))


def _head_mask(width, n_lanes, h):
    lane = lax.broadcasted_iota(jnp.int32, (1, n_lanes), 1)
    return (lane // width) == h


def _group_ones(n, width):
    r = lax.broadcasted_iota(jnp.int32, (n, n), 0) // width
    c = lax.broadcasted_iota(jnp.int32, (n, n), 1) // width
    return jnp.where(r == c, 1.0, 0.0).astype(BF16)


def _mod_kernel(c_ref, w_ref, b_ref, o_ref):
    s = _silu(c_ref[...])
    o_ref[0] = _dot(s.astype(BF16), w_ref[0].astype(BF16)) + b_ref[0]


def _mod_call(cond8, w_mod, b_mod):
    depth, d, n = w_mod.shape
    tn = 1536
    return pl.pallas_call(
        _mod_kernel,
        grid=(depth, n // tn),
        in_specs=[pl.BlockSpec((SUBLANES, d), lambda l, j: (0, 0)),
                  pl.BlockSpec((1, d, tn), lambda l, j: (l, 0, j)),
                  pl.BlockSpec((1, 1, tn), lambda l, j: (l, 0, j))],
        out_specs=pl.BlockSpec((1, SUBLANES, tn), lambda l, j: (l, 0, j)),
        out_shape=jax.ShapeDtypeStruct((depth, SUBLANES, n), F32),
        compiler_params=_cparams(("parallel", "parallel")),
        name="mod",
    )(cond8, w_mod, b_mod.reshape(depth, 1, n))


def _rmsnorm_rows(x, g, n):
    ms = jnp.sum(x * x, axis=-1, keepdims=True) * (1.0 / n)
    return x * lax.rsqrt(ms + EPS) * g


def _rope(x, cos, sin_next, sin_prev, quarter):
    n = x.shape[-1]
    return x * cos + pltpu.roll(x, n - quarter, 1) * sin_next + pltpu.roll(x, quarter, 1) * sin_prev


def _mla_wide_kr(kr_rep):
    lane = lax.broadcasted_iota(jnp.int32, (1, MLA_HEAD_LANES), 1)
    keep = (lane >= B_NOPE) & (lane < B_NOPE + B_ROPE)
    return jnp.concatenate([jnp.where(keep, kr_rep, 0.0)] * B_HEADS, axis=1)


def _inproj_kernel(latent, *refs):
    if latent:
        (x_ref, mod_ref, gpre_ref, w1_ref, gcq_ref, wuq_ref, gckv_ref, wukv_ref,
         ca_ref, sna_ref, spa_ref, cb_ref, snb_ref, spb_ref,
         proj_ref, qw_ref, kw_ref, vb_ref, aqr_ref, akr_ref, qwr_ref) = refs
    else:
        (x_ref, mod_ref, gpre_ref, w1_ref, gcq_ref, wuq_ref, gckv_ref, wukv_ref,
         proj_ref, qw_ref, kw_ref, vb_ref, ckv_ref, ak_ref, av_ref, kr_ref) = refs
    d = D_MODEL
    x = x_ref[...]
    mod = mod_ref[0]
    sh1 = mod[:, 0:d]
    sc1 = mod[:, d:2 * d]
    h = _rmsnorm_rows(x, gpre_ref[...], d) * (1.0 + sc1) + sh1
    proj = _dot(h.astype(BF16), w1_ref[...])
    proj_ref[...] = proj
    cqn = _rmsnorm_rows(proj[:, C_CQ:C_CQ + 256], gcq_ref[...], B_Q_RANK)
    qw = _dot(cqn.astype(BF16), wuq_ref[...]) * (B_SCALE * LOG2E)
    qw_ref[...] = qw.astype(BF16)
    ckvn = _rmsnorm_rows(proj[:, C_CKV:C_CKV + B_KV_RANK], gckv_ref[...], B_KV_RANK)
    if not latent:
        unexpand = lambda c: jnp.concatenate([proj[:, c:c + A_HD], proj[:, c + 2 * A_HD:c + 3 * A_HD]], axis=1)
        ckv_ref[...] = ckvn
        ak_ref[...] = unexpand(C_AK)
        av_ref[...] = unexpand(C_AV)
        kr_ref[...] = proj[:, C_KR:C_KR + B_ROPE]
    knv =_dot(ckvn.astype(BF16), wukv_ref[...])
    vb_ref[...] = knv[:, MLA_WIDE:].astype(BF16)
    kr_wide = _mla_wide_kr(proj[:, C_KR:C_KR + 128])
    if latent:
        ca, sna, spa = (jnp.concatenate([r[...]] * (256 // LANES), axis=1) for r in (ca_ref, sna_ref, spa_ref))
        cb, snb, spb = (jnp.concatenate([r[...]] * B_HEADS, axis=1) for r in (cb_ref, snb_ref, spb_ref))
        aqr_ref[...] = _rope(proj[:, C_AQ:C_AQ + 256], ca, sna, spa, A_HD // 4).astype(BF16)
        akr_ref[...] = _rope(proj[:, C_AK:C_AK + 256], ca, sna, spa, A_HD // 4).astype(BF16)
        qwr_ref[...] = _rope(qw, cb, snb, spb, B_ROPE // 4).astype(BF16)
        kr_wide = _rope(kr_wide, cb, snb, spb, B_ROPE // 4)
    kw_ref[...] = (knv[:, :MLA_WIDE] + kr_wide).astype(BF16)


def _inproj_call(x2d, mod3, gpre, w1, gcq, wuq, gckv, wukv, tables, seq_len, tm):
    n, d = x2d.shape
    latent = tables is not None
    tiles_per_seq = seq_len // tm if latent else 1
    row = lambda w, dt=F32: (pl.BlockSpec((tm, w), lambda i: (i, 0)), jax.ShapeDtypeStruct((n, w), dt))
    params = [gpre, w1, gcq, wuq, gckv, wukv]
    in_specs = [pl.BlockSpec((tm, d), lambda i: (i, 0)),
                _mod_block(mod3, seq_len // tm)] + [_resident(p) for p in params]
    args = [x2d, mod3] + params
    outs = [row(P1), row(MLA_WIDE, BF16), row(MLA_WIDE, BF16), row(256, BF16)]
    if latent:
        for t in tables:
            in_specs.append(pl.BlockSpec((tm, t.shape[1]), lambda i: (i % tiles_per_seq, 0)))
            args.append(t)
        outs += [row(256, BF16), row(256, BF16), row(MLA_WIDE, BF16)]
    else:
        outs += [row(B_KV_RANK), row(A_KV_HEADS * A_HD), row(A_KV_HEADS * A_HD), row(B_ROPE)]
    return pl.pallas_call(
        functools.partial(_inproj_kernel, latent),
        grid=(n // tm,),
        in_specs=in_specs,
        out_specs=[o[0] for o in outs],
        out_shape=[o[1] for o in outs],
        compiler_params=_cparams(("parallel",)),
        name="inproj_lat" if latent else "inproj_ctx",
    )(*_operands(args))


def _softmax_pv(scores, values, scale, sink, hm):
    c1 = None if scale is None else scale * LOG2E
    m = jnp.max(scores[0], axis=-1, keepdims=True)
    for s in scores[1:]:
        m = jnp.maximum(m, jnp.max(s, axis=-1, keepdims=True))
    m2 = m if c1 is None else m * c1
    if sink is not None:
        m2 = jnp.maximum(m2, sink * LOG2E)
    den = None
    pv = None
    for s, v in zip(scores, values):
        p = jnp.exp2((s if c1 is None else s * c1) - m2)
        ps = jnp.sum(p, axis=-1, keepdims=True)
        den = ps if den is None else den + ps
        t = _dot(p.astype(BF16), v)
        pv = t if pv is None else pv + t
    if sink is not None:
        den = den + jnp.exp2(sink * LOG2E - m2)
    return jnp.where(hm, pv / den, 0.0)


def _attn_ctx_kernel(layer, sink_ref, q_ref, k_ref, v_ref, qw_ref, kw_ref, vb_ref, oa_ref, ob_ref):
    q = q_ref[...]
    k = k_ref[...].astype(BF16)
    v = v_ref[...].astype(BF16)
    acc = jnp.zeros(q.shape, F32)
    for h in range(A_HEADS):
        hm = _head_mask(A_HD, 256, h)
        s = _dot_nt(jnp.where(hm, q, 0.0).astype(BF16), k)
        acc = acc + _softmax_pv([s], [v], A_SCALE, sink_ref[layer, h], hm)
    oa_ref[...] = acc
    vb = vb_ref[...]
    acc = jnp.zeros((q.shape[0], B_HEADS * B_VD), F32)
    for h in range(B_HEADS):
        hs = slice(h * MLA_HEAD_LANES, (h + 1) * MLA_HEAD_LANES)
        s = _dot_nt(qw_ref[:, hs], kw_ref[:, hs])
        acc = acc + _softmax_pv([s], [vb], None, None, _head_mask(B_VD, 256, h))
    ob_ref[...] = acc


def _attn_ctx_call(sink, proj, qw, kw, vb, n_seq, t):
    n = proj.shape[0]
    col = lambda w, j: pl.BlockSpec((t, w), lambda b: (b, j))
    out = (pl.BlockSpec((t, 256), lambda b: (b, 0)), jax.ShapeDtypeStruct((n, 256), F32))
    return pl.pallas_call(
        functools.partial(_attn_ctx_kernel, sink.l),
        grid=(n_seq,),
        in_specs=[pl.BlockSpec(memory_space=pltpu.SMEM),
                  col(256, C_AQ // 256), col(256, C_AK // 256), col(256, C_AV // 256),
                  col(MLA_WIDE, 0), col(MLA_WIDE, 0), col(256, 0)],
        out_specs=[out[0], out[0]],
        out_shape=[out[1], out[1]],
        compiler_params=_cparams(("parallel",)),
        name="attn_ctx",
    )(sink.arr, proj, proj, proj, qw, kw, vb)


def _win_attn_kernel(t, layer, sink_ref, qr_ref, q_ref, kp_ref, kc_ref, kn_ref, vp_ref, vc_ref, vn_ref,
                     kctx_ref, vctx_ref, o_ref):
    i = pl.program_id(1)
    qr = qr_ref[...]
    q = q_ref[...]
    tq = q.shape[0]
    kl = jnp.concatenate([kp_ref[...], kc_ref[...], kn_ref[...]], axis=0)
    vl = jnp.concatenate([vp_ref[...], vc_ref[...], vn_ref[...]], axis=0).astype(BF16)
    kctx = kctx_ref[0].astype(BF16)
    vctx = vctx_ref[0].astype(BF16)
    q_pos = i * tq + lax.broadcasted_iota(jnp.int32, (tq, tq + 2 * A_WINDOW), 0)
    k_pos = i * tq - A_WINDOW + lax.broadcasted_iota(jnp.int32, (tq, tq + 2 * A_WINDOW), 1)
    valid = (jnp.abs(q_pos - k_pos) <= A_WINDOW) & (k_pos >= 0) & (k_pos < t)
    acc = jnp.zeros(q.shape, F32)
    for h in range(A_HEADS):
        hm = _head_mask(A_HD, 256, h)
        s_loc = jnp.where(valid, _dot_nt(jnp.where(hm, qr, 0.0).astype(BF16), kl), NEG_INF)
        s_ctx = _dot_nt(jnp.where(hm, q, 0.0).astype(BF16), kctx)
        acc = acc + _softmax_pv([s_loc, s_ctx], [vl, vctx], A_SCALE, sink_ref[layer, h], hm)
    o_ref[...] = acc


def _win_attn_call(sink, aqr, akr, proj, kctx, vctx, n_seq, t):
    n = proj.shape[0]
    tq = min(WIN_ATTN_TQ, t)
    nq = t // tq
    r = tq // A_WINDOW
    nh = t // A_WINDOW
    cur = lambda j: pl.BlockSpec((tq, 256), lambda b, i: (b * nq + i, j))
    prev = lambda j: pl.BlockSpec((A_WINDOW, 256), lambda b, i: (b * nh + jnp.maximum(i * r - 1, 0), j))
    nxt = lambda j: pl.BlockSpec((A_WINDOW, 256), lambda b, i: (b * nh + jnp.minimum((i + 1) * r, nh - 1), j))
    ctx = _cache_block(kctx)
    return pl.pallas_call(
        functools.partial(_win_attn_kernel, t, sink.l),
        grid=(n_seq, nq),
        in_specs=[pl.BlockSpec(memory_space=pltpu.SMEM),
                  cur(0), cur(C_AQ // 256),
                  prev(0), cur(0), nxt(0),
                  prev(C_AV // 256), cur(C_AV // 256), nxt(C_AV // 256),
                  ctx, ctx],
        out_specs=pl.BlockSpec((tq, 256), lambda b, i: (b * nq + i, 0)),
        out_shape=jax.ShapeDtypeStruct((n, 256), F32),
        compiler_params=_cparams(("parallel", "parallel")),
        name="win_attn",
    )(sink.arr, aqr, proj, akr, akr, akr, proj, proj, proj, kctx.arr, vctx.arr)


def _mla_lat_kernel(qw_ref, qwr_ref, kwr_ref, vb_ref, cckv_ref, ckr_ref, wukv_ref, o_ref, s_ref, p_ref, den_ref):
    kvc = _dot(cckv_ref[0].astype(BF16), wukv_ref[...])
    kwc = (kvc[:, :MLA_WIDE] + ckr_ref[0]).astype(BF16)
    vc = kvc[:, MLA_WIDE:].astype(BF16)
    tq = qw_ref.shape[0]
    t = kwr_ref.shape[0]
    n_keys = s_ref.shape[2]
    cb = MLA_COL_BLOCK
    rg = MLA_ROW_GROUP

    def scores_into(slot, h):
        hs = slice(h * MLA_HEAD_LANES, (h + 1) * MLA_HEAD_LANES)
        s_ref[slot, :, 0:t] = _dot_nt(qwr_ref[:, hs], kwr_ref[:, hs])
        s_ref[slot, :, t:n_keys] = _dot_nt(qw_ref[:, hs], kwc[:, hs])

    scores_into(0, 0)
    acc = jnp.zeros((tq, B_HEADS * B_VD), F32)
    for h in range(B_HEADS):
        slot = h % 2
        if h + 1 < B_HEADS:
            scores_into(1 - slot, h + 1)
        for r0 in range(0, tq, rg):
            rows = slice(r0, r0 + rg)
            m = s_ref[slot, rows, 0:cb]
            for c0 in range(cb, n_keys, cb):
                m = jnp.maximum(m, s_ref[slot, rows, c0:c0 + cb])
            m2 = jnp.max(m, axis=-1, keepdims=True)
            part = None
            for c0 in range(0, n_keys, cb):
                p = jnp.exp2(s_ref[slot, rows, c0:c0 + cb] - m2)
                part = p if part is None else part + p
                p_ref[rows, c0:c0 + cb] = p.astype(BF16)
            den_ref[rows, :] = jnp.broadcast_to(jnp.sum(part, axis=-1, keepdims=True), (rg, LANES))
        pv = _dot(p_ref[:, 0:t], vb_ref[...]) + _dot(p_ref[:, t:n_keys], vc)
        acc = acc + jnp.where(_head_mask(B_VD, 256, h), pv / den_ref[:, 0:1], 0.0)
    o_ref[...] = acc


def _mla_lat_call(qw, qwr, kwr, vb, cckv, ckr_wide, wukv, n_seq, t, tq):
    n = qw.shape[0]
    nq = t // tq
    qblk = lambda w: pl.BlockSpec((tq, w), lambda b, i: (b * nq + i, 0))
    seq = lambda w: pl.BlockSpec((t, w), lambda b, i: (b, 0))
    ctx = _cache_block
    return pl.pallas_call(
        _mla_lat_kernel,
        grid=(n_seq, nq),
        in_specs=[qblk(MLA_WIDE), qblk(MLA_WIDE), seq(MLA_WIDE), seq(256), ctx(cckv), ctx(ckr_wide),
                  _resident(wukv)],
        out_specs=pl.BlockSpec((tq, 256), lambda b, i: (b * nq + i, 0)),
        out_shape=jax.ShapeDtypeStruct((n, 256), F32),
        scratch_shapes=[pltpu.VMEM((2, tq, t + cckv.arr.shape[2]), F32), pltpu.VMEM((tq, t + cckv.arr.shape[2]), BF16),
                        pltpu.VMEM((tq, LANES), F32)],
        compiler_params=_cparams(("parallel", "parallel")),
        name="mla_lat",
    )(qw, qwr, kwr, vb, cckv.arr, ckr_wide.arr, wukv.arr)


def _with_halo(prev_ref, cur_ref, next_ref, has_prev, has_next):
    prev = jnp.where(has_prev, prev_ref[...], 0.0)
    nxt = jnp.where(has_next, next_ref[...], 0.0)
    return jnp.concatenate([prev, cur_ref[...], nxt], axis=0)


def _shift_rows(x, k):
    n = x.shape[0]
    return pltpu.roll(x, (-k) % n, 0)


def _local_kernel(t, tm, cp_ref, cc_ref, cn_ref, qp_ref, qc_ref, qn_ref, convw_ref, wpool_ref, cscale_ref,
                  oc_ref, qkv_ref):
    i = pl.program_id(0)
    tiles_per_seq = t // tm
    j = i % tiles_per_seq
    has_prev = j > 0
    has_next = j < tiles_per_seq - 1
    x = _with_halo(cp_ref, cc_ref, cn_ref, has_prev, has_next)
    p2 = x + _shift_rows(x, -1)
    p4 = _shift_rows(p2, 1) + _shift_rows(p2, -1)
    p8 = _shift_rows(p4, 2) + _shift_rows(p4, -2)
    p16 = _shift_rows(p8, 4) + _shift_rows(p8, -4)
    grp = lax.broadcasted_iota(jnp.int32, (1, 256), 1) // C_GW
    win = jnp.where(grp == 0, p2, jnp.where(grp == 1, p4, jnp.where(grp == 2, p8, p16)))[HALO:HALO + tm]
    pos = j * tm + lax.broadcasted_iota(jnp.int32, (tm, 256), 0)
    half = jnp.where(grp == 0, 1, jnp.where(grp == 1, 2, jnp.where(grp == 2, 4, 8)))
    cnt = (jnp.minimum(pos + half, t) - jnp.maximum(pos - half, 0)).astype(F32)
    y = win / cnt - cc_ref[...]
    oc_ref[...] = _dot(y.astype(BF16), wpool_ref[...]) * cscale_ref[...]
    xq = _with_halo(qp_ref, qc_ref, qn_ref, has_prev, has_next)
    w = convw_ref[...]
    pad = D_CONV // 2
    acc = None
    for tap in range(D_CONV):
        term = _shift_rows(xq, tap - pad) * w[tap:tap + 1, :]
        acc = term if acc is None else acc + term
    u = _silu(acc[HALO:HALO + tm])
    ones = _group_ones(256, D_DK)
    q = u[:, 0:256]
    k = u[:, 256:512]
    q = q * lax.rsqrt(_dot_sel_right(q * q, ones) + EPS) * (D_DK ** -0.5)
    k = k * lax.rsqrt(_dot_sel_right(k * k, ones) + EPS)
    qkv_ref[:, 0:256] = q
    qkv_ref[:, 256:512] = k
    qkv_ref[:, 512:768] = u[:, 512:768]


def _local_call(proj, convw8, wpool_bd, cscale, t, tm):
    n = proj.shape[0]
    hb = tm // HALO
    nblk = n // HALO
    cur = lambda w, c: pl.BlockSpec((tm, w), lambda i: (i, c))
    prev = lambda w, c: pl.BlockSpec((HALO, w), lambda i: (jnp.maximum(i * hb - 1, 0), c))
    nxt = lambda w, c: pl.BlockSpec((HALO, w), lambda i: (jnp.minimum((i + 1) * hb, nblk - 1), c))
    full = _resident
    cc, qc = C_CIN // 256, C_QKV // 768
    return pl.pallas_call(
        functools.partial(_local_kernel, t, tm),
        grid=(n // tm,),
        in_specs=[prev(256, cc), cur(256, cc), nxt(256, cc), prev(768, qc), cur(768, qc), nxt(768, qc),
                  full(convw8), full(wpool_bd), full(cscale)],
        out_specs=[pl.BlockSpec((tm, 256), lambda i: (i, 0)), pl.BlockSpec((tm, 768), lambda i: (i, 0))],
        out_shape=[jax.ShapeDtypeStruct((n, 256), F32), jax.ShapeDtypeStruct((n, 768), F32)],
        compiler_params=_cparams(("parallel",)),
        name="local",
    )(proj, proj, proj, proj, proj, proj, convw8.arr, wpool_bd.arr, cscale.arr)


def _gdn_kernel(n_steps, g_chunks, n_sq, has_s0, xf_ref, xb_ref, abf_ref, abb_ref, prow_ref, pcol_ref, *refs):
    if has_s0:
        s0f_ref, s0b_ref, of_ref, ob_ref, sfin_ref, s_ref = refs
    else:
        of_ref, ob_ref, sfin_ref, s_ref = refs
    i = pl.program_id(1)
    c = D_CHUNK
    nh = D_HEADS

    @pl.when(i == 0)
    def _():
        if has_s0:
            for sq in range(n_sq):
                s_ref[sq * nh:(sq + 1) * nh] = s0f_ref[sq]
                s_ref[(n_sq + sq) * nh:(n_sq + sq + 1) * nh] = s0b_ref[sq]
        else:
            s_ref[...] = jnp.zeros(s_ref.shape, F32)

    r = lax.broadcasted_iota(jnp.int32, (c, c), 0)
    cidx = lax.broadcasted_iota(jnp.int32, (c, c), 1)
    lower = jnp.where(cidx <= r, 1.0, 0.0).astype(BF16)
    upper = jnp.where(cidx >= r, 1.0, 0.0).astype(BF16)
    prow = prow_ref[...]
    pcol = pcol_ref[...]

    eye = jnp.where(cidx == r, 1.0, 0.0).astype(BF16)
    sel_lower = jnp.concatenate([lower] * 3, axis=1)
    sel_upper = jnp.concatenate([upper] * 3, axis=1)
    pick = jnp.where(lax.broadcasted_iota(jnp.int32, (16, LANES), 0) == lax.broadcasted_iota(jnp.int32, (16, LANES), 1),
                     1.0, 0.0).astype(BF16)
    sel_rows = jnp.concatenate([pick] * 3, axis=1)
    sel_lower_t = jnp.concatenate([lower] * 3, axis=0)
    sel_upper_t = jnp.concatenate([upper] * 3, axis=0)

    bdot = lambda a, b: jnp.einsum('bik,bkj->bij', a, b, preferred_element_type=F32)
    bdot_nt = lambda a, b: jnp.einsum('bik,bjk->bij', a, b, preferred_element_type=F32)

    def prepare(steps):
        nb = 2 * n_sq * len(steps) * nh
        is_fwd = lax.broadcasted_iota(jnp.int32, (nb, c, c), 0) < nb // 2
        r3 = lax.broadcasted_iota(jnp.int32, (nb, c, c), 1)
        c3 = lax.broadcasted_iota(jnp.int32, (nb, c, c), 2)
        ahead = jnp.where(is_fwd, r3 - c3, c3 - r3)
        incl = ahead >= 0
        strict = ahead > 0
        q_l, k_l, v_l, b_l, gcol_l, grow_l = [], [], [], [], [], []
        for d, sq, step in [(d, sq, step) for d in range(2) for sq in range(n_sq) for step in steps]:
            x_ref, ab_ref = ((xf_ref, abf_ref), (xb_ref, abb_ref))[d]
            gi = step if d == 0 else g_chunks - 1 - step
            rows = slice(gi * c, (gi + 1) * c)
            x = x_ref[sq, rows, :]
            ab = ab_ref[sq, rows, :]
            abr = _dot_nt(sel_rows, jnp.concatenate(_split3(ab), axis=1))
            beta_c = jax.nn.sigmoid(ab)
            g_c = -jnp.exp(prow[0:1, :]) * _softplus(ab + prow[1:2, :])
            g_r = -jnp.exp(pcol[:, 0:1]) * _softplus(abr + pcol[:, 1:2])
            if d == 0:
                gc_c = _dot(sel_lower, jnp.concatenate(_split3(g_c), axis=0))
                gc_r = _dot(jnp.concatenate(_split3(g_r), axis=1), sel_upper_t)
            else:
                gc_c = _dot(sel_upper, jnp.concatenate(_split3(g_c), axis=0))
                gc_r = _dot(jnp.concatenate(_split3(g_r), axis=1), sel_lower_t)
            for h in range(nh):
                hs = slice(h * D_DK, (h + 1) * D_DK)
                lane = nh * d + h
                q_l.append(x[:, 0:256][:, hs])
                k_l.append(x[:, 256:512][:, hs])
                v_l.append(x[:, 512:768][:, hs])
                b_l.append(beta_c[:, lane:lane + 1])
                gcol_l.append(gc_c[:, 8 + lane:9 + lane])
                grow_l.append(gc_r[8 + lane:9 + lane, :])
        qh, kh, vh = jnp.stack(q_l), jnp.stack(k_l), jnp.stack(v_l)
        bcol, gcol, grow = jnp.stack(b_l), jnp.stack(gcol_l), jnp.stack(grow_l)
        glast = jnp.concatenate([gcol[:nb // 2, c - 1:c, :], gcol[nb // 2:, 0:1, :]], axis=0)
        decay = jnp.exp(jnp.where(incl, gcol - grow, NEG_INF))
        kb = kh * bcol
        kq = bdot_nt(jnp.concatenate([kb, qh], axis=1).astype(BF16), kh.astype(BF16))
        p = -jnp.where(strict, kq[:, 0:c] * decay, 0.0)
        egc = jnp.exp(gcol)
        xs = jnp.concatenate([vh * bcol, kb * egc], axis=2)
        for it in range(6):
            ph = p.astype(BF16)
            pl_ = (p - ph.astype(F32)).astype(BF16)
            lhs = jnp.concatenate([ph, pl_, ph], axis=2)
            yh = xs.astype(BF16)
            yl = (xs - yh.astype(F32)).astype(BF16)
            if it < 5:
                yh = jnp.concatenate([yh, ph], axis=2)
                yl = jnp.concatenate([yl, pl_], axis=2)
            m = bdot(lhs, jnp.concatenate([yh, yh, yl], axis=1))
            xs = xs + m[:, :, 0:2 * D_DK]
            if it < 5:
                p = m[:, :, 2 * D_DK:3 * D_DK]
        u = xs[:, :, 0:D_DV]
        w = xs[:, :, D_DV:2 * D_DV]
        a = kq[:, c:2 * c] * decay
        kd = (kh * jnp.exp(glast - gcol)).astype(BF16)
        kdt = bdot_nt(jnp.broadcast_to(eye, (nb, c, c)), kd).astype(BF16)
        lhs_s = jnp.concatenate([w, qh * egc], axis=1).astype(BF16)
        lhs_v = jnp.concatenate([a.astype(BF16), kdt], axis=1)
        return u, lhs_s, lhs_v, jnp.exp(glast)

    prep = prepare(list(range(g_chunks)))

    def chains(arr, step):
        los = [((d * n_sq + sq) * g_chunks + step) * nh for d in range(2) for sq in range(n_sq)]
        return jnp.concatenate([arr[lo:lo + nh] for lo in los], axis=0)

    s = s_ref[...]
    outs = [[[None] * g_chunks for _ in range(n_sq)] for _ in range(2)]
    for step in range(g_chunks):
        u, lhs_s, lhs_v, eg = (chains(arr, step) for arr in prep)
        ws_qs = bdot(lhs_s, s.astype(BF16))
        vb = (u - ws_qs[:, 0:c]).astype(BF16)
        av = bdot(lhs_v, vb)
        o = ws_qs[:, c:2 * c] + av[:, 0:c]
        s = s * eg + av[:, c:2 * c]
        for d in range(2):
            gi = step if d == 0 else g_chunks - 1 - step
            for sq in range(n_sq):
                base = (d * n_sq + sq) * nh
                outs[d][sq][gi] = jnp.concatenate([o[base + h] for h in range(nh)], axis=1)
    for sq in range(n_sq):
        of_ref[sq] = jnp.concatenate(outs[0][sq], axis=0)
        ob_ref[sq] = jnp.concatenate(outs[1][sq], axis=0)
    s_ref[...] = s

    @pl.when(i == n_steps - 1)
    def _():
        for d in range(2):
            for sq in range(n_sq):
                sfin_ref[sq, d] = s[(d * n_sq + sq) * nh:(d * n_sq + sq + 1) * nh]


def _gdn_call(qkvn, proj, prow, pcol, s0, n_seq, t):
    g = GDN_CHUNKS_PER_STEP
    n_sq = math.gcd(n_seq, GDN_SEQS_PER_STEP)
    ns = t // (D_CHUNK * g)
    rows = D_CHUNK * g
    fwd = lambda i: i
    bwd = lambda i: ns - 1 - i
    x3 = qkvn.reshape(n_seq, t, 768)
    p3 = proj.reshape(n_seq, t, P1)
    xblk = lambda f: pl.BlockSpec((n_sq, rows, 768), lambda b, i: (b, f(i), 0))
    abblk = lambda f: pl.BlockSpec((n_sq, rows, 128), lambda b, i: (b, f(i), C_AB // 128))
    oblk = lambda f: pl.BlockSpec((n_sq, rows, 256), lambda b, i: (b, f(i), 0))
    sblk = pl.BlockSpec((n_sq, 2, D_HEADS, D_DK, D_DV), lambda b, i: (b, 0, 0, 0, 0))
    osd = jax.ShapeDtypeStruct((n_seq, t, 256), F32)
    in_specs = [xblk(fwd), xblk(bwd), abblk(fwd), abblk(bwd), _resident(prow), _resident(pcol)]
    args = [x3, x3, p3, p3, prow.arr, pcol.arr]
    if s0 is not None:
        for st in s0:
            in_specs.append(pl.BlockSpec((n_sq, None, D_HEADS, D_DK, D_DV), lambda b, i, l=st.l: (b, l, 0, 0, 0)))
            args.append(st.arr)
    o_f, o_b, s_fin = pl.pallas_call(
        functools.partial(_gdn_kernel, ns, g, n_sq, s0 is not None),
        grid=(n_seq // n_sq, ns),
        in_specs=in_specs,
        out_specs=[oblk(fwd), oblk(bwd), sblk],
        out_shape=[osd, osd, jax.ShapeDtypeStruct((n_seq, 2, D_HEADS, D_DK, D_DV), F32)],
        scratch_shapes=[pltpu.VMEM((2 * n_sq * D_HEADS, D_DK, D_DV), F32)],
        compiler_params=_cparams(("parallel", "arbitrary")),
        name="gdn",
    )(*args)
    return o_f.reshape(n_seq * t, 256), o_b.reshape(n_seq * t, 256), s_fin


def _merge_kernel(x_ref, mod_ref, gpre_ref, oa_ref, ob_ref, oc_ref, of_ref, obw_ref, z_ref, gnorm_ref,
                  wg_ref, wbr_ref, wo_ref, gpost_ref, o_ref):
    d = D_MODEL
    x = x_ref[...]
    mod = mod_ref[0]
    sh1, sc1, ga1 = mod[:, 0:d], mod[:, d:2 * d], mod[:, 2 * d:3 * d]
    h = (_rmsnorm_rows(x, gpre_ref[...], d) * (1.0 + sc1) + sh1).astype(BF16)
    o = of_ref[...] + obw_ref[...]
    ms = _dot_sel_right(o * o, _group_ones(256, D_DV)) * (1.0 / D_DV)
    od = o * lax.rsqrt(ms + EPS) * gnorm_ref[...] * _silu(z_ref[...])
    acc = None
    for m, br in enumerate((oa_ref[...], ob_ref[...], oc_ref[...], od)):
        gate = jax.nn.sigmoid(_dot(h, wg_ref[:, m * d:(m + 1) * d]))
        term = gate * _dot(br.astype(BF16), wbr_ref[m])
        acc = term if acc is None else acc + term
    mix = _dot(acc.astype(BF16), wo_ref[...])
    o_ref[...] = x + ga1 * _rmsnorm_rows(mix, gpost_ref[...], d)


def _merge_call(x2d, mod3, gpre, oa, ob, oc, of, obw, proj, gnorm, wg, wbr, wo, gpost, seq_len, tm):
    n, d = x2d.shape
    tiles_per_seq = seq_len // tm
    row = lambda w, c=0: pl.BlockSpec((tm, w), lambda i: (i, c))
    full = _resident
    return pl.pallas_call(
        _merge_kernel,
        grid=(n // tm,),
        in_specs=[row(d), _mod_block(mod3, tiles_per_seq), full(gpre),
                  row(256), row(256), row(256), row(256), row(256), row(256, C_Z // 256), full(gnorm),
                  full(wg), full(wbr), full(wo), full(gpost)],
        out_specs=row(d),
        out_shape=jax.ShapeDtypeStruct((n, d), F32),
        compiler_params=_cparams(("parallel",)),
        name="merge",
    )(*_operands([x2d, mod3, gpre, oa, ob, oc, of, obw, proj, gnorm, wg, wbr, wo, gpost]))


def _ffn_kernel(x_ref, mod_ref, gpre_ref, wup_ref, wdown_ref, gpost_ref, o_ref):
    d = D_MODEL
    x = x_ref[...]
    mod = mod_ref[0]
    sh2, sc2, ga2 = mod[:, 3 * d:4 * d], mod[:, 4 * d:5 * d], mod[:, 5 * d:6 * d]
    h = (_rmsnorm_rows(x, gpre_ref[...], d) * (1.0 + sc2) + sh2).astype(BF16)
    up = _dot(h, wup_ref[...])
    act = _silu(up[:, 0:D_FF]) * up[:, D_FF:2 * D_FF]
    f = _dot(act.astype(BF16), wdown_ref[...])
    o_ref[...] = x + ga2 * _rmsnorm_rows(f, gpost_ref[...], d)


def _ffn_call(x2d, mod3, gpre, wup, wdown, gpost, seq_len, tm):
    n, d = x2d.shape
    tiles_per_seq = seq_len // tm
    row = lambda w: pl.BlockSpec((tm, w), lambda i: (i, 0))
    full = _resident
    return pl.pallas_call(
        _ffn_kernel,
        grid=(n // tm,),
        in_specs=[row(d), _mod_block(mod3, tiles_per_seq), full(gpre), full(wup), full(wdown), full(gpost)],
        out_specs=row(d),
        out_shape=jax.ShapeDtypeStruct((n, d), F32),
        compiler_params=_cparams(("parallel",)),
        name="ffn",
    )(*_operands([x2d, mod3, gpre, wup, wdown, gpost]))


def _rope_tables(n_tokens, dim, reps):
    n_rows = n_tokens // GRID_W
    row = jnp.repeat(jnp.arange(n_rows), GRID_W).astype(F32)
    col = jnp.tile(jnp.arange(GRID_W), n_rows).astype(F32)
    nfreq = dim // 4
    inv = ROPE_BASE ** (-jnp.arange(nfreq, dtype=F32) / nfreq)
    ang_r = row[:, None] * inv
    ang_c = col[:, None] * inv
    ang = jnp.concatenate([ang_r, ang_r, ang_c, ang_c], axis=-1)
    cos, sin = jnp.cos(ang), jnp.sin(ang)
    even = ((jnp.arange(dim) // nfreq) % 2 == 0)[None, :]
    sin_next = jnp.where(even, -sin, 0.0)
    sin_prev = jnp.where(even, 0.0, sin)
    return tuple(jnp.tile(a, (1, reps)) for a in (cos, sin_next, sin_prev))


def _rope_tables_mla_wide(n_tokens):
    cos, sin_next, sin_prev = _rope_tables(n_tokens, B_ROPE, 1)
    tail = MLA_HEAD_LANES - B_NOPE - B_ROPE
    wide = lambda a, fill: jnp.pad(a, ((0, 0), (B_NOPE, tail)), constant_values=fill)
    return wide(cos, 1.0), wide(sin_next, 0.0), wide(sin_prev, 0.0)


def _split_cols(w):
    out, acc = [], 0
    for s in IN_SIZES:
        out.append(w[..., acc:acc + s])
        acc += s
    return out


def _pack_w_in_kernel(w_ref, w1_ref, wg_ref):
    w = w_ref[...]
    (a_q, a_k, a_v, b_cq, b_ckv, b_kr, c_in, d_qkv, d_z, d_beta, d_alpha, gate) = _split_cols(w)
    z = lambda n: jnp.zeros((w.shape[0], n), w.dtype)
    k0, k1 = a_k[:, :A_HD], a_k[:, A_HD:]
    v0, v1 = a_v[:, :A_HD], a_v[:, A_HD:]
    w1 = jnp.concatenate([a_q, k0, k0, k1, k1, v0, v0, v1, v1, b_cq, z(256 - B_Q_RANK), b_ckv,
                          b_kr, b_kr, b_kr, b_kr, c_in, d_qkv, d_z, d_beta, d_alpha, z(112)], axis=-1)
    w1_ref[...] = w1.astype(BF16)
    wg_ref[...] = gate.astype(BF16)


def _pack_w_in(w):
    w = w.astype(BF16)
    depth, d, p = w.shape
    n_gate = IN_SIZES[-1]
    tr = 256
    return pl.pallas_call(
        _pack_w_in_kernel,
        grid=(depth, d // tr),
        in_specs=[pl.BlockSpec((None, tr, p), lambda l, i: (l, i, 0))],
        out_specs=[pl.BlockSpec((None, tr, P1), lambda l, i: (l, i, 0)),
                   pl.BlockSpec((None, tr, n_gate), lambda l, i: (l, i, 0))],
        out_shape=[jax.ShapeDtypeStruct((depth, d, P1), BF16), jax.ShapeDtypeStruct((depth, d, n_gate), BF16)],
        compiler_params=_cparams(("parallel", "parallel")),
        name="pack_w_in",
    )(w)


def _expand_kv_heads(a):
    return jnp.repeat(a, A_HEADS // A_KV_HEADS, axis=-2).reshape(a.shape[:-2] + (A_HEADS * A_HD,))


def _stacked_weights(g_pre1, g_post1, g_pre2, g_post2, w_in, a_sink, b_g_cq, b_g_ckv, b_w_uq, b_w_ukv,
                     c_w_pool, c_scale, d_conv, d_a_log, d_dt_bias, d_g_norm, w_br, w_o, w_up, w_down):
    depth = w_in.shape[0]
    w1, wg = _pack_w_in(w_in)
    wuq = b_w_uq.reshape(depth, B_Q_RANK, B_HEADS, B_NOPE + B_ROPE)
    wuq = jnp.pad(wuq, ((0, 0), (0, 256 - B_Q_RANK), (0, 0), (0, MLA_HEAD_LANES - B_NOPE - B_ROPE)))
    wuq = wuq.reshape(depth, 256, MLA_WIDE).astype(BF16)
    wukv = b_w_ukv.reshape(depth, B_KV_RANK, B_HEADS, B_NOPE + B_VD)
    wk = jnp.pad(wukv[..., :B_NOPE], ((0, 0), (0, 0), (0, 0), (0, MLA_HEAD_LANES - B_NOPE)))
    wukv = jnp.concatenate([wk.reshape(depth, B_KV_RANK, MLA_WIDE),
                            wukv[..., B_NOPE:].reshape(depth, B_KV_RANK, B_HEADS * B_VD)], -1).astype(BF16)
    same = jnp.eye(C_GROUPS, dtype=F32)[None, :, None, :, None]
    wpool = (c_w_pool[:, :, :, None, :] * same).reshape(depth, C_GROUPS * C_GW, C_GROUPS * C_GW).astype(BF16)
    gdn =jnp.stack([d_a_log.reshape(depth, 2 * D_HEADS), d_dt_bias.reshape(depth, 2 * D_HEADS)], axis=1)
    prow = jnp.pad(gdn, ((0, 0), (0, 0), (8, LANES - 16)))
    pcol = jnp.pad(gdn.transpose(0, 2, 1), ((0, 0), (8, 0), (0, 0)))
    row = lambda a: a[:, None, :]
    return dict(
        w1=w1, wg=wg, wuq=wuq, wukv=wukv,
        gpre1=row(g_pre1), gpost1=row(g_post1), gpre2=row(g_pre2), gpost2=row(g_post2),
        gcq=row(jnp.pad(b_g_cq, ((0, 0), (0, 256 - B_Q_RANK)))), gckv=row(b_g_ckv),
        sink=a_sink, wpool=wpool, cscale=row(c_scale),
        convw=jnp.pad(d_conv, ((0, 0), (0, SUBLANES - D_CONV), (0, 0))),
        prow=prow, pcol=pcol, gnorm=row(jnp.tile(d_g_norm, (1, D_HEADS))),
        wbr=w_br.astype(BF16), wo=w_o.astype(BF16), wup=w_up.astype(BF16), wdown=w_down.astype(BF16))


def _trunk_layer(x2d, mod3, lw, n_seq, t, ctx, tables=None):
    latent = ctx is not None
    n = x2d.shape[0]
    tm = min(512, t) if latent else 256
    tm_dense = min(512, n)
    if latent:
        proj, qw, kwr, vb, aqr, akr, qwr = _inproj_call(
            x2d, mod3, lw["gpre1"], lw["w1"], lw["gcq"], lw["wuq"], lw["gckv"], lw["wukv"], tables, t, tm_dense)
        kctx, vctx, ctx_ckv, ckr_wide, s0f, s0b = ctx
        out_a = _win_attn_call(lw["sink"], aqr, akr, proj, kctx, vctx, n_seq, t)
        out_b = _mla_lat_call(qw, qwr, kwr, vb, ctx_ckv, ckr_wide, lw["wukv"], n_seq, t, min(MLA_TQ, t))
        s0 = (s0f, s0b)
    else:
        proj, qw, kw, vb, ckvn, ak, av, kr = _inproj_call(
            x2d, mod3, lw["gpre1"], lw["w1"], lw["gcq"], lw["wuq"], lw["gckv"], lw["wukv"], None, t, tm_dense)
        out_a, out_b = _attn_ctx_call(lw["sink"], proj, qw, kw, vb, n_seq, t)
        s0 = None
    out_c, qkvn = _local_call(proj, lw["convw"], lw["wpool"], lw["cscale"], t, tm)
    o_f, o_b, s_fin = _gdn_call(qkvn, proj, lw["prow"], lw["pcol"], s0, n_seq, t)
    x1 = _merge_call(x2d, mod3, lw["gpre1"], out_a, out_b, out_c, o_f, o_b, proj, lw["gnorm"],
                     lw["wg"], lw["wbr"], lw["wo"], lw["gpost1"], t, tm_dense)
    x2 = _ffn_call(x1, mod3, lw["gpre2"], lw["wup"], lw["wdown"], lw["gpost2"], t, tm_dense)
    new_ctx = None
    if not latent:
        new_ctx = (ak.reshape(n_seq, t, A_KV_HEADS, A_HD), av.reshape(n_seq, t, A_KV_HEADS, A_HD),
                   ckvn.reshape(n_seq, t, B_KV_RANK), kr.reshape(n_seq, t, B_ROPE), s_fin[:, 0], s_fin[:, 1])
    return x2, new_ctx


def kernel(x_prompt, x_sample, cache_a_k, cache_a_v, cache_b_ckv, cache_b_krope, state_d_fwd, state_d_bwd, c, c_ctx, w_mod, b_mod, g_pre1, g_post1, g_pre2, g_post2, w_in, a_sink, b_g_cq, b_g_ckv, b_w_uq, b_w_ukv, c_w_pool, c_scale, d_conv, d_a_log, d_dt_bias, d_g_norm, w_br, w_o, w_up, w_down):
    depth = w_mod.shape[0]
    bp, tp, d = x_prompt.shape
    bs, ts, _ = x_sample.shape
    assert bs + 1 <= SUBLANES
    cond8 = jnp.zeros((SUBLANES, d), F32).at[0].set(c_ctx).at[1:1 + bs].set(c)
    mod_all = _mod_call(cond8, w_mod, b_mod).reshape(depth, SUBLANES, 1, 6 * d)
    yp = x_prompt.reshape(bp * tp, d)
    ys = x_sample.reshape(bs * ts, d)
    new = [[] for _ in range(6)]
    tables = _rope_tables(ts, A_HD, LANES // A_HD) + _rope_tables_mla_wide(ts)
    kctx_all, vctx_all = _expand_kv_heads(cache_a_k), _expand_kv_heads(cache_a_v)
    ckr_wide_all = jnp.tile(jnp.pad(cache_b_krope, ((0, 0), (0, 0), (0, 0),
                                                    (B_NOPE, MLA_HEAD_LANES - B_NOPE - B_ROPE))), (1, 1, 1, B_HEADS))
    stacked = _stacked_weights(g_pre1, g_post1, g_pre2, g_post2, w_in, a_sink, b_g_cq, b_g_ckv, b_w_uq, b_w_ukv,
                               c_w_pool, c_scale, d_conv, d_a_log, d_dt_bias, d_g_norm, w_br, w_o, w_up, w_down)
    for l in range(depth):
        lw = {k: _Layer(v, l) for k, v in stacked.items()}
        mod_ctx = _Mod(mod_all, l, 0, 1)
        mod_lat = _Mod(mod_all, l, 1, bs)
        yp, nctx = _trunk_layer(yp, mod_ctx, lw, bp, tp, None)
        for acc, val in zip(new, nctx):
            acc.append(val)
        ctx = tuple(_Layer(a, l) for a in (kctx_all, vctx_all, cache_b_ckv, ckr_wide_all, state_d_fwd, state_d_bwd))
        ys, _ = _trunk_layer(ys, mod_lat, lw, bs, ts, ctx, tables)
    return (yp.reshape(bp, tp, d), ys.reshape(bs, ts, d)) + tuple(jnp.stack(v, axis=1) for v in new)
```

```python
import functools
import math
from typing import NamedTuple

import jax
import jax.numpy as jnp
from jax import lax
from jax.experimental import pallas as pl
from jax.experimental.pallas import tpu as pltpu

F32 = jnp.float32
BF16 = jnp.bfloat16

D_MODEL = 1024
GRID_W = 64
ROPE_BASE = 10000.0
EPS = 1e-6
NEG_INF = -1e30
Q_BLOCK = 128
N_BRANCH = 4
BRANCH_W = 256
A_HEADS = 4
A_KV_HEADS = 2
A_HD = 64
A_WINDOW = 128
A_SCALE = A_HD ** -0.5
B_HEADS = 4
B_Q_RANK = 192
B_KV_RANK = 128
B_NOPE = 64
B_ROPE = 32
B_VD = 64
B_SCALE = (B_NOPE + B_ROPE) ** -0.5
C_GROUPS = 4
C_GW = 64
C_WINDOWS = (2, 4, 8, 16)
D_HEADS = 4
D_DK = 64
D_DV = 64
D_CONV = 5
D_CHUNK = 64
FF_RAW = -(-8 * D_MODEL // 3)
D_FF = -(-FF_RAW // 256) * 256
IN_SIZES = (A_HEADS * A_HD, A_KV_HEADS * A_HD, A_KV_HEADS * A_HD, B_Q_RANK, B_KV_RANK, B_ROPE,
            C_GROUPS * C_GW, D_HEADS * (2 * D_DK + D_DV), D_HEADS * D_DV, 2 * D_HEADS, 2 * D_HEADS,
            N_BRANCH * D_MODEL)

LANES = 128
SUBLANES = 8
VMEM_LIMIT_BYTES = 56 * 1024 * 1024
HALO = SUBLANES
GDN_SEQS_PER_STEP = 2
GDN_CHUNKS_PER_STEP = 4
MLA_HEAD_LANES = LANES
MLA_WIDE = B_HEADS * MLA_HEAD_LANES
LOG2E = math.log2(math.e)
DENSE_ROW_GROUPS = 2
MLA_TQ = 512
MLA_ROW_GROUP = 32
MLA_COL_BLOCK = 256
WIN_ATTN_TQ = 512

C_AQ = 0
C_AK = 256
C_AV = 512
C_CQ = 768
C_CKV = 1024
C_KR = 1152
C_CIN = 1280
C_QKV = 1536
C_Z = 2304
C_AB = 2560
P1 = 2688


def _cparams(sem):
    return pltpu.CompilerParams(dimension_semantics=sem, vmem_limit_bytes=VMEM_LIMIT_BYTES)


class _Layer(NamedTuple):
    arr: jax.Array
    l: int


def _resident(p):
    shape = p.arr.shape[1:]
    return pl.BlockSpec((None,) + shape, lambda *_: (p.l,) + (0,) * len(shape), pipeline_mode=pl.Buffered(1))


class _Mod(NamedTuple):
    arr: jax.Array
    l: int
    first: int
    count: int


def _mod_block(m, tiles_per_seq):
    width = m.arr.shape[-1]
    if m.count == 1:
        return pl.BlockSpec((None, 1, 1, width), lambda i: (m.l, m.first, 0, 0))
    return pl.BlockSpec((None, 1, 1, width), lambda i: (m.l, m.first + i // tiles_per_seq, 0, 0))


def _cache_block(p):
    return pl.BlockSpec((1, None) + p.arr.shape[2:], lambda b, i: (b, p.l, 0, 0))


def _operands(args):
    return [a.arr if isinstance(a, (_Layer, _Mod)) else a for a in args]


def _dot(a, b):
    return jnp.dot(a, b, preferred_element_type=F32)


def _dot_nt(a, b):
    return lax.dot_general(a, b, (((1,), (1,)), ((), ())), preferred_element_type=F32)


def _split3(x):
    hi = x.astype(BF16)
    r1 = x - hi.astype(F32)
    mid = r1.astype(BF16)
    lo = (r1 - mid.astype(F32)).astype(BF16)
    return hi, mid, lo


def _dot_sel_right(x, sel):
    hi, mid, lo = _split3(x)
    return _dot(hi, sel) + _dot(mid, sel) + _dot(lo, sel)


def _silu(x):
    return x * jax.nn.sigmoid(x)


def _softplus(x):
    return jnp.maximum(x, 0.0) + jnp.log1p(jnp.exp(-jnp.abs(x)))


def _head_mask(width, n_lanes, h):
    lane = lax.broadcasted_iota(jnp.int32, (1, n_lanes), 1)
    return (lane // width) == h


def _group_ones(n, width):
    r = lax.broadcasted_iota(jnp.int32, (n, n), 0) // width
    c = lax.broadcasted_iota(jnp.int32, (n, n), 1) // width
    return jnp.where(r == c, 1.0, 0.0).astype(BF16)


def _mod_kernel(c_ref, w_ref, b_ref, o_ref):
    s = _silu(c_ref[...])
    o_ref[0] = _dot(s.astype(BF16), w_ref[0].astype(BF16)) + b_ref[0]


def _mod_call(cond8, w_mod, b_mod):
    depth, d, n = w_mod.shape
    tn = 1536
    return pl.pallas_call(
        _mod_kernel,
        grid=(depth, n // tn),
        in_specs=[pl.BlockSpec((SUBLANES, d), lambda l, j: (0, 0)),
                  pl.BlockSpec((1, d, tn), lambda l, j: (l, 0, j)),
                  pl.BlockSpec((1, 1, tn), lambda l, j: (l, 0, j))],
        out_specs=pl.BlockSpec((1, SUBLANES, tn), lambda l, j: (l, 0, j)),
        out_shape=jax.ShapeDtypeStruct((depth, SUBLANES, n), F32),
        compiler_params=_cparams(("parallel", "parallel")),
        name="mod",
    )(cond8, w_mod, b_mod.reshape(depth, 1, n))


def _rmsnorm_rows(x, g, n):
    ms = jnp.sum(x * x, axis=-1, keepdims=True) * (1.0 / n)
    return x * lax.rsqrt(ms + EPS) * g


def _rope(x, cos, sin_next, sin_prev, quarter):
    n = x.shape[-1]
    return x * cos + pltpu.roll(x, n - quarter, 1) * sin_next + pltpu.roll(x, quarter, 1) * sin_prev


def _mla_wide_kr(kr_rep):
    lane = lax.broadcasted_iota(jnp.int32, (1, MLA_HEAD_LANES), 1)
    keep = (lane >= B_NOPE) & (lane < B_NOPE + B_ROPE)
    return jnp.concatenate([jnp.where(keep, kr_rep, 0.0)] * B_HEADS, axis=1)


def _inproj_kernel(latent, *refs):
    if latent:
        (x_ref, mod_ref, gpre_ref, w1_ref, gcq_ref, wuq_ref, gckv_ref, wukv_ref,
         ca_ref, sna_ref, spa_ref, cb_ref, snb_ref, spb_ref,
         proj_ref, qw_ref, kw_ref, vb_ref, aqr_ref, akr_ref, qwr_ref) = refs
    else:
        (x_ref, mod_ref, gpre_ref, w1_ref, gcq_ref, wuq_ref, gckv_ref, wukv_ref,
         proj_ref, qw_ref, kw_ref, vb_ref, ckv_ref, ak_ref, av_ref, kr_ref) = refs
    d = D_MODEL
    x = x_ref[...]
    mod = mod_ref[0]
    sh1 = mod[:, 0:d]
    sc1 = mod[:, d:2 * d]
    h = _rmsnorm_rows(x, gpre_ref[...], d) * (1.0 + sc1) + sh1
    proj = _dot(h.astype(BF16), w1_ref[...])
    proj_ref[...] = proj
    cqn = _rmsnorm_rows(proj[:, C_CQ:C_CQ + 256], gcq_ref[...], B_Q_RANK)
    qw = _dot(cqn.astype(BF16), wuq_ref[...]) * (B_SCALE * LOG2E)
    qw_ref[...] = qw.astype(BF16)
    ckvn = _rmsnorm_rows(proj[:, C_CKV:C_CKV + B_KV_RANK], gckv_ref[...], B_KV_RANK)
    if not latent:
        unexpand = lambda c: jnp.concatenate([proj[:, c:c + A_HD], proj[:, c + 2 * A_HD:c + 3 * A_HD]], axis=1)
        ckv_ref[...] = ckvn
        ak_ref[...] = unexpand(C_AK)
        av_ref[...] = unexpand(C_AV)
        kr_ref[...] = proj[:, C_KR:C_KR + B_ROPE]
    knv =_dot(ckvn.astype(BF16), wukv_ref[...])
    vb_ref[...] = knv[:, MLA_WIDE:].astype(BF16)
    kr_wide = _mla_wide_kr(proj[:, C_KR:C_KR + 128])
    if latent:
        ca, sna, spa = (jnp.concatenate([r[...]] * (256 // LANES), axis=1) for r in (ca_ref, sna_ref, spa_ref))
        cb, snb, spb = (jnp.concatenate([r[...]] * B_HEADS, axis=1) for r in (cb_ref, snb_ref, spb_ref))
        aqr_ref[...] = _rope(proj[:, C_AQ:C_AQ + 256], ca, sna, spa, A_HD // 4).astype(BF16)
        akr_ref[...] = _rope(proj[:, C_AK:C_AK + 256], ca, sna, spa, A_HD // 4).astype(BF16)
        qwr_ref[...] = _rope(qw, cb, snb, spb, B_ROPE // 4).astype(BF16)
        kr_wide = _rope(kr_wide, cb, snb, spb, B_ROPE // 4)
    kw_ref[...] = (knv[:, :MLA_WIDE] + kr_wide).astype(BF16)


def _inproj_call(x2d, mod3, gpre, w1, gcq, wuq, gckv, wukv, tables, seq_len, tm):
    n, d = x2d.shape
    latent = tables is not None
    tiles_per_seq = seq_len // tm if latent else 1
    row = lambda w, dt=F32: (pl.BlockSpec((tm, w), lambda i: (i, 0)), jax.ShapeDtypeStruct((n, w), dt))
    params = [gpre, w1, gcq, wuq, gckv, wukv]
    in_specs = [pl.BlockSpec((tm, d), lambda i: (i, 0)),
                _mod_block(mod3, seq_len // tm)] + [_resident(p) for p in params]
    args = [x2d, mod3] + params
    outs = [row(P1), row(MLA_WIDE, BF16), row(MLA_WIDE, BF16), row(256, BF16)]
    if latent:
        for t in tables:
            in_specs.append(pl.BlockSpec((tm, t.shape[1]), lambda i: (i % tiles_per_seq, 0)))
            args.append(t)
        outs += [row(256, BF16), row(256, BF16), row(MLA_WIDE, BF16)]
    else:
        outs += [row(B_KV_RANK), row(A_KV_HEADS * A_HD), row(A_KV_HEADS * A_HD), row(B_ROPE)]
    return pl.pallas_call(
        functools.partial(_inproj_kernel, latent),
        grid=(n // tm,),
        in_specs=in_specs,
        out_specs=[o[0] for o in outs],
        out_shape=[o[1] for o in outs],
        compiler_params=_cparams(("parallel",)),
        name="inproj_lat" if latent else "inproj_ctx",
    )(*_operands(args))


def _softmax_pv(scores, values, scale, sink, hm):
    c1 = None if scale is None else scale * LOG2E
    m = jnp.max(scores[0], axis=-1, keepdims=True)
    for s in scores[1:]:
        m = jnp.maximum(m, jnp.max(s, axis=-1, keepdims=True))
    m2 = m if c1 is None else m * c1
    if sink is not None:
        m2 = jnp.maximum(m2, sink * LOG2E)
    den = None
    pv = None
    for s, v in zip(scores, values):
        p = jnp.exp2((s if c1 is None else s * c1) - m2)
        ps = jnp.sum(p, axis=-1, keepdims=True)
        den = ps if den is None else den + ps
        t = _dot(p.astype(BF16), v)
        pv = t if pv is None else pv + t
    if sink is not None:
        den = den + jnp.exp2(sink * LOG2E - m2)
    return jnp.where(hm, pv / den, 0.0)


def _attn_ctx_kernel(layer, sink_ref, q_ref, k_ref, v_ref, qw_ref, kw_ref, vb_ref, oa_ref, ob_ref):
    q = q_ref[...]
    k = k_ref[...].astype(BF16)
    v = v_ref[...].astype(BF16)
    acc = jnp.zeros(q.shape, F32)
    for h in range(A_HEADS):
        hm = _head_mask(A_HD, 256, h)
        s = _dot_nt(jnp.where(hm, q, 0.0).astype(BF16), k)
        acc = acc + _softmax_pv([s], [v], A_SCALE, sink_ref[layer, h], hm)
    oa_ref[...] = acc
    vb = vb_ref[...]
    acc = jnp.zeros((q.shape[0], B_HEADS * B_VD), F32)
    for h in range(B_HEADS):
        hs = slice(h * MLA_HEAD_LANES, (h + 1) * MLA_HEAD_LANES)
        s = _dot_nt(qw_ref[:, hs], kw_ref[:, hs])
        acc = acc + _softmax_pv([s], [vb], None, None, _head_mask(B_VD, 256, h))
    ob_ref[...] = acc


def _attn_ctx_call(sink, proj, qw, kw, vb, n_seq, t):
    n = proj.shape[0]
    col = lambda w, j: pl.BlockSpec((t, w), lambda b: (b, j))
    out = (pl.BlockSpec((t, 256), lambda b: (b, 0)), jax.ShapeDtypeStruct((n, 256), F32))
    return pl.pallas_call(
        functools.partial(_attn_ctx_kernel, sink.l),
        grid=(n_seq,),
        in_specs=[pl.BlockSpec(memory_space=pltpu.SMEM),
                  col(256, C_AQ // 256), col(256, C_AK // 256), col(256, C_AV // 256),
                  col(MLA_WIDE, 0), col(MLA_WIDE, 0), col(256, 0)],
        out_specs=[out[0], out[0]],
        out_shape=[out[1], out[1]],
        compiler_params=_cparams(("parallel",)),
        name="attn_ctx",
    )(sink.arr, proj, proj, proj, qw, kw, vb)


def _win_attn_kernel(t, layer, sink_ref, qr_ref, q_ref, kp_ref, kc_ref, kn_ref, vp_ref, vc_ref, vn_ref,
                     kctx_ref, vctx_ref, o_ref):
    i = pl.program_id(1)
    qr = qr_ref[...]
    q = q_ref[...]
    tq = q.shape[0]
    kl = jnp.concatenate([kp_ref[...], kc_ref[...], kn_ref[...]], axis=0)
    vl = jnp.concatenate([vp_ref[...], vc_ref[...], vn_ref[...]], axis=0).astype(BF16)
    kctx = kctx_ref[0].astype(BF16)
    vctx = vctx_ref[0].astype(BF16)
    q_pos = i * tq + lax.broadcasted_iota(jnp.int32, (tq, tq + 2 * A_WINDOW), 0)
    k_pos = i * tq - A_WINDOW + lax.broadcasted_iota(jnp.int32, (tq, tq + 2 * A_WINDOW), 1)
    valid = (jnp.abs(q_pos - k_pos) <= A_WINDOW) & (k_pos >= 0) & (k_pos < t)
    acc = jnp.zeros(q.shape, F32)
    for h in range(A_HEADS):
        hm = _head_mask(A_HD, 256, h)
        s_loc = jnp.where(valid, _dot_nt(jnp.where(hm, qr, 0.0).astype(BF16), kl), NEG_INF)
        s_ctx = _dot_nt(jnp.where(hm, q, 0.0).astype(BF16), kctx)
        acc = acc + _softmax_pv([s_loc, s_ctx], [vl, vctx], A_SCALE, sink_ref[layer, h], hm)
    o_ref[...] = acc


def _win_attn_call(sink, aqr, akr, proj, kctx, vctx, n_seq, t):
    n = proj.shape[0]
    tq = min(WIN_ATTN_TQ, t)
    nq = t // tq
    r = tq // A_WINDOW
    nh = t // A_WINDOW
    cur = lambda j: pl.BlockSpec((tq, 256), lambda b, i: (b * nq + i, j))
    prev = lambda j: pl.BlockSpec((A_WINDOW, 256), lambda b, i: (b * nh + jnp.maximum(i * r - 1, 0), j))
    nxt = lambda j: pl.BlockSpec((A_WINDOW, 256), lambda b, i: (b * nh + jnp.minimum((i + 1) * r, nh - 1), j))
    ctx = _cache_block(kctx)
    return pl.pallas_call(
        functools.partial(_win_attn_kernel, t, sink.l),
        grid=(n_seq, nq),
        in_specs=[pl.BlockSpec(memory_space=pltpu.SMEM),
                  cur(0), cur(C_AQ // 256),
                  prev(0), cur(0), nxt(0),
                  prev(C_AV // 256), cur(C_AV // 256), nxt(C_AV // 256),
                  ctx, ctx],
        out_specs=pl.BlockSpec((tq, 256), lambda b, i: (b * nq + i, 0)),
        out_shape=jax.ShapeDtypeStruct((n, 256), F32),
        compiler_params=_cparams(("parallel", "parallel")),
        name="win_attn",
    )(sink.arr, aqr, proj, akr, akr, akr, proj, proj, proj, kctx.arr, vctx.arr)


def _mla_lat_kernel(qw_ref, qwr_ref, kwr_ref, vb_ref, cckv_ref, ckr_ref, wukv_ref, o_ref, s_ref, p_ref, den_ref):
    kvc = _dot(cckv_ref[0].astype(BF16), wukv_ref[...])
    kwc = (kvc[:, :MLA_WIDE] + ckr_ref[0]).astype(BF16)
    vc = kvc[:, MLA_WIDE:].astype(BF16)
    tq = qw_ref.shape[0]
    t = kwr_ref.shape[0]
    n_keys = s_ref.shape[2]
    cb = MLA_COL_BLOCK
    rg = MLA_ROW_GROUP

    def scores_into(slot, h):
        hs = slice(h * MLA_HEAD_LANES, (h + 1) * MLA_HEAD_LANES)
        s_ref[slot, :, 0:t] = _dot_nt(qwr_ref[:, hs], kwr_ref[:, hs])
        s_ref[slot, :, t:n_keys] = _dot_nt(qw_ref[:, hs], kwc[:, hs])

    scores_into(0, 0)
    acc = jnp.zeros((tq, B_HEADS * B_VD), F32)
    for h in range(B_HEADS):
        slot = h % 2
        if h + 1 < B_HEADS:
            scores_into(1 - slot, h + 1)
        for r0 in range(0, tq, rg):
            rows = slice(r0, r0 + rg)
            m = s_ref[slot, rows, 0:cb]
            for c0 in range(cb, n_keys, cb):
                m = jnp.maximum(m, s_ref[slot, rows, c0:c0 + cb])
            m2 = jnp.max(m, axis=-1, keepdims=True)
            part = None
            for c0 in range(0, n_keys, cb):
                p = jnp.exp2(s_ref[slot, rows, c0:c0 + cb] - m2)
                part = p if part is None else part + p
                p_ref[rows, c0:c0 + cb] = p.astype(BF16)
            den_ref[rows, :] = jnp.broadcast_to(jnp.sum(part, axis=-1, keepdims=True), (rg, LANES))
        pv = _dot(p_ref[:, 0:t], vb_ref[...]) + _dot(p_ref[:, t:n_keys], vc)
        acc = acc + jnp.where(_head_mask(B_VD, 256, h), pv / den_ref[:, 0:1], 0.0)
    o_ref[...] = acc


def _mla_lat_call(qw, qwr, kwr, vb, cckv, ckr_wide, wukv, n_seq, t, tq):
    n = qw.shape[0]
    nq = t // tq
    qblk = lambda w: pl.BlockSpec((tq, w), lambda b, i: (b * nq + i, 0))
    seq = lambda w: pl.BlockSpec((t, w), lambda b, i: (b, 0))
    ctx = _cache_block
    return pl.pallas_call(
        _mla_lat_kernel,
        grid=(n_seq, nq),
        in_specs=[qblk(MLA_WIDE), qblk(MLA_WIDE), seq(MLA_WIDE), seq(256), ctx(cckv), ctx(ckr_wide),
                  _resident(wukv)],
        out_specs=pl.BlockSpec((tq, 256), lambda b, i: (b * nq + i, 0)),
        out_shape=jax.ShapeDtypeStruct((n, 256), F32),
        scratch_shapes=[pltpu.VMEM((2, tq, t + cckv.arr.shape[2]), F32), pltpu.VMEM((tq, t + cckv.arr.shape[2]), BF16),
                        pltpu.VMEM((tq, LANES), F32)],
        compiler_params=_cparams(("parallel", "parallel")),
        name="mla_lat",
    )(qw, qwr, kwr, vb, cckv.arr, ckr_wide.arr, wukv.arr)


def _with_halo(prev_ref, cur_ref, next_ref, has_prev, has_next):
    prev = jnp.where(has_prev, prev_ref[...], 0.0)
    nxt = jnp.where(has_next, next_ref[...], 0.0)
    return jnp.concatenate([prev, cur_ref[...], nxt], axis=0)


def _shift_rows(x, k):
    n = x.shape[0]
    return pltpu.roll(x, (-k) % n, 0)


def _local_kernel(t, tm, cp_ref, cc_ref, cn_ref, qp_ref, qc_ref, qn_ref, convw_ref, wpool_ref, cscale_ref,
                  oc_ref, qkv_ref):
    i = pl.program_id(0)
    tiles_per_seq = t // tm
    j = i % tiles_per_seq
    has_prev = j > 0
    has_next = j < tiles_per_seq - 1
    x = _with_halo(cp_ref, cc_ref, cn_ref, has_prev, has_next)
    p2 = x + _shift_rows(x, -1)
    p4 = _shift_rows(p2, 1) + _shift_rows(p2, -1)
    p8 = _shift_rows(p4, 2) + _shift_rows(p4, -2)
    p16 = _shift_rows(p8, 4) + _shift_rows(p8, -4)
    grp = lax.broadcasted_iota(jnp.int32, (1, 256), 1) // C_GW
    win = jnp.where(grp == 0, p2, jnp.where(grp == 1, p4, jnp.where(grp == 2, p8, p16)))[HALO:HALO + tm]
    pos = j * tm + lax.broadcasted_iota(jnp.int32, (tm, 256), 0)
    half = jnp.where(grp == 0, 1, jnp.where(grp == 1, 2, jnp.where(grp == 2, 4, 8)))
    cnt = (jnp.minimum(pos + half, t) - jnp.maximum(pos - half, 0)).astype(F32)
    y = win / cnt - cc_ref[...]
    oc_ref[...] = _dot(y.astype(BF16), wpool_ref[...]) * cscale_ref[...]
    xq = _with_halo(qp_ref, qc_ref, qn_ref, has_prev, has_next)
    w = convw_ref[...]
    pad = D_CONV // 2
    acc = None
    for tap in range(D_CONV):
        term = _shift_rows(xq, tap - pad) * w[tap:tap + 1, :]
        acc = term if acc is None else acc + term
    u = _silu(acc[HALO:HALO + tm])
    ones = _group_ones(256, D_DK)
    q = u[:, 0:256]
    k = u[:, 256:512]
    q = q * lax.rsqrt(_dot_sel_right(q * q, ones) + EPS) * (D_DK ** -0.5)
    k = k * lax.rsqrt(_dot_sel_right(k * k, ones) + EPS)
    qkv_ref[:, 0:256] = q
    qkv_ref[:, 256:512] = k
    qkv_ref[:, 512:768] = u[:, 512:768]


def _local_call(proj, convw8, wpool_bd, cscale, t, tm):
    n = proj.shape[0]
    hb = tm // HALO
    nblk = n // HALO
    cur = lambda w, c: pl.BlockSpec((tm, w), lambda i: (i, c))
    prev = lambda w, c: pl.BlockSpec((HALO, w), lambda i: (jnp.maximum(i * hb - 1, 0), c))
    nxt = lambda w, c: pl.BlockSpec((HALO, w), lambda i: (jnp.minimum((i + 1) * hb, nblk - 1), c))
    full = _resident
    cc, qc = C_CIN // 256, C_QKV // 768
    return pl.pallas_call(
        functools.partial(_local_kernel, t, tm),
        grid=(n // tm,),
        in_specs=[prev(256, cc), cur(256, cc), nxt(256, cc), prev(768, qc), cur(768, qc), nxt(768, qc),
                  full(convw8), full(wpool_bd), full(cscale)],
        out_specs=[pl.BlockSpec((tm, 256), lambda i: (i, 0)), pl.BlockSpec((tm, 768), lambda i: (i, 0))],
        out_shape=[jax.ShapeDtypeStruct((n, 256), F32), jax.ShapeDtypeStruct((n, 768), F32)],
        compiler_params=_cparams(("parallel",)),
        name="local",
    )(proj, proj, proj, proj, proj, proj, convw8.arr, wpool_bd.arr, cscale.arr)


def _gdn_kernel(n_steps, g_chunks, n_sq, has_s0, xf_ref, xb_ref, abf_ref, abb_ref, prow_ref, pcol_ref, *refs):
    if has_s0:
        s0f_ref, s0b_ref, of_ref, ob_ref, sfin_ref, s_ref = refs
    else:
        of_ref, ob_ref, sfin_ref, s_ref = refs
    i = pl.program_id(1)
    c = D_CHUNK
    nh = D_HEADS

    @pl.when(i == 0)
    def _():
        if has_s0:
            for sq in range(n_sq):
                s_ref[sq * nh:(sq + 1) * nh] = s0f_ref[sq]
                s_ref[(n_sq + sq) * nh:(n_sq + sq + 1) * nh] = s0b_ref[sq]
        else:
            s_ref[...] = jnp.zeros(s_ref.shape, F32)

    r = lax.broadcasted_iota(jnp.int32, (c, c), 0)
    cidx = lax.broadcasted_iota(jnp.int32, (c, c), 1)
    lower = jnp.where(cidx <= r, 1.0, 0.0).astype(BF16)
    upper = jnp.where(cidx >= r, 1.0, 0.0).astype(BF16)
    prow = prow_ref[...]
    pcol = pcol_ref[...]

    eye = jnp.where(cidx == r, 1.0, 0.0).astype(BF16)
    sel_lower = jnp.concatenate([lower] * 3, axis=1)
    sel_upper = jnp.concatenate([upper] * 3, axis=1)
    pick = jnp.where(lax.broadcasted_iota(jnp.int32, (16, LANES), 0) == lax.broadcasted_iota(jnp.int32, (16, LANES), 1),
                     1.0, 0.0).astype(BF16)
    sel_rows = jnp.concatenate([pick] * 3, axis=1)
    sel_lower_t = jnp.concatenate([lower] * 3, axis=0)
    sel_upper_t = jnp.concatenate([upper] * 3, axis=0)

    bdot = lambda a, b: jnp.einsum('bik,bkj->bij', a, b, preferred_element_type=F32)
    bdot_nt = lambda a, b: jnp.einsum('bik,bjk->bij', a, b, preferred_element_type=F32)

    def prepare(steps):
        nb = 2 * n_sq * len(steps) * nh
        is_fwd = lax.broadcasted_iota(jnp.int32, (nb, c, c), 0) < nb // 2
        r3 = lax.broadcasted_iota(jnp.int32, (nb, c, c), 1)
        c3 = lax.broadcasted_iota(jnp.int32, (nb, c, c), 2)
        ahead = jnp.where(is_fwd, r3 - c3, c3 - r3)
        incl = ahead >= 0
        strict = ahead > 0
        q_l, k_l, v_l, b_l, gcol_l, grow_l = [], [], [], [], [], []
        for d, sq, step in [(d, sq, step) for d in range(2) for sq in range(n_sq) for step in steps]:
            x_ref, ab_ref = ((xf_ref, abf_ref), (xb_ref, abb_ref))[d]
            gi = step if d == 0 else g_chunks - 1 - step
            rows = slice(gi * c, (gi + 1) * c)
            x = x_ref[sq, rows, :]
            ab = ab_ref[sq, rows, :]
            abr = _dot_nt(sel_rows, jnp.concatenate(_split3(ab), axis=1))
            beta_c = jax.nn.sigmoid(ab)
            g_c = -jnp.exp(prow[0:1, :]) * _softplus(ab + prow[1:2, :])
            g_r = -jnp.exp(pcol[:, 0:1]) * _softplus(abr + pcol[:, 1:2])
            if d == 0:
                gc_c = _dot(sel_lower, jnp.concatenate(_split3(g_c), axis=0))
                gc_r = _dot(jnp.concatenate(_split3(g_r), axis=1), sel_upper_t)
            else:
                gc_c = _dot(sel_upper, jnp.concatenate(_split3(g_c), axis=0))
                gc_r = _dot(jnp.concatenate(_split3(g_r), axis=1), sel_lower_t)
            for h in range(nh):
                hs = slice(h * D_DK, (h + 1) * D_DK)
                lane = nh * d + h
                q_l.append(x[:, 0:256][:, hs])
                k_l.append(x[:, 256:512][:, hs])
                v_l.append(x[:, 512:768][:, hs])
                b_l.append(beta_c[:, lane:lane + 1])
                gcol_l.append(gc_c[:, 8 + lane:9 + lane])
                grow_l.append(gc_r[8 + lane:9 + lane, :])
        qh, kh, vh = jnp.stack(q_l), jnp.stack(k_l), jnp.stack(v_l)
        bcol, gcol, grow = jnp.stack(b_l), jnp.stack(gcol_l), jnp.stack(grow_l)
        glast = jnp.concatenate([gcol[:nb // 2, c - 1:c, :], gcol[nb // 2:, 0:1, :]], axis=0)
        decay = jnp.exp(jnp.where(incl, gcol - grow, NEG_INF))
        kb = kh * bcol
        kq = bdot_nt(jnp.concatenate([kb, qh], axis=1).astype(BF16), kh.astype(BF16))
        p = -jnp.where(strict, kq[:, 0:c] * decay, 0.0)
        egc = jnp.exp(gcol)
        xs = jnp.concatenate([vh * bcol, kb * egc], axis=2)
        for it in range(6):
            ph = p.astype(BF16)
            pl_ = (p - ph.astype(F32)).astype(BF16)
            lhs = jnp.concatenate([ph, pl_, ph], axis=2)
            yh = xs.astype(BF16)
            yl = (xs - yh.astype(F32)).astype(BF16)
            if it < 5:
                yh = jnp.concatenate([yh, ph], axis=2)
                yl = jnp.concatenate([yl, pl_], axis=2)
            m = bdot(lhs, jnp.concatenate([yh, yh, yl], axis=1))
            xs = xs + m[:, :, 0:2 * D_DK]
            if it < 5:
                p = m[:, :, 2 * D_DK:3 * D_DK]
        u = xs[:, :, 0:D_DV]
        w = xs[:, :, D_DV:2 * D_DV]
        a = kq[:, c:2 * c] * decay
        kd = (kh * jnp.exp(glast - gcol)).astype(BF16)
        kdt = bdot_nt(jnp.broadcast_to(eye, (nb, c, c)), kd).astype(BF16)
        lhs_s = jnp.concatenate([w, qh * egc], axis=1).astype(BF16)
        lhs_v = jnp.concatenate([a.astype(BF16), kdt], axis=1)
        return u, lhs_s, lhs_v, jnp.exp(glast)

    prep = prepare(list(range(g_chunks)))

    def chains(arr, step):
        los = [((d * n_sq + sq) * g_chunks + step) * nh for d in range(2) for sq in range(n_sq)]
        return jnp.concatenate([arr[lo:lo + nh] for lo in los], axis=0)

    s = s_ref[...]
    outs = [[[None] * g_chunks for _ in range(n_sq)] for _ in range(2)]
    for step in range(g_chunks):
        u, lhs_s, lhs_v, eg = (chains(arr, step) for arr in prep)
        ws_qs = bdot(lhs_s, s.astype(BF16))
        vb = (u - ws_qs[:, 0:c]).astype(BF16)
        av = bdot(lhs_v, vb)
        o = ws_qs[:, c:2 * c] + av[:, 0:c]
        s = s * eg + av[:, c:2 * c]
        for d in range(2):
            gi = step if d == 0 else g_chunks - 1 - step
            for sq in range(n_sq):
                base = (d * n_sq + sq) * nh
                outs[d][sq][gi] = jnp.concatenate([o[base + h] for h in range(nh)], axis=1)
    for sq in range(n_sq):
        of_ref[sq] = jnp.concatenate(outs[0][sq], axis=0)
        ob_ref[sq] = jnp.concatenate(outs[1][sq], axis=0)
    s_ref[...] = s

    @pl.when(i == n_steps - 1)
    def _():
        for d in range(2):
            for sq in range(n_sq):
                sfin_ref[sq, d] = s[(d * n_sq + sq) * nh:(d * n_sq + sq + 1) * nh]


def _gdn_call(qkvn, proj, prow, pcol, s0, n_seq, t):
    g = GDN_CHUNKS_PER_STEP
    n_sq = math.gcd(n_seq, GDN_SEQS_PER_STEP)
    ns = t // (D_CHUNK * g)
    rows = D_CHUNK * g
    fwd = lambda i: i
    bwd = lambda i: ns - 1 - i
    x3 = qkvn.reshape(n_seq, t, 768)
    p3 = proj.reshape(n_seq, t, P1)
    xblk = lambda f: pl.BlockSpec((n_sq, rows, 768), lambda b, i: (b, f(i), 0))
    abblk = lambda f: pl.BlockSpec((n_sq, rows, 128), lambda b, i: (b, f(i), C_AB // 128))
    oblk = lambda f: pl.BlockSpec((n_sq, rows, 256), lambda b, i: (b, f(i), 0))
    sblk = pl.BlockSpec((n_sq, 2, D_HEADS, D_DK, D_DV), lambda b, i: (b, 0, 0, 0, 0))
    osd = jax.ShapeDtypeStruct((n_seq, t, 256), F32)
    in_specs = [xblk(fwd), xblk(bwd), abblk(fwd), abblk(bwd), _resident(prow), _resident(pcol)]
    args = [x3, x3, p3, p3, prow.arr, pcol.arr]
    if s0 is not None:
        for st in s0:
            in_specs.append(pl.BlockSpec((n_sq, None, D_HEADS, D_DK, D_DV), lambda b, i, l=st.l: (b, l, 0, 0, 0)))
            args.append(st.arr)
    o_f, o_b, s_fin = pl.pallas_call(
        functools.partial(_gdn_kernel, ns, g, n_sq, s0 is not None),
        grid=(n_seq // n_sq, ns),
        in_specs=in_specs,
        out_specs=[oblk(fwd), oblk(bwd), sblk],
        out_shape=[osd, osd, jax.ShapeDtypeStruct((n_seq, 2, D_HEADS, D_DK, D_DV), F32)],
        scratch_shapes=[pltpu.VMEM((2 * n_sq * D_HEADS, D_DK, D_DV), F32)],
        compiler_params=_cparams(("parallel", "arbitrary")),
        name="gdn",
    )(*args)
    return o_f.reshape(n_seq * t, 256), o_b.reshape(n_seq * t, 256), s_fin


def _row_groups(x):
    rows = x.shape[0] // DENSE_ROW_GROUPS
    return [x[g * rows:(g + 1) * rows] for g in range(DENSE_ROW_GROUPS)]


def _merge_kernel(x_ref, mod_ref, gpre_ref, oa_ref, ob_ref, oc_ref, of_ref, obw_ref, z_ref, gnorm_ref,
                  wg_ref, wbr_ref, wo_ref, gpost_ref, o_ref):
    d = D_MODEL
    mod = mod_ref[0]
    sh1, sc1, ga1 = mod[:, 0:d], mod[:, d:2 * d], mod[:, 2 * d:3 * d]
    xs = _row_groups(x_ref[...])
    hs = [(_rmsnorm_rows(xg, gpre_ref[...], d) * (1.0 + sc1) + sh1).astype(BF16) for xg in xs]
    ods = []
    for o, z in zip(_row_groups(of_ref[...] + obw_ref[...]), _row_groups(z_ref[...])):
        ms = _dot_sel_right(o * o, _group_ones(256, D_DV)) * (1.0 / D_DV)
        ods.append(o * lax.rsqrt(ms + EPS) * gnorm_ref[...] * _silu(z))
    branches = [_row_groups(oa_ref[...]), _row_groups(ob_ref[...]), _row_groups(oc_ref[...]), ods]
    accs = [None] * len(xs)
    for m in range(N_BRANCH):
        gates = [jax.nn.sigmoid(_dot(h, wg_ref[:, m * d:(m + 1) * d])) for h in hs]
        brs = [_dot(br.astype(BF16), wbr_ref[m]) for br in branches[m]]
        accs = [g * b if a is None else a + g * b for a, g, b in zip(accs, gates, brs)]
    mixes = [_dot(a.astype(BF16), wo_ref[...]) for a in accs]
    o_ref[...] = jnp.concatenate([xg + ga1 * _rmsnorm_rows(mix, gpost_ref[...], d) for xg, mix in zip(xs, mixes)],
                                 axis=0)


def _merge_call(x2d, mod3, gpre, oa, ob, oc, of, obw, proj, gnorm, wg, wbr, wo, gpost, seq_len, tm):
    n, d = x2d.shape
    tiles_per_seq = seq_len // tm
    row = lambda w, c=0: pl.BlockSpec((tm, w), lambda i: (i, c))
    full = _resident
    return pl.pallas_call(
        _merge_kernel,
        grid=(n // tm,),
        in_specs=[row(d), _mod_block(mod3, tiles_per_seq), full(gpre),
                  row(256), row(256), row(256), row(256), row(256), row(256, C_Z // 256), full(gnorm),
                  full(wg), full(wbr), full(wo), full(gpost)],
        out_specs=row(d),
        out_shape=jax.ShapeDtypeStruct((n, d), F32),
        compiler_params=_cparams(("parallel",)),
        name="merge",
    )(*_operands([x2d, mod3, gpre, oa, ob, oc, of, obw, proj, gnorm, wg, wbr, wo, gpost]))


def _ffn_kernel(x_ref, mod_ref, gpre_ref, wup_ref, wdown_ref, gpost_ref, o_ref):
    d = D_MODEL
    x = x_ref[...]
    mod = mod_ref[0]
    sh2, sc2, ga2 = mod[:, 3 * d:4 * d], mod[:, 4 * d:5 * d], mod[:, 5 * d:6 * d]
    xs = _row_groups(x)
    hs = [(_rmsnorm_rows(xg, gpre_ref[...], d) * (1.0 + sc2) + sh2).astype(BF16) for xg in xs]
    ups = [_dot(h, wup_ref[...]) for h in hs]
    acts = [(_silu(up[:, 0:D_FF]) * up[:, D_FF:2 * D_FF]).astype(BF16) for up in ups]
    fs = [_dot(act, wdown_ref[...]) for act in acts]
    o_ref[...] = jnp.concatenate([xg + ga2 * _rmsnorm_rows(f, gpost_ref[...], d) for xg, f in zip(xs, fs)], axis=0)


def _ffn_call(x2d, mod3, gpre, wup, wdown, gpost, seq_len, tm):
    n, d = x2d.shape
    tiles_per_seq = seq_len // tm
    row = lambda w: pl.BlockSpec((tm, w), lambda i: (i, 0))
    full = _resident
    return pl.pallas_call(
        _ffn_kernel,
        grid=(n // tm,),
        in_specs=[row(d), _mod_block(mod3, tiles_per_seq), full(gpre), full(wup), full(wdown), full(gpost)],
        out_specs=row(d),
        out_shape=jax.ShapeDtypeStruct((n, d), F32),
        compiler_params=_cparams(("parallel",)),
        name="ffn",
    )(*_operands([x2d, mod3, gpre, wup, wdown, gpost]))


def _rope_tables(n_tokens, dim, reps):
    n_rows = n_tokens // GRID_W
    row = jnp.repeat(jnp.arange(n_rows), GRID_W).astype(F32)
    col = jnp.tile(jnp.arange(GRID_W), n_rows).astype(F32)
    nfreq = dim // 4
    inv = ROPE_BASE ** (-jnp.arange(nfreq, dtype=F32) / nfreq)
    ang_r = row[:, None] * inv
    ang_c = col[:, None] * inv
    ang = jnp.concatenate([ang_r, ang_r, ang_c, ang_c], axis=-1)
    cos, sin = jnp.cos(ang), jnp.sin(ang)
    even = ((jnp.arange(dim) // nfreq) % 2 == 0)[None, :]
    sin_next = jnp.where(even, -sin, 0.0)
    sin_prev = jnp.where(even, 0.0, sin)
    return tuple(jnp.tile(a, (1, reps)) for a in (cos, sin_next, sin_prev))


def _rope_tables_mla_wide(n_tokens):
    cos, sin_next, sin_prev = _rope_tables(n_tokens, B_ROPE, 1)
    tail = MLA_HEAD_LANES - B_NOPE - B_ROPE
    wide = lambda a, fill: jnp.pad(a, ((0, 0), (B_NOPE, tail)), constant_values=fill)
    return wide(cos, 1.0), wide(sin_next, 0.0), wide(sin_prev, 0.0)


def _split_cols(w):
    out, acc = [], 0
    for s in IN_SIZES:
        out.append(w[..., acc:acc + s])
        acc += s
    return out


def _pack_w_in_kernel(w_ref, w1_ref, wg_ref):
    w = w_ref[...]
    (a_q, a_k, a_v, b_cq, b_ckv, b_kr, c_in, d_qkv, d_z, d_beta, d_alpha, gate) = _split_cols(w)
    z = lambda n: jnp.zeros((w.shape[0], n), w.dtype)
    k0, k1 = a_k[:, :A_HD], a_k[:, A_HD:]
    v0, v1 = a_v[:, :A_HD], a_v[:, A_HD:]
    w1 = jnp.concatenate([a_q, k0, k0, k1, k1, v0, v0, v1, v1, b_cq, z(256 - B_Q_RANK), b_ckv,
                          b_kr, b_kr, b_kr, b_kr, c_in, d_qkv, d_z, d_beta, d_alpha, z(112)], axis=-1)
    w1_ref[...] = w1.astype(BF16)
    wg_ref[...] = gate.astype(BF16)


def _pack_w_in(w):
    w = w.astype(BF16)
    depth, d, p = w.shape
    n_gate = IN_SIZES[-1]
    tr = 256
    return pl.pallas_call(
        _pack_w_in_kernel,
        grid=(depth, d // tr),
        in_specs=[pl.BlockSpec((None, tr, p), lambda l, i: (l, i, 0))],
        out_specs=[pl.BlockSpec((None, tr, P1), lambda l, i: (l, i, 0)),
                   pl.BlockSpec((None, tr, n_gate), lambda l, i: (l, i, 0))],
        out_shape=[jax.ShapeDtypeStruct((depth, d, P1), BF16), jax.ShapeDtypeStruct((depth, d, n_gate), BF16)],
        compiler_params=_cparams(("parallel", "parallel")),
        name="pack_w_in",
    )(w)


def _expand_kv_heads(a):
    return jnp.repeat(a, A_HEADS // A_KV_HEADS, axis=-2).reshape(a.shape[:-2] + (A_HEADS * A_HD,))


def _stacked_weights(g_pre1, g_post1, g_pre2, g_post2, w_in, a_sink, b_g_cq, b_g_ckv, b_w_uq, b_w_ukv,
                     c_w_pool, c_scale, d_conv, d_a_log, d_dt_bias, d_g_norm, w_br, w_o, w_up, w_down):
    depth = w_in.shape[0]
    w1, wg = _pack_w_in(w_in)
    wuq = b_w_uq.reshape(depth, B_Q_RANK, B_HEADS, B_NOPE + B_ROPE)
    wuq = jnp.pad(wuq, ((0, 0), (0, 256 - B_Q_RANK), (0, 0), (0, MLA_HEAD_LANES - B_NOPE - B_ROPE)))
    wuq = wuq.reshape(depth, 256, MLA_WIDE).astype(BF16)
    wukv = b_w_ukv.reshape(depth, B_KV_RANK, B_HEADS, B_NOPE + B_VD)
    wk = jnp.pad(wukv[..., :B_NOPE], ((0, 0), (0, 0), (0, 0), (0, MLA_HEAD_LANES - B_NOPE)))
    wukv = jnp.concatenate([wk.reshape(depth, B_KV_RANK, MLA_WIDE),
                            wukv[..., B_NOPE:].reshape(depth, B_KV_RANK, B_HEADS * B_VD)], -1).astype(BF16)
    same = jnp.eye(C_GROUPS, dtype=F32)[None, :, None, :, None]
    wpool = (c_w_pool[:, :, :, None, :] * same).reshape(depth, C_GROUPS * C_GW, C_GROUPS * C_GW).astype(BF16)
    gdn =jnp.stack([d_a_log.reshape(depth, 2 * D_HEADS), d_dt_bias.reshape(depth, 2 * D_HEADS)], axis=1)
    prow = jnp.pad(gdn, ((0, 0), (0, 0), (8, LANES - 16)))
    pcol = jnp.pad(gdn.transpose(0, 2, 1), ((0, 0), (8, 0), (0, 0)))
    row = lambda a: a[:, None, :]
    return dict(
        w1=w1, wg=wg, wuq=wuq, wukv=wukv,
        gpre1=row(g_pre1), gpost1=row(g_post1), gpre2=row(g_pre2), gpost2=row(g_post2),
        gcq=row(jnp.pad(b_g_cq, ((0, 0), (0, 256 - B_Q_RANK)))), gckv=row(b_g_ckv),
        sink=a_sink, wpool=wpool, cscale=row(c_scale),
        convw=jnp.pad(d_conv, ((0, 0), (0, SUBLANES - D_CONV), (0, 0))),
        prow=prow, pcol=pcol, gnorm=row(jnp.tile(d_g_norm, (1, D_HEADS))),
        wbr=w_br.astype(BF16), wo=w_o.astype(BF16), wup=w_up.astype(BF16), wdown=w_down.astype(BF16))


def _trunk_layer(x2d, mod3, lw, n_seq, t, ctx, tables=None):
    latent = ctx is not None
    n = x2d.shape[0]
    tm = min(512, t) if latent else 256
    tm_dense = min(512, n)
    if latent:
        proj, qw, kwr, vb, aqr, akr, qwr = _inproj_call(
            x2d, mod3, lw["gpre1"], lw["w1"], lw["gcq"], lw["wuq"], lw["gckv"], lw["wukv"], tables, t, tm_dense)
        kctx, vctx, ctx_ckv, ckr_wide, s0f, s0b = ctx
        out_a = _win_attn_call(lw["sink"], aqr, akr, proj, kctx, vctx, n_seq, t)
        out_b = _mla_lat_call(qw, qwr, kwr, vb, ctx_ckv, ckr_wide, lw["wukv"], n_seq, t, min(MLA_TQ, t))
        s0 = (s0f, s0b)
    else:
        proj, qw, kw, vb, ckvn, ak, av, kr = _inproj_call(
            x2d, mod3, lw["gpre1"], lw["w1"], lw["gcq"], lw["wuq"], lw["gckv"], lw["wukv"], None, t, tm_dense)
        out_a, out_b = _attn_ctx_call(lw["sink"], proj, qw, kw, vb, n_seq, t)
        s0 = None
    out_c, qkvn = _local_call(proj, lw["convw"], lw["wpool"], lw["cscale"], t, tm)
    o_f, o_b, s_fin = _gdn_call(qkvn, proj, lw["prow"], lw["pcol"], s0, n_seq, t)
    x1 = _merge_call(x2d, mod3, lw["gpre1"], out_a, out_b, out_c, o_f, o_b, proj, lw["gnorm"],
                     lw["wg"], lw["wbr"], lw["wo"], lw["gpost1"], t, tm_dense)
    x2 = _ffn_call(x1, mod3, lw["gpre2"], lw["wup"], lw["wdown"], lw["gpost2"], t, tm_dense)
    new_ctx = None
    if not latent:
        new_ctx = (ak.reshape(n_seq, t, A_KV_HEADS, A_HD), av.reshape(n_seq, t, A_KV_HEADS, A_HD),
                   ckvn.reshape(n_seq, t, B_KV_RANK), kr.reshape(n_seq, t, B_ROPE), s_fin[:, 0], s_fin[:, 1])
    return x2, new_ctx


def kernel(x_prompt, x_sample, cache_a_k, cache_a_v, cache_b_ckv, cache_b_krope, state_d_fwd, state_d_bwd, c, c_ctx, w_mod, b_mod, g_pre1, g_post1, g_pre2, g_post2, w_in, a_sink, b_g_cq, b_g_ckv, b_w_uq, b_w_ukv, c_w_pool, c_scale, d_conv, d_a_log, d_dt_bias, d_g_norm, w_br, w_o, w_up, w_down):
    depth = w_mod.shape[0]
    bp, tp, d = x_prompt.shape
    bs, ts, _ = x_sample.shape
    assert bs + 1 <= SUBLANES
    cond8 = jnp.zeros((SUBLANES, d), F32).at[0].set(c_ctx).at[1:1 + bs].set(c)
    mod_all = _mod_call(cond8, w_mod, b_mod).reshape(depth, SUBLANES, 1, 6 * d)
    yp = x_prompt.reshape(bp * tp, d)
    ys = x_sample.reshape(bs * ts, d)
    new = [[] for _ in range(6)]
    tables = _rope_tables(ts, A_HD, LANES // A_HD) + _rope_tables_mla_wide(ts)
    kctx_all, vctx_all = _expand_kv_heads(cache_a_k), _expand_kv_heads(cache_a_v)
    ckr_wide_all = jnp.tile(jnp.pad(cache_b_krope, ((0, 0), (0, 0), (0, 0),
                                                    (B_NOPE, MLA_HEAD_LANES - B_NOPE - B_ROPE))), (1, 1, 1, B_HEADS))
    stacked = _stacked_weights(g_pre1, g_post1, g_pre2, g_post2, w_in, a_sink, b_g_cq, b_g_ckv, b_w_uq, b_w_ukv,
                               c_w_pool, c_scale, d_conv, d_a_log, d_dt_bias, d_g_norm, w_br, w_o, w_up, w_down)
    for l in range(depth):
        lw = {k: _Layer(v, l) for k, v in stacked.items()}
        mod_ctx = _Mod(mod_all, l, 0, 1)
        mod_lat = _Mod(mod_all, l, 1, bs)
        yp, nctx = _trunk_layer(yp, mod_ctx, lw, bp, tp, None)
        for acc, val in zip(new, nctx):
            acc.append(val)
        ctx = tuple(_Layer(a, l) for a in (kctx_all, vctx_all, cache_b_ckv, ckr_wide_all, state_d_fwd, state_d_bwd))
        ys, _ = _trunk_layer(ys, mod_lat, lw, bs, ts, ctx, tables)
    return (yp.reshape(bp, tp, d), ys.reshape(bs, ts, d)) + tuple(jnp.stack(v, axis=1) for v in new)
```

```python
import functools
import math
from typing import NamedTuple

import jax
import jax.numpy as jnp
from jax import lax
from jax.experimental import pallas as pl
from jax.experimental.pallas import tpu as pltpu

F32 = jnp.float32
BF16 = jnp.bfloat16

D_MODEL = 1024
GRID_W = 64
ROPE_BASE = 10000.0
EPS = 1e-6
NEG_INF = -1e30
Q_BLOCK = 128
N_BRANCH = 4
BRANCH_W = 256
A_HEADS = 4
A_KV_HEADS = 2
A_HD = 64
A_WINDOW = 128
A_SCALE = A_HD ** -0.5
B_HEADS = 4
B_Q_RANK = 192
B_KV_RANK = 128
B_NOPE = 64
B_ROPE = 32
B_VD = 64
B_SCALE = (B_NOPE + B_ROPE) ** -0.5
C_GROUPS = 4
C_GW = 64
C_WINDOWS = (2, 4, 8, 16)
D_HEADS = 4
D_DK = 64
D_DV = 64
D_CONV = 5
D_CHUNK = 64
FF_RAW = -(-8 * D_MODEL // 3)
D_FF = -(-FF_RAW // 256) * 256
IN_SIZES = (A_HEADS * A_HD, A_KV_HEADS * A_HD, A_KV_HEADS * A_HD, B_Q_RANK, B_KV_RANK, B_ROPE,
            C_GROUPS * C_GW, D_HEADS * (2 * D_DK + D_DV), D_HEADS * D_DV, 2 * D_HEADS, 2 * D_HEADS,
            N_BRANCH * D_MODEL)

LANES = 128
SUBLANES = 8
VMEM_LIMIT_BYTES = 56 * 1024 * 1024
HALO = SUBLANES
GDN_SEQS_PER_STEP = 2
GDN_CHUNKS_PER_STEP = 4
MLA_HEAD_LANES = LANES
MLA_WIDE = B_HEADS * MLA_HEAD_LANES
LOG2E = math.log2(math.e)
DENSE_ROW_GROUPS = 2
MLA_TQ = 512
SOFTMAX_ROW_GROUP = 32
SOFTMAX_COL_BLOCK = 256
WIN_ATTN_TQ = 512

C_AQ = 0
C_AK = 256
C_AV = 512
C_CQ = 768
C_CKV = 1024
C_KR = 1152
C_CIN = 1280
C_QKV = 1536
C_Z = 2304
C_AB = 2560
P1 = 2688


def _cparams(sem):
    return pltpu.CompilerParams(dimension_semantics=sem, vmem_limit_bytes=VMEM_LIMIT_BYTES)


class _Layer(NamedTuple):
    arr: jax.Array
    l: int


def _resident(p):
    shape = p.arr.shape[1:]
    return pl.BlockSpec((None,) + shape, lambda *_: (p.l,) + (0,) * len(shape), pipeline_mode=pl.Buffered(1))


class _Mod(NamedTuple):
    arr: jax.Array
    l: int
    first: int
    count: int


def _mod_block(m, tiles_per_seq):
    width = m.arr.shape[-1]
    if m.count == 1:
        return pl.BlockSpec((None, 1, 1, width), lambda i: (m.l, m.first, 0, 0))
    return pl.BlockSpec((None, 1, 1, width), lambda i: (m.l, m.first + i // tiles_per_seq, 0, 0))


def _cache_block(p):
    return pl.BlockSpec((1, None) + p.arr.shape[2:], lambda b, i: (b, p.l, 0, 0))


def _operands(args):
    return [a.arr if isinstance(a, (_Layer, _Mod)) else a for a in args]


def _dot(a, b):
    return jnp.dot(a, b, preferred_element_type=F32)


def _dot_nt(a, b):
    return lax.dot_general(a, b, (((1,), (1,)), ((), ())), preferred_element_type=F32)


def _split3(x):
    hi = x.astype(BF16)
    r1 = x - hi.astype(F32)
    mid = r1.astype(BF16)
    lo = (r1 - mid.astype(F32)).astype(BF16)
    return hi, mid, lo


def _dot_sel_right(x, sel):
    hi, mid, lo = _split3(x)
    return _dot(hi, sel) + _dot(mid, sel) + _dot(lo, sel)


def _silu(x):
    return x * jax.nn.sigmoid(x)


def _softplus(x):
    return jnp.maximum(x, 0.0) + jnp.log1p(jnp.exp(-jnp.abs(x)))


def _head_mask(width, n_lanes, h):
    lane = lax.broadcasted_iota(jnp.int32, (1, n_lanes), 1)
    return (lane // width) == h


def _group_ones(n, width):
    r = lax.broadcasted_iota(jnp.int32, (n, n), 0) // width
    c = lax.broadcasted_iota(jnp.int32, (n, n), 1) // width
    return jnp.where(r == c, 1.0, 0.0).astype(BF16)


def _mod_kernel(c_ref, w_ref, b_ref, o_ref):
    s = _silu(c_ref[...])
    o_ref[0] = _dot(s.astype(BF16), w_ref[0].astype(BF16)) + b_ref[0]


def _mod_call(cond8, w_mod, b_mod):
    depth, d, n = w_mod.shape
    tn = 1536
    return pl.pallas_call(
        _mod_kernel,
        grid=(depth, n // tn),
        in_specs=[pl.BlockSpec((SUBLANES, d), lambda l, j: (0, 0)),
                  pl.BlockSpec((1, d, tn), lambda l, j: (l, 0, j)),
                  pl.BlockSpec((1, 1, tn), lambda l, j: (l, 0, j))],
        out_specs=pl.BlockSpec((1, SUBLANES, tn), lambda l, j: (l, 0, j)),
        out_shape=jax.ShapeDtypeStruct((depth, SUBLANES, n), F32),
        compiler_params=_cparams(("parallel", "parallel")),
        name="mod",
    )(cond8, w_mod, b_mod.reshape(depth, 1, n))


def _rmsnorm_rows(x, g, n):
    ms = jnp.sum(x * x, axis=-1, keepdims=True) * (1.0 / n)
    return x * lax.rsqrt(ms + EPS) * g


def _rope(x, cos, sin_next, sin_prev, quarter):
    n = x.shape[-1]
    return x * cos + pltpu.roll(x, n - quarter, 1) * sin_next + pltpu.roll(x, quarter, 1) * sin_prev


def _mla_wide_kr(kr_rep):
    lane = lax.broadcasted_iota(jnp.int32, (1, MLA_HEAD_LANES), 1)
    keep = (lane >= B_NOPE) & (lane < B_NOPE + B_ROPE)
    return jnp.concatenate([jnp.where(keep, kr_rep, 0.0)] * B_HEADS, axis=1)


def _inproj_kernel(latent, *refs):
    if latent:
        (x_ref, mod_ref, gpre_ref, w1_ref, gcq_ref, wuq_ref, gckv_ref, wukv_ref,
         ca_ref, sna_ref, spa_ref, cb_ref, snb_ref, spb_ref,
         proj_ref, qw_ref, kw_ref, vb_ref, aqr_ref, akr_ref, qwr_ref) = refs
    else:
        (x_ref, mod_ref, gpre_ref, w1_ref, gcq_ref, wuq_ref, gckv_ref, wukv_ref,
         proj_ref, qw_ref, kw_ref, vb_ref, ckv_ref, ak_ref, av_ref, kr_ref) = refs
    d = D_MODEL
    mod = mod_ref[0]
    sh1 = mod[:, 0:d]
    sc1 = mod[:, d:2 * d]
    n_rows = x_ref.shape[0] // DENSE_ROW_GROUPS
    groups = [slice(g * n_rows, (g + 1) * n_rows) for g in range(DENSE_ROW_GROUPS)]
    hs = [_rmsnorm_rows(x_ref[rows, :], gpre_ref[...], d) * (1.0 + sc1) + sh1 for rows in groups]
    projs = [_dot(h.astype(BF16), w1_ref[...]) for h in hs]
    for rows, proj in zip(groups, projs):
        proj_ref[rows, :] = proj
        cqn = _rmsnorm_rows(proj[:, C_CQ:C_CQ + 256], gcq_ref[...], B_Q_RANK)
        qw = _dot(cqn.astype(BF16), wuq_ref[...]) * (B_SCALE * LOG2E)
        qw_ref[rows, :] = qw.astype(BF16)
        ckvn = _rmsnorm_rows(proj[:, C_CKV:C_CKV + B_KV_RANK], gckv_ref[...], B_KV_RANK)
        if not latent:
            unexpand = lambda c: jnp.concatenate([proj[:, c:c + A_HD], proj[:, c + 2 * A_HD:c + 3 * A_HD]], axis=1)
            ckv_ref[rows, :] = ckvn
            ak_ref[rows, :] = unexpand(C_AK)
            av_ref[rows, :] = unexpand(C_AV)
            kr_ref[rows, :] = proj[:, C_KR:C_KR + B_ROPE]
        knv = _dot(ckvn.astype(BF16), wukv_ref[...])
        vb_ref[rows, :] = knv[:, MLA_WIDE:].astype(BF16)
        kr_wide = _mla_wide_kr(proj[:, C_KR:C_KR + 128])
        if latent:
            ca, sna, spa = (jnp.concatenate([r[rows, :]] * (256 // LANES), axis=1) for r in (ca_ref, sna_ref, spa_ref))
            cb, snb, spb = (jnp.concatenate([r[rows, :]] * B_HEADS, axis=1) for r in (cb_ref, snb_ref, spb_ref))
            aqr_ref[rows, :] = _rope(proj[:, C_AQ:C_AQ + 256], ca, sna, spa, A_HD // 4).astype(BF16)
            akr_ref[rows, :] = _rope(proj[:, C_AK:C_AK + 256], ca, sna, spa, A_HD // 4).astype(BF16)
            qwr_ref[rows, :] = _rope(qw, cb, snb, spb, B_ROPE // 4).astype(BF16)
            kr_wide = _rope(kr_wide, cb, snb, spb, B_ROPE // 4)
        kw_ref[rows, :] = (knv[:, :MLA_WIDE] + kr_wide).astype(BF16)


def _inproj_call(x2d, mod3, gpre, w1, gcq, wuq, gckv, wukv, tables, seq_len, tm):
    n, d = x2d.shape
    latent = tables is not None
    tiles_per_seq = seq_len // tm if latent else 1
    row = lambda w, dt=F32: (pl.BlockSpec((tm, w), lambda i: (i, 0)), jax.ShapeDtypeStruct((n, w), dt))
    params = [gpre, w1, gcq, wuq, gckv, wukv]
    in_specs = [pl.BlockSpec((tm, d), lambda i: (i, 0)),
                _mod_block(mod3, seq_len // tm)] + [_resident(p) for p in params]
    args = [x2d, mod3] + params
    outs = [row(P1), row(MLA_WIDE, BF16), row(MLA_WIDE, BF16), row(256, BF16)]
    if latent:
        for t in tables:
            in_specs.append(pl.BlockSpec((tm, t.shape[1]), lambda i: (i % tiles_per_seq, 0)))
            args.append(t)
        outs += [row(256, BF16), row(256, BF16), row(MLA_WIDE, BF16)]
    else:
        outs += [row(B_KV_RANK), row(A_KV_HEADS * A_HD), row(A_KV_HEADS * A_HD), row(B_ROPE)]
    return pl.pallas_call(
        functools.partial(_inproj_kernel, latent),
        grid=(n // tm,),
        in_specs=in_specs,
        out_specs=[o[0] for o in outs],
        out_shape=[o[1] for o in outs],
        compiler_params=_cparams(("parallel",)),
        name="inproj_lat" if latent else "inproj_ctx",
    )(*_operands(args))


def _softmax_pv(scores, values, scale, sink, hm):
    c1 = None if scale is None else scale * LOG2E
    m = jnp.max(scores[0], axis=-1, keepdims=True)
    for s in scores[1:]:
        m = jnp.maximum(m, jnp.max(s, axis=-1, keepdims=True))
    m2 = m if c1 is None else m * c1
    if sink is not None:
        m2 = jnp.maximum(m2, sink * LOG2E)
    den = None
    pv = None
    for s, v in zip(scores, values):
        p = jnp.exp2((s if c1 is None else s * c1) - m2)
        ps = jnp.sum(p, axis=-1, keepdims=True)
        den = ps if den is None else den + ps
        t = _dot(p.astype(BF16), v)
        pv = t if pv is None else pv + t
    if sink is not None:
        den = den + jnp.exp2(sink * LOG2E - m2)
    return jnp.where(hm, pv / den, 0.0)


def _attn_ctx_kernel(layer, sink_ref, q_ref, k_ref, v_ref, qw_ref, kw_ref, vb_ref, oa_ref, ob_ref):
    q = q_ref[...]
    k = k_ref[...].astype(BF16)
    v = v_ref[...].astype(BF16)
    acc = jnp.zeros(q.shape, F32)
    for h in range(A_HEADS):
        hm = _head_mask(A_HD, 256, h)
        s = _dot_nt(jnp.where(hm, q, 0.0).astype(BF16), k)
        acc = acc + _softmax_pv([s], [v], A_SCALE, sink_ref[layer, h], hm)
    oa_ref[...] = acc
    vb = vb_ref[...]
    acc = jnp.zeros((q.shape[0], B_HEADS * B_VD), F32)
    for h in range(B_HEADS):
        hs = slice(h * MLA_HEAD_LANES, (h + 1) * MLA_HEAD_LANES)
        s = _dot_nt(qw_ref[:, hs], kw_ref[:, hs])
        acc = acc + _softmax_pv([s], [vb], None, None, _head_mask(B_VD, 256, h))
    ob_ref[...] = acc


def _attn_ctx_call(sink, proj, qw, kw, vb, n_seq, t):
    n = proj.shape[0]
    col = lambda w, j: pl.BlockSpec((t, w), lambda b: (b, j))
    out = (pl.BlockSpec((t, 256), lambda b: (b, 0)), jax.ShapeDtypeStruct((n, 256), F32))
    return pl.pallas_call(
        functools.partial(_attn_ctx_kernel, sink.l),
        grid=(n_seq,),
        in_specs=[pl.BlockSpec(memory_space=pltpu.SMEM),
                  col(256, C_AQ // 256), col(256, C_AK // 256), col(256, C_AV // 256),
                  col(MLA_WIDE, 0), col(MLA_WIDE, 0), col(256, 0)],
        out_specs=[out[0], out[0]],
        out_shape=[out[1], out[1]],
        compiler_params=_cparams(("parallel",)),
        name="attn_ctx",
    )(sink.arr, proj, proj, proj, qw, kw, vb)


def _softmax_tiled(s_ref, slot, p_ref, den_ref, scale, sink):
    tq, n_keys = p_ref.shape
    cb, rg = SOFTMAX_COL_BLOCK, SOFTMAX_ROW_GROUP
    c1 = None if scale is None else scale * LOG2E
    for r0 in range(0, tq, rg):
        rows = slice(r0, r0 + rg)
        m = s_ref[slot, rows, 0:cb]
        for c0 in range(cb, n_keys, cb):
            m = jnp.maximum(m, s_ref[slot, rows, c0:c0 + cb])
        m2 = jnp.max(m, axis=-1, keepdims=True)
        if c1 is not None:
            m2 = m2 * c1
        if sink is not None:
            m2 = jnp.maximum(m2, sink * LOG2E)
        part = None
        for c0 in range(0, n_keys, cb):
            s = s_ref[slot, rows, c0:c0 + cb]
            p = jnp.exp2((s if c1 is None else s * c1) - m2)
            part = p if part is None else part + p
            p_ref[rows, c0:c0 + cb] = p.astype(BF16)
        den = jnp.sum(part, axis=-1, keepdims=True)
        if sink is not None:
            den = den + jnp.exp2(sink * LOG2E - m2)
        den_ref[rows, :] = jnp.broadcast_to(den, (rg, LANES))


def _win_attn_kernel(t, layer, sink_ref, qr_ref, q_ref, kp_ref, kc_ref, kn_ref, vp_ref, vc_ref, vn_ref,
                     kctx_ref, vctx_ref, o_ref, s_ref, p_ref, den_ref):
    i = pl.program_id(1)
    qr = qr_ref[...]
    q = q_ref[...]
    tq = q.shape[0]
    n_loc = tq + 2 * A_WINDOW
    n_keys = s_ref.shape[2]
    kl = jnp.concatenate([kp_ref[...], kc_ref[...], kn_ref[...]], axis=0)
    vl = jnp.concatenate([vp_ref[...], vc_ref[...], vn_ref[...]], axis=0).astype(BF16)
    kctx = kctx_ref[0].astype(BF16)
    vctx = vctx_ref[0].astype(BF16)
    q_pos = i * tq + lax.broadcasted_iota(jnp.int32, (tq, n_loc), 0)
    k_pos = i * tq - A_WINDOW + lax.broadcasted_iota(jnp.int32, (tq, n_loc), 1)
    valid = (jnp.abs(q_pos - k_pos) <= A_WINDOW) & (k_pos >= 0) & (k_pos < t)

    def scores_into(slot, h):
        hm = _head_mask(A_HD, 256, h)
        s_ref[slot, :, 0:n_loc] = jnp.where(valid, _dot_nt(jnp.where(hm, qr, 0.0).astype(BF16), kl), NEG_INF)
        s_ref[slot, :, n_loc:n_keys] = _dot_nt(jnp.where(hm, q, 0.0).astype(BF16), kctx)

    scores_into(0, 0)
    acc = jnp.zeros(q.shape, F32)
    for h in range(A_HEADS):
        slot = h % 2
        if h + 1 < A_HEADS:
            scores_into(1 - slot, h + 1)
        _softmax_tiled(s_ref, slot, p_ref, den_ref, A_SCALE, sink_ref[layer, h])
        pv = _dot(p_ref[:, 0:n_loc], vl) + _dot(p_ref[:, n_loc:n_keys], vctx)
        acc = acc + jnp.where(_head_mask(A_HD, 256, h), pv / den_ref[:, 0:1], 0.0)
    o_ref[...] = acc


def _win_attn_call(sink, aqr, akr, proj, kctx, vctx, n_seq, t):
    n = proj.shape[0]
    tq = min(WIN_ATTN_TQ, t)
    nq = t // tq
    r = tq // A_WINDOW
    nh = t // A_WINDOW
    n_keys = tq + 2 * A_WINDOW + kctx.arr.shape[2]
    cur = lambda j: pl.BlockSpec((tq, 256), lambda b, i: (b * nq + i, j))
    prev = lambda j: pl.BlockSpec((A_WINDOW, 256), lambda b, i: (b * nh + jnp.maximum(i * r - 1, 0), j))
    nxt = lambda j: pl.BlockSpec((A_WINDOW, 256), lambda b, i: (b * nh + jnp.minimum((i + 1) * r, nh - 1), j))
    ctx = _cache_block(kctx)
    return pl.pallas_call(
        functools.partial(_win_attn_kernel, t, sink.l),
        grid=(n_seq, nq),
        in_specs=[pl.BlockSpec(memory_space=pltpu.SMEM),
                  cur(0), cur(C_AQ // 256),
                  prev(0), cur(0), nxt(0),
                  prev(C_AV // 256), cur(C_AV // 256), nxt(C_AV // 256),
                  ctx, ctx],
        out_specs=pl.BlockSpec((tq, 256), lambda b, i: (b * nq + i, 0)),
        out_shape=jax.ShapeDtypeStruct((n, 256), F32),
        scratch_shapes=[pltpu.VMEM((2, tq, n_keys), F32), pltpu.VMEM((tq, n_keys), BF16), pltpu.VMEM((tq, LANES), F32)],
        compiler_params=_cparams(("parallel", "parallel")),
        name="win_attn",
    )(sink.arr, aqr, proj, akr, akr, akr, proj, proj, proj, kctx.arr, vctx.arr)


def _mla_lat_kernel(qw_ref, qwr_ref, kwr_ref, vb_ref, cckv_ref, ckr_ref, wukv_ref, o_ref, s_ref, p_ref, den_ref):
    kvc = _dot(cckv_ref[0].astype(BF16), wukv_ref[...])
    kwc = (kvc[:, :MLA_WIDE] + ckr_ref[0]).astype(BF16)
    vc = kvc[:, MLA_WIDE:].astype(BF16)
    tq = qw_ref.shape[0]
    t = kwr_ref.shape[0]
    n_keys = s_ref.shape[2]

    def scores_into(slot, h):
        hs = slice(h * MLA_HEAD_LANES, (h + 1) * MLA_HEAD_LANES)
        s_ref[slot, :, 0:t] = _dot_nt(qwr_ref[:, hs], kwr_ref[:, hs])
        s_ref[slot, :, t:n_keys] = _dot_nt(qw_ref[:, hs], kwc[:, hs])

    scores_into(0, 0)
    acc = jnp.zeros((tq, B_HEADS * B_VD), F32)
    for h in range(B_HEADS):
        slot = h % 2
        if h + 1 < B_HEADS:
            scores_into(1 - slot, h + 1)
        _softmax_tiled(s_ref, slot, p_ref, den_ref, None, None)
        pv =_dot(p_ref[:, 0:t], vb_ref[...]) + _dot(p_ref[:, t:n_keys], vc)
        acc = acc + jnp.where(_head_mask(B_VD, 256, h), pv / den_ref[:, 0:1], 0.0)
    o_ref[...] = acc


def _mla_lat_call(qw, qwr, kwr, vb, cckv, ckr_wide, wukv, n_seq, t, tq):
    n = qw.shape[0]
    nq = t // tq
    qblk = lambda w: pl.BlockSpec((tq, w), lambda b, i: (b * nq + i, 0))
    seq = lambda w: pl.BlockSpec((t, w), lambda b, i: (b, 0))
    ctx = _cache_block
    return pl.pallas_call(
        _mla_lat_kernel,
        grid=(n_seq, nq),
        in_specs=[qblk(MLA_WIDE), qblk(MLA_WIDE), seq(MLA_WIDE), seq(256), ctx(cckv), ctx(ckr_wide),
                  _resident(wukv)],
        out_specs=pl.BlockSpec((tq, 256), lambda b, i: (b * nq + i, 0)),
        out_shape=jax.ShapeDtypeStruct((n, 256), F32),
        scratch_shapes=[pltpu.VMEM((2, tq, t + cckv.arr.shape[2]), F32), pltpu.VMEM((tq, t + cckv.arr.shape[2]), BF16),
                        pltpu.VMEM((tq, LANES), F32)],
        compiler_params=_cparams(("parallel", "parallel")),
        name="mla_lat",
    )(qw, qwr, kwr, vb, cckv.arr, ckr_wide.arr, wukv.arr)


def _with_halo(prev_ref, cur_ref, next_ref, has_prev, has_next):
    prev = jnp.where(has_prev, prev_ref[...], 0.0)
    nxt = jnp.where(has_next, next_ref[...], 0.0)
    return jnp.concatenate([prev, cur_ref[...], nxt], axis=0)


def _shift_rows(x, k):
    n = x.shape[0]
    return pltpu.roll(x, (-k) % n, 0)


def _local_kernel(t, tm, cp_ref, cc_ref, cn_ref, qp_ref, qc_ref, qn_ref, convw_ref, wpool_ref, cscale_ref,
                  oc_ref, qkv_ref):
    i = pl.program_id(0)
    tiles_per_seq = t // tm
    j = i % tiles_per_seq
    has_prev = j > 0
    has_next = j < tiles_per_seq - 1
    x = _with_halo(cp_ref, cc_ref, cn_ref, has_prev, has_next)
    p2 = x + _shift_rows(x, -1)
    p4 = _shift_rows(p2, 1) + _shift_rows(p2, -1)
    p8 = _shift_rows(p4, 2) + _shift_rows(p4, -2)
    p16 = _shift_rows(p8, 4) + _shift_rows(p8, -4)
    grp = lax.broadcasted_iota(jnp.int32, (1, 256), 1) // C_GW
    win = jnp.where(grp == 0, p2, jnp.where(grp == 1, p4, jnp.where(grp == 2, p8, p16)))[HALO:HALO + tm]
    pos = j * tm + lax.broadcasted_iota(jnp.int32, (tm, 256), 0)
    half = jnp.where(grp == 0, 1, jnp.where(grp == 1, 2, jnp.where(grp == 2, 4, 8)))
    cnt = (jnp.minimum(pos + half, t) - jnp.maximum(pos - half, 0)).astype(F32)
    y = win / cnt - cc_ref[...]
    oc_ref[...] = _dot(y.astype(BF16), wpool_ref[...]) * cscale_ref[...]
    xq = _with_halo(qp_ref, qc_ref, qn_ref, has_prev, has_next)
    w = convw_ref[...]
    pad = D_CONV // 2
    acc = None
    for tap in range(D_CONV):
        term = _shift_rows(xq, tap - pad) * w[tap:tap + 1, :]
        acc = term if acc is None else acc + term
    u = _silu(acc[HALO:HALO + tm])
    ones = _group_ones(256, D_DK)
    q = u[:, 0:256]
    k = u[:, 256:512]
    q = q * lax.rsqrt(_dot_sel_right(q * q, ones) + EPS) * (D_DK ** -0.5)
    k = k * lax.rsqrt(_dot_sel_right(k * k, ones) + EPS)
    qkv_ref[:, 0:256] = q
    qkv_ref[:, 256:512] = k
    qkv_ref[:, 512:768] = u[:, 512:768]


def _local_call(proj, convw8, wpool_bd, cscale, t, tm):
    n = proj.shape[0]
    hb = tm // HALO
    nblk = n // HALO
    cur = lambda w, c: pl.BlockSpec((tm, w), lambda i: (i, c))
    prev = lambda w, c: pl.BlockSpec((HALO, w), lambda i: (jnp.maximum(i * hb - 1, 0), c))
    nxt = lambda w, c: pl.BlockSpec((HALO, w), lambda i: (jnp.minimum((i + 1) * hb, nblk - 1), c))
    full = _resident
    cc, qc = C_CIN // 256, C_QKV // 768
    return pl.pallas_call(
        functools.partial(_local_kernel, t, tm),
        grid=(n // tm,),
        in_specs=[prev(256, cc), cur(256, cc), nxt(256, cc), prev(768, qc), cur(768, qc), nxt(768, qc),
                  full(convw8), full(wpool_bd), full(cscale)],
        out_specs=[pl.BlockSpec((tm, 256), lambda i: (i, 0)), pl.BlockSpec((tm, 768), lambda i: (i, 0))],
        out_shape=[jax.ShapeDtypeStruct((n, 256), F32), jax.ShapeDtypeStruct((n, 768), F32)],
        compiler_params=_cparams(("parallel",)),
        name="local",
    )(proj, proj, proj, proj, proj, proj, convw8.arr, wpool_bd.arr, cscale.arr)


def _gdn_kernel(n_steps, g_chunks, n_sq, has_s0, xf_ref, xb_ref, abf_ref, abb_ref, prow_ref, pcol_ref, *refs):
    if has_s0:
        s0f_ref, s0b_ref, of_ref, ob_ref, sfin_ref, s_ref = refs
    else:
        of_ref, ob_ref, sfin_ref, s_ref = refs
    i = pl.program_id(1)
    c = D_CHUNK
    nh = D_HEADS

    @pl.when(i == 0)
    def _():
        if has_s0:
            for sq in range(n_sq):
                s_ref[sq * nh:(sq + 1) * nh] = s0f_ref[sq]
                s_ref[(n_sq + sq) * nh:(n_sq + sq + 1) * nh] = s0b_ref[sq]
        else:
            s_ref[...] = jnp.zeros(s_ref.shape, F32)

    r = lax.broadcasted_iota(jnp.int32, (c, c), 0)
    cidx = lax.broadcasted_iota(jnp.int32, (c, c), 1)
    lower = jnp.where(cidx <= r, 1.0, 0.0).astype(BF16)
    upper = jnp.where(cidx >= r, 1.0, 0.0).astype(BF16)
    prow = prow_ref[...]
    pcol = pcol_ref[...]

    eye = jnp.where(cidx == r, 1.0, 0.0).astype(BF16)
    sel_lower = jnp.concatenate([lower] * 3, axis=1)
    sel_upper = jnp.concatenate([upper] * 3, axis=1)
    pick = jnp.where(lax.broadcasted_iota(jnp.int32, (16, LANES), 0) == lax.broadcasted_iota(jnp.int32, (16, LANES), 1),
                     1.0, 0.0).astype(BF16)
    sel_rows = jnp.concatenate([pick] * 3, axis=1)
    sel_lower_t = jnp.concatenate([lower] * 3, axis=0)
    sel_upper_t = jnp.concatenate([upper] * 3, axis=0)

    bdot = lambda a, b: jnp.einsum('bik,bkj->bij', a, b, preferred_element_type=F32)
    bdot_nt = lambda a, b: jnp.einsum('bik,bjk->bij', a, b, preferred_element_type=F32)

    def prepare(steps):
        nb = 2 * n_sq * len(steps) * nh
        is_fwd = lax.broadcasted_iota(jnp.int32, (nb, c, c), 0) < nb // 2
        r3 = lax.broadcasted_iota(jnp.int32, (nb, c, c), 1)
        c3 = lax.broadcasted_iota(jnp.int32, (nb, c, c), 2)
        ahead = jnp.where(is_fwd, r3 - c3, c3 - r3)
        incl = ahead >= 0
        strict = ahead > 0
        q_l, k_l, v_l, b_l, gcol_l, grow_l = [], [], [], [], [], []
        for d, sq, step in [(d, sq, step) for d in range(2) for sq in range(n_sq) for step in steps]:
            x_ref, ab_ref = ((xf_ref, abf_ref), (xb_ref, abb_ref))[d]
            gi = step if d == 0 else g_chunks - 1 - step
            rows = slice(gi * c, (gi + 1) * c)
            x = x_ref[sq, rows, :]
            ab = ab_ref[sq, rows, :]
            abr = _dot_nt(sel_rows, jnp.concatenate(_split3(ab), axis=1))
            beta_c = jax.nn.sigmoid(ab)
            g_c = -jnp.exp(prow[0:1, :]) * _softplus(ab + prow[1:2, :])
            g_r = -jnp.exp(pcol[:, 0:1]) * _softplus(abr + pcol[:, 1:2])
            if d == 0:
                gc_c = _dot(sel_lower, jnp.concatenate(_split3(g_c), axis=0))
                gc_r = _dot(jnp.concatenate(_split3(g_r), axis=1), sel_upper_t)
            else:
                gc_c = _dot(sel_upper, jnp.concatenate(_split3(g_c), axis=0))
                gc_r = _dot(jnp.concatenate(_split3(g_r), axis=1), sel_lower_t)
            for h in range(nh):
                hs = slice(h * D_DK, (h + 1) * D_DK)
                lane = nh * d + h
                q_l.append(x[:, 0:256][:, hs])
                k_l.append(x[:, 256:512][:, hs])
                v_l.append(x[:, 512:768][:, hs])
                b_l.append(beta_c[:, lane:lane + 1])
                gcol_l.append(gc_c[:, 8 + lane:9 + lane])
                grow_l.append(gc_r[8 + lane:9 + lane, :])
        qh, kh, vh = jnp.stack(q_l), jnp.stack(k_l), jnp.stack(v_l)
        bcol, gcol, grow = jnp.stack(b_l), jnp.stack(gcol_l), jnp.stack(grow_l)
        glast = jnp.concatenate([gcol[:nb // 2, c - 1:c, :], gcol[nb // 2:, 0:1, :]], axis=0)
        decay = jnp.exp(jnp.where(incl, gcol - grow, NEG_INF))
        kb = kh * bcol
        kq = bdot_nt(jnp.concatenate([kb, qh], axis=1).astype(BF16), kh.astype(BF16))
        p = -jnp.where(strict, kq[:, 0:c] * decay, 0.0)
        egc = jnp.exp(gcol)
        xs = jnp.concatenate([vh * bcol, kb * egc], axis=2)
        for it in range(6):
            ph = p.astype(BF16)
            pl_ = (p - ph.astype(F32)).astype(BF16)
            lhs = jnp.concatenate([ph, pl_, ph], axis=2)
            yh = xs.astype(BF16)
            yl = (xs - yh.astype(F32)).astype(BF16)
            if it < 5:
                yh = jnp.concatenate([yh, ph], axis=2)
                yl = jnp.concatenate([yl, pl_], axis=2)
            m = bdot(lhs, jnp.concatenate([yh, yh, yl], axis=1))
            xs = xs + m[:, :, 0:2 * D_DK]
            if it < 5:
                p = m[:, :, 2 * D_DK:3 * D_DK]
        u = xs[:, :, 0:D_DV]
        w = xs[:, :, D_DV:2 * D_DV]
        a = kq[:, c:2 * c] * decay
        kd = (kh * jnp.exp(glast - gcol)).astype(BF16)
        kdt = bdot_nt(jnp.broadcast_to(eye, (nb, c, c)), kd).astype(BF16)
        lhs_s = jnp.concatenate([w, qh * egc], axis=1).astype(BF16)
        lhs_v = jnp.concatenate([a.astype(BF16), kdt], axis=1)
        return u, lhs_s, lhs_v, jnp.exp(glast)

    prep = prepare(list(range(g_chunks)))

    def chains(arr, step):
        los = [((d * n_sq + sq) * g_chunks + step) * nh for d in range(2) for sq in range(n_sq)]
        return jnp.concatenate([arr[lo:lo + nh] for lo in los], axis=0)

    s = s_ref[...]
    outs = [[[None] * g_chunks for _ in range(n_sq)] for _ in range(2)]
    for step in range(g_chunks):
        u, lhs_s, lhs_v, eg = (chains(arr, step) for arr in prep)
        ws_qs = bdot(lhs_s, s.astype(BF16))
        vb = (u - ws_qs[:, 0:c]).astype(BF16)
        av = bdot(lhs_v, vb)
        o = ws_qs[:, c:2 * c] + av[:, 0:c]
        s = s * eg + av[:, c:2 * c]
        for d in range(2):
            gi = step if d == 0 else g_chunks - 1 - step
            for sq in range(n_sq):
                base = (d * n_sq + sq) * nh
                outs[d][sq][gi] = jnp.concatenate([o[base + h] for h in range(nh)], axis=1)
    for sq in range(n_sq):
        of_ref[sq] = jnp.concatenate(outs[0][sq], axis=0)
        ob_ref[sq] = jnp.concatenate(outs[1][sq], axis=0)
    s_ref[...] = s

    @pl.when(i == n_steps - 1)
    def _():
        for d in range(2):
            for sq in range(n_sq):
                sfin_ref[sq, d] = s[(d * n_sq + sq) * nh:(d * n_sq + sq + 1) * nh]


def _gdn_call(qkvn, proj, prow, pcol, s0, n_seq, t):
    g = GDN_CHUNKS_PER_STEP
    n_sq = math.gcd(n_seq, GDN_SEQS_PER_STEP)
    ns = t // (D_CHUNK * g)
    rows = D_CHUNK * g
    fwd = lambda i: i
    bwd = lambda i: ns - 1 - i
    x3 = qkvn.reshape(n_seq, t, 768)
    p3 = proj.reshape(n_seq, t, P1)
    xblk = lambda f: pl.BlockSpec((n_sq, rows, 768), lambda b, i: (b, f(i), 0))
    abblk = lambda f: pl.BlockSpec((n_sq, rows, 128), lambda b, i: (b, f(i), C_AB // 128))
    oblk = lambda f: pl.BlockSpec((n_sq, rows, 256), lambda b, i: (b, f(i), 0))
    sblk = pl.BlockSpec((n_sq, 2, D_HEADS, D_DK, D_DV), lambda b, i: (b, 0, 0, 0, 0))
    osd = jax.ShapeDtypeStruct((n_seq, t, 256), F32)
    in_specs = [xblk(fwd), xblk(bwd), abblk(fwd), abblk(bwd), _resident(prow), _resident(pcol)]
    args = [x3, x3, p3, p3, prow.arr, pcol.arr]
    if s0 is not None:
        for st in s0:
            in_specs.append(pl.BlockSpec((n_sq, None, D_HEADS, D_DK, D_DV), lambda b, i, l=st.l: (b, l, 0, 0, 0)))
            args.append(st.arr)
    o_f, o_b, s_fin = pl.pallas_call(
        functools.partial(_gdn_kernel, ns, g, n_sq, s0 is not None),
        grid=(n_seq // n_sq, ns),
        in_specs=in_specs,
        out_specs=[oblk(fwd), oblk(bwd), sblk],
        out_shape=[osd, osd, jax.ShapeDtypeStruct((n_seq, 2, D_HEADS, D_DK, D_DV), F32)],
        scratch_shapes=[pltpu.VMEM((2 * n_sq * D_HEADS, D_DK, D_DV), F32)],
        compiler_params=_cparams(("parallel", "arbitrary")),
        name="gdn",
    )(*args)
    return o_f.reshape(n_seq * t, 256), o_b.reshape(n_seq * t, 256), s_fin


def _row_groups(x):
    rows = x.shape[0] // DENSE_ROW_GROUPS
    return [x[g * rows:(g + 1) * rows] for g in range(DENSE_ROW_GROUPS)]


def _merge_kernel(x_ref, mod_ref, gpre_ref, oa_ref, ob_ref, oc_ref, of_ref, obw_ref, z_ref, gnorm_ref,
                  wg_ref, wbr_ref, wo_ref, gpost_ref, o_ref):
    d = D_MODEL
    mod = mod_ref[0]
    sh1, sc1, ga1 = mod[:, 0:d], mod[:, d:2 * d], mod[:, 2 * d:3 * d]
    xs = _row_groups(x_ref[...])
    hs = [(_rmsnorm_rows(xg, gpre_ref[...], d) * (1.0 + sc1) + sh1).astype(BF16) for xg in xs]
    ods = []
    for o, z in zip(_row_groups(of_ref[...] + obw_ref[...]), _row_groups(z_ref[...])):
        ms = _dot_sel_right(o * o, _group_ones(256, D_DV)) * (1.0 / D_DV)
        ods.append(o * lax.rsqrt(ms + EPS) * gnorm_ref[...] * _silu(z))
    branches = [_row_groups(oa_ref[...]), _row_groups(ob_ref[...]), _row_groups(oc_ref[...]), ods]
    accs = [None] * len(xs)
    for m in range(N_BRANCH):
        gates = [jax.nn.sigmoid(_dot(h, wg_ref[:, m * d:(m + 1) * d])) for h in hs]
        brs = [_dot(br.astype(BF16), wbr_ref[m]) for br in branches[m]]
        accs = [g * b if a is None else a + g * b for a, g, b in zip(accs, gates, brs)]
    mixes = [_dot(a.astype(BF16), wo_ref[...]) for a in accs]
    o_ref[...] = jnp.concatenate([xg + ga1 * _rmsnorm_rows(mix, gpost_ref[...], d) for xg, mix in zip(xs, mixes)],
                                 axis=0)


def _merge_call(x2d, mod3, gpre, oa, ob, oc, of, obw, proj, gnorm, wg, wbr, wo, gpost, seq_len, tm):
    n, d = x2d.shape
    tiles_per_seq = seq_len // tm
    row = lambda w, c=0: pl.BlockSpec((tm, w), lambda i: (i, c))
    full = _resident
    return pl.pallas_call(
        _merge_kernel,
        grid=(n // tm,),
        in_specs=[row(d), _mod_block(mod3, tiles_per_seq), full(gpre),
                  row(256), row(256), row(256), row(256), row(256), row(256, C_Z // 256), full(gnorm),
                  full(wg), full(wbr), full(wo), full(gpost)],
        out_specs=row(d),
        out_shape=jax.ShapeDtypeStruct((n, d), F32),
        compiler_params=_cparams(("parallel",)),
        name="merge",
    )(*_operands([x2d, mod3, gpre, oa, ob, oc, of, obw, proj, gnorm, wg, wbr, wo, gpost]))


def _ffn_kernel(x_ref, mod_ref, gpre_ref, wup_ref, wdown_ref, gpost_ref, o_ref):
    d = D_MODEL
    x = x_ref[...]
    mod = mod_ref[0]
    sh2, sc2, ga2 = mod[:, 3 * d:4 * d], mod[:, 4 * d:5 * d], mod[:, 5 * d:6 * d]
    xs = _row_groups(x)
    hs = [(_rmsnorm_rows(xg, gpre_ref[...], d) * (1.0 + sc2) + sh2).astype(BF16) for xg in xs]
    ups = [_dot(h, wup_ref[...]) for h in hs]
    acts = [(_silu(up[:, 0:D_FF]) * up[:, D_FF:2 * D_FF]).astype(BF16) for up in ups]
    fs = [_dot(act, wdown_ref[...]) for act in acts]
    o_ref[...] = jnp.concatenate([xg + ga2 * _rmsnorm_rows(f, gpost_ref[...], d) for xg, f in zip(xs, fs)], axis=0)


def _ffn_call(x2d, mod3, gpre, wup, wdown, gpost, seq_len, tm):
    n, d = x2d.shape
    tiles_per_seq = seq_len // tm
    row = lambda w: pl.BlockSpec((tm, w), lambda i: (i, 0))
    full = _resident
    return pl.pallas_call(
        _ffn_kernel,
        grid=(n // tm,),
        in_specs=[row(d), _mod_block(mod3, tiles_per_seq), full(gpre), full(wup), full(wdown), full(gpost)],
        out_specs=row(d),
        out_shape=jax.ShapeDtypeStruct((n, d), F32),
        compiler_params=_cparams(("parallel",)),
        name="ffn",
    )(*_operands([x2d, mod3, gpre, wup, wdown, gpost]))


def _rope_tables(n_tokens, dim, reps):
    n_rows = n_tokens // GRID_W
    row = jnp.repeat(jnp.arange(n_rows), GRID_W).astype(F32)
    col = jnp.tile(jnp.arange(GRID_W), n_rows).astype(F32)
    nfreq = dim // 4
    inv = ROPE_BASE ** (-jnp.arange(nfreq, dtype=F32) / nfreq)
    ang_r = row[:, None] * inv
    ang_c = col[:, None] * inv
    ang = jnp.concatenate([ang_r, ang_r, ang_c, ang_c], axis=-1)
    cos, sin = jnp.cos(ang), jnp.sin(ang)
    even = ((jnp.arange(dim) // nfreq) % 2 == 0)[None, :]
    sin_next = jnp.where(even, -sin, 0.0)
    sin_prev = jnp.where(even, 0.0, sin)
    return tuple(jnp.tile(a, (1, reps)) for a in (cos, sin_next, sin_prev))


def _rope_tables_mla_wide(n_tokens):
    cos, sin_next, sin_prev = _rope_tables(n_tokens, B_ROPE, 1)
    tail = MLA_HEAD_LANES - B_NOPE - B_ROPE
    wide = lambda a, fill: jnp.pad(a, ((0, 0), (B_NOPE, tail)), constant_values=fill)
    return wide(cos, 1.0), wide(sin_next, 0.0), wide(sin_prev, 0.0)


def _split_cols(w):
    out, acc = [], 0
    for s in IN_SIZES:
        out.append(w[..., acc:acc + s])
        acc += s
    return out


def _pack_w_in_kernel(w_ref, w1_ref, wg_ref):
    w = w_ref[...]
    (a_q, a_k, a_v, b_cq, b_ckv, b_kr, c_in, d_qkv, d_z, d_beta, d_alpha, gate) = _split_cols(w)
    z = lambda n: jnp.zeros((w.shape[0], n), w.dtype)
    k0, k1 = a_k[:, :A_HD], a_k[:, A_HD:]
    v0, v1 = a_v[:, :A_HD], a_v[:, A_HD:]
    w1 = jnp.concatenate([a_q, k0, k0, k1, k1, v0, v0, v1, v1, b_cq, z(256 - B_Q_RANK), b_ckv,
                          b_kr, b_kr, b_kr, b_kr, c_in, d_qkv, d_z, d_beta, d_alpha, z(112)], axis=-1)
    w1_ref[...] = w1.astype(BF16)
    wg_ref[...] = gate.astype(BF16)


def _pack_w_in(w):
    w = w.astype(BF16)
    depth, d, p = w.shape
    n_gate = IN_SIZES[-1]
    tr = 256
    return pl.pallas_call(
        _pack_w_in_kernel,
        grid=(depth, d // tr),
        in_specs=[pl.BlockSpec((None, tr, p), lambda l, i: (l, i, 0))],
        out_specs=[pl.BlockSpec((None, tr, P1), lambda l, i: (l, i, 0)),
                   pl.BlockSpec((None, tr, n_gate), lambda l, i: (l, i, 0))],
        out_shape=[jax.ShapeDtypeStruct((depth, d, P1), BF16), jax.ShapeDtypeStruct((depth, d, n_gate), BF16)],
        compiler_params=_cparams(("parallel", "parallel")),
        name="pack_w_in",
    )(w)


def _expand_kv_heads(a):
    return jnp.repeat(a, A_HEADS // A_KV_HEADS, axis=-2).reshape(a.shape[:-2] + (A_HEADS * A_HD,))


def _stacked_weights(g_pre1, g_post1, g_pre2, g_post2, w_in, a_sink, b_g_cq, b_g_ckv, b_w_uq, b_w_ukv,
                     c_w_pool, c_scale, d_conv, d_a_log, d_dt_bias, d_g_norm, w_br, w_o, w_up, w_down):
    depth = w_in.shape[0]
    w1, wg = _pack_w_in(w_in)
    wuq = b_w_uq.reshape(depth, B_Q_RANK, B_HEADS, B_NOPE + B_ROPE)
    wuq = jnp.pad(wuq, ((0, 0), (0, 256 - B_Q_RANK), (0, 0), (0, MLA_HEAD_LANES - B_NOPE - B_ROPE)))
    wuq = wuq.reshape(depth, 256, MLA_WIDE).astype(BF16)
    wukv = b_w_ukv.reshape(depth, B_KV_RANK, B_HEADS, B_NOPE + B_VD)
    wk = jnp.pad(wukv[..., :B_NOPE], ((0, 0), (0, 0), (0, 0), (0, MLA_HEAD_LANES - B_NOPE)))
    wukv = jnp.concatenate([wk.reshape(depth, B_KV_RANK, MLA_WIDE),
                            wukv[..., B_NOPE:].reshape(depth, B_KV_RANK, B_HEADS * B_VD)], -1).astype(BF16)
    same = jnp.eye(C_GROUPS, dtype=F32)[None, :, None, :, None]
    wpool = (c_w_pool[:, :, :, None, :] * same).reshape(depth, C_GROUPS * C_GW, C_GROUPS * C_GW).astype(BF16)
    gdn =jnp.stack([d_a_log.reshape(depth, 2 * D_HEADS), d_dt_bias.reshape(depth, 2 * D_HEADS)], axis=1)
    prow = jnp.pad(gdn, ((0, 0), (0, 0), (8, LANES - 16)))
    pcol = jnp.pad(gdn.transpose(0, 2, 1), ((0, 0), (8, 0), (0, 0)))
    row = lambda a: a[:, None, :]
    return dict(
        w1=w1, wg=wg, wuq=wuq, wukv=wukv,
        gpre1=row(g_pre1), gpost1=row(g_post1), gpre2=row(g_pre2), gpost2=row(g_post2),
        gcq=row(jnp.pad(b_g_cq, ((0, 0), (0, 256 - B_Q_RANK)))), gckv=row(b_g_ckv),
        sink=a_sink, wpool=wpool, cscale=row(c_scale),
        convw=jnp.pad(d_conv, ((0, 0), (0, SUBLANES - D_CONV), (0, 0))),
        prow=prow, pcol=pcol, gnorm=row(jnp.tile(d_g_norm, (1, D_HEADS))),
        wbr=w_br.astype(BF16), wo=w_o.astype(BF16), wup=w_up.astype(BF16), wdown=w_down.astype(BF16))


def _trunk_layer(x2d, mod3, lw, n_seq, t, ctx, tables=None):
    latent = ctx is not None
    n = x2d.shape[0]
    tm = min(512, t) if latent else 256
    tm_dense = min(512, n)
    if latent:
        proj, qw, kwr, vb, aqr, akr, qwr = _inproj_call(
            x2d, mod3, lw["gpre1"], lw["w1"], lw["gcq"], lw["wuq"], lw["gckv"], lw["wukv"], tables, t, tm_dense)
        kctx, vctx, ctx_ckv, ckr_wide, s0f, s0b = ctx
        out_a = _win_attn_call(lw["sink"], aqr, akr, proj, kctx, vctx, n_seq, t)
        out_b = _mla_lat_call(qw, qwr, kwr, vb, ctx_ckv, ckr_wide, lw["wukv"], n_seq, t, min(MLA_TQ, t))
        s0 = (s0f, s0b)
    else:
        proj, qw, kw, vb, ckvn, ak, av, kr = _inproj_call(
            x2d, mod3, lw["gpre1"], lw["w1"], lw["gcq"], lw["wuq"], lw["gckv"], lw["wukv"], None, t, tm_dense)
        out_a, out_b = _attn_ctx_call(lw["sink"], proj, qw, kw, vb, n_seq, t)
        s0 = None
    out_c, qkvn = _local_call(proj, lw["convw"], lw["wpool"], lw["cscale"], t, tm)
    o_f, o_b, s_fin = _gdn_call(qkvn, proj, lw["prow"], lw["pcol"], s0, n_seq, t)
    x1 = _merge_call(x2d, mod3, lw["gpre1"], out_a, out_b, out_c, o_f, o_b, proj, lw["gnorm"],
                     lw["wg"], lw["wbr"], lw["wo"], lw["gpost1"], t, tm_dense)
    x2 = _ffn_call(x1, mod3, lw["gpre2"], lw["wup"], lw["wdown"], lw["gpost2"], t, tm_dense)
    new_ctx = None
    if not latent:
        new_ctx = (ak.reshape(n_seq, t, A_KV_HEADS, A_HD), av.reshape(n_seq, t, A_KV_HEADS, A_HD),
                   ckvn.reshape(n_seq, t, B_KV_RANK), kr.reshape(n_seq, t, B_ROPE), s_fin[:, 0], s_fin[:, 1])
    return x2, new_ctx


def kernel(x_prompt, x_sample, cache_a_k, cache_a_v, cache_b_ckv, cache_b_krope, state_d_fwd, state_d_bwd, c, c_ctx, w_mod, b_mod, g_pre1, g_post1, g_pre2, g_post2, w_in, a_sink, b_g_cq, b_g_ckv, b_w_uq, b_w_ukv, c_w_pool, c_scale, d_conv, d_a_log, d_dt_bias, d_g_norm, w_br, w_o, w_up, w_down):
    depth = w_mod.shape[0]
    bp, tp, d = x_prompt.shape
    bs, ts, _ = x_sample.shape
    assert bs + 1 <= SUBLANES
    cond8 = jnp.zeros((SUBLANES, d), F32).at[0].set(c_ctx).at[1:1 + bs].set(c)
    mod_all = _mod_call(cond8, w_mod, b_mod).reshape(depth, SUBLANES, 1, 6 * d)
    yp = x_prompt.reshape(bp * tp, d)
    ys = x_sample.reshape(bs * ts, d)
    new = [[] for _ in range(6)]
    tables = _rope_tables(ts, A_HD, LANES // A_HD) + _rope_tables_mla_wide(ts)
    kctx_all, vctx_all = _expand_kv_heads(cache_a_k), _expand_kv_heads(cache_a_v)
    ckr_wide_all = jnp.tile(jnp.pad(cache_b_krope, ((0, 0), (0, 0), (0, 0),
                                                    (B_NOPE, MLA_HEAD_LANES - B_NOPE - B_ROPE))), (1, 1, 1, B_HEADS))
    stacked = _stacked_weights(g_pre1, g_post1, g_pre2, g_post2, w_in, a_sink, b_g_cq, b_g_ckv, b_w_uq, b_w_ukv,
                               c_w_pool, c_scale, d_conv, d_a_log, d_dt_bias, d_g_norm, w_br, w_o, w_up, w_down)
    for l in range(depth):
        lw = {k: _Layer(v, l) for k, v in stacked.items()}
        mod_ctx = _Mod(mod_all, l, 0, 1)
        mod_lat = _Mod(mod_all, l, 1, bs)
        yp, nctx = _trunk_layer(yp, mod_ctx, lw, bp, tp, None)
        for acc, val in zip(new, nctx):
            acc.append(val)
        ctx = tuple(_Layer(a, l) for a in (kctx_all, vctx_all, cache_b_ckv, ckr_wide_all, state_d_fwd, state_d_bwd))
        ys, _ = _trunk_layer(ys, mod_lat, lw, bs, ts, ctx, tables)
    return (yp.reshape(bp, tp, d), ys.reshape(bs, ts, d)) + tuple(jnp.stack(v, axis=1) for v in new)
```

```python
import functools
import math
from typing import NamedTuple

import jax
import jax.numpy as jnp
from jax import lax
from jax.experimental import pallas as pl
from jax.experimental.pallas import tpu as pltpu

F32 = jnp.float32
BF16 = jnp.bfloat16

D_MODEL = 1024
GRID_W = 64
ROPE_BASE = 10000.0
EPS = 1e-6
NEG_INF = -1e30
Q_BLOCK = 128
N_BRANCH = 4
BRANCH_W = 256
A_HEADS = 4
A_KV_HEADS = 2
A_HD = 64
A_WINDOW = 128
A_SCALE = A_HD ** -0.5
B_HEADS = 4
B_Q_RANK = 192
B_KV_RANK = 128
B_NOPE = 64
B_ROPE = 32
B_VD = 64
B_SCALE = (B_NOPE + B_ROPE) ** -0.5
C_GROUPS = 4
C_GW = 64
C_WINDOWS = (2, 4, 8, 16)
D_HEADS = 4
D_DK = 64
D_DV = 64
D_CONV = 5
D_CHUNK = 64
FF_RAW = -(-8 * D_MODEL // 3)
D_FF = -(-FF_RAW // 256) * 256
IN_SIZES = (A_HEADS * A_HD, A_KV_HEADS * A_HD, A_KV_HEADS * A_HD, B_Q_RANK, B_KV_RANK, B_ROPE,
            C_GROUPS * C_GW, D_HEADS * (2 * D_DK + D_DV), D_HEADS * D_DV, 2 * D_HEADS, 2 * D_HEADS,
            N_BRANCH * D_MODEL)

LANES = 128
SUBLANES = 8
VMEM_LIMIT_BYTES = 56 * 1024 * 1024
HALO = SUBLANES
GDN_SEQS_PER_STEP = 2
GDN_CHUNKS_PER_STEP = 4
MLA_HEAD_LANES = LANES
MLA_WIDE = B_HEADS * MLA_HEAD_LANES
LOG2E = math.log2(math.e)
DENSE_ROW_GROUPS = 2
MLA_TQ = 512
MLA_ROW_GROUP = 32
MLA_COL_BLOCK = 256
WIN_ATTN_TQ = 512

C_AQ = 0
C_AK = 256
C_AV = 512
C_QKV = 768
C_CIN = 1536
C_Z = 1792
C_AB = 2048
P1_OUT = 2176
C_CQ = 2176
C_CKV = 2432
C_KR = 2560
P1 = 2688


def _cparams(sem):
    return pltpu.CompilerParams(dimension_semantics=sem, vmem_limit_bytes=VMEM_LIMIT_BYTES)


class _Layer(NamedTuple):
    arr: jax.Array
    l: int


def _resident(p):
    shape = p.arr.shape[1:]
    return pl.BlockSpec((None,) + shape, lambda *_: (p.l,) + (0,) * len(shape), pipeline_mode=pl.Buffered(1))


class _Mod(NamedTuple):
    arr: jax.Array
    l: int
    first: int
    count: int


def _mod_block(m, tiles_per_seq):
    width = m.arr.shape[-1]
    if m.count == 1:
        return pl.BlockSpec((None, 1, 1, width), lambda i: (m.l, m.first, 0, 0))
    return pl.BlockSpec((None, 1, 1, width), lambda i: (m.l, m.first + i // tiles_per_seq, 0, 0))


def _cache_block(p):
    return pl.BlockSpec((1, None) + p.arr.shape[2:], lambda b, i: (b, p.l, 0, 0))


def _operands(args):
    return [a.arr if isinstance(a, (_Layer, _Mod)) else a for a in args]


def _dot(a, b):
    return jnp.dot(a, b, preferred_element_type=F32)


def _dot_nt(a, b):
    return lax.dot_general(a, b, (((1,), (1,)), ((), ())), preferred_element_type=F32)


def _split3(x):
    hi = x.astype(BF16)
    r1 = x - hi.astype(F32)
    mid = r1.astype(BF16)
    lo = (r1 - mid.astype(F32)).astype(BF16)
    return hi, mid, lo


def _dot_sel_right(x, sel):
    hi, mid, lo = _split3(x)
    return _dot(hi, sel) + _dot(mid, sel) + _dot(lo, sel)


def _silu(x):
    return x * jax.nn.sigmoid(x)


def _softplus(x):
    return jnp.maximum(x, 0.0) + jnp.log1p(jnp.exp(-jnp.abs(x)))


def _head_mask(width, n_lanes, h):
    lane = lax.broadcasted_iota(jnp.int32, (1, n_lanes), 1)
    return (lane // width) == h


def _group_ones(n, width):
    r = lax.broadcasted_iota(jnp.int32, (n, n), 0) // width
    c = lax.broadcasted_iota(jnp.int32, (n, n), 1) // width
    return jnp.where(r == c, 1.0, 0.0).astype(BF16)


def _mod_kernel(c_ref, w_ref, b_ref, o_ref):
    s = _silu(c_ref[...])
    o_ref[0] = _dot(s.astype(BF16), w_ref[0].astype(BF16)) + b_ref[0]


def _mod_call(cond8, w_mod, b_mod):
    depth, d, n = w_mod.shape
    tn = 1536
    return pl.pallas_call(
        _mod_kernel,
        grid=(depth, n // tn),
        in_specs=[pl.BlockSpec((SUBLANES, d), lambda l, j: (0, 0)),
                  pl.BlockSpec((1, d, tn), lambda l, j: (l, 0, j)),
                  pl.BlockSpec((1, 1, tn), lambda l, j: (l, 0, j))],
        out_specs=pl.BlockSpec((1, SUBLANES, tn), lambda l, j: (l, 0, j)),
        out_shape=jax.ShapeDtypeStruct((depth, SUBLANES, n), F32),
        compiler_params=_cparams(("parallel", "parallel")),
        name="mod",
    )(cond8, w_mod, b_mod.reshape(depth, 1, n))


def _rmsnorm_rows(x, g, n):
    ms = jnp.sum(x * x, axis=-1, keepdims=True) * (1.0 / n)
    return x * lax.rsqrt(ms + EPS) * g


def _rope(x, cos, sin_next, sin_prev, quarter):
    n = x.shape[-1]
    return x * cos + pltpu.roll(x, n - quarter, 1) * sin_next + pltpu.roll(x, quarter, 1) * sin_prev


def _mla_wide_kr(kr_rep):
    lane = lax.broadcasted_iota(jnp.int32, (1, MLA_HEAD_LANES), 1)
    keep = (lane >= B_NOPE) & (lane < B_NOPE + B_ROPE)
    return jnp.concatenate([jnp.where(keep, kr_rep, 0.0)] * B_HEADS, axis=1)


def _inproj_kernel(latent, *refs):
    if latent:
        (x_ref, mod_ref, gpre_ref, w1_ref, gcq_ref, wuq_ref, gckv_ref, wukv_ref,
         ca_ref, sna_ref, spa_ref, cb_ref, snb_ref, spb_ref,
         proj_ref, qw_ref, kw_ref, vb_ref, aqr_ref, akr_ref, qwr_ref) = refs
    else:
        (x_ref, mod_ref, gpre_ref, w1_ref, gcq_ref, wuq_ref, gckv_ref, wukv_ref,
         proj_ref, qw_ref, kw_ref, vb_ref, ckv_ref, ak_ref, av_ref, kr_ref) = refs
    d = D_MODEL
    x = x_ref[...]
    mod = mod_ref[0]
    sh1 = mod[:, 0:d]
    sc1 = mod[:, d:2 * d]
    h = _rmsnorm_rows(x, gpre_ref[...], d) * (1.0 + sc1) + sh1
    proj = _dot(h.astype(BF16), w1_ref[...])
    proj_ref[...] = proj[:, :P1_OUT]
    cqn = _rmsnorm_rows(proj[:, C_CQ:C_CQ + 256], gcq_ref[...], B_Q_RANK)
    qw = _dot(cqn.astype(BF16), wuq_ref[...]) * (B_SCALE * LOG2E)
    qw_ref[...] = qw.astype(BF16)
    ckvn = _rmsnorm_rows(proj[:, C_CKV:C_CKV + B_KV_RANK], gckv_ref[...], B_KV_RANK)
    if not latent:
        unexpand = lambda c: jnp.concatenate([proj[:, c:c + A_HD], proj[:, c + 2 * A_HD:c + 3 * A_HD]], axis=1)
        ckv_ref[...] = ckvn
        ak_ref[...] = unexpand(C_AK)
        av_ref[...] = unexpand(C_AV)
        kr_ref[...] = proj[:, C_KR:C_KR + B_ROPE]
    knv =_dot(ckvn.astype(BF16), wukv_ref[...])
    vb_ref[...] = knv[:, MLA_WIDE:].astype(BF16)
    kr_wide = _mla_wide_kr(proj[:, C_KR:C_KR + 128])
    if latent:
        ca, sna, spa = (jnp.concatenate([r[...]] * (256 // LANES), axis=1) for r in (ca_ref, sna_ref, spa_ref))
        cb, snb, spb = (jnp.concatenate([r[...]] * B_HEADS, axis=1) for r in (cb_ref, snb_ref, spb_ref))
        aqr_ref[...] = _rope(proj[:, C_AQ:C_AQ + 256], ca, sna, spa, A_HD // 4).astype(BF16)
        akr_ref[...] = _rope(proj[:, C_AK:C_AK + 256], ca, sna, spa, A_HD // 4).astype(BF16)
        qwr_ref[...] = _rope(qw, cb, snb, spb, B_ROPE // 4).astype(BF16)
        kr_wide = _rope(kr_wide, cb, snb, spb, B_ROPE // 4)
    kw_ref[...] = (knv[:, :MLA_WIDE] + kr_wide).astype(BF16)


def _inproj_call(x2d, mod3, gpre, w1, gcq, wuq, gckv, wukv, tables, seq_len, tm):
    n, d = x2d.shape
    latent = tables is not None
    tiles_per_seq = seq_len // tm if latent else 1
    row = lambda w, dt=F32: (pl.BlockSpec((tm, w), lambda i: (i, 0)), jax.ShapeDtypeStruct((n, w), dt))
    params = [gpre, w1, gcq, wuq, gckv, wukv]
    in_specs = [pl.BlockSpec((tm, d), lambda i: (i, 0)),
                _mod_block(mod3, seq_len // tm)] + [_resident(p) for p in params]
    args = [x2d, mod3] + params
    outs = [row(P1_OUT), row(MLA_WIDE, BF16), row(MLA_WIDE, BF16), row(256, BF16)]
    if latent:
        for t in tables:
            in_specs.append(pl.BlockSpec((tm, t.shape[1]), lambda i: (i % tiles_per_seq, 0)))
            args.append(t)
        outs += [row(256, BF16), row(256, BF16), row(MLA_WIDE, BF16)]
    else:
        outs += [row(B_KV_RANK), row(A_KV_HEADS * A_HD), row(A_KV_HEADS * A_HD), row(B_ROPE)]
    return pl.pallas_call(
        functools.partial(_inproj_kernel, latent),
        grid=(n // tm,),
        in_specs=in_specs,
        out_specs=[o[0] for o in outs],
        out_shape=[o[1] for o in outs],
        compiler_params=_cparams(("parallel",)),
        name="inproj_lat" if latent else "inproj_ctx",
    )(*_operands(args))


def _softmax_pv(scores, values, scale, sink, hm):
    c1 = None if scale is None else scale * LOG2E
    m = jnp.max(scores[0], axis=-1, keepdims=True)
    for s in scores[1:]:
        m = jnp.maximum(m, jnp.max(s, axis=-1, keepdims=True))
    m2 = m if c1 is None else m * c1
    if sink is not None:
        m2 = jnp.maximum(m2, sink * LOG2E)
    den = None
    pv = None
    for s, v in zip(scores, values):
        p = jnp.exp2((s if c1 is None else s * c1) - m2)
        ps = jnp.sum(p, axis=-1, keepdims=True)
        den = ps if den is None else den + ps
        t = _dot(p.astype(BF16), v)
        pv = t if pv is None else pv + t
    if sink is not None:
        den = den + jnp.exp2(sink * LOG2E - m2)
    return jnp.where(hm, pv / den, 0.0)


def _attn_ctx_kernel(layer, sink_ref, q_ref, k_ref, v_ref, qw_ref, kw_ref, vb_ref, oa_ref, ob_ref):
    q = q_ref[...]
    k = k_ref[...].astype(BF16)
    v = v_ref[...].astype(BF16)
    acc = jnp.zeros(q.shape, F32)
    for h in range(A_HEADS):
        hm = _head_mask(A_HD, 256, h)
        s = _dot_nt(jnp.where(hm, q, 0.0).astype(BF16), k)
        acc = acc + _softmax_pv([s], [v], A_SCALE, sink_ref[layer, h], hm)
    oa_ref[...] = acc
    vb = vb_ref[...]
    acc = jnp.zeros((q.shape[0], B_HEADS * B_VD), F32)
    for h in range(B_HEADS):
        hs = slice(h * MLA_HEAD_LANES, (h + 1) * MLA_HEAD_LANES)
        s = _dot_nt(qw_ref[:, hs], kw_ref[:, hs])
        acc = acc + _softmax_pv([s], [vb], None, None, _head_mask(B_VD, 256, h))
    ob_ref[...] = acc


def _attn_ctx_call(sink, proj, qw, kw, vb, n_seq, t):
    n = proj.shape[0]
    col = lambda w, j: pl.BlockSpec((t, w), lambda b: (b, j))
    out = (pl.BlockSpec((t, 256), lambda b: (b, 0)), jax.ShapeDtypeStruct((n, 256), F32))
    return pl.pallas_call(
        functools.partial(_attn_ctx_kernel, sink.l),
        grid=(n_seq,),
        in_specs=[pl.BlockSpec(memory_space=pltpu.SMEM),
                  col(256, C_AQ // 256), col(256, C_AK // 256), col(256, C_AV // 256),
                  col(MLA_WIDE, 0), col(MLA_WIDE, 0), col(256, 0)],
        out_specs=[out[0], out[0]],
        out_shape=[out[1], out[1]],
        compiler_params=_cparams(("parallel",)),
        name="attn_ctx",
    )(sink.arr, proj, proj, proj, qw, kw, vb)


def _win_attn_kernel(t, layer, sink_ref, qr_ref, q_ref, kp_ref, kc_ref, kn_ref, vp_ref, vc_ref, vn_ref,
                     kctx_ref, vctx_ref, o_ref):
    i = pl.program_id(1)
    qr = qr_ref[...]
    q = q_ref[...]
    tq = q.shape[0]
    kl = jnp.concatenate([kp_ref[...], kc_ref[...], kn_ref[...]], axis=0)
    vl = jnp.concatenate([vp_ref[...], vc_ref[...], vn_ref[...]], axis=0).astype(BF16)
    kctx = kctx_ref[0].astype(BF16)
    vctx = vctx_ref[0].astype(BF16)
    q_pos = i * tq + lax.broadcasted_iota(jnp.int32, (tq, tq + 2 * A_WINDOW), 0)
    k_pos = i * tq - A_WINDOW + lax.broadcasted_iota(jnp.int32, (tq, tq + 2 * A_WINDOW), 1)
    valid = (jnp.abs(q_pos - k_pos) <= A_WINDOW) & (k_pos >= 0) & (k_pos < t)
    acc = jnp.zeros(q.shape, F32)
    for h in range(A_HEADS):
        hm = _head_mask(A_HD, 256, h)
        s_loc = jnp.where(valid, _dot_nt(jnp.where(hm, qr, 0.0).astype(BF16), kl), NEG_INF)
        s_ctx = _dot_nt(jnp.where(hm, q, 0.0).astype(BF16), kctx)
        acc = acc + _softmax_pv([s_loc, s_ctx], [vl, vctx], A_SCALE, sink_ref[layer, h], hm)
    o_ref[...] = acc


def _win_attn_call(sink, aqr, akr, proj, kctx, vctx, n_seq, t):
    n = proj.shape[0]
    tq = min(WIN_ATTN_TQ, t)
    nq = t // tq
    r = tq // A_WINDOW
    nh = t // A_WINDOW
    cur = lambda j: pl.BlockSpec((tq, 256), lambda b, i: (b * nq + i, j))
    prev = lambda j: pl.BlockSpec((A_WINDOW, 256), lambda b, i: (b * nh + jnp.maximum(i * r - 1, 0), j))
    nxt = lambda j: pl.BlockSpec((A_WINDOW, 256), lambda b, i: (b * nh + jnp.minimum((i + 1) * r, nh - 1), j))
    ctx = _cache_block(kctx)
    return pl.pallas_call(
        functools.partial(_win_attn_kernel, t, sink.l),
        grid=(n_seq, nq),
        in_specs=[pl.BlockSpec(memory_space=pltpu.SMEM),
                  cur(0), cur(C_AQ // 256),
                  prev(0), cur(0), nxt(0),
                  prev(C_AV // 256), cur(C_AV // 256), nxt(C_AV // 256),
                  ctx, ctx],
        out_specs=pl.BlockSpec((tq, 256), lambda b, i: (b * nq + i, 0)),
        out_shape=jax.ShapeDtypeStruct((n, 256), F32),
        compiler_params=_cparams(("parallel", "parallel")),
        name="win_attn",
    )(sink.arr, aqr, proj, akr, akr, akr, proj, proj, proj, kctx.arr, vctx.arr)


def _mla_lat_kernel(qw_ref, qwr_ref, kwr_ref, vb_ref, cckv_ref, ckr_ref, wukv_ref, o_ref, s_ref, p_ref, den_ref):
    kvc = _dot(cckv_ref[0].astype(BF16), wukv_ref[...])
    kwc = (kvc[:, :MLA_WIDE] + ckr_ref[0]).astype(BF16)
    vc = kvc[:, MLA_WIDE:].astype(BF16)
    tq = qw_ref.shape[0]
    t = kwr_ref.shape[0]
    n_keys = s_ref.shape[2]
    cb = MLA_COL_BLOCK
    rg = MLA_ROW_GROUP

    def scores_into(slot, h):
        hs = slice(h * MLA_HEAD_LANES, (h + 1) * MLA_HEAD_LANES)
        s_ref[slot, :, 0:t] = _dot_nt(qwr_ref[:, hs], kwr_ref[:, hs])
        s_ref[slot, :, t:n_keys] = _dot_nt(qw_ref[:, hs], kwc[:, hs])

    scores_into(0, 0)
    acc = jnp.zeros((tq, B_HEADS * B_VD), F32)
    for h in range(B_HEADS):
        slot = h % 2
        if h + 1 < B_HEADS:
            scores_into(1 - slot, h + 1)
        for r0 in range(0, tq, rg):
            rows = slice(r0, r0 + rg)
            m = s_ref[slot, rows, 0:cb]
            for c0 in range(cb, n_keys, cb):
                m = jnp.maximum(m, s_ref[slot, rows, c0:c0 + cb])
            m2 = jnp.max(m, axis=-1, keepdims=True)
            part = None
            for c0 in range(0, n_keys, cb):
                p = jnp.exp2(s_ref[slot, rows, c0:c0 + cb] - m2)
                part = p if part is None else part + p
                p_ref[rows, c0:c0 + cb] = p.astype(BF16)
            den_ref[rows, :] = jnp.broadcast_to(jnp.sum(part, axis=-1, keepdims=True), (rg, LANES))
        pv = _dot(p_ref[:, 0:t], vb_ref[...]) + _dot(p_ref[:, t:n_keys], vc)
        acc = acc + jnp.where(_head_mask(B_VD, 256, h), pv / den_ref[:, 0:1], 0.0)
    o_ref[...] = acc


def _mla_lat_call(qw, qwr, kwr, vb, cckv, ckr_wide, wukv, n_seq, t, tq):
    n = qw.shape[0]
    nq = t // tq
    qblk = lambda w: pl.BlockSpec((tq, w), lambda b, i: (b * nq + i, 0))
    seq = lambda w: pl.BlockSpec((t, w), lambda b, i: (b, 0))
    ctx = _cache_block
    return pl.pallas_call(
        _mla_lat_kernel,
        grid=(n_seq, nq),
        in_specs=[qblk(MLA_WIDE), qblk(MLA_WIDE), seq(MLA_WIDE), seq(256), ctx(cckv), ctx(ckr_wide),
                  _resident(wukv)],
        out_specs=pl.BlockSpec((tq, 256), lambda b, i: (b * nq + i, 0)),
        out_shape=jax.ShapeDtypeStruct((n, 256), F32),
        scratch_shapes=[pltpu.VMEM((2, tq, t + cckv.arr.shape[2]), F32), pltpu.VMEM((tq, t + cckv.arr.shape[2]), BF16),
                        pltpu.VMEM((tq, LANES), F32)],
        compiler_params=_cparams(("parallel", "parallel")),
        name="mla_lat",
    )(qw, qwr, kwr, vb, cckv.arr, ckr_wide.arr, wukv.arr)


def _with_halo(prev_ref, cur_ref, next_ref, has_prev, has_next):
    prev = jnp.where(has_prev, prev_ref[...], 0.0)
    nxt = jnp.where(has_next, next_ref[...], 0.0)
    return jnp.concatenate([prev, cur_ref[...], nxt], axis=0)


def _shift_rows(x, k):
    n = x.shape[0]
    return pltpu.roll(x, (-k) % n, 0)


def _local_kernel(t, tm, cp_ref, cc_ref, cn_ref, qp_ref, qc_ref, qn_ref, convw_ref, wpool_ref, cscale_ref,
                  oc_ref, qkv_ref):
    i = pl.program_id(0)
    tiles_per_seq = t // tm
    j = i % tiles_per_seq
    has_prev = j > 0
    has_next = j < tiles_per_seq - 1
    x = _with_halo(cp_ref, cc_ref, cn_ref, has_prev, has_next)
    p2 = x + _shift_rows(x, -1)
    p4 = _shift_rows(p2, 1) + _shift_rows(p2, -1)
    p8 = _shift_rows(p4, 2) + _shift_rows(p4, -2)
    p16 = _shift_rows(p8, 4) + _shift_rows(p8, -4)
    grp = lax.broadcasted_iota(jnp.int32, (1, 256), 1) // C_GW
    win = jnp.where(grp == 0, p2, jnp.where(grp == 1, p4, jnp.where(grp == 2, p8, p16)))[HALO:HALO + tm]
    pos = j * tm + lax.broadcasted_iota(jnp.int32, (tm, 256), 0)
    half = jnp.where(grp == 0, 1, jnp.where(grp == 1, 2, jnp.where(grp == 2, 4, 8)))
    cnt = (jnp.minimum(pos + half, t) - jnp.maximum(pos - half, 0)).astype(F32)
    y = win / cnt - cc_ref[...]
    oc_ref[...] = _dot(y.astype(BF16), wpool_ref[...]) * cscale_ref[...]
    xq = _with_halo(qp_ref, qc_ref, qn_ref, has_prev, has_next)
    w = convw_ref[...]
    pad = D_CONV // 2
    acc = None
    for tap in range(D_CONV):
        term = _shift_rows(xq, tap - pad) * w[tap:tap + 1, :]
        acc = term if acc is None else acc + term
    u = _silu(acc[HALO:HALO + tm])
    ones = _group_ones(256, D_DK)
    q = u[:, 0:256]
    k = u[:, 256:512]
    q = q * lax.rsqrt(_dot_sel_right(q * q, ones) + EPS) * (D_DK ** -0.5)
    k = k * lax.rsqrt(_dot_sel_right(k * k, ones) + EPS)
    qkv_ref[:, 0:256] = q
    qkv_ref[:, 256:512] = k
    qkv_ref[:, 512:768] = u[:, 512:768]


def _local_call(proj, convw8, wpool_bd, cscale, t, tm):
    n = proj.shape[0]
    hb = tm // HALO
    nblk = n // HALO
    cur = lambda w, c: pl.BlockSpec((tm, w), lambda i: (i, c))
    prev = lambda w, c: pl.BlockSpec((HALO, w), lambda i: (jnp.maximum(i * hb - 1, 0), c))
    nxt = lambda w, c: pl.BlockSpec((HALO, w), lambda i: (jnp.minimum((i + 1) * hb, nblk - 1), c))
    full = _resident
    cc, qc = C_CIN // 256, C_QKV // 768
    return pl.pallas_call(
        functools.partial(_local_kernel, t, tm),
        grid=(n // tm,),
        in_specs=[prev(256, cc), cur(256, cc), nxt(256, cc), prev(768, qc), cur(768, qc), nxt(768, qc),
                  full(convw8), full(wpool_bd), full(cscale)],
        out_specs=[pl.BlockSpec((tm, 256), lambda i: (i, 0)), pl.BlockSpec((tm, 768), lambda i: (i, 0))],
        out_shape=[jax.ShapeDtypeStruct((n, 256), F32), jax.ShapeDtypeStruct((n, 768), F32)],
        compiler_params=_cparams(("parallel",)),
        name="local",
    )(proj, proj, proj, proj, proj, proj, convw8.arr, wpool_bd.arr, cscale.arr)


def _gdn_kernel(n_steps, g_chunks, n_sq, has_s0, xf_ref, xb_ref, abf_ref, abb_ref, prow_ref, pcol_ref, *refs):
    if has_s0:
        s0f_ref, s0b_ref, of_ref, ob_ref, sfin_ref, s_ref = refs
    else:
        of_ref, ob_ref, sfin_ref, s_ref = refs
    i = pl.program_id(1)
    c = D_CHUNK
    nh = D_HEADS

    @pl.when(i == 0)
    def _():
        if has_s0:
            for sq in range(n_sq):
                s_ref[sq * nh:(sq + 1) * nh] = s0f_ref[sq]
                s_ref[(n_sq + sq) * nh:(n_sq + sq + 1) * nh] = s0b_ref[sq]
        else:
            s_ref[...] = jnp.zeros(s_ref.shape, F32)

    r = lax.broadcasted_iota(jnp.int32, (c, c), 0)
    cidx = lax.broadcasted_iota(jnp.int32, (c, c), 1)
    lower = jnp.where(cidx <= r, 1.0, 0.0).astype(BF16)
    upper = jnp.where(cidx >= r, 1.0, 0.0).astype(BF16)
    prow = prow_ref[...]
    pcol = pcol_ref[...]

    eye = jnp.where(cidx == r, 1.0, 0.0).astype(BF16)
    sel_lower = jnp.concatenate([lower] * 3, axis=1)
    sel_upper = jnp.concatenate([upper] * 3, axis=1)
    pick = jnp.where(lax.broadcasted_iota(jnp.int32, (16, LANES), 0) == lax.broadcasted_iota(jnp.int32, (16, LANES), 1),
                     1.0, 0.0).astype(BF16)
    sel_rows = jnp.concatenate([pick] * 3, axis=1)
    sel_lower_t = jnp.concatenate([lower] * 3, axis=0)
    sel_upper_t = jnp.concatenate([upper] * 3, axis=0)

    bdot = lambda a, b: jnp.einsum('bik,bkj->bij', a, b, preferred_element_type=F32)
    bdot_nt = lambda a, b: jnp.einsum('bik,bjk->bij', a, b, preferred_element_type=F32)

    def prepare(steps):
        nb = 2 * n_sq * len(steps) * nh
        is_fwd = lax.broadcasted_iota(jnp.int32, (nb, c, c), 0) < nb // 2
        r3 = lax.broadcasted_iota(jnp.int32, (nb, c, c), 1)
        c3 = lax.broadcasted_iota(jnp.int32, (nb, c, c), 2)
        ahead = jnp.where(is_fwd, r3 - c3, c3 - r3)
        incl = ahead >= 0
        strict = ahead > 0
        q_l, k_l, v_l, b_l, gcol_l, grow_l = [], [], [], [], [], []
        for d, sq, step in [(d, sq, step) for d in range(2) for sq in range(n_sq) for step in steps]:
            x_ref, ab_ref = ((xf_ref, abf_ref), (xb_ref, abb_ref))[d]
            gi = step if d == 0 else g_chunks - 1 - step
            rows = slice(gi * c, (gi + 1) * c)
            x = x_ref[sq, rows, :]
            ab = ab_ref[sq, rows, :]
            abr = _dot_nt(sel_rows, jnp.concatenate(_split3(ab), axis=1))
            beta_c = jax.nn.sigmoid(ab)
            g_c = -jnp.exp(prow[0:1, :]) * _softplus(ab + prow[1:2, :])
            g_r = -jnp.exp(pcol[:, 0:1]) * _softplus(abr + pcol[:, 1:2])
            if d == 0:
                gc_c = _dot(sel_lower, jnp.concatenate(_split3(g_c), axis=0))
                gc_r = _dot(jnp.concatenate(_split3(g_r), axis=1), sel_upper_t)
            else:
                gc_c = _dot(sel_upper, jnp.concatenate(_split3(g_c), axis=0))
                gc_r = _dot(jnp.concatenate(_split3(g_r), axis=1), sel_lower_t)
            for h in range(nh):
                hs = slice(h * D_DK, (h + 1) * D_DK)
                lane = nh * d + h
                q_l.append(x[:, 0:256][:, hs])
                k_l.append(x[:, 256:512][:, hs])
                v_l.append(x[:, 512:768][:, hs])
                b_l.append(beta_c[:, lane:lane + 1])
                gcol_l.append(gc_c[:, 8 + lane:9 + lane])
                grow_l.append(gc_r[8 + lane:9 + lane, :])
        qh, kh, vh = jnp.stack(q_l), jnp.stack(k_l), jnp.stack(v_l)
        bcol, gcol, grow = jnp.stack(b_l), jnp.stack(gcol_l), jnp.stack(grow_l)
        glast = jnp.concatenate([gcol[:nb // 2, c - 1:c, :], gcol[nb // 2:, 0:1, :]], axis=0)
        decay = jnp.exp(jnp.where(incl, gcol - grow, NEG_INF))
        kb = kh * bcol
        kq = bdot_nt(jnp.concatenate([kb, qh], axis=1).astype(BF16), kh.astype(BF16))
        p = -jnp.where(strict, kq[:, 0:c] * decay, 0.0)
        egc = jnp.exp(gcol)
        xs = jnp.concatenate([vh * bcol, kb * egc], axis=2)
        for it in range(6):
            ph = p.astype(BF16)
            pl_ = (p - ph.astype(F32)).astype(BF16)
            lhs = jnp.concatenate([ph, pl_, ph], axis=2)
            yh = xs.astype(BF16)
            yl = (xs - yh.astype(F32)).astype(BF16)
            if it < 5:
                yh = jnp.concatenate([yh, ph], axis=2)
                yl = jnp.concatenate([yl, pl_], axis=2)
            m = bdot(lhs, jnp.concatenate([yh, yh, yl], axis=1))
            xs = xs + m[:, :, 0:2 * D_DK]
            if it < 5:
                p = m[:, :, 2 * D_DK:3 * D_DK]
        u = xs[:, :, 0:D_DV]
        w = xs[:, :, D_DV:2 * D_DV]
        a = kq[:, c:2 * c] * decay
        kd = (kh * jnp.exp(glast - gcol)).astype(BF16)
        kdt = bdot_nt(jnp.broadcast_to(eye, (nb, c, c)), kd).astype(BF16)
        lhs_s = jnp.concatenate([w, qh * egc], axis=1).astype(BF16)
        lhs_v = jnp.concatenate([a.astype(BF16), kdt], axis=1)
        return u, lhs_s, lhs_v, jnp.exp(glast)

    prep = prepare(list(range(g_chunks)))

    def chains(arr, step):
        los = [((d * n_sq + sq) * g_chunks + step) * nh for d in range(2) for sq in range(n_sq)]
        return jnp.concatenate([arr[lo:lo + nh] for lo in los], axis=0)

    s = s_ref[...]
    outs = [[[None] * g_chunks for _ in range(n_sq)] for _ in range(2)]
    for step in range(g_chunks):
        u, lhs_s, lhs_v, eg = (chains(arr, step) for arr in prep)
        ws_qs = bdot(lhs_s, s.astype(BF16))
        vb = (u - ws_qs[:, 0:c]).astype(BF16)
        av = bdot(lhs_v, vb)
        o = ws_qs[:, c:2 * c] + av[:, 0:c]
        s = s * eg + av[:, c:2 * c]
        for d in range(2):
            gi = step if d == 0 else g_chunks - 1 - step
            for sq in range(n_sq):
                base = (d * n_sq + sq) * nh
                outs[d][sq][gi] = jnp.concatenate([o[base + h] for h in range(nh)], axis=1)
    for sq in range(n_sq):
        of_ref[sq] = jnp.concatenate(outs[0][sq], axis=0)
        ob_ref[sq] = jnp.concatenate(outs[1][sq], axis=0)
    s_ref[...] = s

    @pl.when(i == n_steps - 1)
    def _():
        for d in range(2):
            for sq in range(n_sq):
                sfin_ref[sq, d] = s[(d * n_sq + sq) * nh:(d * n_sq + sq + 1) * nh]


def _gdn_call(qkvn, proj, prow, pcol, s0, n_seq, t):
    g = GDN_CHUNKS_PER_STEP
    n_sq = math.gcd(n_seq, GDN_SEQS_PER_STEP)
    ns = t // (D_CHUNK * g)
    rows = D_CHUNK * g
    fwd = lambda i: i
    bwd = lambda i: ns - 1 - i
    x3 = qkvn.reshape(n_seq, t, 768)
    p3 = proj.reshape(n_seq, t, P1_OUT)
    xblk = lambda f: pl.BlockSpec((n_sq, rows, 768), lambda b, i: (b, f(i), 0))
    abblk = lambda f: pl.BlockSpec((n_sq, rows, 128), lambda b, i: (b, f(i), C_AB // 128))
    oblk = lambda f: pl.BlockSpec((n_sq, rows, 256), lambda b, i: (b, f(i), 0))
    sblk = pl.BlockSpec((n_sq, 2, D_HEADS, D_DK, D_DV), lambda b, i: (b, 0, 0, 0, 0))
    osd = jax.ShapeDtypeStruct((n_seq, t, 256), F32)
    in_specs = [xblk(fwd), xblk(bwd), abblk(fwd), abblk(bwd), _resident(prow), _resident(pcol)]
    args = [x3, x3, p3, p3, prow.arr, pcol.arr]
    if s0 is not None:
        for st in s0:
            in_specs.append(pl.BlockSpec((n_sq, None, D_HEADS, D_DK, D_DV), lambda b, i, l=st.l: (b, l, 0, 0, 0)))
            args.append(st.arr)
    o_f, o_b, s_fin = pl.pallas_call(
        functools.partial(_gdn_kernel, ns, g, n_sq, s0 is not None),
        grid=(n_seq // n_sq, ns),
        in_specs=in_specs,
        out_specs=[oblk(fwd), oblk(bwd), sblk],
        out_shape=[osd, osd, jax.ShapeDtypeStruct((n_seq, 2, D_HEADS, D_DK, D_DV), F32)],
        scratch_shapes=[pltpu.VMEM((2 * n_sq * D_HEADS, D_DK, D_DV), F32)],
        compiler_params=_cparams(("parallel", "arbitrary")),
        name="gdn",
    )(*args)
    return o_f.reshape(n_seq * t, 256), o_b.reshape(n_seq * t, 256), s_fin


def _row_groups(x):
    rows = x.shape[0] // DENSE_ROW_GROUPS
    return [x[g * rows:(g + 1) * rows] for g in range(DENSE_ROW_GROUPS)]


def _merge_kernel(x_ref, mod_ref, gpre_ref, oa_ref, ob_ref, oc_ref, of_ref, obw_ref, z_ref, gnorm_ref,
                  wg_ref, wbr_ref, wo_ref, gpost_ref, o_ref):
    d = D_MODEL
    mod = mod_ref[0]
    sh1, sc1, ga1 = mod[:, 0:d], mod[:, d:2 * d], mod[:, 2 * d:3 * d]
    xs = _row_groups(x_ref[...])
    hs = [(_rmsnorm_rows(xg, gpre_ref[...], d) * (1.0 + sc1) + sh1).astype(BF16) for xg in xs]
    ods = []
    for o, z in zip(_row_groups(of_ref[...] + obw_ref[...]), _row_groups(z_ref[...])):
        ms = _dot_sel_right(o * o, _group_ones(256, D_DV)) * (1.0 / D_DV)
        ods.append(o * lax.rsqrt(ms + EPS) * gnorm_ref[...] * _silu(z))
    branches = [_row_groups(oa_ref[...]), _row_groups(ob_ref[...]), _row_groups(oc_ref[...]), ods]
    accs = [None] * len(xs)
    for m in range(N_BRANCH):
        gates = [jax.nn.sigmoid(_dot(h, wg_ref[:, m * d:(m + 1) * d])) for h in hs]
        brs = [_dot(br.astype(BF16), wbr_ref[m]) for br in branches[m]]
        accs = [g * b if a is None else a + g * b for a, g, b in zip(accs, gates, brs)]
    mixes = [_dot(a.astype(BF16), wo_ref[...]) for a in accs]
    o_ref[...] = jnp.concatenate([xg + ga1 * _rmsnorm_rows(mix, gpost_ref[...], d) for xg, mix in zip(xs, mixes)],
                                 axis=0)


def _merge_call(x2d, mod3, gpre, oa, ob, oc, of, obw, proj, gnorm, wg, wbr, wo, gpost, seq_len, tm):
    n, d = x2d.shape
    tiles_per_seq = seq_len // tm
    row = lambda w, c=0: pl.BlockSpec((tm, w), lambda i: (i, c))
    full = _resident
    return pl.pallas_call(
        _merge_kernel,
        grid=(n // tm,),
        in_specs=[row(d), _mod_block(mod3, tiles_per_seq), full(gpre),
                  row(256), row(256), row(256), row(256), row(256), row(256, C_Z // 256), full(gnorm),
                  full(wg), full(wbr), full(wo), full(gpost)],
        out_specs=row(d),
        out_shape=jax.ShapeDtypeStruct((n, d), F32),
        compiler_params=_cparams(("parallel",)),
        name="merge",
    )(*_operands([x2d, mod3, gpre, oa, ob, oc, of, obw, proj, gnorm, wg, wbr, wo, gpost]))


def _ffn_kernel(x_ref, mod_ref, gpre_ref, wup_ref, wdown_ref, gpost_ref, o_ref):
    d = D_MODEL
    x = x_ref[...]
    mod = mod_ref[0]
    sh2, sc2, ga2 = mod[:, 3 * d:4 * d], mod[:, 4 * d:5 * d], mod[:, 5 * d:6 * d]
    xs = _row_groups(x)
    hs = [(_rmsnorm_rows(xg, gpre_ref[...], d) * (1.0 + sc2) + sh2).astype(BF16) for xg in xs]
    ups = [_dot(h, wup_ref[...]) for h in hs]
    acts = [(_silu(up[:, 0:D_FF]) * up[:, D_FF:2 * D_FF]).astype(BF16) for up in ups]
    fs = [_dot(act, wdown_ref[...]) for act in acts]
    o_ref[...] = jnp.concatenate([xg + ga2 * _rmsnorm_rows(f, gpost_ref[...], d) for xg, f in zip(xs, fs)], axis=0)


def _ffn_call(x2d, mod3, gpre, wup, wdown, gpost, seq_len, tm):
    n, d = x2d.shape
    tiles_per_seq = seq_len // tm
    row = lambda w: pl.BlockSpec((tm, w), lambda i: (i, 0))
    full = _resident
    return pl.pallas_call(
        _ffn_kernel,
        grid=(n // tm,),
        in_specs=[row(d), _mod_block(mod3, tiles_per_seq), full(gpre), full(wup), full(wdown), full(gpost)],
        out_specs=row(d),
        out_shape=jax.ShapeDtypeStruct((n, d), F32),
        compiler_params=_cparams(("parallel",)),
        name="ffn",
    )(*_operands([x2d, mod3, gpre, wup, wdown, gpost]))


def _rope_tables(n_tokens, dim, reps):
    n_rows = n_tokens // GRID_W
    row = jnp.repeat(jnp.arange(n_rows), GRID_W).astype(F32)
    col = jnp.tile(jnp.arange(GRID_W), n_rows).astype(F32)
    nfreq = dim // 4
    inv = ROPE_BASE ** (-jnp.arange(nfreq, dtype=F32) / nfreq)
    ang_r = row[:, None] * inv
    ang_c = col[:, None] * inv
    ang = jnp.concatenate([ang_r, ang_r, ang_c, ang_c], axis=-1)
    cos, sin = jnp.cos(ang), jnp.sin(ang)
    even = ((jnp.arange(dim) // nfreq) % 2 == 0)[None, :]
    sin_next = jnp.where(even, -sin, 0.0)
    sin_prev = jnp.where(even, 0.0, sin)
    return tuple(jnp.tile(a, (1, reps)) for a in (cos, sin_next, sin_prev))


def _rope_tables_mla_wide(n_tokens):
    cos, sin_next, sin_prev = _rope_tables(n_tokens, B_ROPE, 1)
    tail = MLA_HEAD_LANES - B_NOPE - B_ROPE
    wide = lambda a, fill: jnp.pad(a, ((0, 0), (B_NOPE, tail)), constant_values=fill)
    return wide(cos, 1.0), wide(sin_next, 0.0), wide(sin_prev, 0.0)


def _split_cols(w):
    out, acc = [], 0
    for s in IN_SIZES:
        out.append(w[..., acc:acc + s])
        acc += s
    return out


def _pack_w_in_kernel(w_ref, w1_ref, wg_ref):
    w = w_ref[...]
    (a_q, a_k, a_v, b_cq, b_ckv, b_kr, c_in, d_qkv, d_z, d_beta, d_alpha, gate) = _split_cols(w)
    z = lambda n: jnp.zeros((w.shape[0], n), w.dtype)
    k0, k1 = a_k[:, :A_HD], a_k[:, A_HD:]
    v0, v1 = a_v[:, :A_HD], a_v[:, A_HD:]
    w1 = jnp.concatenate([a_q, k0, k0, k1, k1, v0, v0, v1, v1, d_qkv, c_in, d_z, d_beta, d_alpha, z(112),
                          b_cq, z(256 - B_Q_RANK), b_ckv, b_kr, b_kr, b_kr, b_kr], axis=-1)
    w1_ref[...] = w1.astype(BF16)
    wg_ref[...] = gate.astype(BF16)


def _pack_w_in(w):
    w = w.astype(BF16)
    depth, d, p = w.shape
    n_gate = IN_SIZES[-1]
    tr = 256
    return pl.pallas_call(
        _pack_w_in_kernel,
        grid=(depth, d // tr),
        in_specs=[pl.BlockSpec((None, tr, p), lambda l, i: (l, i, 0))],
        out_specs=[pl.BlockSpec((None, tr, P1), lambda l, i: (l, i, 0)),
                   pl.BlockSpec((None, tr, n_gate), lambda l, i: (l, i, 0))],
        out_shape=[jax.ShapeDtypeStruct((depth, d, P1), BF16), jax.ShapeDtypeStruct((depth, d, n_gate), BF16)],
        compiler_params=_cparams(("parallel", "parallel")),
        name="pack_w_in",
    )(w)


def _expand_kv_heads(a):
    return jnp.repeat(a, A_HEADS // A_KV_HEADS, axis=-2).reshape(a.shape[:-2] + (A_HEADS * A_HD,))


def _stacked_weights(g_pre1, g_post1, g_pre2, g_post2, w_in, a_sink, b_g_cq, b_g_ckv, b_w_uq, b_w_ukv,
                     c_w_pool, c_scale, d_conv, d_a_log, d_dt_bias, d_g_norm, w_br, w_o, w_up, w_down):
    depth = w_in.shape[0]
    w1, wg = _pack_w_in(w_in)
    wuq = b_w_uq.reshape(depth, B_Q_RANK, B_HEADS, B_NOPE + B_ROPE)
    wuq = jnp.pad(wuq, ((0, 0), (0, 256 - B_Q_RANK), (0, 0), (0, MLA_HEAD_LANES - B_NOPE - B_ROPE)))
    wuq = wuq.reshape(depth, 256, MLA_WIDE).astype(BF16)
    wukv = b_w_ukv.reshape(depth, B_KV_RANK, B_HEADS, B_NOPE + B_VD)
    wk = jnp.pad(wukv[..., :B_NOPE], ((0, 0), (0, 0), (0, 0), (0, MLA_HEAD_LANES - B_NOPE)))
    wukv = jnp.concatenate([wk.reshape(depth, B_KV_RANK, MLA_WIDE),
                            wukv[..., B_NOPE:].reshape(depth, B_KV_RANK, B_HEADS * B_VD)], -1).astype(BF16)
    same = jnp.eye(C_GROUPS, dtype=F32)[None, :, None, :, None]
    wpool = (c_w_pool[:, :, :, None, :] * same).reshape(depth, C_GROUPS * C_GW, C_GROUPS * C_GW).astype(BF16)
    gdn =jnp.stack([d_a_log.reshape(depth, 2 * D_HEADS), d_dt_bias.reshape(depth, 2 * D_HEADS)], axis=1)
    prow = jnp.pad(gdn, ((0, 0), (0, 0), (8, LANES - 16)))
    pcol = jnp.pad(gdn.transpose(0, 2, 1), ((0, 0), (8, 0), (0, 0)))
    row = lambda a: a[:, None, :]
    return dict(
        w1=w1, wg=wg, wuq=wuq, wukv=wukv,
        gpre1=row(g_pre1), gpost1=row(g_post1), gpre2=row(g_pre2), gpost2=row(g_post2),
        gcq=row(jnp.pad(b_g_cq, ((0, 0), (0, 256 - B_Q_RANK)))), gckv=row(b_g_ckv),
        sink=a_sink, wpool=wpool, cscale=row(c_scale),
        convw=jnp.pad(d_conv, ((0, 0), (0, SUBLANES - D_CONV), (0, 0))),
        prow=prow, pcol=pcol, gnorm=row(jnp.tile(d_g_norm, (1, D_HEADS))),
        wbr=w_br.astype(BF16), wo=w_o.astype(BF16), wup=w_up.astype(BF16), wdown=w_down.astype(BF16))


def _trunk_layer(x2d, mod3, lw, n_seq, t, ctx, tables=None):
    latent = ctx is not None
    n = x2d.shape[0]
    tm = min(512, t) if latent else 256
    tm_dense = min(512, n)
    if latent:
        proj, qw, kwr, vb, aqr, akr, qwr = _inproj_call(
            x2d, mod3, lw["gpre1"], lw["w1"], lw["gcq"], lw["wuq"], lw["gckv"], lw["wukv"], tables, t, tm_dense)
        kctx, vctx, ctx_ckv, ckr_wide, s0f, s0b = ctx
        out_a = _win_attn_call(lw["sink"], aqr, akr, proj, kctx, vctx, n_seq, t)
        out_b = _mla_lat_call(qw, qwr, kwr, vb, ctx_ckv, ckr_wide, lw["wukv"], n_seq, t, min(MLA_TQ, t))
        s0 = (s0f, s0b)
    else:
        proj, qw, kw, vb, ckvn, ak, av, kr = _inproj_call(
            x2d, mod3, lw["gpre1"], lw["w1"], lw["gcq"], lw["wuq"], lw["gckv"], lw["wukv"], None, t, tm_dense)
        out_a, out_b = _attn_ctx_call(lw["sink"], proj, qw, kw, vb, n_seq, t)
        s0 = None
    out_c, qkvn = _local_call(proj, lw["convw"], lw["wpool"], lw["cscale"], t, tm)
    o_f, o_b, s_fin = _gdn_call(qkvn, proj, lw["prow"], lw["pcol"], s0, n_seq, t)
    x1 = _merge_call(x2d, mod3, lw["gpre1"], out_a, out_b, out_c, o_f, o_b, proj, lw["gnorm"],
                     lw["wg"], lw["wbr"], lw["wo"], lw["gpost1"], t, tm_dense)
    x2 = _ffn_call(x1, mod3, lw["gpre2"], lw["wup"], lw["wdown"], lw["gpost2"], t, tm_dense)
    new_ctx = None
    if not latent:
        new_ctx = (ak.reshape(n_seq, t, A_KV_HEADS, A_HD), av.reshape(n_seq, t, A_KV_HEADS, A_HD),
                   ckvn.reshape(n_seq, t, B_KV_RANK), kr.reshape(n_seq, t, B_ROPE), s_fin[:, 0], s_fin[:, 1])
    return x2, new_ctx


def kernel(x_prompt, x_sample, cache_a_k, cache_a_v, cache_b_ckv, cache_b_krope, state_d_fwd, state_d_bwd, c, c_ctx, w_mod, b_mod, g_pre1, g_post1, g_pre2, g_post2, w_in, a_sink, b_g_cq, b_g_ckv, b_w_uq, b_w_ukv, c_w_pool, c_scale, d_conv, d_a_log, d_dt_bias, d_g_norm, w_br, w_o, w_up, w_down):
    depth = w_mod.shape[0]
    bp, tp, d = x_prompt.shape
    bs, ts, _ = x_sample.shape
    assert bs + 1 <= SUBLANES
    cond8 = jnp.zeros((SUBLANES, d), F32).at[0].set(c_ctx).at[1:1 + bs].set(c)
    mod_all = _mod_call(cond8, w_mod, b_mod).reshape(depth, SUBLANES, 1, 6 * d)
    yp = x_prompt.reshape(bp * tp, d)
    ys = x_sample.reshape(bs * ts, d)
    new = [[] for _ in range(6)]
    tables = _rope_tables(ts, A_HD, LANES // A_HD) + _rope_tables_mla_wide(ts)
    kctx_all, vctx_all = _expand_kv_heads(cache_a_k), _expand_kv_heads(cache_a_v)
    ckr_wide_all = jnp.tile(jnp.pad(cache_b_krope, ((0, 0), (0, 0), (0, 0),
                                                    (B_NOPE, MLA_HEAD_LANES - B_NOPE - B_ROPE))), (1, 1, 1, B_HEADS))
    stacked = _stacked_weights(g_pre1, g_post1, g_pre2, g_post2, w_in, a_sink, b_g_cq, b_g_ckv, b_w_uq, b_w_ukv,
                               c_w_pool, c_scale, d_conv, d_a_log, d_dt_bias, d_g_norm, w_br, w_o, w_up, w_down)
    for l in range(depth):
        lw = {k: _Layer(v, l) for k, v in stacked.items()}
        mod_ctx = _Mod(mod_all, l, 0, 1)
        mod_lat = _Mod(mod_all, l, 1, bs)
        yp, nctx = _trunk_layer(yp, mod_ctx, lw, bp, tp, None)
        for acc, val in zip(new, nctx):
            acc.append(val)
        ctx = tuple(_Layer(a, l) for a in (kctx_all, vctx_all, cache_b_ckv, ckr_wide_all, state_d_fwd, state_d_bwd))
        ys, _ = _trunk_layer(ys, mod_lat, lw, bs, ts, ctx, tables)
    return (yp.reshape(bp, tp, d), ys.reshape(bs, ts, d)) + tuple(jnp.stack(v, axis=1) for v in new)
```

```python
import functools
import math
from typing import NamedTuple

import jax
import jax.numpy as jnp
from jax import lax
from jax.experimental import pallas as pl
from jax.experimental.pallas import tpu as pltpu

F32 = jnp.float32
BF16 = jnp.bfloat16

D_MODEL = 1024
GRID_W = 64
ROPE_BASE = 10000.0
EPS = 1e-6
NEG_INF = -1e30
N_BRANCH = 4
A_HEADS = 4
A_KV_HEADS = 2
A_HD = 64
A_WINDOW = 128
A_SCALE = A_HD ** -0.5
B_HEADS = 4
B_Q_RANK = 192
B_KV_RANK = 128
B_NOPE = 64
B_ROPE = 32
B_VD = 64
B_SCALE = (B_NOPE + B_ROPE) ** -0.5
C_GROUPS = 4
C_GW = 64
D_HEADS = 4
D_DK = 64
D_DV = 64
D_CONV = 5
D_CHUNK = 64
FF_RAW = -(-8 * D_MODEL // 3)
D_FF = -(-FF_RAW // 256) * 256
IN_SIZES = (A_HEADS * A_HD, A_KV_HEADS * A_HD, A_KV_HEADS * A_HD, B_Q_RANK, B_KV_RANK, B_ROPE,
            C_GROUPS * C_GW, D_HEADS * (2 * D_DK + D_DV), D_HEADS * D_DV, 2 * D_HEADS, 2 * D_HEADS,
            N_BRANCH * D_MODEL)

LANES = 128
SUBLANES = 8
VMEM_LIMIT_BYTES = 56 * 1024 * 1024
HALO = SUBLANES
GDN_SEQS_PER_STEP = 2
GDN_CHUNKS_PER_STEP = 4
MLA_HEAD_LANES = LANES
MLA_WIDE = B_HEADS * MLA_HEAD_LANES
LOG2E = math.log2(math.e)
DENSE_ROW_GROUPS = 4
MLA_TQ = 512
MLA_ROW_GROUP = 32
MLA_COL_BLOCK = 256
WIN_ATTN_TQ = 512

C_AQ = 0
C_AK = 256
C_AV = 512
C_CQ = 768
C_CKV = 1024
C_KR = 1152
C_CIN = 1280
C_QKV = 1536
C_Z = 2304
C_AB = 2560
P1 = 2688


def _cparams(sem):
    return pltpu.CompilerParams(dimension_semantics=sem, vmem_limit_bytes=VMEM_LIMIT_BYTES)


class _Layer(NamedTuple):
    arr: jax.Array
    l: int


def _resident(p):
    shape = p.arr.shape[1:]
    return pl.BlockSpec((None,) + shape, lambda *_: (p.l,) + (0,) * len(shape), pipeline_mode=pl.Buffered(1))


class _Mod(NamedTuple):
    arr: jax.Array
    l: int
    first: int
    count: int


def _mod_block(m, tiles_per_seq):
    width = m.arr.shape[-1]
    if m.count == 1:
        return pl.BlockSpec((None, 1, 1, width), lambda i: (m.l, m.first, 0, 0))
    return pl.BlockSpec((None, 1, 1, width), lambda i: (m.l, m.first + i // tiles_per_seq, 0, 0))


def _cache_block(p):
    return pl.BlockSpec((1, None) + p.arr.shape[2:], lambda b, i: (b, p.l, 0, 0))


def _operands(args):
    return [a.arr if isinstance(a, (_Layer, _Mod)) else a for a in args]


def _dot(a, b):
    return jnp.dot(a, b, preferred_element_type=F32)


def _dot_nt(a, b):
    return lax.dot_general(a, b, (((1,), (1,)), ((), ())), preferred_element_type=F32)


def _split3(x):
    hi = x.astype(BF16)
    r1 = x - hi.astype(F32)
    mid = r1.astype(BF16)
    lo = (r1 - mid.astype(F32)).astype(BF16)
    return hi, mid, lo


def _dot_sel_right(x, sel):
    hi, mid, lo = _split3(x)
    return _dot(hi, sel) + _dot(mid, sel) + _dot(lo, sel)


def _silu(x):
    return x * jax.nn.sigmoid(x)


def _softplus(x):
    return jnp.maximum(x, 0.0) + jnp.log1p(jnp.exp(-jnp.abs(x)))


def _head_mask(width, n_lanes, h):
    lane = lax.broadcasted_iota(jnp.int32, (1, n_lanes), 1)
    return (lane // width) == h


def _group_ones(n, width):
    r = lax.broadcasted_iota(jnp.int32, (n, n), 0) // width
    c = lax.broadcasted_iota(jnp.int32, (n, n), 1) // width
    return jnp.where(r == c, 1.0, 0.0).astype(BF16)


def _mod_kernel(c_ref, w_ref, b_ref, o_ref):
    s = _silu(c_ref[...])
    o_ref[0] = _dot(s.astype(BF16), w_ref[0].astype(BF16)) + b_ref[0]


def _mod_call(cond8, w_mod, b_mod):
    depth, d, n = w_mod.shape
    tn = 1536
    return pl.pallas_call(
        _mod_kernel,
        grid=(depth, n // tn),
        in_specs=[pl.BlockSpec((SUBLANES, d), lambda l, j: (0, 0)),
                  pl.BlockSpec((1, d, tn), lambda l, j: (l, 0, j)),
                  pl.BlockSpec((1, 1, tn), lambda l, j: (l, 0, j))],
        out_specs=pl.BlockSpec((1, SUBLANES, tn), lambda l, j: (l, 0, j)),
        out_shape=jax.ShapeDtypeStruct((depth, SUBLANES, n), F32),
        compiler_params=_cparams(("parallel", "parallel")),
        name="mod",
    )(cond8, w_mod, b_mod.reshape(depth, 1, n))


def _rmsnorm_rows(x, g, n):
    ms = jnp.sum(x * x, axis=-1, keepdims=True) * (1.0 / n)
    return x * lax.rsqrt(ms + EPS) * g


def _rope(x, cos, sin_next, sin_prev, quarter):
    n = x.shape[-1]
    return x * cos + pltpu.roll(x, n - quarter, 1) * sin_next + pltpu.roll(x, quarter, 1) * sin_prev


def _mla_wide_kr(kr_rep):
    lane = lax.broadcasted_iota(jnp.int32, (1, MLA_HEAD_LANES), 1)
    keep = (lane >= B_NOPE) & (lane < B_NOPE + B_ROPE)
    return jnp.concatenate([jnp.where(keep, kr_rep, 0.0)] * B_HEADS, axis=1)


def _inproj_kernel(latent, *refs):
    if latent:
        (x_ref, mod_ref, gpre_ref, w1_ref, gcq_ref, wuq_ref, gckv_ref, wukv_ref,
         ca_ref, sna_ref, spa_ref, cb_ref, snb_ref, spb_ref,
         proj_ref, qw_ref, kw_ref, vb_ref, aqr_ref, akr_ref, qwr_ref) = refs
    else:
        (x_ref, mod_ref, gpre_ref, w1_ref, gcq_ref, wuq_ref, gckv_ref, wukv_ref,
         proj_ref, qw_ref, kw_ref, vb_ref, ckv_ref, ak_ref, av_ref, kr_ref) = refs
    d = D_MODEL
    x = x_ref[...]
    mod = mod_ref[0]
    sh1 = mod[:, 0:d]
    sc1 = mod[:, d:2 * d]
    h = _rmsnorm_rows(x, gpre_ref[...], d) * (1.0 + sc1) + sh1
    proj = _dot(h.astype(BF16), w1_ref[...])
    proj_ref[...] = proj
    cqn = _rmsnorm_rows(proj[:, C_CQ:C_CQ + 256], gcq_ref[...], B_Q_RANK)
    qw = _dot(cqn.astype(BF16), wuq_ref[...]) * (B_SCALE * LOG2E)
    qw_ref[...] = qw.astype(BF16)
    ckvn = _rmsnorm_rows(proj[:, C_CKV:C_CKV + B_KV_RANK], gckv_ref[...], B_KV_RANK)
    if not latent:
        unexpand = lambda c: jnp.concatenate([proj[:, c:c + A_HD], proj[:, c + 2 * A_HD:c + 3 * A_HD]], axis=1)
        ckv_ref[...] = ckvn
        ak_ref[...] = unexpand(C_AK)
        av_ref[...] = unexpand(C_AV)
        kr_ref[...] = proj[:, C_KR:C_KR + B_ROPE]
    knv =_dot(ckvn.astype(BF16), wukv_ref[...])
    vb_ref[...] = knv[:, MLA_WIDE:].astype(BF16)
    kr_wide = _mla_wide_kr(proj[:, C_KR:C_KR + 128])
    if latent:
        ca, sna, spa = (jnp.concatenate([r[...]] * (256 // LANES), axis=1) for r in (ca_ref, sna_ref, spa_ref))
        cb, snb, spb = (jnp.concatenate([r[...]] * B_HEADS, axis=1) for r in (cb_ref, snb_ref, spb_ref))
        aqr_ref[...] = _rope(proj[:, C_AQ:C_AQ + 256], ca, sna, spa, A_HD // 4).astype(BF16)
        akr_ref[...] = _rope(proj[:, C_AK:C_AK + 256], ca, sna, spa, A_HD // 4).astype(BF16)
        qwr_ref[...] = _rope(qw, cb, snb, spb, B_ROPE // 4).astype(BF16)
        kr_wide = _rope(kr_wide, cb, snb, spb, B_ROPE // 4)
    kw_ref[...] = (knv[:, :MLA_WIDE] + kr_wide).astype(BF16)


def _inproj_call(x2d, mod3, gpre, w1, gcq, wuq, gckv, wukv, tables, seq_len, tm):
    n, d = x2d.shape
    latent = tables is not None
    tiles_per_seq = seq_len // tm if latent else 1
    row = lambda w, dt=F32: (pl.BlockSpec((tm, w), lambda i: (i, 0)), jax.ShapeDtypeStruct((n, w), dt))
    params = [gpre, w1, gcq, wuq, gckv, wukv]
    in_specs = [pl.BlockSpec((tm, d), lambda i: (i, 0)),
                _mod_block(mod3, seq_len // tm)] + [_resident(p) for p in params]
    args = [x2d, mod3] + params
    outs = [row(P1), row(MLA_WIDE, BF16), row(MLA_WIDE, BF16), row(256, BF16)]
    if latent:
        for t in tables:
            in_specs.append(pl.BlockSpec((tm, t.shape[1]), lambda i: (i % tiles_per_seq, 0)))
            args.append(t)
        outs += [row(256, BF16), row(256, BF16), row(MLA_WIDE, BF16)]
    else:
        outs += [row(B_KV_RANK), row(A_KV_HEADS * A_HD), row(A_KV_HEADS * A_HD), row(B_ROPE)]
    return pl.pallas_call(
        functools.partial(_inproj_kernel, latent),
        grid=(n // tm,),
        in_specs=in_specs,
        out_specs=[o[0] for o in outs],
        out_shape=[o[1] for o in outs],
        compiler_params=_cparams(("parallel",)),
        name="inproj_lat" if latent else "inproj_ctx",
    )(*_operands(args))


def _softmax_pv(scores, values, scale, sink, hm):
    c1 = None if scale is None else scale * LOG2E
    m = jnp.max(scores[0], axis=-1, keepdims=True)
    for s in scores[1:]:
        m = jnp.maximum(m, jnp.max(s, axis=-1, keepdims=True))
    m2 = m if c1 is None else m * c1
    if sink is not None:
        m2 = jnp.maximum(m2, sink * LOG2E)
    den = None
    pv = None
    for s, v in zip(scores, values):
        p = jnp.exp2((s if c1 is None else s * c1) - m2)
        ps = jnp.sum(p, axis=-1, keepdims=True)
        den = ps if den is None else den + ps
        t = _dot(p.astype(BF16), v)
        pv = t if pv is None else pv + t
    if sink is not None:
        den = den + jnp.exp2(sink * LOG2E - m2)
    return jnp.where(hm, pv / den, 0.0)


def _attn_ctx_kernel(layer, sink_ref, q_ref, k_ref, v_ref, qw_ref, kw_ref, vb_ref, oa_ref, ob_ref):
    q = q_ref[...]
    k = k_ref[...].astype(BF16)
    v = v_ref[...].astype(BF16)
    acc = jnp.zeros(q.shape, F32)
    for h in range(A_HEADS):
        hm = _head_mask(A_HD, 256, h)
        s = _dot_nt(jnp.where(hm, q, 0.0).astype(BF16), k)
        acc = acc + _softmax_pv([s], [v], A_SCALE, sink_ref[layer, h], hm)
    oa_ref[...] = acc
    vb = vb_ref[...]
    acc = jnp.zeros((q.shape[0], B_HEADS * B_VD), F32)
    for h in range(B_HEADS):
        hs = slice(h * MLA_HEAD_LANES, (h + 1) * MLA_HEAD_LANES)
        s = _dot_nt(qw_ref[:, hs], kw_ref[:, hs])
        acc = acc + _softmax_pv([s], [vb], None, None, _head_mask(B_VD, 256, h))
    ob_ref[...] = acc


def _attn_ctx_call(sink, proj, qw, kw, vb, n_seq, t):
    n = proj.shape[0]
    col = lambda w, j: pl.BlockSpec((t, w), lambda b: (b, j))
    out = (pl.BlockSpec((t, 256), lambda b: (b, 0)), jax.ShapeDtypeStruct((n, 256), F32))
    return pl.pallas_call(
        functools.partial(_attn_ctx_kernel, sink.l),
        grid=(n_seq,),
        in_specs=[pl.BlockSpec(memory_space=pltpu.SMEM),
                  col(256, C_AQ // 256), col(256, C_AK // 256), col(256, C_AV // 256),
                  col(MLA_WIDE, 0), col(MLA_WIDE, 0), col(256, 0)],
        out_specs=[out[0], out[0]],
        out_shape=[out[1], out[1]],
        compiler_params=_cparams(("parallel",)),
        name="attn_ctx",
    )(sink.arr, proj, proj, proj, qw, kw, vb)


def _win_attn_kernel(t, layer, sink_ref, qr_ref, q_ref, kp_ref, kc_ref, kn_ref, vp_ref, vc_ref, vn_ref,
                     kctx_ref, vctx_ref, o_ref):
    i = pl.program_id(1)
    qr = qr_ref[...]
    q = q_ref[...]
    tq = q.shape[0]
    kl = jnp.concatenate([kp_ref[...], kc_ref[...], kn_ref[...]], axis=0)
    vl = jnp.concatenate([vp_ref[...], vc_ref[...], vn_ref[...]], axis=0).astype(BF16)
    kctx = kctx_ref[0].astype(BF16)
    vctx = vctx_ref[0].astype(BF16)
    q_pos = i * tq + lax.broadcasted_iota(jnp.int32, (tq, tq + 2 * A_WINDOW), 0)
    k_pos = i * tq - A_WINDOW + lax.broadcasted_iota(jnp.int32, (tq, tq + 2 * A_WINDOW), 1)
    valid = (jnp.abs(q_pos - k_pos) <= A_WINDOW) & (k_pos >= 0) & (k_pos < t)
    acc = jnp.zeros(q.shape, F32)
    for h in range(A_HEADS):
        hm = _head_mask(A_HD, 256, h)
        s_loc = jnp.where(valid, _dot_nt(jnp.where(hm, qr, 0.0).astype(BF16), kl), NEG_INF)
        s_ctx = _dot_nt(jnp.where(hm, q, 0.0).astype(BF16), kctx)
        acc = acc + _softmax_pv([s_loc, s_ctx], [vl, vctx], A_SCALE, sink_ref[layer, h], hm)
    o_ref[...] = acc


def _win_attn_call(sink, aqr, akr, proj, kctx, vctx, n_seq, t):
    n = proj.shape[0]
    tq = min(WIN_ATTN_TQ, t)
    nq = t // tq
    r = tq // A_WINDOW
    nh = t // A_WINDOW
    cur = lambda j: pl.BlockSpec((tq, 256), lambda b, i: (b * nq + i, j))
    prev = lambda j: pl.BlockSpec((A_WINDOW, 256), lambda b, i: (b * nh + jnp.maximum(i * r - 1, 0), j))
    nxt = lambda j: pl.BlockSpec((A_WINDOW, 256), lambda b, i: (b * nh + jnp.minimum((i + 1) * r, nh - 1), j))
    ctx = _cache_block(kctx)
    return pl.pallas_call(
        functools.partial(_win_attn_kernel, t, sink.l),
        grid=(n_seq, nq),
        in_specs=[pl.BlockSpec(memory_space=pltpu.SMEM),
                  cur(0), cur(C_AQ // 256),
                  prev(0), cur(0), nxt(0),
                  prev(C_AV // 256), cur(C_AV // 256), nxt(C_AV // 256),
                  ctx, ctx],
        out_specs=pl.BlockSpec((tq, 256), lambda b, i: (b * nq + i, 0)),
        out_shape=jax.ShapeDtypeStruct((n, 256), F32),
        compiler_params=_cparams(("parallel", "parallel")),
        name="win_attn",
    )(sink.arr, aqr, proj, akr, akr, akr, proj, proj, proj, kctx.arr, vctx.arr)


def _mla_lat_kernel(qw_ref, qwr_ref, kwr_ref, vb_ref, cckv_ref, ckr_ref, wukv_ref, o_ref, s_ref, p_ref, den_ref):
    kvc = _dot(cckv_ref[0].astype(BF16), wukv_ref[...])
    kwc = (kvc[:, :MLA_WIDE] + ckr_ref[0]).astype(BF16)
    vc = kvc[:, MLA_WIDE:].astype(BF16)
    tq = qw_ref.shape[0]
    t = kwr_ref.shape[0]
    n_keys = s_ref.shape[2]
    cb = MLA_COL_BLOCK
    rg = MLA_ROW_GROUP

    def scores_into(slot, h):
        hs = slice(h * MLA_HEAD_LANES, (h + 1) * MLA_HEAD_LANES)
        s_ref[slot, :, 0:t] = _dot_nt(qwr_ref[:, hs], kwr_ref[:, hs])
        s_ref[slot, :, t:n_keys] = _dot_nt(qw_ref[:, hs], kwc[:, hs])

    scores_into(0, 0)
    acc = jnp.zeros((tq, B_HEADS * B_VD), F32)
    for h in range(B_HEADS):
        slot = h % 2
        if h + 1 < B_HEADS:
            scores_into(1 - slot, h + 1)
        for r0 in range(0, tq, rg):
            rows = slice(r0, r0 + rg)
            m = s_ref[slot, rows, 0:cb]
            for c0 in range(cb, n_keys, cb):
                m = jnp.maximum(m, s_ref[slot, rows, c0:c0 + cb])
            m2 = jnp.max(m, axis=-1, keepdims=True)
            part = None
            for c0 in range(0, n_keys, cb):
                p = jnp.exp2(s_ref[slot, rows, c0:c0 + cb] - m2)
                part = p if part is None else part + p
                p_ref[rows, c0:c0 + cb] = p.astype(BF16)
            den_ref[rows, :] = jnp.broadcast_to(jnp.sum(part, axis=-1, keepdims=True), (rg, LANES))
        pv = _dot(p_ref[:, 0:t], vb_ref[...]) + _dot(p_ref[:, t:n_keys], vc)
        acc = acc + jnp.where(_head_mask(B_VD, 256, h), pv / den_ref[:, 0:1], 0.0)
    o_ref[...] = acc


def _mla_lat_call(qw, qwr, kwr, vb, cckv, ckr_wide, wukv, n_seq, t, tq):
    n = qw.shape[0]
    nq = t // tq
    qblk = lambda w: pl.BlockSpec((tq, w), lambda b, i: (b * nq + i, 0))
    seq = lambda w: pl.BlockSpec((t, w), lambda b, i: (b, 0))
    ctx = _cache_block
    return pl.pallas_call(
        _mla_lat_kernel,
        grid=(n_seq, nq),
        in_specs=[qblk(MLA_WIDE), qblk(MLA_WIDE), seq(MLA_WIDE), seq(256), ctx(cckv), ctx(ckr_wide),
                  _resident(wukv)],
        out_specs=pl.BlockSpec((tq, 256), lambda b, i: (b * nq + i, 0)),
        out_shape=jax.ShapeDtypeStruct((n, 256), F32),
        scratch_shapes=[pltpu.VMEM((2, tq, t + cckv.arr.shape[2]), F32), pltpu.VMEM((tq, t + cckv.arr.shape[2]), BF16),
                        pltpu.VMEM((tq, LANES), F32)],
        compiler_params=_cparams(("parallel", "parallel")),
        name="mla_lat",
    )(qw, qwr, kwr, vb, cckv.arr, ckr_wide.arr, wukv.arr)


def _with_halo(prev_ref, cur_ref, next_ref, has_prev, has_next):
    prev = jnp.where(has_prev, prev_ref[...], 0.0)
    nxt = jnp.where(has_next, next_ref[...], 0.0)
    return jnp.concatenate([prev, cur_ref[...], nxt], axis=0)


def _shift_rows(x, k):
    n = x.shape[0]
    return pltpu.roll(x, (-k) % n, 0)


def _local_kernel(t, tm, cp_ref, cc_ref, cn_ref, qp_ref, qc_ref, qn_ref, convw_ref, wpool_ref, cscale_ref,
                  oc_ref, qkv_ref):
    i = pl.program_id(0)
    tiles_per_seq = t // tm
    j = i % tiles_per_seq
    has_prev = j > 0
    has_next = j < tiles_per_seq - 1
    x = _with_halo(cp_ref, cc_ref, cn_ref, has_prev, has_next)
    p2 = x + _shift_rows(x, -1)
    p4 = _shift_rows(p2, 1) + _shift_rows(p2, -1)
    p8 = _shift_rows(p4, 2) + _shift_rows(p4, -2)
    p16 = _shift_rows(p8, 4) + _shift_rows(p8, -4)
    grp = lax.broadcasted_iota(jnp.int32, (1, 256), 1) // C_GW
    win = jnp.where(grp == 0, p2, jnp.where(grp == 1, p4, jnp.where(grp == 2, p8, p16)))[HALO:HALO + tm]
    pos = j * tm + lax.broadcasted_iota(jnp.int32, (tm, 256), 0)
    half = jnp.where(grp == 0, 1, jnp.where(grp == 1, 2, jnp.where(grp == 2, 4, 8)))
    cnt = (jnp.minimum(pos + half, t) - jnp.maximum(pos - half, 0)).astype(F32)
    y = win / cnt - cc_ref[...]
    oc_ref[...] = _dot(y.astype(BF16), wpool_ref[...]) * cscale_ref[...]
    xq = _with_halo(qp_ref, qc_ref, qn_ref, has_prev, has_next)
    w = convw_ref[...]
    pad = D_CONV // 2
    acc = None
    for tap in range(D_CONV):
        term = _shift_rows(xq, tap - pad) * w[tap:tap + 1, :]
        acc = term if acc is None else acc + term
    u = _silu(acc[HALO:HALO + tm])
    ones = _group_ones(256, D_DK)
    q = u[:, 0:256]
    k = u[:, 256:512]
    q = q * lax.rsqrt(_dot_sel_right(q * q, ones) + EPS) * (D_DK ** -0.5)
    k = k * lax.rsqrt(_dot_sel_right(k * k, ones) + EPS)
    qkv_ref[:, 0:256] = q
    qkv_ref[:, 256:512] = k
    qkv_ref[:, 512:768] = u[:, 512:768]


def _local_call(proj, convw8, wpool_bd, cscale, t, tm):
    n = proj.shape[0]
    hb = tm // HALO
    nblk = n // HALO
    cur = lambda w, c: pl.BlockSpec((tm, w), lambda i: (i, c))
    prev = lambda w, c: pl.BlockSpec((HALO, w), lambda i: (jnp.maximum(i * hb - 1, 0), c))
    nxt = lambda w, c: pl.BlockSpec((HALO, w), lambda i: (jnp.minimum((i + 1) * hb, nblk - 1), c))
    full = _resident
    cc, qc = C_CIN // 256, C_QKV // 768
    return pl.pallas_call(
        functools.partial(_local_kernel, t, tm),
        grid=(n // tm,),
        in_specs=[prev(256, cc), cur(256, cc), nxt(256, cc), prev(768, qc), cur(768, qc), nxt(768, qc),
                  full(convw8), full(wpool_bd), full(cscale)],
        out_specs=[pl.BlockSpec((tm, 256), lambda i: (i, 0)), pl.BlockSpec((tm, 768), lambda i: (i, 0))],
        out_shape=[jax.ShapeDtypeStruct((n, 256), F32), jax.ShapeDtypeStruct((n, 768), F32)],
        compiler_params=_cparams(("parallel",)),
        name="local",
    )(proj, proj, proj, proj, proj, proj, convw8.arr, wpool_bd.arr, cscale.arr)


def _gdn_kernel(n_steps, g_chunks, n_sq, has_s0, xf_ref, xb_ref, abf_ref, abb_ref, prow_ref, pcol_ref, *refs):
    if has_s0:
        s0f_ref, s0b_ref, of_ref, ob_ref, sfin_ref, s_ref = refs
    else:
        of_ref, ob_ref, sfin_ref, s_ref = refs
    i = pl.program_id(1)
    c = D_CHUNK
    nh = D_HEADS

    @pl.when(i == 0)
    def _():
        if has_s0:
            for sq in range(n_sq):
                s_ref[sq * nh:(sq + 1) * nh] = s0f_ref[sq]
                s_ref[(n_sq + sq) * nh:(n_sq + sq + 1) * nh] = s0b_ref[sq]
        else:
            s_ref[...] = jnp.zeros(s_ref.shape, F32)

    r = lax.broadcasted_iota(jnp.int32, (c, c), 0)
    cidx = lax.broadcasted_iota(jnp.int32, (c, c), 1)
    lower = jnp.where(cidx <= r, 1.0, 0.0).astype(BF16)
    upper = jnp.where(cidx >= r, 1.0, 0.0).astype(BF16)
    prow = prow_ref[...]
    pcol = pcol_ref[...]

    eye = jnp.where(cidx == r, 1.0, 0.0).astype(BF16)
    sel_lower = jnp.concatenate([lower] * 3, axis=1)
    sel_upper = jnp.concatenate([upper] * 3, axis=1)
    pick = jnp.where(lax.broadcasted_iota(jnp.int32, (16, LANES), 0) == lax.broadcasted_iota(jnp.int32, (16, LANES), 1),
                     1.0, 0.0).astype(BF16)
    sel_rows = jnp.concatenate([pick] * 3, axis=1)
    sel_lower_t = jnp.concatenate([lower] * 3, axis=0)
    sel_upper_t = jnp.concatenate([upper] * 3, axis=0)

    bdot = lambda a, b: jnp.einsum('bik,bkj->bij', a, b, preferred_element_type=F32)
    bdot_nt = lambda a, b: jnp.einsum('bik,bjk->bij', a, b, preferred_element_type=F32)

    def prepare(steps):
        nb = 2 * n_sq * len(steps) * nh
        is_fwd = lax.broadcasted_iota(jnp.int32, (nb, c, c), 0) < nb // 2
        r3 = lax.broadcasted_iota(jnp.int32, (nb, c, c), 1)
        c3 = lax.broadcasted_iota(jnp.int32, (nb, c, c), 2)
        ahead = jnp.where(is_fwd, r3 - c3, c3 - r3)
        incl = ahead >= 0
        strict = ahead > 0
        q_l, k_l, v_l, b_l, gcol_l, grow_l = [], [], [], [], [], []
        for d, sq, step in [(d, sq, step) for d in range(2) for sq in range(n_sq) for step in steps]:
            x_ref, ab_ref = ((xf_ref, abf_ref), (xb_ref, abb_ref))[d]
            gi = step if d == 0 else g_chunks - 1 - step
            rows = slice(gi * c, (gi + 1) * c)
            x = x_ref[sq, rows, :]
            ab = ab_ref[sq, rows, :]
            abr = _dot_nt(sel_rows, jnp.concatenate(_split3(ab), axis=1))
            beta_c = jax.nn.sigmoid(ab)
            g_c = -jnp.exp(prow[0:1, :]) * _softplus(ab + prow[1:2, :])
            g_r = -jnp.exp(pcol[:, 0:1]) * _softplus(abr + pcol[:, 1:2])
            if d == 0:
                gc_c = _dot(sel_lower, jnp.concatenate(_split3(g_c), axis=0))
                gc_r = _dot(jnp.concatenate(_split3(g_r), axis=1), sel_upper_t)
            else:
                gc_c = _dot(sel_upper, jnp.concatenate(_split3(g_c), axis=0))
                gc_r = _dot(jnp.concatenate(_split3(g_r), axis=1), sel_lower_t)
            for h in range(nh):
                hs = slice(h * D_DK, (h + 1) * D_DK)
                lane = nh * d + h
                q_l.append(x[:, 0:256][:, hs])
                k_l.append(x[:, 256:512][:, hs])
                v_l.append(x[:, 512:768][:, hs])
                b_l.append(beta_c[:, lane:lane + 1])
                gcol_l.append(gc_c[:, 8 + lane:9 + lane])
                grow_l.append(gc_r[8 + lane:9 + lane, :])
        qh, kh, vh = jnp.stack(q_l), jnp.stack(k_l), jnp.stack(v_l)
        bcol, gcol, grow = jnp.stack(b_l), jnp.stack(gcol_l), jnp.stack(grow_l)
        glast = jnp.concatenate([gcol[:nb // 2, c - 1:c, :], gcol[nb // 2:, 0:1, :]], axis=0)
        decay = jnp.exp(jnp.where(incl, gcol - grow, NEG_INF))
        kb = kh * bcol
        kq = bdot_nt(jnp.concatenate([kb, qh], axis=1).astype(BF16), kh.astype(BF16))
        p = -jnp.where(strict, kq[:, 0:c] * decay, 0.0)
        egc = jnp.exp(gcol)
        xs = jnp.concatenate([vh * bcol, kb * egc], axis=2)
        for it in range(6):
            ph = p.astype(BF16)
            pl_ = (p - ph.astype(F32)).astype(BF16)
            lhs = jnp.concatenate([ph, pl_, ph], axis=2)
            yh = xs.astype(BF16)
            yl = (xs - yh.astype(F32)).astype(BF16)
            if it < 5:
                yh = jnp.concatenate([yh, ph], axis=2)
                yl = jnp.concatenate([yl, pl_], axis=2)
            m = bdot(lhs, jnp.concatenate([yh, yh, yl], axis=1))
            xs = xs + m[:, :, 0:2 * D_DK]
            if it < 5:
                p = m[:, :, 2 * D_DK:3 * D_DK]
        u = xs[:, :, 0:D_DV]
        w = xs[:, :, D_DV:2 * D_DV]
        a = kq[:, c:2 * c] * decay
        kd = (kh * jnp.exp(glast - gcol)).astype(BF16)
        kdt = bdot_nt(jnp.broadcast_to(eye, (nb, c, c)), kd).astype(BF16)
        lhs_s = jnp.concatenate([w, qh * egc], axis=1).astype(BF16)
        lhs_v = jnp.concatenate([a.astype(BF16), kdt], axis=1)
        return u, lhs_s, lhs_v, jnp.exp(glast)

    prep = prepare(list(range(g_chunks)))

    def chains(arr, step):
        los = [((d * n_sq + sq) * g_chunks + step) * nh for d in range(2) for sq in range(n_sq)]
        return jnp.concatenate([arr[lo:lo + nh] for lo in los], axis=0)

    s = s_ref[...]
    outs = [[[None] * g_chunks for _ in range(n_sq)] for _ in range(2)]
    for step in range(g_chunks):
        u, lhs_s, lhs_v, eg = (chains(arr, step) for arr in prep)
        ws_qs = bdot(lhs_s, s.astype(BF16))
        vb = (u - ws_qs[:, 0:c]).astype(BF16)
        av = bdot(lhs_v, vb)
        o = ws_qs[:, c:2 * c] + av[:, 0:c]
        s = s * eg + av[:, c:2 * c]
        for d in range(2):
            gi = step if d == 0 else g_chunks - 1 - step
            for sq in range(n_sq):
                base = (d * n_sq + sq) * nh
                outs[d][sq][gi] = jnp.concatenate([o[base + h] for h in range(nh)], axis=1)
    for sq in range(n_sq):
        of_ref[sq] = jnp.concatenate(outs[0][sq], axis=0)
        ob_ref[sq] = jnp.concatenate(outs[1][sq], axis=0)
    s_ref[...] = s

    @pl.when(i == n_steps - 1)
    def _():
        for d in range(2):
            for sq in range(n_sq):
                sfin_ref[sq, d] = s[(d * n_sq + sq) * nh:(d * n_sq + sq + 1) * nh]


def _gdn_call(qkvn, proj, prow, pcol, s0, n_seq, t):
    g = GDN_CHUNKS_PER_STEP
    n_sq = math.gcd(n_seq, GDN_SEQS_PER_STEP)
    ns = t // (D_CHUNK * g)
    rows = D_CHUNK * g
    fwd = lambda i: i
    bwd = lambda i: ns - 1 - i
    x3 = qkvn.reshape(n_seq, t, 768)
    p3 = proj.reshape(n_seq, t, P1)
    xblk = lambda f: pl.BlockSpec((n_sq, rows, 768), lambda b, i: (b, f(i), 0))
    abblk = lambda f: pl.BlockSpec((n_sq, rows, 128), lambda b, i: (b, f(i), C_AB // 128))
    oblk = lambda f: pl.BlockSpec((n_sq, rows, 256), lambda b, i: (b, f(i), 0))
    sblk = pl.BlockSpec((n_sq, 2, D_HEADS, D_DK, D_DV), lambda b, i: (b, 0, 0, 0, 0))
    osd = jax.ShapeDtypeStruct((n_seq, t, 256), F32)
    in_specs = [xblk(fwd), xblk(bwd), abblk(fwd), abblk(bwd), _resident(prow), _resident(pcol)]
    args = [x3, x3, p3, p3, prow.arr, pcol.arr]
    if s0 is not None:
        for st in s0:
            in_specs.append(pl.BlockSpec((n_sq, None, D_HEADS, D_DK, D_DV), lambda b, i, l=st.l: (b, l, 0, 0, 0)))
            args.append(st.arr)
    o_f, o_b, s_fin = pl.pallas_call(
        functools.partial(_gdn_kernel, ns, g, n_sq, s0 is not None),
        grid=(n_seq // n_sq, ns),
        in_specs=in_specs,
        out_specs=[oblk(fwd), oblk(bwd), sblk],
        out_shape=[osd, osd, jax.ShapeDtypeStruct((n_seq, 2, D_HEADS, D_DK, D_DV), F32)],
        scratch_shapes=[pltpu.VMEM((2 * n_sq * D_HEADS, D_DK, D_DV), F32)],
        compiler_params=_cparams(("parallel", "arbitrary")),
        name="gdn",
    )(*args)
    return o_f.reshape(n_seq * t, 256), o_b.reshape(n_seq * t, 256), s_fin


def _row_groups(x):
    rows = x.shape[0] // DENSE_ROW_GROUPS
    return [x[g * rows:(g + 1) * rows] for g in range(DENSE_ROW_GROUPS)]


def _merge_kernel(x_ref, mod_ref, gpre_ref, oa_ref, ob_ref, oc_ref, of_ref, obw_ref, z_ref, gnorm_ref,
                  wg_ref, wbr_ref, wo_ref, gpost_ref, o_ref):
    d = D_MODEL
    mod = mod_ref[0]
    sh1, sc1, ga1 = mod[:, 0:d], mod[:, d:2 * d], mod[:, 2 * d:3 * d]
    xs = _row_groups(x_ref[...])
    hs = [(_rmsnorm_rows(xg, gpre_ref[...], d) * (1.0 + sc1) + sh1).astype(BF16) for xg in xs]
    ods = []
    for o, z in zip(_row_groups(of_ref[...] + obw_ref[...]), _row_groups(z_ref[...])):
        ms = _dot_sel_right(o * o, _group_ones(256, D_DV)) * (1.0 / D_DV)
        ods.append(o * lax.rsqrt(ms + EPS) * gnorm_ref[...] * _silu(z))
    branches = [_row_groups(oa_ref[...]), _row_groups(ob_ref[...]), _row_groups(oc_ref[...]), ods]
    accs = [None] * len(xs)
    for m in range(N_BRANCH):
        gates = [jax.nn.sigmoid(_dot(h, wg_ref[:, m * d:(m + 1) * d])) for h in hs]
        brs = [_dot(br.astype(BF16), wbr_ref[m]) for br in branches[m]]
        accs = [g * b if a is None else a + g * b for a, g, b in zip(accs, gates, brs)]
    mixes = [_dot(a.astype(BF16), wo_ref[...]) for a in accs]
    o_ref[...] = jnp.concatenate([xg + ga1 * _rmsnorm_rows(mix, gpost_ref[...], d) for xg, mix in zip(xs, mixes)],
                                 axis=0)


def _merge_call(x2d, mod3, gpre, oa, ob, oc, of, obw, proj, gnorm, wg, wbr, wo, gpost, seq_len, tm):
    n, d = x2d.shape
    tiles_per_seq = seq_len // tm
    row = lambda w, c=0: pl.BlockSpec((tm, w), lambda i: (i, c))
    full = _resident
    return pl.pallas_call(
        _merge_kernel,
        grid=(n // tm,),
        in_specs=[row(d), _mod_block(mod3, tiles_per_seq), full(gpre),
                  row(256), row(256), row(256), row(256), row(256), row(256, C_Z // 256), full(gnorm),
                  full(wg), full(wbr), full(wo), full(gpost)],
        out_specs=row(d),
        out_shape=jax.ShapeDtypeStruct((n, d), F32),
        compiler_params=_cparams(("parallel",)),
        name="merge",
    )(*_operands([x2d, mod3, gpre, oa, ob, oc, of, obw, proj, gnorm, wg, wbr, wo, gpost]))


def _ffn_kernel(x_ref, mod_ref, gpre_ref, wup_ref, wdown_ref, gpost_ref, o_ref):
    d = D_MODEL
    x = x_ref[...]
    mod = mod_ref[0]
    sh2, sc2, ga2 = mod[:, 3 * d:4 * d], mod[:, 4 * d:5 * d], mod[:, 5 * d:6 * d]
    xs = _row_groups(x)
    hs = [(_rmsnorm_rows(xg, gpre_ref[...], d) * (1.0 + sc2) + sh2).astype(BF16) for xg in xs]
    ups = [_dot(h, wup_ref[...]) for h in hs]
    acts = [(_silu(up[:, 0:D_FF]) * up[:, D_FF:2 * D_FF]).astype(BF16) for up in ups]
    fs = [_dot(act, wdown_ref[...]) for act in acts]
    o_ref[...] = jnp.concatenate([xg + ga2 * _rmsnorm_rows(f, gpost_ref[...], d) for xg, f in zip(xs, fs)], axis=0)


def _ffn_call(x2d, mod3, gpre, wup, wdown, gpost, seq_len, tm):
    n, d = x2d.shape
    tiles_per_seq = seq_len // tm
    row = lambda w: pl.BlockSpec((tm, w), lambda i: (i, 0))
    full = _resident
    return pl.pallas_call(
        _ffn_kernel,
        grid=(n // tm,),
        in_specs=[row(d), _mod_block(mod3, tiles_per_seq), full(gpre), full(wup), full(wdown), full(gpost)],
        out_specs=row(d),
        out_shape=jax.ShapeDtypeStruct((n, d), F32),
        compiler_params=_cparams(("parallel",)),
        name="ffn",
    )(*_operands([x2d, mod3, gpre, wup, wdown, gpost]))


def _rope_tables(n_tokens, dim, reps):
    n_rows = n_tokens // GRID_W
    row = jnp.repeat(jnp.arange(n_rows), GRID_W).astype(F32)
    col = jnp.tile(jnp.arange(GRID_W), n_rows).astype(F32)
    nfreq = dim // 4
    inv = ROPE_BASE ** (-jnp.arange(nfreq, dtype=F32) / nfreq)
    ang_r = row[:, None] * inv
    ang_c = col[:, None] * inv
    ang = jnp.concatenate([ang_r, ang_r, ang_c, ang_c], axis=-1)
    cos, sin = jnp.cos(ang), jnp.sin(ang)
    even = ((jnp.arange(dim) // nfreq) % 2 == 0)[None, :]
    sin_next = jnp.where(even, -sin, 0.0)
    sin_prev = jnp.where(even, 0.0, sin)
    return tuple(jnp.tile(a, (1, reps)) for a in (cos, sin_next, sin_prev))


def _rope_tables_mla_wide(n_tokens):
    cos, sin_next, sin_prev = _rope_tables(n_tokens, B_ROPE, 1)
    tail = MLA_HEAD_LANES - B_NOPE - B_ROPE
    wide = lambda a, fill: jnp.pad(a, ((0, 0), (B_NOPE, tail)), constant_values=fill)
    return wide(cos, 1.0), wide(sin_next, 0.0), wide(sin_prev, 0.0)


def _split_cols(w):
    out, acc = [], 0
    for s in IN_SIZES:
        out.append(w[..., acc:acc + s])
        acc += s
    return out


def _pack_w_in_kernel(w_ref, w1_ref, wg_ref):
    w = w_ref[...]
    (a_q, a_k, a_v, b_cq, b_ckv, b_kr, c_in, d_qkv, d_z, d_beta, d_alpha, gate) = _split_cols(w)
    z = lambda n: jnp.zeros((w.shape[0], n), w.dtype)
    k0, k1 = a_k[:, :A_HD], a_k[:, A_HD:]
    v0, v1 = a_v[:, :A_HD], a_v[:, A_HD:]
    w1 = jnp.concatenate([a_q, k0, k0, k1, k1, v0, v0, v1, v1, b_cq, z(256 - B_Q_RANK), b_ckv,
                          b_kr, b_kr, b_kr, b_kr, c_in, d_qkv, d_z, d_beta, d_alpha, z(112)], axis=-1)
    w1_ref[...] = w1.astype(BF16)
    wg_ref[...] = gate.astype(BF16)


def _pack_w_in(w):
    w = w.astype(BF16)
    depth, d, p = w.shape
    n_gate = IN_SIZES[-1]
    tr = 256
    return pl.pallas_call(
        _pack_w_in_kernel,
        grid=(depth, d // tr),
        in_specs=[pl.BlockSpec((None, tr, p), lambda l, i: (l, i, 0))],
        out_specs=[pl.BlockSpec((None, tr, P1), lambda l, i: (l, i, 0)),
                   pl.BlockSpec((None, tr, n_gate), lambda l, i: (l, i, 0))],
        out_shape=[jax.ShapeDtypeStruct((depth, d, P1), BF16), jax.ShapeDtypeStruct((depth, d, n_gate), BF16)],
        compiler_params=_cparams(("parallel", "parallel")),
        name="pack_w_in",
    )(w)


def _expand_kv_heads(a):
    return jnp.repeat(a, A_HEADS // A_KV_HEADS, axis=-2).reshape(a.shape[:-2] + (A_HEADS * A_HD,))


def _stacked_weights(g_pre1, g_post1, g_pre2, g_post2, w_in, a_sink, b_g_cq, b_g_ckv, b_w_uq, b_w_ukv,
                     c_w_pool, c_scale, d_conv, d_a_log, d_dt_bias, d_g_norm, w_br, w_o, w_up, w_down):
    depth = w_in.shape[0]
    w1, wg = _pack_w_in(w_in)
    wuq = b_w_uq.reshape(depth, B_Q_RANK, B_HEADS, B_NOPE + B_ROPE)
    wuq = jnp.pad(wuq, ((0, 0), (0, 256 - B_Q_RANK), (0, 0), (0, MLA_HEAD_LANES - B_NOPE - B_ROPE)))
    wuq = wuq.reshape(depth, 256, MLA_WIDE).astype(BF16)
    wukv = b_w_ukv.reshape(depth, B_KV_RANK, B_HEADS, B_NOPE + B_VD)
    wk = jnp.pad(wukv[..., :B_NOPE], ((0, 0), (0, 0), (0, 0), (0, MLA_HEAD_LANES - B_NOPE)))
    wukv = jnp.concatenate([wk.reshape(depth, B_KV_RANK, MLA_WIDE),
                            wukv[..., B_NOPE:].reshape(depth, B_KV_RANK, B_HEADS * B_VD)], -1).astype(BF16)
    same = jnp.eye(C_GROUPS, dtype=F32)[None, :, None, :, None]
    wpool = (c_w_pool[:, :, :, None, :] * same).reshape(depth, C_GROUPS * C_GW, C_GROUPS * C_GW).astype(BF16)
    gdn =jnp.stack([d_a_log.reshape(depth, 2 * D_HEADS), d_dt_bias.reshape(depth, 2 * D_HEADS)], axis=1)
    prow = jnp.pad(gdn, ((0, 0), (0, 0), (8, LANES - 16)))
    pcol = jnp.pad(gdn.transpose(0, 2, 1), ((0, 0), (8, 0), (0, 0)))
    row = lambda a: a[:, None, :]
    return dict(
        w1=w1, wg=wg, wuq=wuq, wukv=wukv,
        gpre1=row(g_pre1), gpost1=row(g_post1), gpre2=row(g_pre2), gpost2=row(g_post2),
        gcq=row(jnp.pad(b_g_cq, ((0, 0), (0, 256 - B_Q_RANK)))), gckv=row(b_g_ckv),
        sink=a_sink, wpool=wpool, cscale=row(c_scale),
        convw=jnp.pad(d_conv, ((0, 0), (0, SUBLANES - D_CONV), (0, 0))),
        prow=prow, pcol=pcol, gnorm=row(jnp.tile(d_g_norm, (1, D_HEADS))),
        wbr=w_br.astype(BF16), wo=w_o.astype(BF16), wup=w_up.astype(BF16), wdown=w_down.astype(BF16))


def _trunk_layer(x2d, mod3, lw, n_seq, t, ctx, tables=None):
    latent = ctx is not None
    n = x2d.shape[0]
    tm = min(512, t) if latent else 256
    tm_dense = min(512, n)
    if latent:
        proj, qw, kwr, vb, aqr, akr, qwr = _inproj_call(
            x2d, mod3, lw["gpre1"], lw["w1"], lw["gcq"], lw["wuq"], lw["gckv"], lw["wukv"], tables, t, tm_dense)
        kctx, vctx, ctx_ckv, ckr_wide, s0f, s0b = ctx
        out_a = _win_attn_call(lw["sink"], aqr, akr, proj, kctx, vctx, n_seq, t)
        out_b = _mla_lat_call(qw, qwr, kwr, vb, ctx_ckv, ckr_wide, lw["wukv"], n_seq, t, min(MLA_TQ, t))
        s0 = (s0f, s0b)
    else:
        proj, qw, kw, vb, ckvn, ak, av, kr = _inproj_call(
            x2d, mod3, lw["gpre1"], lw["w1"], lw["gcq"], lw["wuq"], lw["gckv"], lw["wukv"], None, t, tm_dense)
        out_a, out_b = _attn_ctx_call(lw["sink"], proj, qw, kw, vb, n_seq, t)
        s0 = None
    out_c, qkvn = _local_call(proj, lw["convw"], lw["wpool"], lw["cscale"], t, tm)
    o_f, o_b, s_fin = _gdn_call(qkvn, proj, lw["prow"], lw["pcol"], s0, n_seq, t)
    x1 = _merge_call(x2d, mod3, lw["gpre1"], out_a, out_b, out_c, o_f, o_b, proj, lw["gnorm"],
                     lw["wg"], lw["wbr"], lw["wo"], lw["gpost1"], t, tm_dense)
    x2 = _ffn_call(x1, mod3, lw["gpre2"], lw["wup"], lw["wdown"], lw["gpost2"], t, tm_dense)
    new_ctx = None
    if not latent:
        new_ctx = (ak.reshape(n_seq, t, A_KV_HEADS, A_HD), av.reshape(n_seq, t, A_KV_HEADS, A_HD),
                   ckvn.reshape(n_seq, t, B_KV_RANK), kr.reshape(n_seq, t, B_ROPE), s_fin[:, 0], s_fin[:, 1])
    return x2, new_ctx


def kernel(x_prompt, x_sample, cache_a_k, cache_a_v, cache_b_ckv, cache_b_krope, state_d_fwd, state_d_bwd, c, c_ctx, w_mod, b_mod, g_pre1, g_post1, g_pre2, g_post2, w_in, a_sink, b_g_cq, b_g_ckv, b_w_uq, b_w_ukv, c_w_pool, c_scale, d_conv, d_a_log, d_dt_bias, d_g_norm, w_br, w_o, w_up, w_down):
    depth = w_mod.shape[0]
    bp, tp, d = x_prompt.shape
    bs, ts, _ = x_sample.shape
    assert bs + 1 <= SUBLANES
    cond8 = jnp.zeros((SUBLANES, d), F32).at[0].set(c_ctx).at[1:1 + bs].set(c)
    mod_all = _mod_call(cond8, w_mod, b_mod).reshape(depth, SUBLANES, 1, 6 * d)
    yp = x_prompt.reshape(bp * tp, d)
    ys = x_sample.reshape(bs * ts, d)
    new = [[] for _ in range(6)]
    tables = _rope_tables(ts, A_HD, LANES // A_HD) + _rope_tables_mla_wide(ts)
    kctx_all, vctx_all = _expand_kv_heads(cache_a_k), _expand_kv_heads(cache_a_v)
    ckr_wide_all = jnp.tile(jnp.pad(cache_b_krope, ((0, 0), (0, 0), (0, 0),
                                                    (B_NOPE, MLA_HEAD_LANES - B_NOPE - B_ROPE))), (1, 1, 1, B_HEADS))
    stacked = _stacked_weights(g_pre1, g_post1, g_pre2, g_post2, w_in, a_sink, b_g_cq, b_g_ckv, b_w_uq, b_w_ukv,
                               c_w_pool, c_scale, d_conv, d_a_log, d_dt_bias, d_g_norm, w_br, w_o, w_up, w_down)
    for l in range(depth):
        lw = {k: _Layer(v, l) for k, v in stacked.items()}
        mod_ctx = _Mod(mod_all, l, 0, 1)
        mod_lat = _Mod(mod_all, l, 1, bs)
        yp, nctx = _trunk_layer(yp, mod_ctx, lw, bp, tp, None)
        for acc, val in zip(new, nctx):
            acc.append(val)
        ctx = tuple(_Layer(a, l) for a in (kctx_all, vctx_all, cache_b_ckv, ckr_wide_all, state_d_fwd, state_d_bwd))
        ys, _ = _trunk_layer(ys, mod_lat, lw, bs, ts, ctx, tables)
    return (yp.reshape(bp, tp, d), ys.reshape(bs, ts, d)) + tuple(jnp.stack(v, axis=1) for v in new)
```

```python
import functools
import math
from typing import NamedTuple

import jax
import jax.numpy as jnp
from jax import lax
from jax.experimental import pallas as pl
from jax.experimental.pallas import tpu as pltpu

F32 = jnp.float32
BF16 = jnp.bfloat16

D_MODEL = 1024
GRID_W = 64
ROPE_BASE = 10000.0
EPS = 1e-6
NEG_INF = -1e30
Q_BLOCK = 128
N_BRANCH = 4
BRANCH_W = 256
A_HEADS = 4
A_KV_HEADS = 2
A_HD = 64
A_WINDOW = 128
A_SCALE = A_HD ** -0.5
B_HEADS = 4
B_Q_RANK = 192
B_KV_RANK = 128
B_NOPE = 64
B_ROPE = 32
B_VD = 64
B_SCALE = (B_NOPE + B_ROPE) ** -0.5
C_GROUPS = 4
C_GW = 64
C_WINDOWS = (2, 4, 8, 16)
D_HEADS = 4
D_DK = 64
D_DV = 64
D_CONV = 5
D_CHUNK = 64
FF_RAW = -(-8 * D_MODEL // 3)
D_FF = -(-FF_RAW // 256) * 256
IN_SIZES = (A_HEADS * A_HD, A_KV_HEADS * A_HD, A_KV_HEADS * A_HD, B_Q_RANK, B_KV_RANK, B_ROPE,
            C_GROUPS * C_GW, D_HEADS * (2 * D_DK + D_DV), D_HEADS * D_DV, 2 * D_HEADS, 2 * D_HEADS,
            N_BRANCH * D_MODEL)

LANES = 128
SUBLANES = 8
VMEM_LIMIT_BYTES = 56 * 1024 * 1024
HALO = SUBLANES
GDN_SEQS_PER_STEP = 2
GDN_CHUNKS_PER_STEP = 4
MLA_HEAD_LANES = LANES
MLA_WIDE = B_HEADS * MLA_HEAD_LANES
LOG2E = math.log2(math.e)
DENSE_ROW_GROUPS = 2
MLA_TQ = 512
MLA_ROW_GROUP = 32
MLA_COL_BLOCK = 256
WIN_ATTN_TQ = 256

C_AQ = 0
C_AK = 256
C_AV = 512
C_CQ = 768
C_CKV = 1024
C_KR = 1152
C_CIN = 1280
C_QKV = 1536
C_Z = 2304
C_AB = 2560
P1 = 2688


def _cparams(sem):
    return pltpu.CompilerParams(dimension_semantics=sem, vmem_limit_bytes=VMEM_LIMIT_BYTES)


class _Layer(NamedTuple):
    arr: jax.Array
    l: int


def _resident(p):
    shape = p.arr.shape[1:]
    return pl.BlockSpec((None,) + shape, lambda *_: (p.l,) + (0,) * len(shape), pipeline_mode=pl.Buffered(1))


class _Mod(NamedTuple):
    arr: jax.Array
    l: int
    first: int
    count: int


def _mod_block(m, tiles_per_seq):
    width = m.arr.shape[-1]
    if m.count == 1:
        return pl.BlockSpec((None, 1, 1, width), lambda i: (m.l, m.first, 0, 0))
    return pl.BlockSpec((None, 1, 1, width), lambda i: (m.l, m.first + i // tiles_per_seq, 0, 0))


def _cache_block(p):
    return pl.BlockSpec((1, None) + p.arr.shape[2:], lambda b, i: (b, p.l, 0, 0))


def _operands(args):
    return [a.arr if isinstance(a, (_Layer, _Mod)) else a for a in args]


def _dot(a, b):
    return jnp.dot(a, b, preferred_element_type=F32)


def _dot_nt(a, b):
    return lax.dot_general(a, b, (((1,), (1,)), ((), ())), preferred_element_type=F32)


def _split3(x):
    hi = x.astype(BF16)
    r1 = x - hi.astype(F32)
    mid = r1.astype(BF16)
    lo = (r1 - mid.astype(F32)).astype(BF16)
    return hi, mid, lo


def _dot_sel_right(x, sel):
    hi, mid, lo = _split3(x)
    return _dot(hi, sel) + _dot(mid, sel) + _dot(lo, sel)


def _silu(x):
    return x * jax.nn.sigmoid(x)


def _softplus(x):
    return jnp.maximum(x, 0.0) + jnp.log1p(jnp.exp(-jnp.abs(x)))


def _head_mask(width, n_lanes, h):
    lane = lax.broadcasted_iota(jnp.int32, (1, n_lanes), 1)
    return (lane // width) == h


def _group_ones(n, width):
    r = lax.broadcasted_iota(jnp.int32, (n, n), 0) // width
    c = lax.broadcasted_iota(jnp.int32, (n, n), 1) // width
    return jnp.where(r == c, 1.0, 0.0).astype(BF16)


def _mod_kernel(c_ref, w_ref, b_ref, o_ref):
    s = _silu(c_ref[...])
    o_ref[0] = _dot(s.astype(BF16), w_ref[0].astype(BF16)) + b_ref[0]


def _mod_call(cond8, w_mod, b_mod):
    depth, d, n = w_mod.shape
    tn = 1536
    return pl.pallas_call(
        _mod_kernel,
        grid=(depth, n // tn),
        in_specs=[pl.BlockSpec((SUBLANES, d), lambda l, j: (0, 0)),
                  pl.BlockSpec((1, d, tn), lambda l, j: (l, 0, j)),
                  pl.BlockSpec((1, 1, tn), lambda l, j: (l, 0, j))],
        out_specs=pl.BlockSpec((1, SUBLANES, tn), lambda l, j: (l, 0, j)),
        out_shape=jax.ShapeDtypeStruct((depth, SUBLANES, n), F32),
        compiler_params=_cparams(("parallel", "parallel")),
        name="mod",
    )(cond8, w_mod, b_mod.reshape(depth, 1, n))


def _rmsnorm_rows(x, g, n):
    ms = jnp.sum(x * x, axis=-1, keepdims=True) * (1.0 / n)
    return x * lax.rsqrt(ms + EPS) * g


def _rope(x, cos, sin_next, sin_prev, quarter):
    n = x.shape[-1]
    return x * cos + pltpu.roll(x, n - quarter, 1) * sin_next + pltpu.roll(x, quarter, 1) * sin_prev


def _mla_wide_kr(kr_rep):
    lane = lax.broadcasted_iota(jnp.int32, (1, MLA_HEAD_LANES), 1)
    keep = (lane >= B_NOPE) & (lane < B_NOPE + B_ROPE)
    return jnp.concatenate([jnp.where(keep, kr_rep, 0.0)] * B_HEADS, axis=1)


def _inproj_kernel(latent, *refs):
    if latent:
        (x_ref, mod_ref, gpre_ref, w1_ref, gcq_ref, wuq_ref, gckv_ref, wukv_ref,
         ca_ref, sna_ref, spa_ref, cb_ref, snb_ref, spb_ref,
         proj_ref, qw_ref, kw_ref, vb_ref, aqr_ref, akr_ref, qwr_ref) = refs
    else:
        (x_ref, mod_ref, gpre_ref, w1_ref, gcq_ref, wuq_ref, gckv_ref, wukv_ref,
         proj_ref, qw_ref, kw_ref, vb_ref, ckv_ref, ak_ref, av_ref, kr_ref) = refs
    d = D_MODEL
    x = x_ref[...]
    mod = mod_ref[0]
    sh1 = mod[:, 0:d]
    sc1 = mod[:, d:2 * d]
    h = _rmsnorm_rows(x, gpre_ref[...], d) * (1.0 + sc1) + sh1
    proj = _dot(h.astype(BF16), w1_ref[...])
    proj_ref[...] = proj
    cqn = _rmsnorm_rows(proj[:, C_CQ:C_CQ + 256], gcq_ref[...], B_Q_RANK)
    qw = _dot(cqn.astype(BF16), wuq_ref[...]) * (B_SCALE * LOG2E)
    qw_ref[...] = qw.astype(BF16)
    ckvn = _rmsnorm_rows(proj[:, C_CKV:C_CKV + B_KV_RANK], gckv_ref[...], B_KV_RANK)
    if not latent:
        unexpand = lambda c: jnp.concatenate([proj[:, c:c + A_HD], proj[:, c + 2 * A_HD:c + 3 * A_HD]], axis=1)
        ckv_ref[...] = ckvn
        ak_ref[...] = unexpand(C_AK)
        av_ref[...] = unexpand(C_AV)
        kr_ref[...] = proj[:, C_KR:C_KR + B_ROPE]
    knv =_dot(ckvn.astype(BF16), wukv_ref[...])
    vb_ref[...] = knv[:, MLA_WIDE:].astype(BF16)
    kr_wide = _mla_wide_kr(proj[:, C_KR:C_KR + 128])
    if latent:
        ca, sna, spa = (jnp.concatenate([r[...]] * (256 // LANES), axis=1) for r in (ca_ref, sna_ref, spa_ref))
        cb, snb, spb = (jnp.concatenate([r[...]] * B_HEADS, axis=1) for r in (cb_ref, snb_ref, spb_ref))
        aqr_ref[...] = _rope(proj[:, C_AQ:C_AQ + 256], ca, sna, spa, A_HD // 4).astype(BF16)
        akr_ref[...] = _rope(proj[:, C_AK:C_AK + 256], ca, sna, spa, A_HD // 4).astype(BF16)
        qwr_ref[...] = _rope(qw, cb, snb, spb, B_ROPE // 4).astype(BF16)
        kr_wide = _rope(kr_wide, cb, snb, spb, B_ROPE // 4)
    kw_ref[...] = (knv[:, :MLA_WIDE] + kr_wide).astype(BF16)


def _inproj_call(x2d, mod3, gpre, w1, gcq, wuq, gckv, wukv, tables, seq_len, tm):
    n, d = x2d.shape
    latent = tables is not None
    tiles_per_seq = seq_len // tm if latent else 1
    row = lambda w, dt=F32: (pl.BlockSpec((tm, w), lambda i: (i, 0)), jax.ShapeDtypeStruct((n, w), dt))
    params = [gpre, w1, gcq, wuq, gckv, wukv]
    in_specs = [pl.BlockSpec((tm, d), lambda i: (i, 0)),
                _mod_block(mod3, seq_len // tm)] + [_resident(p) for p in params]
    args = [x2d, mod3] + params
    outs = [row(P1), row(MLA_WIDE, BF16), row(MLA_WIDE, BF16), row(256, BF16)]
    if latent:
        for t in tables:
            in_specs.append(pl.BlockSpec((tm, t.shape[1]), lambda i: (i % tiles_per_seq, 0)))
            args.append(t)
        outs += [row(256, BF16), row(256, BF16), row(MLA_WIDE, BF16)]
    else:
        outs += [row(B_KV_RANK), row(A_KV_HEADS * A_HD), row(A_KV_HEADS * A_HD), row(B_ROPE)]
    return pl.pallas_call(
        functools.partial(_inproj_kernel, latent),
        grid=(n // tm,),
        in_specs=in_specs,
        out_specs=[o[0] for o in outs],
        out_shape=[o[1] for o in outs],
        compiler_params=_cparams(("parallel",)),
        name="inproj_lat" if latent else "inproj_ctx",
    )(*_operands(args))


def _softmax_pv(scores, values, scale, sink, hm):
    c1 = None if scale is None else scale * LOG2E
    m = jnp.max(scores[0], axis=-1, keepdims=True)
    for s in scores[1:]:
        m = jnp.maximum(m, jnp.max(s, axis=-1, keepdims=True))
    m2 = m if c1 is None else m * c1
    if sink is not None:
        m2 = jnp.maximum(m2, sink * LOG2E)
    den = None
    pv = None
    for s, v in zip(scores, values):
        p = jnp.exp2((s if c1 is None else s * c1) - m2)
        ps = jnp.sum(p, axis=-1, keepdims=True)
        den = ps if den is None else den + ps
        t = _dot(p.astype(BF16), v)
        pv = t if pv is None else pv + t
    if sink is not None:
        den = den + jnp.exp2(sink * LOG2E - m2)
    return jnp.where(hm, pv / den, 0.0)


def _attn_ctx_kernel(layer, sink_ref, q_ref, k_ref, v_ref, qw_ref, kw_ref, vb_ref, oa_ref, ob_ref):
    q = q_ref[...]
    k = k_ref[...].astype(BF16)
    v = v_ref[...].astype(BF16)
    acc = jnp.zeros(q.shape, F32)
    for h in range(A_HEADS):
        hm = _head_mask(A_HD, 256, h)
        s = _dot_nt(jnp.where(hm, q, 0.0).astype(BF16), k)
        acc = acc + _softmax_pv([s], [v], A_SCALE, sink_ref[layer, h], hm)
    oa_ref[...] = acc
    vb = vb_ref[...]
    acc = jnp.zeros((q.shape[0], B_HEADS * B_VD), F32)
    for h in range(B_HEADS):
        hs = slice(h * MLA_HEAD_LANES, (h + 1) * MLA_HEAD_LANES)
        s = _dot_nt(qw_ref[:, hs], kw_ref[:, hs])
        acc = acc + _softmax_pv([s], [vb], None, None, _head_mask(B_VD, 256, h))
    ob_ref[...] = acc


def _attn_ctx_call(sink, proj, qw, kw, vb, n_seq, t):
    n = proj.shape[0]
    col = lambda w, j: pl.BlockSpec((t, w), lambda b: (b, j))
    out = (pl.BlockSpec((t, 256), lambda b: (b, 0)), jax.ShapeDtypeStruct((n, 256), F32))
    return pl.pallas_call(
        functools.partial(_attn_ctx_kernel, sink.l),
        grid=(n_seq,),
        in_specs=[pl.BlockSpec(memory_space=pltpu.SMEM),
                  col(256, C_AQ // 256), col(256, C_AK // 256), col(256, C_AV // 256),
                  col(MLA_WIDE, 0), col(MLA_WIDE, 0), col(256, 0)],
        out_specs=[out[0], out[0]],
        out_shape=[out[1], out[1]],
        compiler_params=_cparams(("parallel",)),
        name="attn_ctx",
    )(sink.arr, proj, proj, proj, qw, kw, vb)


def _win_attn_kernel(t, layer, sink_ref, qr_ref, q_ref, kp_ref, kc_ref, kn_ref, vp_ref, vc_ref, vn_ref,
                     kctx_ref, vctx_ref, o_ref):
    i = pl.program_id(1)
    qr = qr_ref[...]
    q = q_ref[...]
    tq = q.shape[0]
    kl = jnp.concatenate([kp_ref[...], kc_ref[...], kn_ref[...]], axis=0)
    vl = jnp.concatenate([vp_ref[...], vc_ref[...], vn_ref[...]], axis=0).astype(BF16)
    kctx = kctx_ref[0].astype(BF16)
    vctx = vctx_ref[0].astype(BF16)
    q_pos = i * tq + lax.broadcasted_iota(jnp.int32, (tq, tq + 2 * A_WINDOW), 0)
    k_pos = i * tq - A_WINDOW + lax.broadcasted_iota(jnp.int32, (tq, tq + 2 * A_WINDOW), 1)
    valid = (jnp.abs(q_pos - k_pos) <= A_WINDOW) & (k_pos >= 0) & (k_pos < t)
    acc = jnp.zeros(q.shape, F32)
    for h in range(A_HEADS):
        hm = _head_mask(A_HD, 256, h)
        s_loc = jnp.where(valid, _dot_nt(jnp.where(hm, qr, 0.0).astype(BF16), kl), NEG_INF)
        s_ctx = _dot_nt(jnp.where(hm, q, 0.0).astype(BF16), kctx)
        acc = acc + _softmax_pv([s_loc, s_ctx], [vl, vctx], A_SCALE, sink_ref[layer, h], hm)
    o_ref[...] = acc


def _win_attn_call(sink, aqr, akr, proj, kctx, vctx, n_seq, t):
    n = proj.shape[0]
    tq = min(WIN_ATTN_TQ, t)
    nq = t // tq
    r = tq // A_WINDOW
    nh = t // A_WINDOW
    cur = lambda j: pl.BlockSpec((tq, 256), lambda b, i: (b * nq + i, j))
    prev = lambda j: pl.BlockSpec((A_WINDOW, 256), lambda b, i: (b * nh + jnp.maximum(i * r - 1, 0), j))
    nxt = lambda j: pl.BlockSpec((A_WINDOW, 256), lambda b, i: (b * nh + jnp.minimum((i + 1) * r, nh - 1), j))
    ctx = _cache_block(kctx)
    return pl.pallas_call(
        functools.partial(_win_attn_kernel, t, sink.l),
        grid=(n_seq, nq),
        in_specs=[pl.BlockSpec(memory_space=pltpu.SMEM),
                  cur(0), cur(C_AQ // 256),
                  prev(0), cur(0), nxt(0),
                  prev(C_AV // 256), cur(C_AV // 256), nxt(C_AV // 256),
                  ctx, ctx],
        out_specs=pl.BlockSpec((tq, 256), lambda b, i: (b * nq + i, 0)),
        out_shape=jax.ShapeDtypeStruct((n, 256), F32),
        compiler_params=_cparams(("parallel", "parallel")),
        name="win_attn",
    )(sink.arr, aqr, proj, akr, akr, akr, proj, proj, proj, kctx.arr, vctx.arr)


def _mla_lat_kernel(qw_ref, qwr_ref, kwr_ref, vb_ref, cckv_ref, ckr_ref, wukv_ref, o_ref, s_ref, p_ref, den_ref):
    kvc = _dot(cckv_ref[0].astype(BF16), wukv_ref[...])
    kwc = (kvc[:, :MLA_WIDE] + ckr_ref[0]).astype(BF16)
    vc = kvc[:, MLA_WIDE:].astype(BF16)
    tq = qw_ref.shape[0]
    t = kwr_ref.shape[0]
    n_keys = s_ref.shape[2]
    cb = MLA_COL_BLOCK
    rg = MLA_ROW_GROUP

    def scores_into(slot, h):
        hs = slice(h * MLA_HEAD_LANES, (h + 1) * MLA_HEAD_LANES)
        s_ref[slot, :, 0:t] = _dot_nt(qwr_ref[:, hs], kwr_ref[:, hs])
        s_ref[slot, :, t:n_keys] = _dot_nt(qw_ref[:, hs], kwc[:, hs])

    scores_into(0, 0)
    acc = jnp.zeros((tq, B_HEADS * B_VD), F32)
    for h in range(B_HEADS):
        slot = h % 2
        if h + 1 < B_HEADS:
            scores_into(1 - slot, h + 1)
        for r0 in range(0, tq, rg):
            rows = slice(r0, r0 + rg)
            m = s_ref[slot, rows, 0:cb]
            for c0 in range(cb, n_keys, cb):
                m = jnp.maximum(m, s_ref[slot, rows, c0:c0 + cb])
            m2 = jnp.max(m, axis=-1, keepdims=True)
            part = None
            for c0 in range(0, n_keys, cb):
                p = jnp.exp2(s_ref[slot, rows, c0:c0 + cb] - m2)
                part = p if part is None else part + p
                p_ref[rows, c0:c0 + cb] = p.astype(BF16)
            den_ref[rows, :] = jnp.broadcast_to(jnp.sum(part, axis=-1, keepdims=True), (rg, LANES))
        pv = _dot(p_ref[:, 0:t], vb_ref[...]) + _dot(p_ref[:, t:n_keys], vc)
        acc = acc + jnp.where(_head_mask(B_VD, 256, h), pv / den_ref[:, 0:1], 0.0)
    o_ref[...] = acc


def _mla_lat_call(qw, qwr, kwr, vb, cckv, ckr_wide, wukv, n_seq, t, tq):
    n = qw.shape[0]
    nq = t // tq
    qblk = lambda w: pl.BlockSpec((tq, w), lambda b, i: (b * nq + i, 0))
    seq = lambda w: pl.BlockSpec((t, w), lambda b, i: (b, 0))
    ctx = _cache_block
    return pl.pallas_call(
        _mla_lat_kernel,
        grid=(n_seq, nq),
        in_specs=[qblk(MLA_WIDE), qblk(MLA_WIDE), seq(MLA_WIDE), seq(256), ctx(cckv), ctx(ckr_wide),
                  _resident(wukv)],
        out_specs=pl.BlockSpec((tq, 256), lambda b, i: (b * nq + i, 0)),
        out_shape=jax.ShapeDtypeStruct((n, 256), F32),
        scratch_shapes=[pltpu.VMEM((2, tq, t + cckv.arr.shape[2]), F32), pltpu.VMEM((tq, t + cckv.arr.shape[2]), BF16),
                        pltpu.VMEM((tq, LANES), F32)],
        compiler_params=_cparams(("parallel", "parallel")),
        name="mla_lat",
    )(qw, qwr, kwr, vb, cckv.arr, ckr_wide.arr, wukv.arr)


def _with_halo(prev_ref, cur_ref, next_ref, has_prev, has_next):
    prev = jnp.where(has_prev, prev_ref[...], 0.0)
    nxt = jnp.where(has_next, next_ref[...], 0.0)
    return jnp.concatenate([prev, cur_ref[...], nxt], axis=0)


def _shift_rows(x, k):
    n = x.shape[0]
    return pltpu.roll(x, (-k) % n, 0)


def _local_kernel(t, tm, cp_ref, cc_ref, cn_ref, qp_ref, qc_ref, qn_ref, convw_ref, wpool_ref, cscale_ref,
                  oc_ref, qkv_ref):
    i = pl.program_id(0)
    tiles_per_seq = t // tm
    j = i % tiles_per_seq
    has_prev = j > 0
    has_next = j < tiles_per_seq - 1
    x = _with_halo(cp_ref, cc_ref, cn_ref, has_prev, has_next)
    p2 = x + _shift_rows(x, -1)
    p4 = _shift_rows(p2, 1) + _shift_rows(p2, -1)
    p8 = _shift_rows(p4, 2) + _shift_rows(p4, -2)
    p16 = _shift_rows(p8, 4) + _shift_rows(p8, -4)
    grp = lax.broadcasted_iota(jnp.int32, (1, 256), 1) // C_GW
    win = jnp.where(grp == 0, p2, jnp.where(grp == 1, p4, jnp.where(grp == 2, p8, p16)))[HALO:HALO + tm]
    pos = j * tm + lax.broadcasted_iota(jnp.int32, (tm, 256), 0)
    half = jnp.where(grp == 0, 1, jnp.where(grp == 1, 2, jnp.where(grp == 2, 4, 8)))
    cnt = (jnp.minimum(pos + half, t) - jnp.maximum(pos - half, 0)).astype(F32)
    y = win / cnt - cc_ref[...]
    oc_ref[...] = _dot(y.astype(BF16), wpool_ref[...]) * cscale_ref[...]
    xq = _with_halo(qp_ref, qc_ref, qn_ref, has_prev, has_next)
    w = convw_ref[...]
    pad = D_CONV // 2
    acc = None
    for tap in range(D_CONV):
        term = _shift_rows(xq, tap - pad) * w[tap:tap + 1, :]
        acc = term if acc is None else acc + term
    u = _silu(acc[HALO:HALO + tm])
    ones = _group_ones(256, D_DK)
    q = u[:, 0:256]
    k = u[:, 256:512]
    q = q * lax.rsqrt(_dot_sel_right(q * q, ones) + EPS) * (D_DK ** -0.5)
    k = k * lax.rsqrt(_dot_sel_right(k * k, ones) + EPS)
    qkv_ref[:, 0:256] = q
    qkv_ref[:, 256:512] = k
    qkv_ref[:, 512:768] = u[:, 512:768]


def _local_call(proj, convw8, wpool_bd, cscale, t, tm):
    n = proj.shape[0]
    hb = tm // HALO
    nblk = n // HALO
    cur = lambda w, c: pl.BlockSpec((tm, w), lambda i: (i, c))
    prev = lambda w, c: pl.BlockSpec((HALO, w), lambda i: (jnp.maximum(i * hb - 1, 0), c))
    nxt = lambda w, c: pl.BlockSpec((HALO, w), lambda i: (jnp.minimum((i + 1) * hb, nblk - 1), c))
    full = _resident
    cc, qc = C_CIN // 256, C_QKV // 768
    return pl.pallas_call(
        functools.partial(_local_kernel, t, tm),
        grid=(n // tm,),
        in_specs=[prev(256, cc), cur(256, cc), nxt(256, cc), prev(768, qc), cur(768, qc), nxt(768, qc),
                  full(convw8), full(wpool_bd), full(cscale)],
        out_specs=[pl.BlockSpec((tm, 256), lambda i: (i, 0)), pl.BlockSpec((tm, 768), lambda i: (i, 0))],
        out_shape=[jax.ShapeDtypeStruct((n, 256), F32), jax.ShapeDtypeStruct((n, 768), F32)],
        compiler_params=_cparams(("parallel",)),
        name="local",
    )(proj, proj, proj, proj, proj, proj, convw8.arr, wpool_bd.arr, cscale.arr)


def _gdn_kernel(n_steps, g_chunks, n_sq, has_s0, xf_ref, xb_ref, abf_ref, abb_ref, prow_ref, pcol_ref, *refs):
    if has_s0:
        s0f_ref, s0b_ref, of_ref, ob_ref, sfin_ref, s_ref = refs
    else:
        of_ref, ob_ref, sfin_ref, s_ref = refs
    i = pl.program_id(1)
    c = D_CHUNK
    nh = D_HEADS

    @pl.when(i == 0)
    def _():
        if has_s0:
            for sq in range(n_sq):
                s_ref[sq * nh:(sq + 1) * nh] = s0f_ref[sq]
                s_ref[(n_sq + sq) * nh:(n_sq + sq + 1) * nh] = s0b_ref[sq]
        else:
            s_ref[...] = jnp.zeros(s_ref.shape, F32)

    r = lax.broadcasted_iota(jnp.int32, (c, c), 0)
    cidx = lax.broadcasted_iota(jnp.int32, (c, c), 1)
    lower = jnp.where(cidx <= r, 1.0, 0.0).astype(BF16)
    upper = jnp.where(cidx >= r, 1.0, 0.0).astype(BF16)
    prow = prow_ref[...]
    pcol = pcol_ref[...]

    eye = jnp.where(cidx == r, 1.0, 0.0).astype(BF16)
    sel_lower = jnp.concatenate([lower] * 3, axis=1)
    sel_upper = jnp.concatenate([upper] * 3, axis=1)
    pick = jnp.where(lax.broadcasted_iota(jnp.int32, (16, LANES), 0) == lax.broadcasted_iota(jnp.int32, (16, LANES), 1),
                     1.0, 0.0).astype(BF16)
    sel_rows = jnp.concatenate([pick] * 3, axis=1)
    sel_lower_t = jnp.concatenate([lower] * 3, axis=0)
    sel_upper_t = jnp.concatenate([upper] * 3, axis=0)

    bdot = lambda a, b: jnp.einsum('bik,bkj->bij', a, b, preferred_element_type=F32)
    bdot_nt = lambda a, b: jnp.einsum('bik,bjk->bij', a, b, preferred_element_type=F32)

    def prepare(steps):
        nb = 2 * n_sq * len(steps) * nh
        is_fwd = lax.broadcasted_iota(jnp.int32, (nb, c, c), 0) < nb // 2
        r3 = lax.broadcasted_iota(jnp.int32, (nb, c, c), 1)
        c3 = lax.broadcasted_iota(jnp.int32, (nb, c, c), 2)
        ahead = jnp.where(is_fwd, r3 - c3, c3 - r3)
        incl = ahead >= 0
        strict = ahead > 0
        q_l, k_l, v_l, b_l, gcol_l, grow_l = [], [], [], [], [], []
        for d, sq, step in [(d, sq, step) for d in range(2) for sq in range(n_sq) for step in steps]:
            x_ref, ab_ref = ((xf_ref, abf_ref), (xb_ref, abb_ref))[d]
            gi = step if d == 0 else g_chunks - 1 - step
            rows = slice(gi * c, (gi + 1) * c)
            x = x_ref[sq, rows, :]
            ab = ab_ref[sq, rows, :]
            abr = _dot_nt(sel_rows, jnp.concatenate(_split3(ab), axis=1))
            beta_c = jax.nn.sigmoid(ab)
            g_c = -jnp.exp(prow[0:1, :]) * _softplus(ab + prow[1:2, :])
            g_r = -jnp.exp(pcol[:, 0:1]) * _softplus(abr + pcol[:, 1:2])
            if d == 0:
                gc_c = _dot(sel_lower, jnp.concatenate(_split3(g_c), axis=0))
                gc_r = _dot(jnp.concatenate(_split3(g_r), axis=1), sel_upper_t)
            else:
                gc_c = _dot(sel_upper, jnp.concatenate(_split3(g_c), axis=0))
                gc_r = _dot(jnp.concatenate(_split3(g_r), axis=1), sel_lower_t)
            for h in range(nh):
                hs = slice(h * D_DK, (h + 1) * D_DK)
                lane = nh * d + h
                q_l.append(x[:, 0:256][:, hs])
                k_l.append(x[:, 256:512][:, hs])
                v_l.append(x[:, 512:768][:, hs])
                b_l.append(beta_c[:, lane:lane + 1])
                gcol_l.append(gc_c[:, 8 + lane:9 + lane])
                grow_l.append(gc_r[8 + lane:9 + lane, :])
        qh, kh, vh = jnp.stack(q_l), jnp.stack(k_l), jnp.stack(v_l)
        bcol, gcol, grow = jnp.stack(b_l), jnp.stack(gcol_l), jnp.stack(grow_l)
        glast = jnp.concatenate([gcol[:nb // 2, c - 1:c, :], gcol[nb // 2:, 0:1, :]], axis=0)
        decay = jnp.exp(jnp.where(incl, gcol - grow, NEG_INF))
        kb = kh * bcol
        kq = bdot_nt(jnp.concatenate([kb, qh], axis=1).astype(BF16), kh.astype(BF16))
        p = -jnp.where(strict, kq[:, 0:c] * decay, 0.0)
        egc = jnp.exp(gcol)
        xs = jnp.concatenate([vh * bcol, kb * egc], axis=2)
        for it in range(6):
            ph = p.astype(BF16)
            pl_ = (p - ph.astype(F32)).astype(BF16)
            lhs = jnp.concatenate([ph, pl_, ph], axis=2)
            yh = xs.astype(BF16)
            yl = (xs - yh.astype(F32)).astype(BF16)
            if it < 5:
                yh = jnp.concatenate([yh, ph], axis=2)
                yl = jnp.concatenate([yl, pl_], axis=2)
            m = bdot(lhs, jnp.concatenate([yh, yh, yl], axis=1))
            xs = xs + m[:, :, 0:2 * D_DK]
            if it < 5:
                p = m[:, :, 2 * D_DK:3 * D_DK]
        u = xs[:, :, 0:D_DV]
        w = xs[:, :, D_DV:2 * D_DV]
        a = kq[:, c:2 * c] * decay
        kd = (kh * jnp.exp(glast - gcol)).astype(BF16)
        kdt = bdot_nt(jnp.broadcast_to(eye, (nb, c, c)), kd).astype(BF16)
        lhs_s = jnp.concatenate([w, qh * egc], axis=1).astype(BF16)
        lhs_v = jnp.concatenate([a.astype(BF16), kdt], axis=1)
        return u, lhs_s, lhs_v, jnp.exp(glast)

    prep = prepare(list(range(g_chunks)))

    def chains(arr, step):
        los = [((d * n_sq + sq) * g_chunks + step) * nh for d in range(2) for sq in range(n_sq)]
        return jnp.concatenate([arr[lo:lo + nh] for lo in los], axis=0)

    s = s_ref[...]
    outs = [[[None] * g_chunks for _ in range(n_sq)] for _ in range(2)]
    for step in range(g_chunks):
        u, lhs_s, lhs_v, eg = (chains(arr, step) for arr in prep)
        ws_qs = bdot(lhs_s, s.astype(BF16))
        vb = (u - ws_qs[:, 0:c]).astype(BF16)
        av = bdot(lhs_v, vb)
        o = ws_qs[:, c:2 * c] + av[:, 0:c]
        s = s * eg + av[:, c:2 * c]
        for d in range(2):
            gi = step if d == 0 else g_chunks - 1 - step
            for sq in range(n_sq):
                base = (d * n_sq + sq) * nh
                outs[d][sq][gi] = jnp.concatenate([o[base + h] for h in range(nh)], axis=1)
    for sq in range(n_sq):
        of_ref[sq] = jnp.concatenate(outs[0][sq], axis=0)
        ob_ref[sq] = jnp.concatenate(outs[1][sq], axis=0)
    s_ref[...] = s

    @pl.when(i == n_steps - 1)
    def _():
        for d in range(2):
            for sq in range(n_sq):
                sfin_ref[sq, d] = s[(d * n_sq + sq) * nh:(d * n_sq + sq + 1) * nh]


def _gdn_call(qkvn, proj, prow, pcol, s0, n_seq, t):
    g = GDN_CHUNKS_PER_STEP
    n_sq = math.gcd(n_seq, GDN_SEQS_PER_STEP)
    ns = t // (D_CHUNK * g)
    rows = D_CHUNK * g
    fwd = lambda i: i
    bwd = lambda i: ns - 1 - i
    x3 = qkvn.reshape(n_seq, t, 768)
    p3 = proj.reshape(n_seq, t, P1)
    xblk = lambda f: pl.BlockSpec((n_sq, rows, 768), lambda b, i: (b, f(i), 0))
    abblk = lambda f: pl.BlockSpec((n_sq, rows, 128), lambda b, i: (b, f(i), C_AB // 128))
    oblk = lambda f: pl.BlockSpec((n_sq, rows, 256), lambda b, i: (b, f(i), 0))
    sblk = pl.BlockSpec((n_sq, 2, D_HEADS, D_DK, D_DV), lambda b, i: (b, 0, 0, 0, 0))
    osd = jax.ShapeDtypeStruct((n_seq, t, 256), F32)
    in_specs = [xblk(fwd), xblk(bwd), abblk(fwd), abblk(bwd), _resident(prow), _resident(pcol)]
    args = [x3, x3, p3, p3, prow.arr, pcol.arr]
    if s0 is not None:
        for st in s0:
            in_specs.append(pl.BlockSpec((n_sq, None, D_HEADS, D_DK, D_DV), lambda b, i, l=st.l: (b, l, 0, 0, 0)))
            args.append(st.arr)
    o_f, o_b, s_fin = pl.pallas_call(
        functools.partial(_gdn_kernel, ns, g, n_sq, s0 is not None),
        grid=(n_seq // n_sq, ns),
        in_specs=in_specs,
        out_specs=[oblk(fwd), oblk(bwd), sblk],
        out_shape=[osd, osd, jax.ShapeDtypeStruct((n_seq, 2, D_HEADS, D_DK, D_DV), F32)],
        scratch_shapes=[pltpu.VMEM((2 * n_sq * D_HEADS, D_DK, D_DV), F32)],
        compiler_params=_cparams(("parallel", "arbitrary")),
        name="gdn",
    )(*args)
    return o_f.reshape(n_seq * t, 256), o_b.reshape(n_seq * t, 256), s_fin


def _row_groups(x):
    rows = x.shape[0] // DENSE_ROW_GROUPS
    return [x[g * rows:(g + 1) * rows] for g in range(DENSE_ROW_GROUPS)]


def _merge_kernel(x_ref, mod_ref, gpre_ref, oa_ref, ob_ref, oc_ref, of_ref, obw_ref, z_ref, gnorm_ref,
                  wg_ref, wbr_ref, wo_ref, gpost_ref, o_ref):
    d = D_MODEL
    mod = mod_ref[0]
    sh1, sc1, ga1 = mod[:, 0:d], mod[:, d:2 * d], mod[:, 2 * d:3 * d]
    xs = _row_groups(x_ref[...])
    hs = [(_rmsnorm_rows(xg, gpre_ref[...], d) * (1.0 + sc1) + sh1).astype(BF16) for xg in xs]
    ods = []
    for o, z in zip(_row_groups(of_ref[...] + obw_ref[...]), _row_groups(z_ref[...])):
        ms = _dot_sel_right(o * o, _group_ones(256, D_DV)) * (1.0 / D_DV)
        ods.append(o * lax.rsqrt(ms + EPS) * gnorm_ref[...] * _silu(z))
    branches = [_row_groups(oa_ref[...]), _row_groups(ob_ref[...]), _row_groups(oc_ref[...]), ods]
    accs = [None] * len(xs)
    for m in range(N_BRANCH):
        gates = [jax.nn.sigmoid(_dot(h, wg_ref[:, m * d:(m + 1) * d])) for h in hs]
        brs = [_dot(br.astype(BF16), wbr_ref[m]) for br in branches[m]]
        accs = [g * b if a is None else a + g * b for a, g, b in zip(accs, gates, brs)]
    mixes = [_dot(a.astype(BF16), wo_ref[...]) for a in accs]
    o_ref[...] = jnp.concatenate([xg + ga1 * _rmsnorm_rows(mix, gpost_ref[...], d) for xg, mix in zip(xs, mixes)],
                                 axis=0)


def _merge_call(x2d, mod3, gpre, oa, ob, oc, of, obw, proj, gnorm, wg, wbr, wo, gpost, seq_len, tm):
    n, d = x2d.shape
    tiles_per_seq = seq_len // tm
    row = lambda w, c=0: pl.BlockSpec((tm, w), lambda i: (i, c))
    full = _resident
    return pl.pallas_call(
        _merge_kernel,
        grid=(n // tm,),
        in_specs=[row(d), _mod_block(mod3, tiles_per_seq), full(gpre),
                  row(256), row(256), row(256), row(256), row(256), row(256, C_Z // 256), full(gnorm),
                  full(wg), full(wbr), full(wo), full(gpost)],
        out_specs=row(d),
        out_shape=jax.ShapeDtypeStruct((n, d), F32),
        compiler_params=_cparams(("parallel",)),
        name="merge",
    )(*_operands([x2d, mod3, gpre, oa, ob, oc, of, obw, proj, gnorm, wg, wbr, wo, gpost]))


def _ffn_kernel(x_ref, mod_ref, gpre_ref, wup_ref, wdown_ref, gpost_ref, o_ref):
    d = D_MODEL
    x = x_ref[...]
    mod = mod_ref[0]
    sh2, sc2, ga2 = mod[:, 3 * d:4 * d], mod[:, 4 * d:5 * d], mod[:, 5 * d:6 * d]
    xs = _row_groups(x)
    hs = [(_rmsnorm_rows(xg, gpre_ref[...], d) * (1.0 + sc2) + sh2).astype(BF16) for xg in xs]
    ups = [_dot(h, wup_ref[...]) for h in hs]
    acts = [(_silu(up[:, 0:D_FF]) * up[:, D_FF:2 * D_FF]).astype(BF16) for up in ups]
    fs = [_dot(act, wdown_ref[...]) for act in acts]
    o_ref[...] = jnp.concatenate([xg + ga2 * _rmsnorm_rows(f, gpost_ref[...], d) for xg, f in zip(xs, fs)], axis=0)


def _ffn_call(x2d, mod3, gpre, wup, wdown, gpost, seq_len, tm):
    n, d = x2d.shape
    tiles_per_seq = seq_len // tm
    row = lambda w: pl.BlockSpec((tm, w), lambda i: (i, 0))
    full = _resident
    return pl.pallas_call(
        _ffn_kernel,
        grid=(n // tm,),
        in_specs=[row(d), _mod_block(mod3, tiles_per_seq), full(gpre), full(wup), full(wdown), full(gpost)],
        out_specs=row(d),
        out_shape=jax.ShapeDtypeStruct((n, d), F32),
        compiler_params=_cparams(("parallel",)),
        name="ffn",
    )(*_operands([x2d, mod3, gpre, wup, wdown, gpost]))


def _rope_tables(n_tokens, dim, reps):
    n_rows = n_tokens // GRID_W
    row = jnp.repeat(jnp.arange(n_rows), GRID_W).astype(F32)
    col = jnp.tile(jnp.arange(GRID_W), n_rows).astype(F32)
    nfreq = dim // 4
    inv = ROPE_BASE ** (-jnp.arange(nfreq, dtype=F32) / nfreq)
    ang_r = row[:, None] * inv
    ang_c = col[:, None] * inv
    ang = jnp.concatenate([ang_r, ang_r, ang_c, ang_c], axis=-1)
    cos, sin = jnp.cos(ang), jnp.sin(ang)
    even = ((jnp.arange(dim) // nfreq) % 2 == 0)[None, :]
    sin_next = jnp.where(even, -sin, 0.0)
    sin_prev = jnp.where(even, 0.0, sin)
    return tuple(jnp.tile(a, (1, reps)) for a in (cos, sin_next, sin_prev))


def _rope_tables_mla_wide(n_tokens):
    cos, sin_next, sin_prev = _rope_tables(n_tokens, B_ROPE, 1)
    tail = MLA_HEAD_LANES - B_NOPE - B_ROPE
    wide = lambda a, fill: jnp.pad(a, ((0, 0), (B_NOPE, tail)), constant_values=fill)
    return wide(cos, 1.0), wide(sin_next, 0.0), wide(sin_prev, 0.0)


def _split_cols(w):
    out, acc = [], 0
    for s in IN_SIZES:
        out.append(w[..., acc:acc + s])
        acc += s
    return out


def _pack_w_in_kernel(w_ref, w1_ref, wg_ref):
    w = w_ref[...]
    (a_q, a_k, a_v, b_cq, b_ckv, b_kr, c_in, d_qkv, d_z, d_beta, d_alpha, gate) = _split_cols(w)
    z = lambda n: jnp.zeros((w.shape[0], n), w.dtype)
    k0, k1 = a_k[:, :A_HD], a_k[:, A_HD:]
    v0, v1 = a_v[:, :A_HD], a_v[:, A_HD:]
    w1 = jnp.concatenate([a_q, k0, k0, k1, k1, v0, v0, v1, v1, b_cq, z(256 - B_Q_RANK), b_ckv,
                          b_kr, b_kr, b_kr, b_kr, c_in, d_qkv, d_z, d_beta, d_alpha, z(112)], axis=-1)
    w1_ref[...] = w1.astype(BF16)
    wg_ref[...] = gate.astype(BF16)


def _pack_w_in(w):
    w = w.astype(BF16)
    depth, d, p = w.shape
    n_gate = IN_SIZES[-1]
    tr = 256
    return pl.pallas_call(
        _pack_w_in_kernel,
        grid=(depth, d // tr),
        in_specs=[pl.BlockSpec((None, tr, p), lambda l, i: (l, i, 0))],
        out_specs=[pl.BlockSpec((None, tr, P1), lambda l, i: (l, i, 0)),
                   pl.BlockSpec((None, tr, n_gate), lambda l, i: (l, i, 0))],
        out_shape=[jax.ShapeDtypeStruct((depth, d, P1), BF16), jax.ShapeDtypeStruct((depth, d, n_gate), BF16)],
        compiler_params=_cparams(("parallel", "parallel")),
        name="pack_w_in",
    )(w)


def _expand_kv_heads(a):
    return jnp.repeat(a, A_HEADS // A_KV_HEADS, axis=-2).reshape(a.shape[:-2] + (A_HEADS * A_HD,))


def _stacked_weights(g_pre1, g_post1, g_pre2, g_post2, w_in, a_sink, b_g_cq, b_g_ckv, b_w_uq, b_w_ukv,
                     c_w_pool, c_scale, d_conv, d_a_log, d_dt_bias, d_g_norm, w_br, w_o, w_up, w_down):
    depth = w_in.shape[0]
    w1, wg = _pack_w_in(w_in)
    wuq = b_w_uq.reshape(depth, B_Q_RANK, B_HEADS, B_NOPE + B_ROPE)
    wuq = jnp.pad(wuq, ((0, 0), (0, 256 - B_Q_RANK), (0, 0), (0, MLA_HEAD_LANES - B_NOPE - B_ROPE)))
    wuq = wuq.reshape(depth, 256, MLA_WIDE).astype(BF16)
    wukv = b_w_ukv.reshape(depth, B_KV_RANK, B_HEADS, B_NOPE + B_VD)
    wk = jnp.pad(wukv[..., :B_NOPE], ((0, 0), (0, 0), (0, 0), (0, MLA_HEAD_LANES - B_NOPE)))
    wukv = jnp.concatenate([wk.reshape(depth, B_KV_RANK, MLA_WIDE),
                            wukv[..., B_NOPE:].reshape(depth, B_KV_RANK, B_HEADS * B_VD)], -1).astype(BF16)
    same = jnp.eye(C_GROUPS, dtype=F32)[None, :, None, :, None]
    wpool = (c_w_pool[:, :, :, None, :] * same).reshape(depth, C_GROUPS * C_GW, C_GROUPS * C_GW).astype(BF16)
    gdn =jnp.stack([d_a_log.reshape(depth, 2 * D_HEADS), d_dt_bias.reshape(depth, 2 * D_HEADS)], axis=1)
    prow = jnp.pad(gdn, ((0, 0), (0, 0), (8, LANES - 16)))
    pcol = jnp.pad(gdn.transpose(0, 2, 1), ((0, 0), (8, 0), (0, 0)))
    row = lambda a: a[:, None, :]
    return dict(
        w1=w1, wg=wg, wuq=wuq, wukv=wukv,
        gpre1=row(g_pre1), gpost1=row(g_post1), gpre2=row(g_pre2), gpost2=row(g_post2),
        gcq=row(jnp.pad(b_g_cq, ((0, 0), (0, 256 - B_Q_RANK)))), gckv=row(b_g_ckv),
        sink=a_sink, wpool=wpool, cscale=row(c_scale),
        convw=jnp.pad(d_conv, ((0, 0), (0, SUBLANES - D_CONV), (0, 0))),
        prow=prow, pcol=pcol, gnorm=row(jnp.tile(d_g_norm, (1, D_HEADS))),
        wbr=w_br.astype(BF16), wo=w_o.astype(BF16), wup=w_up.astype(BF16), wdown=w_down.astype(BF16))


def _trunk_layer(x2d, mod3, lw, n_seq, t, ctx, tables=None):
    latent = ctx is not None
    n = x2d.shape[0]
    tm = min(512, t) if latent else 256
    tm_dense = min(512, n)
    if latent:
        proj, qw, kwr, vb, aqr, akr, qwr = _inproj_call(
            x2d, mod3, lw["gpre1"], lw["w1"], lw["gcq"], lw["wuq"], lw["gckv"], lw["wukv"], tables, t, tm_dense)
        kctx, vctx, ctx_ckv, ckr_wide, s0f, s0b = ctx
        out_a = _win_attn_call(lw["sink"], aqr, akr, proj, kctx, vctx, n_seq, t)
        out_b = _mla_lat_call(qw, qwr, kwr, vb, ctx_ckv, ckr_wide, lw["wukv"], n_seq, t, min(MLA_TQ, t))
        s0 = (s0f, s0b)
    else:
        proj, qw, kw, vb, ckvn, ak, av, kr = _inproj_call(
            x2d, mod3, lw["gpre1"], lw["w1"], lw["gcq"], lw["wuq"], lw["gckv"], lw["wukv"], None, t, tm_dense)
        out_a, out_b = _attn_ctx_call(lw["sink"], proj, qw, kw, vb, n_seq, t)
        s0 = None
    out_c, qkvn = _local_call(proj, lw["convw"], lw["wpool"], lw["cscale"], t, tm)
    o_f, o_b, s_fin = _gdn_call(qkvn, proj, lw["prow"], lw["pcol"], s0, n_seq, t)
    x1 = _merge_call(x2d, mod3, lw["gpre1"], out_a, out_b, out_c, o_f, o_b, proj, lw["gnorm"],
                     lw["wg"], lw["wbr"], lw["wo"], lw["gpost1"], t, tm_dense)
    x2 = _ffn_call(x1, mod3, lw["gpre2"], lw["wup"], lw["wdown"], lw["gpost2"], t, tm_dense)
    new_ctx = None
    if not latent:
        new_ctx = (ak.reshape(n_seq, t, A_KV_HEADS, A_HD), av.reshape(n_seq, t, A_KV_HEADS, A_HD),
                   ckvn.reshape(n_seq, t, B_KV_RANK), kr.reshape(n_seq, t, B_ROPE), s_fin[:, 0], s_fin[:, 1])
    return x2, new_ctx


def kernel(x_prompt, x_sample, cache_a_k, cache_a_v, cache_b_ckv, cache_b_krope, state_d_fwd, state_d_bwd, c, c_ctx, w_mod, b_mod, g_pre1, g_post1, g_pre2, g_post2, w_in, a_sink, b_g_cq, b_g_ckv, b_w_uq, b_w_ukv, c_w_pool, c_scale, d_conv, d_a_log, d_dt_bias, d_g_norm, w_br, w_o, w_up, w_down):
    depth = w_mod.shape[0]
    bp, tp, d = x_prompt.shape
    bs, ts, _ = x_sample.shape
    assert bs + 1 <= SUBLANES
    cond8 = jnp.zeros((SUBLANES, d), F32).at[0].set(c_ctx).at[1:1 + bs].set(c)
    mod_all = _mod_call(cond8, w_mod, b_mod).reshape(depth, SUBLANES, 1, 6 * d)
    yp = x_prompt.reshape(bp * tp, d)
    ys = x_sample.reshape(bs * ts, d)
    new = [[] for _ in range(6)]
    tables = _rope_tables(ts, A_HD, LANES // A_HD) + _rope_tables_mla_wide(ts)
    kctx_all, vctx_all = _expand_kv_heads(cache_a_k), _expand_kv_heads(cache_a_v)
    ckr_wide_all = jnp.tile(jnp.pad(cache_b_krope, ((0, 0), (0, 0), (0, 0),
                                                    (B_NOPE, MLA_HEAD_LANES - B_NOPE - B_ROPE))), (1, 1, 1, B_HEADS))
    stacked = _stacked_weights(g_pre1, g_post1, g_pre2, g_post2, w_in, a_sink, b_g_cq, b_g_ckv, b_w_uq, b_w_ukv,
                               c_w_pool, c_scale, d_conv, d_a_log, d_dt_bias, d_g_norm, w_br, w_o, w_up, w_down)
    for l in range(depth):
        lw = {k: _Layer(v, l) for k, v in stacked.items()}
        mod_ctx = _Mod(mod_all, l, 0, 1)
        mod_lat = _Mod(mod_all, l, 1, bs)
        yp, nctx = _trunk_layer(yp, mod_ctx, lw, bp, tp, None)
        for acc, val in zip(new, nctx):
            acc.append(val)
        ctx = tuple(_Layer(a, l) for a in (kctx_all, vctx_all, cache_b_ckv, ckr_wide_all, state_d_fwd, state_d_bwd))
        ys, _ = _trunk_layer(ys, mod_lat, lw, bs, ts, ctx, tables)
    return (yp.reshape(bp, tp, d), ys.reshape(bs, ts, d)) + tuple(jnp.stack(v, axis=1) for v in new)
```

```python
import functools
import math
from typing import NamedTuple

import jax
import jax.numpy as jnp
from jax import lax
from jax.experimental import pallas as pl
from jax.experimental.pallas import tpu as pltpu

F32 = jnp.float32
BF16 = jnp.bfloat16

D_MODEL = 1024
GRID_W = 64
ROPE_BASE = 10000.0
EPS = 1e-6
NEG_INF = -1e30
Q_BLOCK = 128
N_BRANCH = 4
BRANCH_W = 256
A_HEADS = 4
A_KV_HEADS = 2
A_HD = 64
A_WINDOW = 128
A_SCALE = A_HD ** -0.5
B_HEADS = 4
B_Q_RANK = 192
B_KV_RANK = 128
B_NOPE = 64
B_ROPE = 32
B_VD = 64
B_SCALE = (B_NOPE + B_ROPE) ** -0.5
C_GROUPS = 4
C_GW = 64
C_WINDOWS = (2, 4, 8, 16)
D_HEADS = 4
D_DK = 64
D_DV = 64
D_CONV = 5
D_CHUNK = 64
FF_RAW = -(-8 * D_MODEL // 3)
D_FF = -(-FF_RAW // 256) * 256
IN_SIZES = (A_HEADS * A_HD, A_KV_HEADS * A_HD, A_KV_HEADS * A_HD, B_Q_RANK, B_KV_RANK, B_ROPE,
            C_GROUPS * C_GW, D_HEADS * (2 * D_DK + D_DV), D_HEADS * D_DV, 2 * D_HEADS, 2 * D_HEADS,
            N_BRANCH * D_MODEL)

LANES = 128
SUBLANES = 8
VMEM_LIMIT_BYTES = 56 * 1024 * 1024
HALO = SUBLANES
GDN_SEQS_PER_STEP = 2
GDN_CHUNKS_PER_STEP = 4
MLA_HEAD_LANES = LANES
MLA_WIDE = B_HEADS * MLA_HEAD_LANES
LOG2E = math.log2(math.e)
DENSE_ROW_GROUPS = 2
MLA_TQ = 512
MLA_ROW_GROUP = 32
MLA_COL_BLOCK = 256
WIN_ATTN_TQ = 256

C_AQ = 0
C_AK = 256
C_AV = 512
C_CQ = 768
C_CKV = 1024
C_KR = 1152
C_CIN = 1280
C_QKV = 1536
C_Z = 2304
C_AB = 2560
P1 = 2688


def _cparams(sem):
    return pltpu.CompilerParams(dimension_semantics=sem, vmem_limit_bytes=VMEM_LIMIT_BYTES)


class _Layer(NamedTuple):
    arr: jax.Array
    l: int


def _resident(p):
    shape = p.arr.shape[1:]
    return pl.BlockSpec((None,) + shape, lambda *_: (p.l,) + (0,) * len(shape), pipeline_mode=pl.Buffered(1))


class _Mod(NamedTuple):
    arr: jax.Array
    l: int
    first: int
    count: int


def _mod_block(m, tiles_per_seq):
    width = m.arr.shape[-1]
    if m.count == 1:
        return pl.BlockSpec((None, 1, 1, width), lambda i: (m.l, m.first, 0, 0))
    return pl.BlockSpec((None, 1, 1, width), lambda i: (m.l, m.first + i // tiles_per_seq, 0, 0))


def _cache_block(p):
    return pl.BlockSpec((1, None) + p.arr.shape[2:], lambda b, i: (b, p.l, 0, 0))


def _operands(args):
    return [a.arr if isinstance(a, (_Layer, _Mod)) else a for a in args]


def _dot(a, b):
    return jnp.dot(a, b, preferred_element_type=F32)


def _dot_nt(a, b):
    return lax.dot_general(a, b, (((1,), (1,)), ((), ())), preferred_element_type=F32)


def _split3(x):
    hi = x.astype(BF16)
    r1 = x - hi.astype(F32)
    mid = r1.astype(BF16)
    lo = (r1 - mid.astype(F32)).astype(BF16)
    return hi, mid, lo


def _dot_sel_right(x, sel):
    hi, mid, lo = _split3(x)
    return _dot(hi, sel) + _dot(mid, sel) + _dot(lo, sel)


def _silu(x):
    return x * jax.nn.sigmoid(x)


def _softplus(x):
    return jnp.maximum(x, 0.0) + jnp.log1p(jnp.exp(-jnp.abs(x)))


def _head_mask(width, n_lanes, h):
    lane = lax.broadcasted_iota(jnp.int32, (1, n_lanes), 1)
    return (lane // width) == h


def _group_ones(n, width):
    r = lax.broadcasted_iota(jnp.int32, (n, n), 0) // width
    c = lax.broadcasted_iota(jnp.int32, (n, n), 1) // width
    return jnp.where(r == c, 1.0, 0.0).astype(BF16)


def _mod_kernel(c_ref, w_ref, b_ref, o_ref):
    s = _silu(c_ref[...])
    o_ref[0] = _dot(s.astype(BF16), w_ref[0].astype(BF16)) + b_ref[0]


def _mod_call(cond8, w_mod, b_mod):
    depth, d, n = w_mod.shape
    tn = 1536
    return pl.pallas_call(
        _mod_kernel,
        grid=(depth, n // tn),
        in_specs=[pl.BlockSpec((SUBLANES, d), lambda l, j: (0, 0)),
                  pl.BlockSpec((1, d, tn), lambda l, j: (l, 0, j)),
                  pl.BlockSpec((1, 1, tn), lambda l, j: (l, 0, j))],
        out_specs=pl.BlockSpec((1, SUBLANES, tn), lambda l, j: (l, 0, j)),
        out_shape=jax.ShapeDtypeStruct((depth, SUBLANES, n), F32),
        compiler_params=_cparams(("parallel", "parallel")),
        name="mod",
    )(cond8, w_mod, b_mod.reshape(depth, 1, n))


def _rmsnorm_rows(x, g, n):
    ms = jnp.sum(x * x, axis=-1, keepdims=True) * (1.0 / n)
    return x * lax.rsqrt(ms + EPS) * g


def _rope(x, cos, sin_next, sin_prev, quarter):
    n = x.shape[-1]
    return x * cos + pltpu.roll(x, n - quarter, 1) * sin_next + pltpu.roll(x, quarter, 1) * sin_prev


def _mla_wide_kr(kr_rep):
    lane = lax.broadcasted_iota(jnp.int32, (1, MLA_HEAD_LANES), 1)
    keep = (lane >= B_NOPE) & (lane < B_NOPE + B_ROPE)
    return jnp.concatenate([jnp.where(keep, kr_rep, 0.0)] * B_HEADS, axis=1)


def _inproj_kernel(latent, *refs):
    if latent:
        (x_ref, mod_ref, gpre_ref, w1_ref, gcq_ref, wuq_ref, gckv_ref, wukv_ref,
         ca_ref, sna_ref, spa_ref, cb_ref, snb_ref, spb_ref,
         proj_ref, qw_ref, kw_ref, vb_ref, aqr_ref, akr_ref, qwr_ref) = refs
    else:
        (x_ref, mod_ref, gpre_ref, w1_ref, gcq_ref, wuq_ref, gckv_ref, wukv_ref,
         proj_ref, qw_ref, kw_ref, vb_ref, ckv_ref, ak_ref, av_ref, kr_ref) = refs
    d = D_MODEL
    x = x_ref[...]
    mod = mod_ref[0]
    sh1 = mod[:, 0:d]
    sc1 = mod[:, d:2 * d]
    h = _rmsnorm_rows(x, gpre_ref[...], d) * (1.0 + sc1) + sh1
    proj = _dot(h.astype(BF16), w1_ref[...])
    proj_ref[...] = proj
    cqn = _rmsnorm_rows(proj[:, C_CQ:C_CQ + 256], gcq_ref[...], B_Q_RANK)
    qw = _dot(cqn.astype(BF16), wuq_ref[...]) * (B_SCALE * LOG2E)
    qw_ref[...] = qw.astype(BF16)
    ckvn = _rmsnorm_rows(proj[:, C_CKV:C_CKV + B_KV_RANK], gckv_ref[...], B_KV_RANK)
    if not latent:
        unexpand = lambda c: jnp.concatenate([proj[:, c:c + A_HD], proj[:, c + 2 * A_HD:c + 3 * A_HD]], axis=1)
        ckv_ref[...] = ckvn
        ak_ref[...] = unexpand(C_AK)
        av_ref[...] = unexpand(C_AV)
        kr_ref[...] = proj[:, C_KR:C_KR + B_ROPE]
    knv =_dot(ckvn.astype(BF16), wukv_ref[...])
    vb_ref[...] = knv[:, MLA_WIDE:].astype(BF16)
    kr_wide = _mla_wide_kr(proj[:, C_KR:C_KR + 128])
    if latent:
        ca, sna, spa = (jnp.concatenate([r[...]] * (256 // LANES), axis=1) for r in (ca_ref, sna_ref, spa_ref))
        cb, snb, spb = (jnp.concatenate([r[...]] * B_HEADS, axis=1) for r in (cb_ref, snb_ref, spb_ref))
        aqr_ref[...] = _rope(proj[:, C_AQ:C_AQ + 256], ca, sna, spa, A_HD // 4).astype(BF16)
        akr_ref[...] = _rope(proj[:, C_AK:C_AK + 256], ca, sna, spa, A_HD // 4).astype(BF16)
        qwr_ref[...] = _rope(qw, cb, snb, spb, B_ROPE // 4).astype(BF16)
        kr_wide = _rope(kr_wide, cb, snb, spb, B_ROPE // 4)
    kw_ref[...] = (knv[:, :MLA_WIDE] + kr_wide).astype(BF16)


def _inproj_call(x2d, mod3, gpre, w1, gcq, wuq, gckv, wukv, tables, seq_len, tm):
    n, d = x2d.shape
    latent = tables is not None
    tiles_per_seq = seq_len // tm if latent else 1
    row = lambda w, dt=F32: (pl.BlockSpec((tm, w), lambda i: (i, 0)), jax.ShapeDtypeStruct((n, w), dt))
    params = [gpre, w1, gcq, wuq, gckv, wukv]
    in_specs = [pl.BlockSpec((tm, d), lambda i: (i, 0)),
                _mod_block(mod3, seq_len // tm)] + [_resident(p) for p in params]
    args = [x2d, mod3] + params
    outs = [row(P1), row(MLA_WIDE, BF16), row(MLA_WIDE, BF16), row(256, BF16)]
    if latent:
        for t in tables:
            in_specs.append(pl.BlockSpec((tm, t.shape[1]), lambda i: (i % tiles_per_seq, 0)))
            args.append(t)
        outs += [row(256, BF16), row(256, BF16), row(MLA_WIDE, BF16)]
    else:
        outs += [row(B_KV_RANK), row(A_KV_HEADS * A_HD), row(A_KV_HEADS * A_HD), row(B_ROPE)]
    return pl.pallas_call(
        functools.partial(_inproj_kernel, latent),
        grid=(n // tm,),
        in_specs=in_specs,
        out_specs=[o[0] for o in outs],
        out_shape=[o[1] for o in outs],
        compiler_params=_cparams(("parallel",)),
        name="inproj_lat" if latent else "inproj_ctx",
    )(*_operands(args))


def _softmax_pv(scores, values, scale, sink, hm):
    c1 = None if scale is None else scale * LOG2E
    m = jnp.max(scores[0], axis=-1, keepdims=True)
    for s in scores[1:]:
        m = jnp.maximum(m, jnp.max(s, axis=-1, keepdims=True))
    m2 = m if c1 is None else m * c1
    if sink is not None:
        m2 = jnp.maximum(m2, sink * LOG2E)
    den = None
    pv = None
    for s, v in zip(scores, values):
        p = jnp.exp2((s if c1 is None else s * c1) - m2)
        ps = jnp.sum(p, axis=-1, keepdims=True)
        den = ps if den is None else den + ps
        t = _dot(p.astype(BF16), v)
        pv = t if pv is None else pv + t
    if sink is not None:
        den = den + jnp.exp2(sink * LOG2E - m2)
    return jnp.where(hm, pv / den, 0.0)


def _attn_ctx_kernel(layer, sink_ref, q_ref, k_ref, v_ref, qw_ref, kw_ref, vb_ref, oa_ref, ob_ref):
    q = q_ref[...]
    k = k_ref[...].astype(BF16)
    v = v_ref[...].astype(BF16)
    acc = jnp.zeros(q.shape, F32)
    for h in range(A_HEADS):
        hm = _head_mask(A_HD, 256, h)
        s = _dot_nt(jnp.where(hm, q, 0.0).astype(BF16), k)
        acc = acc + _softmax_pv([s], [v], A_SCALE, sink_ref[layer, h], hm)
    oa_ref[...] = acc
    vb = vb_ref[...]
    acc = jnp.zeros((q.shape[0], B_HEADS * B_VD), F32)
    for h in range(B_HEADS):
        hs = slice(h * MLA_HEAD_LANES, (h + 1) * MLA_HEAD_LANES)
        s = _dot_nt(qw_ref[:, hs], kw_ref[:, hs])
        acc = acc + _softmax_pv([s], [vb], None, None, _head_mask(B_VD, 256, h))
    ob_ref[...] = acc


def _attn_ctx_call(sink, proj, qw, kw, vb, n_seq, t):
    n = proj.shape[0]
    col = lambda w, j: pl.BlockSpec((t, w), lambda b: (b, j))
    out = (pl.BlockSpec((t, 256), lambda b: (b, 0)), jax.ShapeDtypeStruct((n, 256), F32))
    return pl.pallas_call(
        functools.partial(_attn_ctx_kernel, sink.l),
        grid=(n_seq,),
        in_specs=[pl.BlockSpec(memory_space=pltpu.SMEM),
                  col(256, C_AQ // 256), col(256, C_AK // 256), col(256, C_AV // 256),
                  col(MLA_WIDE, 0), col(MLA_WIDE, 0), col(256, 0)],
        out_specs=[out[0], out[0]],
        out_shape=[out[1], out[1]],
        compiler_params=_cparams(("parallel",)),
        name="attn_ctx",
    )(sink.arr, proj, proj, proj, qw, kw, vb)


def _win_attn_kernel(t, layer, sink_ref, qr_ref, q_ref, kp_ref, kc_ref, kn_ref, vp_ref, vc_ref, vn_ref,
                     kctx_ref, vctx_ref, o_ref):
    i = pl.program_id(1)
    qr = qr_ref[...]
    q = q_ref[...]
    tq = q.shape[0]
    kl = jnp.concatenate([kp_ref[...], kc_ref[...], kn_ref[...]], axis=0)
    vl = jnp.concatenate([vp_ref[...], vc_ref[...], vn_ref[...]], axis=0).astype(BF16)
    kctx = kctx_ref[0].astype(BF16)
    vctx = vctx_ref[0].astype(BF16)
    q_pos = i * tq + lax.broadcasted_iota(jnp.int32, (tq, tq + 2 * A_WINDOW), 0)
    k_pos = i * tq - A_WINDOW + lax.broadcasted_iota(jnp.int32, (tq, tq + 2 * A_WINDOW), 1)
    valid = (jnp.abs(q_pos - k_pos) <= A_WINDOW) & (k_pos >= 0) & (k_pos < t)
    acc = jnp.zeros(q.shape, F32)
    for h in range(A_HEADS):
        hm = _head_mask(A_HD, 256, h)
        s_loc = jnp.where(valid, _dot_nt(jnp.where(hm, qr, 0.0).astype(BF16), kl), NEG_INF)
        s_ctx = _dot_nt(jnp.where(hm, q, 0.0).astype(BF16), kctx)
        acc = acc + _softmax_pv([s_loc, s_ctx], [vl, vctx], A_SCALE, sink_ref[layer, h], hm)
    o_ref[...] = acc


def _win_attn_call(sink, aqr, akr, proj, kctx, vctx, n_seq, t):
    n = proj.shape[0]
    tq = min(WIN_ATTN_TQ, t)
    nq = t // tq
    r = tq // A_WINDOW
    nh = t // A_WINDOW
    cur = lambda j: pl.BlockSpec((tq, 256), lambda b, i: (b * nq + i, j))
    prev = lambda j: pl.BlockSpec((A_WINDOW, 256), lambda b, i: (b * nh + jnp.maximum(i * r - 1, 0), j))
    nxt = lambda j: pl.BlockSpec((A_WINDOW, 256), lambda b, i: (b * nh + jnp.minimum((i + 1) * r, nh - 1), j))
    ctx = _cache_block(kctx)
    return pl.pallas_call(
        functools.partial(_win_attn_kernel, t, sink.l),
        grid=(n_seq, nq),
        in_specs=[pl.BlockSpec(memory_space=pltpu.SMEM),
                  cur(0), cur(C_AQ // 256),
                  prev(0), cur(0), nxt(0),
                  prev(C_AV // 256), cur(C_AV // 256), nxt(C_AV // 256),
                  ctx, ctx],
        out_specs=pl.BlockSpec((tq, 256), lambda b, i: (b * nq + i, 0)),
        out_shape=jax.ShapeDtypeStruct((n, 256), F32),
        compiler_params=_cparams(("parallel", "parallel")),
        name="win_attn",
    )(sink.arr, aqr, proj, akr, akr, akr, proj, proj, proj, kctx.arr, vctx.arr)


def _mla_lat_kernel(qw_ref, qwr_ref, kwr_ref, vb_ref, cckv_ref, ckr_ref, wukv_ref, o_ref, s_ref, p_ref, den_ref):
    kvc = _dot(cckv_ref[0].astype(BF16), wukv_ref[...])
    kwc = (kvc[:, :MLA_WIDE] + ckr_ref[0]).astype(BF16)
    vc = kvc[:, MLA_WIDE:].astype(BF16)
    tq = qw_ref.shape[0]
    t = kwr_ref.shape[0]
    n_keys = s_ref.shape[2]
    cb = MLA_COL_BLOCK
    rg = MLA_ROW_GROUP

    def scores_into(slot, h):
        hs = slice(h * MLA_HEAD_LANES, (h + 1) * MLA_HEAD_LANES)
        s_ref[slot, :, 0:t] = _dot_nt(qwr_ref[:, hs], kwr_ref[:, hs])
        s_ref[slot, :, t:n_keys] = _dot_nt(qw_ref[:, hs], kwc[:, hs])

    scores_into(0, 0)
    acc = jnp.zeros((tq, B_HEADS * B_VD), F32)
    for h in range(B_HEADS):
        slot = h % 2
        if h + 1 < B_HEADS:
            scores_into(1 - slot, h + 1)
        for r0 in range(0, tq, rg):
            rows = slice(r0, r0 + rg)
            m = s_ref[slot, rows, 0:cb]
            for c0 in range(cb, n_keys, cb):
                m = jnp.maximum(m, s_ref[slot, rows, c0:c0 + cb])
            m2 = jnp.max(m, axis=-1, keepdims=True)
            part = None
            for c0 in range(0, n_keys, cb):
                p = jnp.exp2(s_ref[slot, rows, c0:c0 + cb] - m2)
                part = p if part is None else part + p
                p_ref[rows, c0:c0 + cb] = p.astype(BF16)
            den_ref[rows, :] = jnp.broadcast_to(jnp.sum(part, axis=-1, keepdims=True), (rg, LANES))
        pv = _dot(p_ref[:, 0:t], vb_ref[...]) + _dot(p_ref[:, t:n_keys], vc)
        acc = acc + jnp.where(_head_mask(B_VD, 256, h), pv / den_ref[:, 0:1], 0.0)
    o_ref[...] = acc


def _mla_lat_call(qw, qwr, kwr, vb, cckv, ckr_wide, wukv, n_seq, t, tq):
    n = qw.shape[0]
    nq = t // tq
    qblk = lambda w: pl.BlockSpec((tq, w), lambda b, i: (b * nq + i, 0))
    seq = lambda w: pl.BlockSpec((t, w), lambda b, i: (b, 0))
    ctx = _cache_block
    return pl.pallas_call(
        _mla_lat_kernel,
        grid=(n_seq, nq),
        in_specs=[qblk(MLA_WIDE), qblk(MLA_WIDE), seq(MLA_WIDE), seq(256), ctx(cckv), ctx(ckr_wide),
                  _resident(wukv)],
        out_specs=pl.BlockSpec((tq, 256), lambda b, i: (b * nq + i, 0)),
        out_shape=jax.ShapeDtypeStruct((n, 256), F32),
        scratch_shapes=[pltpu.VMEM((2, tq, t + cckv.arr.shape[2]), F32), pltpu.VMEM((tq, t + cckv.arr.shape[2]), BF16),
                        pltpu.VMEM((tq, LANES), F32)],
        compiler_params=_cparams(("parallel", "parallel")),
        name="mla_lat",
    )(qw, qwr, kwr, vb, cckv.arr, ckr_wide.arr, wukv.arr)


def _with_halo(prev_ref, cur_ref, next_ref, has_prev, has_next):
    prev = jnp.where(has_prev, prev_ref[...], 0.0)
    nxt = jnp.where(has_next, next_ref[...], 0.0)
    return jnp.concatenate([prev, cur_ref[...], nxt], axis=0)


def _shift_rows(x, k):
    n = x.shape[0]
    return pltpu.roll(x, (-k) % n, 0)


def _local_kernel(t, tm, cp_ref, cc_ref, cn_ref, qp_ref, qc_ref, qn_ref, convw_ref, wpool_ref, cscale_ref,
                  oc_ref, qkv_ref):
    i = pl.program_id(0)
    tiles_per_seq = t // tm
    j = i % tiles_per_seq
    has_prev = j > 0
    has_next = j < tiles_per_seq - 1
    x = _with_halo(cp_ref, cc_ref, cn_ref, has_prev, has_next)
    p2 = x + _shift_rows(x, -1)
    p4 = _shift_rows(p2, 1) + _shift_rows(p2, -1)
    p8 = _shift_rows(p4, 2) + _shift_rows(p4, -2)
    p16 = _shift_rows(p8, 4) + _shift_rows(p8, -4)
    grp = lax.broadcasted_iota(jnp.int32, (1, 256), 1) // C_GW
    win = jnp.where(grp == 0, p2, jnp.where(grp == 1, p4, jnp.where(grp == 2, p8, p16)))[HALO:HALO + tm]
    pos = j * tm + lax.broadcasted_iota(jnp.int32, (tm, 256), 0)
    half = jnp.where(grp == 0, 1, jnp.where(grp == 1, 2, jnp.where(grp == 2, 4, 8)))
    cnt = (jnp.minimum(pos + half, t) - jnp.maximum(pos - half, 0)).astype(F32)
    y = win / cnt - cc_ref[...]
    oc_ref[...] = _dot(y.astype(BF16), wpool_ref[...]) * cscale_ref[...]
    xq = _with_halo(qp_ref, qc_ref, qn_ref, has_prev, has_next)
    w = convw_ref[...]
    pad = D_CONV // 2
    acc = None
    for tap in range(D_CONV):
        term = _shift_rows(xq, tap - pad) * w[tap:tap + 1, :]
        acc = term if acc is None else acc + term
    u = _silu(acc[HALO:HALO + tm])
    ones = _group_ones(256, D_DK)
    q = u[:, 0:256]
    k = u[:, 256:512]
    q = q * lax.rsqrt(_dot_sel_right(q * q, ones) + EPS) * (D_DK ** -0.5)
    k = k * lax.rsqrt(_dot_sel_right(k * k, ones) + EPS)
    qkv_ref[:, 0:256] = q
    qkv_ref[:, 256:512] = k
    qkv_ref[:, 512:768] = u[:, 512:768]


def _local_call(proj, convw8, wpool_bd, cscale, t, tm):
    n = proj.shape[0]
    hb = tm // HALO
    nblk = n // HALO
    cur = lambda w, c: pl.BlockSpec((tm, w), lambda i: (i, c))
    prev = lambda w, c: pl.BlockSpec((HALO, w), lambda i: (jnp.maximum(i * hb - 1, 0), c))
    nxt = lambda w, c: pl.BlockSpec((HALO, w), lambda i: (jnp.minimum((i + 1) * hb, nblk - 1), c))
    full = _resident
    cc, qc = C_CIN // 256, C_QKV // 768
    return pl.pallas_call(
        functools.partial(_local_kernel, t, tm),
        grid=(n // tm,),
        in_specs=[prev(256, cc), cur(256, cc), nxt(256, cc), prev(768, qc), cur(768, qc), nxt(768, qc),
                  full(convw8), full(wpool_bd), full(cscale)],
        out_specs=[pl.BlockSpec((tm, 256), lambda i: (i, 0)), pl.BlockSpec((tm, 768), lambda i: (i, 0))],
        out_shape=[jax.ShapeDtypeStruct((n, 256), F32), jax.ShapeDtypeStruct((n, 768), F32)],
        compiler_params=_cparams(("parallel",)),
        name="local",
    )(proj, proj, proj, proj, proj, proj, convw8.arr, wpool_bd.arr, cscale.arr)


def _gdn_kernel(n_steps, g_chunks, n_sq, has_s0, xf_ref, xb_ref, abf_ref, abb_ref, prow_ref, pcol_ref, *refs):
    if has_s0:
        s0f_ref, s0b_ref, of_ref, ob_ref, sfin_ref, s_ref = refs
    else:
        of_ref, ob_ref, sfin_ref, s_ref = refs
    i = pl.program_id(1)
    c = D_CHUNK
    nh = D_HEADS

    @pl.when(i == 0)
    def _():
        if has_s0:
            for sq in range(n_sq):
                s_ref[sq * nh:(sq + 1) * nh] = s0f_ref[sq]
                s_ref[(n_sq + sq) * nh:(n_sq + sq + 1) * nh] = s0b_ref[sq]
        else:
            s_ref[...] = jnp.zeros(s_ref.shape, F32)

    r = lax.broadcasted_iota(jnp.int32, (c, c), 0)
    cidx = lax.broadcasted_iota(jnp.int32, (c, c), 1)
    lower = jnp.where(cidx <= r, 1.0, 0.0).astype(BF16)
    upper = jnp.where(cidx >= r, 1.0, 0.0).astype(BF16)
    prow = prow_ref[...]
    pcol = pcol_ref[...]

    eye = jnp.where(cidx == r, 1.0, 0.0).astype(BF16)
    sel_lower = jnp.concatenate([lower] * 3, axis=1)
    sel_upper = jnp.concatenate([upper] * 3, axis=1)
    pick = jnp.where(lax.broadcasted_iota(jnp.int32, (16, LANES), 0) == lax.broadcasted_iota(jnp.int32, (16, LANES), 1),
                     1.0, 0.0).astype(BF16)
    sel_rows = jnp.concatenate([pick] * 3, axis=1)
    sel_lower_t = jnp.concatenate([lower] * 3, axis=0)
    sel_upper_t = jnp.concatenate([upper] * 3, axis=0)

    bdot = lambda a, b: jnp.einsum('bik,bkj->bij', a, b, preferred_element_type=F32)
    bdot_nt = lambda a, b: jnp.einsum('bik,bjk->bij', a, b, preferred_element_type=F32)

    def prepare(steps):
        nb = 2 * n_sq * len(steps) * nh
        is_fwd = lax.broadcasted_iota(jnp.int32, (nb, c, c), 0) < nb // 2
        r3 = lax.broadcasted_iota(jnp.int32, (nb, c, c), 1)
        c3 = lax.broadcasted_iota(jnp.int32, (nb, c, c), 2)
        ahead = jnp.where(is_fwd, r3 - c3, c3 - r3)
        incl = ahead >= 0
        strict = ahead > 0
        q_l, k_l, v_l, b_l, gcol_l, grow_l = [], [], [], [], [], []
        for d, sq, step in [(d, sq, step) for d in range(2) for sq in range(n_sq) for step in steps]:
            x_ref, ab_ref = ((xf_ref, abf_ref), (xb_ref, abb_ref))[d]
            gi = step if d == 0 else g_chunks - 1 - step
            rows = slice(gi * c, (gi + 1) * c)
            x = x_ref[sq, rows, :]
            ab = ab_ref[sq, rows, :]
            abr = _dot_nt(sel_rows, jnp.concatenate(_split3(ab), axis=1))
            beta_c = jax.nn.sigmoid(ab)
            g_c = -jnp.exp(prow[0:1, :]) * _softplus(ab + prow[1:2, :])
            g_r = -jnp.exp(pcol[:, 0:1]) * _softplus(abr + pcol[:, 1:2])
            if d == 0:
                gc_c = _dot(sel_lower, jnp.concatenate(_split3(g_c), axis=0))
                gc_r = _dot(jnp.concatenate(_split3(g_r), axis=1), sel_upper_t)
            else:
                gc_c = _dot(sel_upper, jnp.concatenate(_split3(g_c), axis=0))
                gc_r = _dot(jnp.concatenate(_split3(g_r), axis=1), sel_lower_t)
            for h in range(nh):
                hs = slice(h * D_DK, (h + 1) * D_DK)
                lane = nh * d + h
                q_l.append(x[:, 0:256][:, hs])
                k_l.append(x[:, 256:512][:, hs])
                v_l.append(x[:, 512:768][:, hs])
                b_l.append(beta_c[:, lane:lane + 1])
                gcol_l.append(gc_c[:, 8 + lane:9 + lane])
                grow_l.append(gc_r[8 + lane:9 + lane, :])
        qh, kh, vh = jnp.stack(q_l), jnp.stack(k_l), jnp.stack(v_l)
        bcol, gcol, grow = jnp.stack(b_l), jnp.stack(gcol_l), jnp.stack(grow_l)
        glast = jnp.concatenate([gcol[:nb // 2, c - 1:c, :], gcol[nb // 2:, 0:1, :]], axis=0)
        decay = jnp.exp(jnp.where(incl, gcol - grow, NEG_INF))
        kb = kh * bcol
        kq = bdot_nt(jnp.concatenate([kb, qh], axis=1).astype(BF16), kh.astype(BF16))
        p = -jnp.where(strict, kq[:, 0:c] * decay, 0.0)
        egc = jnp.exp(gcol)
        xs = jnp.concatenate([vh * bcol, kb * egc], axis=2)
        for it in range(6):
            ph = p.astype(BF16)
            pl_ = (p - ph.astype(F32)).astype(BF16)
            lhs = jnp.concatenate([ph, pl_, ph], axis=2)
            yh = xs.astype(BF16)
            yl = (xs - yh.astype(F32)).astype(BF16)
            if it < 5:
                yh = jnp.concatenate([yh, ph], axis=2)
                yl = jnp.concatenate([yl, pl_], axis=2)
            m = bdot(lhs, jnp.concatenate([yh, yh, yl], axis=1))
            xs = xs + m[:, :, 0:2 * D_DK]
            if it < 5:
                p = m[:, :, 2 * D_DK:3 * D_DK]
        u = xs[:, :, 0:D_DV]
        w = xs[:, :, D_DV:2 * D_DV]
        a = kq[:, c:2 * c] * decay
        kd = (kh * jnp.exp(glast - gcol)).astype(BF16)
        kdt = bdot_nt(jnp.broadcast_to(eye, (nb, c, c)), kd).astype(BF16)
        lhs_s = jnp.concatenate([w, qh * egc], axis=1).astype(BF16)
        lhs_v = jnp.concatenate([a.astype(BF16), kdt], axis=1)
        return u, lhs_s, lhs_v, jnp.exp(glast)

    prep = prepare(list(range(g_chunks)))

    def chains(arr, step):
        los = [((d * n_sq + sq) * g_chunks + step) * nh for d in range(2) for sq in range(n_sq)]
        return jnp.concatenate([arr[lo:lo + nh] for lo in los], axis=0)

    s = s_ref[...]
    outs = [[[None] * g_chunks for _ in range(n_sq)] for _ in range(2)]
    for step in range(g_chunks):
        u, lhs_s, lhs_v, eg = (chains(arr, step) for arr in prep)
        ws_qs = bdot(lhs_s, s.astype(BF16))
        vb = (u - ws_qs[:, 0:c]).astype(BF16)
        av = bdot(lhs_v, vb)
        o = ws_qs[:, c:2 * c] + av[:, 0:c]
        s = s * eg + av[:, c:2 * c]
        for d in range(2):
            gi = step if d == 0 else g_chunks - 1 - step
            for sq in range(n_sq):
                base = (d * n_sq + sq) * nh
                outs[d][sq][gi] = jnp.concatenate([o[base + h] for h in range(nh)], axis=1)
    for sq in range(n_sq):
        of_ref[sq] = jnp.concatenate(outs[0][sq], axis=0)
        ob_ref[sq] = jnp.concatenate(outs[1][sq], axis=0)
    s_ref[...] = s

    @pl.when(i == n_steps - 1)
    def _():
        for d in range(2):
            for sq in range(n_sq):
                sfin_ref[sq, d] = s[(d * n_sq + sq) * nh:(d * n_sq + sq + 1) * nh]


def _gdn_call(qkvn, proj, prow, pcol, s0, n_seq, t):
    g = GDN_CHUNKS_PER_STEP
    n_sq = math.gcd(n_seq, GDN_SEQS_PER_STEP)
    ns = t // (D_CHUNK * g)
    rows = D_CHUNK * g
    fwd = lambda i: i
    bwd = lambda i: ns - 1 - i
    x3 = qkvn.reshape(n_seq, t, 768)
    p3 = proj.reshape(n_seq, t, P1)
    xblk = lambda f: pl.BlockSpec((n_sq, rows, 768), lambda b, i: (b, f(i), 0))
    abblk = lambda f: pl.BlockSpec((n_sq, rows, 128), lambda b, i: (b, f(i), C_AB // 128))
    oblk = lambda f: pl.BlockSpec((n_sq, rows, 256), lambda b, i: (b, f(i), 0))
    sblk = pl.BlockSpec((n_sq, 2, D_HEADS, D_DK, D_DV), lambda b, i: (b, 0, 0, 0, 0))
    osd = jax.ShapeDtypeStruct((n_seq, t, 256), F32)
    in_specs = [xblk(fwd), xblk(bwd), abblk(fwd), abblk(bwd), _resident(prow), _resident(pcol)]
    args = [x3, x3, p3, p3, prow.arr, pcol.arr]
    if s0 is not None:
        for st in s0:
            in_specs.append(pl.BlockSpec((n_sq, None, D_HEADS, D_DK, D_DV), lambda b, i, l=st.l: (b, l, 0, 0, 0)))
            args.append(st.arr)
    o_f, o_b, s_fin = pl.pallas_call(
        functools.partial(_gdn_kernel, ns, g, n_sq, s0 is not None),
        grid=(n_seq // n_sq, ns),
        in_specs=in_specs,
        out_specs=[oblk(fwd), oblk(bwd), sblk],
        out_shape=[osd, osd, jax.ShapeDtypeStruct((n_seq, 2, D_HEADS, D_DK, D_DV), F32)],
        scratch_shapes=[pltpu.VMEM((2 * n_sq * D_HEADS, D_DK, D_DV), F32)],
        compiler_params=_cparams(("parallel", "arbitrary")),
        name="gdn",
    )(*args)
    return o_f.reshape(n_seq * t, 256), o_b.reshape(n_seq * t, 256), s_fin


def _row_groups(x):
    rows = x.shape[0] // DENSE_ROW_GROUPS
    return [x[g * rows:(g + 1) * rows] for g in range(DENSE_ROW_GROUPS)]


def _merge_kernel(x_ref, mod_ref, gpre_ref, oa_ref, ob_ref, oc_ref, of_ref, obw_ref, z_ref, gnorm_ref,
                  wg_ref, wbr_ref, wo_ref, gpost_ref, o_ref):
    d = D_MODEL
    mod = mod_ref[0]
    sh1, sc1, ga1 = mod[:, 0:d], mod[:, d:2 * d], mod[:, 2 * d:3 * d]
    xs = _row_groups(x_ref[...])
    hs = [(_rmsnorm_rows(xg, gpre_ref[...], d) * (1.0 + sc1) + sh1).astype(BF16) for xg in xs]
    ods = []
    for o, z in zip(_row_groups(of_ref[...] + obw_ref[...]), _row_groups(z_ref[...])):
        ms = _dot_sel_right(o * o, _group_ones(256, D_DV)) * (1.0 / D_DV)
        ods.append(o * lax.rsqrt(ms + EPS) * gnorm_ref[...] * _silu(z))
    branches = [_row_groups(oa_ref[...]), _row_groups(ob_ref[...]), _row_groups(oc_ref[...]), ods]
    accs = [None] * len(xs)
    for m in range(N_BRANCH):
        gates = [jax.nn.sigmoid(_dot(h, wg_ref[:, m * d:(m + 1) * d])) for h in hs]
        brs = [_dot(br.astype(BF16), wbr_ref[m]) for br in branches[m]]
        accs = [g * b if a is None else a + g * b for a, g, b in zip(accs, gates, brs)]
    mixes = [_dot(a.astype(BF16), wo_ref[...]) for a in accs]
    o_ref[...] = jnp.concatenate([xg + ga1 * _rmsnorm_rows(mix, gpost_ref[...], d) for xg, mix in zip(xs, mixes)],
                                 axis=0)


def _merge_call(x2d, mod3, gpre, oa, ob, oc, of, obw, proj, gnorm, wg, wbr, wo, gpost, seq_len, tm):
    n, d = x2d.shape
    tiles_per_seq = seq_len // tm
    row = lambda w, c=0: pl.BlockSpec((tm, w), lambda i: (i, c))
    full = _resident
    return pl.pallas_call(
        _merge_kernel,
        grid=(n // tm,),
        in_specs=[row(d), _mod_block(mod3, tiles_per_seq), full(gpre),
                  row(256), row(256), row(256), row(256), row(256), row(256, C_Z // 256), full(gnorm),
                  full(wg), full(wbr), full(wo), full(gpost)],
        out_specs=row(d),
        out_shape=jax.ShapeDtypeStruct((n, d), F32),
        compiler_params=_cparams(("parallel",)),
        name="merge",
    )(*_operands([x2d, mod3, gpre, oa, ob, oc, of, obw, proj, gnorm, wg, wbr, wo, gpost]))


def _ffn_kernel(x_ref, mod_ref, gpre_ref, wup_ref, wdown_ref, gpost_ref, o_ref):
    d = D_MODEL
    x = x_ref[...]
    mod = mod_ref[0]
    sh2, sc2, ga2 = mod[:, 3 * d:4 * d], mod[:, 4 * d:5 * d], mod[:, 5 * d:6 * d]
    xs = _row_groups(x)
    hs = [(_rmsnorm_rows(xg, gpre_ref[...], d) * (1.0 + sc2) + sh2).astype(BF16) for xg in xs]
    ups = [_dot(h, wup_ref[...]) for h in hs]
    acts = [(_silu(up[:, 0:D_FF]) * up[:, D_FF:2 * D_FF]).astype(BF16) for up in ups]
    fs = [_dot(act, wdown_ref[...]) for act in acts]
    o_ref[...] = jnp.concatenate([xg + ga2 * _rmsnorm_rows(f, gpost_ref[...], d) for xg, f in zip(xs, fs)], axis=0)


def _merge_ffn_kernel(x_ref, mod_ref, gpre1_ref, oa_ref, ob_ref, oc_ref, of_ref, obw_ref, z_ref, gnorm_ref,
                      wg_ref, wbr_ref, wo_ref, gpost1_ref, gpre2_ref, wup_ref, wdown_ref, gpost2_ref, o_ref):
    d = D_MODEL
    x = x_ref[...]
    mod = mod_ref[0]
    sh1, sc1, ga1 = mod[:, 0:d], mod[:, d:2 * d], mod[:, 2 * d:3 * d]
    sh2, sc2, ga2 = mod[:, 3 * d:4 * d], mod[:, 4 * d:5 * d], mod[:, 5 * d:6 * d]
    h = (_rmsnorm_rows(x, gpre1_ref[...], d) * (1.0 + sc1) + sh1).astype(BF16)
    o = of_ref[...] + obw_ref[...]
    ms = _dot_sel_right(o * o, _group_ones(256, D_DV)) * (1.0 / D_DV)
    od = o * lax.rsqrt(ms + EPS) * gnorm_ref[...] * _silu(z_ref[...])
    acc = None
    for m, br in enumerate((oa_ref[...], ob_ref[...], oc_ref[...], od)):
        gate = jax.nn.sigmoid(_dot(h, wg_ref[:, m * d:(m + 1) * d]))
        term = gate * _dot(br.astype(BF16), wbr_ref[m])
        acc = term if acc is None else acc + term
    x1 = x + ga1 * _rmsnorm_rows(_dot(acc.astype(BF16), wo_ref[...]), gpost1_ref[...], d)
    h2 = (_rmsnorm_rows(x1, gpre2_ref[...], d) * (1.0 + sc2) + sh2).astype(BF16)
    up = _dot(h2, wup_ref[...])
    act = (_silu(up[:, 0:D_FF]) * up[:, D_FF:2 * D_FF]).astype(BF16)
    o_ref[...] = x1 + ga2 * _rmsnorm_rows(_dot(act, wdown_ref[...]), gpost2_ref[...], d)


def _merge_ffn_call(x2d, mod3, gpre1, oa, ob, oc, of, obw, proj, gnorm, wg, wbr, wo, gpost1, gpre2, wup, wdown, gpost2,
                    seq_len, tm):
    n, d = x2d.shape
    tiles_per_seq = seq_len // tm
    row = lambda w, c=0: pl.BlockSpec((tm, w), lambda i: (i, c))
    full = _resident
    return pl.pallas_call(
        _merge_ffn_kernel,
        grid=(n // tm,),
        in_specs=[row(d), _mod_block(mod3, tiles_per_seq), full(gpre1),
                  row(256), row(256), row(256), row(256), row(256), row(256, C_Z // 256), full(gnorm),
                  full(wg), full(wbr), full(wo), full(gpost1), full(gpre2), full(wup), full(wdown), full(gpost2)],
        out_specs=row(d),
        out_shape=jax.ShapeDtypeStruct((n, d), F32),
        compiler_params=_cparams(("parallel",)),
        name="merge_ffn",
    )(*_operands([x2d, mod3, gpre1, oa, ob, oc, of, obw, proj, gnorm, wg, wbr, wo, gpost1, gpre2, wup, wdown, gpost2]))


def _ffn_call(x2d, mod3, gpre, wup, wdown, gpost, seq_len, tm):
    n, d = x2d.shape
    tiles_per_seq = seq_len // tm
    row = lambda w: pl.BlockSpec((tm, w), lambda i: (i, 0))
    full = _resident
    return pl.pallas_call(
        _ffn_kernel,
        grid=(n // tm,),
        in_specs=[row(d), _mod_block(mod3, tiles_per_seq), full(gpre), full(wup), full(wdown), full(gpost)],
        out_specs=row(d),
        out_shape=jax.ShapeDtypeStruct((n, d), F32),
        compiler_params=_cparams(("parallel",)),
        name="ffn",
    )(*_operands([x2d, mod3, gpre, wup, wdown, gpost]))


def _rope_tables(n_tokens, dim, reps):
    n_rows = n_tokens // GRID_W
    row = jnp.repeat(jnp.arange(n_rows), GRID_W).astype(F32)
    col = jnp.tile(jnp.arange(GRID_W), n_rows).astype(F32)
    nfreq = dim // 4
    inv = ROPE_BASE ** (-jnp.arange(nfreq, dtype=F32) / nfreq)
    ang_r = row[:, None] * inv
    ang_c = col[:, None] * inv
    ang = jnp.concatenate([ang_r, ang_r, ang_c, ang_c], axis=-1)
    cos, sin = jnp.cos(ang), jnp.sin(ang)
    even = ((jnp.arange(dim) // nfreq) % 2 == 0)[None, :]
    sin_next = jnp.where(even, -sin, 0.0)
    sin_prev = jnp.where(even, 0.0, sin)
    return tuple(jnp.tile(a, (1, reps)) for a in (cos, sin_next, sin_prev))


def _rope_tables_mla_wide(n_tokens):
    cos, sin_next, sin_prev = _rope_tables(n_tokens, B_ROPE, 1)
    tail = MLA_HEAD_LANES - B_NOPE - B_ROPE
    wide = lambda a, fill: jnp.pad(a, ((0, 0), (B_NOPE, tail)), constant_values=fill)
    return wide(cos, 1.0), wide(sin_next, 0.0), wide(sin_prev, 0.0)


def _split_cols(w):
    out, acc = [], 0
    for s in IN_SIZES:
        out.append(w[..., acc:acc + s])
        acc += s
    return out


def _pack_w_in_kernel(w_ref, w1_ref, wg_ref):
    w = w_ref[...]
    (a_q, a_k, a_v, b_cq, b_ckv, b_kr, c_in, d_qkv, d_z, d_beta, d_alpha, gate) = _split_cols(w)
    z = lambda n: jnp.zeros((w.shape[0], n), w.dtype)
    k0, k1 = a_k[:, :A_HD], a_k[:, A_HD:]
    v0, v1 = a_v[:, :A_HD], a_v[:, A_HD:]
    w1 = jnp.concatenate([a_q, k0, k0, k1, k1, v0, v0, v1, v1, b_cq, z(256 - B_Q_RANK), b_ckv,
                          b_kr, b_kr, b_kr, b_kr, c_in, d_qkv, d_z, d_beta, d_alpha, z(112)], axis=-1)
    w1_ref[...] = w1.astype(BF16)
    wg_ref[...] = gate.astype(BF16)


def _pack_w_in(w):
    w = w.astype(BF16)
    depth, d, p = w.shape
    n_gate = IN_SIZES[-1]
    tr = 256
    return pl.pallas_call(
        _pack_w_in_kernel,
        grid=(depth, d // tr),
        in_specs=[pl.BlockSpec((None, tr, p), lambda l, i: (l, i, 0))],
        out_specs=[pl.BlockSpec((None, tr, P1), lambda l, i: (l, i, 0)),
                   pl.BlockSpec((None, tr, n_gate), lambda l, i: (l, i, 0))],
        out_shape=[jax.ShapeDtypeStruct((depth, d, P1), BF16), jax.ShapeDtypeStruct((depth, d, n_gate), BF16)],
        compiler_params=_cparams(("parallel", "parallel")),
        name="pack_w_in",
    )(w)


def _expand_kv_heads(a):
    return jnp.repeat(a, A_HEADS // A_KV_HEADS, axis=-2).reshape(a.shape[:-2] + (A_HEADS * A_HD,))


def _stacked_weights(g_pre1, g_post1, g_pre2, g_post2, w_in, a_sink, b_g_cq, b_g_ckv, b_w_uq, b_w_ukv,
                     c_w_pool, c_scale, d_conv, d_a_log, d_dt_bias, d_g_norm, w_br, w_o, w_up, w_down):
    depth = w_in.shape[0]
    w1, wg = _pack_w_in(w_in)
    wuq = b_w_uq.reshape(depth, B_Q_RANK, B_HEADS, B_NOPE + B_ROPE)
    wuq = jnp.pad(wuq, ((0, 0), (0, 256 - B_Q_RANK), (0, 0), (0, MLA_HEAD_LANES - B_NOPE - B_ROPE)))
    wuq = wuq.reshape(depth, 256, MLA_WIDE).astype(BF16)
    wukv = b_w_ukv.reshape(depth, B_KV_RANK, B_HEADS, B_NOPE + B_VD)
    wk = jnp.pad(wukv[..., :B_NOPE], ((0, 0), (0, 0), (0, 0), (0, MLA_HEAD_LANES - B_NOPE)))
    wukv = jnp.concatenate([wk.reshape(depth, B_KV_RANK, MLA_WIDE),
                            wukv[..., B_NOPE:].reshape(depth, B_KV_RANK, B_HEADS * B_VD)], -1).astype(BF16)
    same = jnp.eye(C_GROUPS, dtype=F32)[None, :, None, :, None]
    wpool = (c_w_pool[:, :, :, None, :] * same).reshape(depth, C_GROUPS * C_GW, C_GROUPS * C_GW).astype(BF16)
    gdn =jnp.stack([d_a_log.reshape(depth, 2 * D_HEADS), d_dt_bias.reshape(depth, 2 * D_HEADS)], axis=1)
    prow = jnp.pad(gdn, ((0, 0), (0, 0), (8, LANES - 16)))
    pcol = jnp.pad(gdn.transpose(0, 2, 1), ((0, 0), (8, 0), (0, 0)))
    row = lambda a: a[:, None, :]
    return dict(
        w1=w1, wg=wg, wuq=wuq, wukv=wukv,
        gpre1=row(g_pre1), gpost1=row(g_post1), gpre2=row(g_pre2), gpost2=row(g_post2),
        gcq=row(jnp.pad(b_g_cq, ((0, 0), (0, 256 - B_Q_RANK)))), gckv=row(b_g_ckv),
        sink=a_sink, wpool=wpool, cscale=row(c_scale),
        convw=jnp.pad(d_conv, ((0, 0), (0, SUBLANES - D_CONV), (0, 0))),
        prow=prow, pcol=pcol, gnorm=row(jnp.tile(d_g_norm, (1, D_HEADS))),
        wbr=w_br.astype(BF16), wo=w_o.astype(BF16), wup=w_up.astype(BF16), wdown=w_down.astype(BF16))


def _trunk_layer(x2d, mod3, lw, n_seq, t, ctx, tables=None):
    latent = ctx is not None
    n = x2d.shape[0]
    tm = min(512, t) if latent else 256
    tm_dense = min(512, n)
    if latent:
        proj, qw, kwr, vb, aqr, akr, qwr = _inproj_call(
            x2d, mod3, lw["gpre1"], lw["w1"], lw["gcq"], lw["wuq"], lw["gckv"], lw["wukv"], tables, t, tm_dense)
        kctx, vctx, ctx_ckv, ckr_wide, s0f, s0b = ctx
        out_a = _win_attn_call(lw["sink"], aqr, akr, proj, kctx, vctx, n_seq, t)
        out_b = _mla_lat_call(qw, qwr, kwr, vb, ctx_ckv, ckr_wide, lw["wukv"], n_seq, t, min(MLA_TQ, t))
        s0 = (s0f, s0b)
    else:
        proj, qw, kw, vb, ckvn, ak, av, kr = _inproj_call(
            x2d, mod3, lw["gpre1"], lw["w1"], lw["gcq"], lw["wuq"], lw["gckv"], lw["wukv"], None, t, tm_dense)
        out_a, out_b = _attn_ctx_call(lw["sink"], proj, qw, kw, vb, n_seq, t)
        s0 = None
    out_c, qkvn = _local_call(proj, lw["convw"], lw["wpool"], lw["cscale"], t, tm)
    o_f, o_b, s_fin = _gdn_call(qkvn, proj, lw["prow"], lw["pcol"], s0, n_seq, t)
    x2 = _merge_ffn_call(x2d, mod3, lw["gpre1"], out_a, out_b, out_c, o_f, o_b, proj, lw["gnorm"],
                         lw["wg"], lw["wbr"], lw["wo"], lw["gpost1"], lw["gpre2"], lw["wup"], lw["wdown"],
                         lw["gpost2"], t, min(256, tm_dense))
    new_ctx = None
    if not latent:
        new_ctx = (ak.reshape(n_seq, t, A_KV_HEADS, A_HD), av.reshape(n_seq, t, A_KV_HEADS, A_HD),
                   ckvn.reshape(n_seq, t, B_KV_RANK), kr.reshape(n_seq, t, B_ROPE), s_fin[:, 0], s_fin[:, 1])
    return x2, new_ctx


def kernel(x_prompt, x_sample, cache_a_k, cache_a_v, cache_b_ckv, cache_b_krope, state_d_fwd, state_d_bwd, c, c_ctx, w_mod, b_mod, g_pre1, g_post1, g_pre2, g_post2, w_in, a_sink, b_g_cq, b_g_ckv, b_w_uq, b_w_ukv, c_w_pool, c_scale, d_conv, d_a_log, d_dt_bias, d_g_norm, w_br, w_o, w_up, w_down):
    depth = w_mod.shape[0]
    bp, tp, d = x_prompt.shape
    bs, ts, _ = x_sample.shape
    assert bs + 1 <= SUBLANES
    cond8 = jnp.zeros((SUBLANES, d), F32).at[0].set(c_ctx).at[1:1 + bs].set(c)
    mod_all = _mod_call(cond8, w_mod, b_mod).reshape(depth, SUBLANES, 1, 6 * d)
    yp = x_prompt.reshape(bp * tp, d)
    ys = x_sample.reshape(bs * ts, d)
    new = [[] for _ in range(6)]
    tables = _rope_tables(ts, A_HD, LANES // A_HD) + _rope_tables_mla_wide(ts)
    kctx_all, vctx_all = _expand_kv_heads(cache_a_k), _expand_kv_heads(cache_a_v)
    ckr_wide_all = jnp.tile(jnp.pad(cache_b_krope, ((0, 0), (0, 0), (0, 0),
                                                    (B_NOPE, MLA_HEAD_LANES - B_NOPE - B_ROPE))), (1, 1, 1, B_HEADS))
    stacked = _stacked_weights(g_pre1, g_post1, g_pre2, g_post2, w_in, a_sink, b_g_cq, b_g_ckv, b_w_uq, b_w_ukv,
                               c_w_pool, c_scale, d_conv, d_a_log, d_dt_bias, d_g_norm, w_br, w_o, w_up, w_down)
    for l in range(depth):
        lw = {k: _Layer(v, l) for k, v in stacked.items()}
        mod_ctx = _Mod(mod_all, l, 0, 1)
        mod_lat = _Mod(mod_all, l, 1, bs)
        yp, nctx = _trunk_layer(yp, mod_ctx, lw, bp, tp, None)
        for acc, val in zip(new, nctx):
            acc.append(val)
        ctx = tuple(_Layer(a, l) for a in (kctx_all, vctx_all, cache_b_ckv, ckr_wide_all, state_d_fwd, state_d_bwd))
        ys, _ = _trunk_layer(ys, mod_lat, lw, bs, ts, ctx, tables)
    return (yp.reshape(bp, tp, d), ys.reshape(bs, ts, d)) + tuple(jnp.stack(v, axis=1) for v in new)
```
